```python
import math
import jax, jax.numpy as jnp
from jax import lax
import numpy as np

D_MODEL = 2048
BATCH = 4
SEQ = 2048
DEPTH = 1
DEC_BATCH = 128
DEC_SEQ = 8
PAST_LEN = 16384
PAGE_SIZE = 128

ML_HEADS = 4
ML_DK = 256
ML_DV = 512
ML_QK = ML_HEADS * ML_DK
ML_V = ML_HEADS * ML_DV
HG_HEADS = 8
HG_DK = 128
HG_DV = 256
HG_K = HG_HEADS * HG_DK
HG_V = HG_HEADS * HG_DV
D_FF = 4 * D_MODEL
CHUNK = 64
LN_EPS = 1e-5
DEEPNORM_ALPHA = (2.0 * DEPTH) ** 0.25
DEEPNORM_BETA = (8.0 * DEPTH) ** -0.25
SPLITS = (ML_QK, ML_QK, ML_V, ML_HEADS, ML_HEADS, ML_V, HG_K, HG_K, HG_V, HG_V, D_MODEL, D_MODEL)
D_IN = sum(SPLITS)

kernel_name = "hybrid_mlstm_hgrn2_deepnorm_step"


def _layernorm(x, g, b):
    xf = x.astype(jnp.float32)
    mu = jnp.mean(xf, axis=-1, keepdims=True)
    var = jnp.mean(jnp.square(xf - mu), axis=-1, keepdims=True)
    return ((xf - mu) * lax.rsqrt(var + LN_EPS) * g.astype(jnp.float32) + b.astype(jnp.float32)).astype(x.dtype)


def _split_chunks(a, nc, L):
    a = a.reshape(a.shape[:2] + (nc, L) + a.shape[3:])
    return jnp.moveaxis(a, 2, 0)


def _merge_chunks(h):
    h = jnp.moveaxis(h, 0, 2)
    return h.reshape(h.shape[:2] + (h.shape[2] * h.shape[3],) + h.shape[4:])


def _mlstm(q, k, v, ig, lf, C0, n0, m0):
    T = q.shape[2]
    L = math.gcd(T, CHUNK)
    nc = T // L
    causal = jnp.tril(jnp.ones((L, L), dtype=bool))

    def step(carry, xs):
        C, n, m = carry
        qc, kc, vc, igc, lfc = xs
        b = jnp.cumsum(lfc, axis=-1)
        logd = b[..., :, None] - b[..., None, :] + igc[..., None, :]
        logd = jnp.where(causal, logd, -jnp.inf)
        inter = b + m[..., None]
        m_t = jnp.maximum(inter, jnp.max(logd, axis=-1))
        d = jnp.exp(logd - m_t[..., None])
        w_inter = jnp.exp(inter - m_t)
        s = jnp.einsum('bhtk,bhsk->bhts', qc, kc) * d
        num = jnp.einsum('bhts,bhsv->bhtv', s, vc) + w_inter[..., None] * jnp.einsum('bhtk,bhkv->bhtv', qc, C)
        den = jnp.sum(s, axis=-1) + w_inter * jnp.einsum('bhtk,bhk->bht', qc, n)
        h = num / jnp.maximum(jnp.abs(den), jnp.exp(-m_t))[..., None]
        m_new = m_t[..., -1]
        w_end = jnp.exp(b[..., -1:] - b + igc - m_new[..., None])
        decay = jnp.exp(b[..., -1] + m - m_new)
        C_new = decay[..., None, None] * C + jnp.einsum('bhs,bhsk,bhsv->bhkv', w_end, kc, vc)
        n_new = decay[..., None] * n + jnp.einsum('bhs,bhsk->bhk', w_end, kc)
        return (C_new, n_new, m_new), h

    xs = (_split_chunks(q, nc, L), _split_chunks(k, nc, L), _split_chunks(v, nc, L),
          _split_chunks(ig, nc, L), _split_chunks(lf, nc, L))
    (C, n, m), h = lax.scan(step, (C0, n0, m0), xs)
    return _merge_chunks(h), C, n, m


def _hgrn2(q, lf, kin, i, S0):
    T = q.shape[2]
    L = math.gcd(T, CHUNK)
    nc = T // L
    causal = jnp.tril(jnp.ones((L, L), dtype=bool))

    def step(S, xs):
        qc, lfc, kc, ic = xs
        G = jnp.cumsum(lfc, axis=2)
        rel = G[:, :, :, None, :] - G[:, :, None, :, :]
        rel = jnp.where(causal[:, :, None], rel, -jnp.inf)
        a = jnp.sum(qc[:, :, :, None, :] * jnp.exp(rel) * kc[:, :, None, :, :], axis=-1)
        o = jnp.einsum('bhts,bhsv->bhtv', a, ic) + jnp.einsum('bhtk,bhkv->bhtv', qc * jnp.exp(G), S)
        G_end = G[:, :, -1]
        S_new = jnp.exp(G_end)[..., None] * S + jnp.einsum('bhsk,bhsv->bhkv', kc * jnp.exp(G_end[:, :, None] - G), ic)
        return S_new, o

    xs = (_split_chunks(q, nc, L), _split_chunks(lf, nc, L), _split_chunks(kin, nc, L), _split_chunks(i, nc, L))
    S, o = lax.scan(step, S0, xs)
    return _merge_chunks(o), S


def _heads(a, n_heads):
    B, T, _ = a.shape
    return a.reshape(B, T, n_heads, -1).transpose(0, 2, 1, 3).astype(jnp.float32)


def _unheads(h):
    B, H, T, D = h.shape
    return h.transpose(0, 2, 1, 3).reshape(B, T, H * D)


def _layer(x, C0, n0, m0, S0, lb, w_in, b_ig, b_fg, ml_norm_g, hg_norm_g,
           w_branch_a, w_branch_b, w_out, ln1_g, ln1_b, w_up, w_down, ln2_g, ln2_b):
    f32 = jnp.float32
    offs = []
    acc = 0
    for s in SPLITS[:-1]:
        acc += s
        offs.append(acc)
    proj = jnp.einsum('btd,de->bte', x, w_in)
    (ml_q, ml_k, ml_v, ml_i, ml_f, ml_o, hg_q, hg_f, hg_i, hg_g, gate_a, gate_b) = jnp.split(proj, offs, axis=-1)

    q = _heads(ml_q, ML_HEADS)
    k = _heads(ml_k, ML_HEADS) * (ML_DK ** -0.5)
    v = _heads(ml_v, ML_HEADS)
    ig = (ml_i.astype(f32) + b_ig.astype(f32)).transpose(0, 2, 1)
    lf = jax.nn.log_sigmoid(ml_f.astype(f32) + b_fg.astype(f32)).transpose(0, 2, 1)
    h, C, n, m = _mlstm(q, k, v, ig, lf, C0.astype(f32), n0.astype(f32), m0.astype(f32))
    mu = jnp.mean(h, axis=-1, keepdims=True)
    var = jnp.mean(jnp.square(h - mu), axis=-1, keepdims=True)
    h = _unheads((h - mu) * lax.rsqrt(var + LN_EPS)) * ml_norm_g.astype(f32)
    branch_a = (h * jax.nn.sigmoid(ml_o.astype(f32))).astype(x.dtype)

    hq = _heads(hg_q, HG_HEADS)
    lbh = lb.reshape(HG_HEADS, 1, HG_DK)
    f = lbh + (1.0 - lbh) * jax.nn.sigmoid(_heads(hg_f, HG_HEADS))
    o, S = _hgrn2(hq, jnp.log(f), 1.0 - f, _heads(hg_i, HG_HEADS), S0.astype(f32))
    o = o * lax.rsqrt(jnp.mean(jnp.square(o), axis=-1, keepdims=True) + LN_EPS)
    o = _unheads(o) * hg_norm_g.astype(f32)
    branch_b = (o * jax.nn.silu(hg_g.astype(f32))).astype(x.dtype)

    ya = jnp.einsum('btv,vd->btd', branch_a, w_branch_a)
    yb = jnp.einsum('btv,vd->btd', branch_b, w_branch_b)
    merged = jax.nn.sigmoid(gate_a) * ya + jax.nn.sigmoid(gate_b) * yb
    mix = jnp.einsum('btd,de->bte', merged, w_out)
    x1 = _layernorm(DEEPNORM_ALPHA * x + mix, ln1_g, ln1_b)

    hid = jnp.square(jax.nn.relu(jnp.einsum('btd,df->btf', x1, w_up)))
    ff = jnp.einsum('btf,fd->btd', hid, w_down)
    y = _layernorm(DEEPNORM_ALPHA * x1 + ff, ln2_g, ln2_b)
    dt = x.dtype
    return y, C.astype(dt), n.astype(dt), m.astype(dt), S.astype(dt)


def setup_inputs(seed: int = 0) -> dict:
    key = jax.random.key(seed)
    ks = jax.random.split(key, 24)
    f32 = jnp.float32

    def nrm(k, shape, s):
        return jax.random.normal(k, shape, f32) * s

    return {
        "x_prompt": nrm(ks[0], (BATCH, SEQ, D_MODEL), 1.0),
        "x_sample": nrm(ks[1], (DEC_BATCH, DEC_SEQ, D_MODEL), 1.0),
        "state_mlstm_C": nrm(ks[2], (DEPTH, DEC_BATCH, ML_HEADS, ML_DK, ML_DV), 1.0),
        "state_mlstm_n": nrm(ks[3], (DEPTH, DEC_BATCH, ML_HEADS, ML_DK), 1.0),
        "state_mlstm_m": jax.random.uniform(ks[4], (DEPTH, DEC_BATCH, ML_HEADS), f32, 0.0, 3.0),
        "state_hgrn_S": nrm(ks[5], (DEPTH, DEC_BATCH, HG_HEADS, HG_DK, HG_DV), 1.0),
        "hg_lb_logits": nrm(ks[6], (DEPTH + 1, HG_K), 0.1),
        "w_in": nrm(ks[7], (DEPTH, D_MODEL, D_IN), D_MODEL ** -0.5),
        "b_ig": nrm(ks[8], (DEPTH, ML_HEADS), 0.1),
        "b_fg": jnp.linspace(3.0, 6.0, ML_HEADS, dtype=f32)[None, :] + nrm(ks[9], (DEPTH, ML_HEADS), 0.1),
        "ml_norm_g": 1.0 + nrm(ks[10], (DEPTH, ML_V), 0.05),
        "hg_norm_g": 1.0 + nrm(ks[11], (DEPTH, HG_V), 0.05),
        "w_branch_a": nrm(ks[12], (DEPTH, ML_V, D_MODEL), ML_V ** -0.5),
        "w_branch_b": nrm(ks[13], (DEPTH, HG_V, D_MODEL), HG_V ** -0.5),
        "w_out": nrm(ks[14], (DEPTH, D_MODEL, D_MODEL), DEEPNORM_BETA * D_MODEL ** -0.5),
        "ln1_g": 1.0 + nrm(ks[15], (DEPTH, D_MODEL), 0.05),
        "ln1_b": nrm(ks[16], (DEPTH, D_MODEL), 0.02),
        "w_up": nrm(ks[17], (DEPTH, D_MODEL, D_FF), D_MODEL ** -0.5),
        "w_down": nrm(ks[18], (DEPTH, D_FF, D_MODEL), DEEPNORM_BETA * D_FF ** -0.5),
        "ln2_g": 1.0 + nrm(ks[19], (DEPTH, D_MODEL), 0.05),
        "ln2_b": nrm(ks[20], (DEPTH, D_MODEL), 0.02),
    }


def reference(x_prompt, x_sample, state_mlstm_C, state_mlstm_n, state_mlstm_m, state_hgrn_S,
              hg_lb_logits, w_in, b_ig, b_fg, ml_norm_g, hg_norm_g, w_branch_a, w_branch_b, w_out,
              ln1_g, ln1_b, w_up, w_down, ln2_g, ln2_b):
    dt = x_prompt.dtype
    lb_all = jnp.cumsum(jax.nn.softmax(hg_lb_logits.astype(jnp.float32), axis=0), axis=0)
    xp, xs = x_prompt, x_sample
    Cp_l, np_l, mp_l, Sp_l = [], [], [], []
    Cs_l, ns_l, ms_l, Ss_l = [], [], [], []
    for l in range(DEPTH):
        params = (lb_all[l], w_in[l], b_ig[l], b_fg[l], ml_norm_g[l], hg_norm_g[l],
                  w_branch_a[l], w_branch_b[l], w_out[l], ln1_g[l], ln1_b[l], w_up[l], w_down[l], ln2_g[l], ln2_b[l])
        C0 = jnp.zeros((BATCH, ML_HEADS, ML_DK, ML_DV), dt)
        n0 = jnp.zeros((BATCH, ML_HEADS, ML_DK), dt)
        m0 = jnp.zeros((BATCH, ML_HEADS), dt)
        S0 = jnp.zeros((BATCH, HG_HEADS, HG_DK, HG_DV), dt)
        xp, Cp, n_p, mp, Sp = _layer(xp, C0, n0, m0, S0, *params)
        xs, Cs, n_s, ms, Ss = _layer(xs, state_mlstm_C[l], state_mlstm_n[l], state_mlstm_m[l], state_hgrn_S[l], *params)
        Cp_l.append(Cp); np_l.append(n_p); mp_l.append(mp); Sp_l.append(Sp)
        Cs_l.append(Cs); ns_l.append(n_s); ms_l.append(ms); Ss_l.append(Ss)
    return (xp, xs,
            jnp.stack(Cp_l), jnp.stack(np_l), jnp.stack(mp_l), jnp.stack(Sp_l),
            jnp.stack(Cs_l), jnp.stack(ns_l), jnp.stack(ms_l), jnp.stack(Ss_l))
```

```python
import functools

import jax
import jax.numpy as jnp
from jax import lax
from jax.experimental import pallas as pl
from jax.experimental.pallas import tpu as pltpu

F32 = jnp.float32
BF16 = jnp.bfloat16

D_MODEL = 2048
DEPTH = 1
ML_HEADS, ML_DK, ML_DV = 4, 256, 512
HG_HEADS, HG_DK, HG_DV = 8, 128, 256
ML_QK = ML_HEADS * ML_DK
ML_V = ML_HEADS * ML_DV
HG_K = HG_HEADS * HG_DK
HG_V = HG_HEADS * HG_DV
D_FF = 4 * D_MODEL
LN_EPS = 1e-5
DEEPNORM_ALPHA = (2.0 * DEPTH) ** 0.25
ML_K_SCALE = ML_DK ** -0.5
LANES = 128

_OFF_ML_Q = 0
_OFF_ML_K = _OFF_ML_Q + ML_QK
_OFF_ML_V = _OFF_ML_K + ML_QK
_OFF_ML_I = _OFF_ML_V + ML_V
_OFF_ML_F = _OFF_ML_I + ML_HEADS
_OFF_ML_O = _OFF_ML_F + ML_HEADS
_OFF_HG_Q = _OFF_ML_O + ML_V
_OFF_HG_F = _OFF_HG_Q + HG_K
_OFF_HG_I = _OFF_HG_F + HG_K
_OFF_HG_G = _OFF_HG_I + HG_V
_OFF_GATE_A = _OFF_HG_G + HG_V
_OFF_GATE_B = _OFF_GATE_A + D_MODEL
D_IN = _OFF_GATE_B + D_MODEL

_P_ML_V = 0
_P_ML_O = _P_ML_V + ML_V
_P_HG_I = _P_ML_O + ML_V
_P_HG_G = _P_HG_I + HG_V
_P_GATE_A = _P_HG_G + HG_V
_P_GATE_B = _P_GATE_A + D_MODEL
_P_ML_Q = _P_GATE_B + D_MODEL
_P_ML_K = _P_ML_Q + ML_QK
_P_HG_Q = _P_ML_K + ML_QK
_P_WIDTH = _P_HG_Q + HG_K
_PF_WIDTH = HG_K + LANES


def _sigmoid(x):
    return 1.0 / (1.0 + jnp.exp(-x))


def _log_sigmoid(x):
    return jnp.minimum(x, 0.0) - jnp.log1p(jnp.exp(-jnp.abs(x)))


def _split3(x):
    hi = x.astype(BF16)
    r = x - hi.astype(F32)
    mid = r.astype(BF16)
    lo = (r - mid.astype(F32)).astype(BF16)
    return hi, mid, lo


def _dot01_right(t01, x):
    hi, mid, lo = _split3(x)
    d = lambda a: jnp.dot(t01, a, preferred_element_type=F32)
    return (d(hi) + d(mid)) + d(lo)


def _dot01_left(x, t01):
    hi, mid, lo = _split3(x)
    d = lambda a: jnp.dot(a, t01, preferred_element_type=F32)
    return (d(hi) + d(mid)) + d(lo)


def _dot_nt(a, b):
    return lax.dot_general(a, b, (((1,), (1,)), ((), ())), preferred_element_type=F32)


def _dot_tn(a, b):
    return lax.dot_general(a, b, (((0,), (0,)), ((), ())), preferred_element_type=F32)


def _seq_masks(rows, seq_len):
    ri = lax.broadcasted_iota(jnp.int32, (rows, rows), 0)
    ci = lax.broadcasted_iota(jnp.int32, (rows, rows), 1)
    lower = jnp.where(ci <= ri, 1.0, 0.0)
    upper = jnp.where(ri <= ci, 1.0, 0.0)
    if seq_len == rows:
        return jnp.ones((rows, rows), F32), lower, upper
    shift = seq_len.bit_length() - 1
    same = jnp.where((ri >> shift) == (ci >> shift), 1.0, 0.0)
    return same, same * lower, same * upper


def _mm_kernel(x_ref, w_ref, o_ref):
    o_ref[...] = jnp.dot(x_ref[...], w_ref[...], preferred_element_type=F32).astype(o_ref.dtype)


def _matmul(x, w, out_dtype, tm, tn):
    m, k = x.shape
    n = w.shape[1]
    return pl.pallas_call(
        _mm_kernel,
        grid=(m // tm, n // tn),
        in_specs=[pl.BlockSpec((tm, k), lambda i, j: (i, 0)),
                  pl.BlockSpec((k, tn), lambda i, j: (0, j))],
        out_specs=pl.BlockSpec((tm, tn), lambda i, j: (i, j)),
        out_shape=jax.ShapeDtypeStruct((m, n), out_dtype),
        compiler_params=pltpu.CompilerParams(dimension_semantics=("parallel", "arbitrary")),
        name="in_proj",
    )(x, w)


def _mlstm_kernel(*refs, rows, seq_len, zero_init):
    if zero_init:
        (q_ref, k_ref, v_ref, og_ref, gc_ref, gr_ref, brow_ref, bcol_ref, ng_ref,
         ba_ref, c_ref, n_ref, mrow_ref, m_sc) = refs
        c0_ref, n0_ref = c_ref, n_ref
    else:
        (q_ref, k_ref, v_ref, og_ref, gc_ref, gr_ref, brow_ref, bcol_ref, ng_ref, c0_ref, n0_ref,
         ba_ref, c_ref, n_ref, mrow_ref) = refs
    nseq = rows // seq_len
    head = pl.program_id(1)

    if zero_init:
        @pl.when(pl.program_id(2) == 0)
        def _init():
            c_ref[...] = jnp.zeros(c_ref.shape, F32)
            n_ref[...] = jnp.zeros(n_ref.shape, F32)
            m_sc[...] = jnp.zeros(m_sc.shape, F32)

    same, lower, upper = _seq_masks(rows, seq_len)
    causal = lower > 0.5
    lane = lax.broadcasted_iota(jnp.int32, (rows, LANES), 1)
    sub = lax.broadcasted_iota(jnp.int32, (8, rows), 0)

    def sel_lane(x, idx):
        return jnp.sum(jnp.where(lane == idx, x, 0.0), axis=1, keepdims=True)

    def sel_sub(x, idx):
        return jnp.sum(jnp.where(sub == idx, x, 0.0), axis=0, keepdims=True)

    gc_raw = gc_ref[...]
    gc = gc_raw + brow_ref[...]
    lf_cols = _log_sigmoid(gc)
    b_cols = _dot01_right(lower.astype(BF16), lf_cols)
    ig_col = sel_lane(gc, head)
    b_col = sel_lane(b_cols, head + ML_HEADS)
    if zero_init:
        m_prev = jnp.broadcast_to(m_sc[0:1, 0:1], (rows, 1))
    else:
        m_prev = sel_lane(gc_raw, head + 2 * ML_HEADS)

    gr = gr_ref[0] + bcol_ref[...]
    b_rows = _dot01_left(_log_sigmoid(gr), upper.astype(BF16))
    ig_row = sel_sub(gr, head)
    b_row = sel_sub(b_rows, head + ML_HEADS)

    logd = jnp.where(causal, (b_col - b_row) + ig_row, -jnp.inf)
    m_t = jnp.maximum(b_col + m_prev, jnp.max(logd, axis=1, keepdims=True))
    d = jnp.exp(logd - m_t)
    w_inter = jnp.exp(b_col + m_prev - m_t)

    qb = q_ref[...]
    kb = k_ref[...]
    vb = v_ref[...]
    s = _dot_nt(qb, kb) * (d * ML_K_SCALE)
    num = jnp.dot(s.astype(BF16), vb, preferred_element_type=F32)
    den = jnp.sum(s, axis=1, keepdims=True)

    if nseq == 1:
        b_end = b_col[rows - 1:rows, :]
        m_new = m_t[rows - 1:rows, :]
    else:
        b_end = sel_lane(_dot01_right(same.astype(BF16), lf_cols), head + ML_HEADS)
        ci = lax.broadcasted_iota(jnp.int32, (rows, rows), 1)
        last = same * jnp.where((ci & (seq_len - 1)) == seq_len - 1, 1.0, 0.0)
        m_new = _dot01_right(last.astype(BF16), jnp.broadcast_to(m_t, (rows, LANES)))[:, 0:1]
    w_end = jnp.exp(b_end - b_col + ig_col - m_new)
    decay = jnp.exp(b_end + m_prev - m_new)

    qf = qb.astype(F32)
    kw = (w_end * ML_K_SCALE) * kb.astype(F32)

    if nseq == 1:
        c_prev = c0_ref[0, 0]
        n_prev = n0_ref[0, 0]
        q_c = jnp.dot(qb, c_prev.astype(BF16), preferred_element_type=F32)
        q_n = jnp.sum(qf * n_prev, axis=1, keepdims=True)
        dec = decay[0:1, :]
        c_ref[0, 0] = dec * c_prev + _dot_tn(kw.astype(BF16), vb)
        n_ref[0, 0] = dec * n_prev + jnp.sum(kw, axis=0, keepdims=True)
    else:
        shift = seq_len.bit_length() - 1
        row_seq = lax.broadcasted_iota(jnp.int32, (rows, 1), 0) >> shift
        q_c = jnp.zeros((rows, ML_DV), F32)
        q_n = jnp.zeros((rows, 1), F32)
        for j in range(nseq):
            in_seq = row_seq == j
            c_prev = c0_ref[j, 0]
            n_prev = n0_ref[j, 0]
            q_c = jnp.where(in_seq, jnp.dot(qb, c_prev.astype(BF16), preferred_element_type=F32), q_c)
            q_n = jnp.where(in_seq, jnp.sum(qf * n_prev, axis=1, keepdims=True), q_n)
            kw_j = jnp.where(in_seq, kw, 0.0)
            dec = decay[j * seq_len:j * seq_len + 1, :]
            c_ref[j, 0] = dec * c_prev + _dot_tn(kw_j.astype(BF16), vb)
            n_ref[j, 0] = dec * n_prev + jnp.sum(kw_j, axis=0, keepdims=True)

    num = num + w_inter * q_c
    den = den + w_inter * q_n
    hh = num / jnp.maximum(jnp.abs(den), jnp.exp(-m_t))
    mu = jnp.mean(hh, axis=1, keepdims=True)
    xc = hh - mu
    var = jnp.mean(xc * xc, axis=1, keepdims=True)
    hn = xc * lax.rsqrt(var + LN_EPS) * ng_ref[...]
    ba_ref[...] = (hn * _sigmoid(og_ref[...].astype(F32))).astype(BF16)
    mrow_ref[0] = jnp.broadcast_to(m_t, (rows, LANES))
    if zero_init:
        m_sc[...] = jnp.broadcast_to(m_new, m_sc.shape)


def _mlstm(proj, gates_col, gates_row, brow, bcol, norm_g, c0, n0, *, batch, seq, rows, seq_len):
    tokens = batch * seq
    zero_init = c0 is None
    nseq = rows // seq_len
    nchunk = seq // seq_len if nseq == 1 else 1
    ngroup = tokens // (rows * nchunk)
    bq, bv = ML_DK, ML_DV

    def row_blk(g, h, c):
        return g * nchunk + c

    in_specs = [
        pl.BlockSpec((rows, bq), lambda g, h, c: (row_blk(g, h, c), _P_ML_Q // bq + h)),
        pl.BlockSpec((rows, bq), lambda g, h, c: (row_blk(g, h, c), _P_ML_K // bq + h)),
        pl.BlockSpec((rows, bv), lambda g, h, c: (row_blk(g, h, c), _P_ML_V // bv + h)),
        pl.BlockSpec((rows, bv), lambda g, h, c: (row_blk(g, h, c), _P_ML_O // bv + h)),
        pl.BlockSpec((rows, LANES), lambda g, h, c: (row_blk(g, h, c), 0)),
        pl.BlockSpec((1, 8, rows), lambda g, h, c: (row_blk(g, h, c), 0, 0)),
        pl.BlockSpec((1, LANES), lambda g, h, c: (0, 0)),
        pl.BlockSpec((8, 1), lambda g, h, c: (0, 0)),
        pl.BlockSpec((1, bv), lambda g, h, c: (0, h)),
    ]
    args = [proj, proj, proj, proj, gates_col, gates_row, brow, bcol, norm_g]
    state_specs = [
        pl.BlockSpec((nseq, 1, ML_DK, ML_DV), lambda g, h, c: (g, h, 0, 0)),
        pl.BlockSpec((nseq, 1, 1, ML_DK), lambda g, h, c: (g, h, 0, 0)),
    ]
    scratch = []
    if zero_init:
        scratch = [pltpu.VMEM((1, LANES), F32)]
    else:
        in_specs += state_specs
        args += [c0, n0]
    out_specs = [
        pl.BlockSpec((rows, bv), lambda g, h, c: (row_blk(g, h, c), h)),
        *state_specs,
        pl.BlockSpec((1, rows, LANES), lambda g, h, c: (h, row_blk(g, h, c), 0)),
    ]
    out_shape = [
        jax.ShapeDtypeStruct((tokens, ML_V), BF16),
        jax.ShapeDtypeStruct((batch, ML_HEADS, ML_DK, ML_DV), F32),
        jax.ShapeDtypeStruct((batch, ML_HEADS, 1, ML_DK), F32),
        jax.ShapeDtypeStruct((ML_HEADS, tokens, LANES), F32),
    ]
    return pl.pallas_call(
        functools.partial(_mlstm_kernel, rows=rows, seq_len=seq_len, zero_init=zero_init),
        grid=(ngroup, ML_HEADS, nchunk),
        in_specs=in_specs,
        out_specs=out_specs,
        out_shape=out_shape,
        scratch_shapes=scratch,
        compiler_params=pltpu.CompilerParams(
            dimension_semantics=("parallel", "parallel", "arbitrary")),
        name="mlstm",
    )(*args)


def _hgrn_intra(q, kin, g_cum, rows, direct, levels):
    width = min(rows, LANES)
    ngroups = rows // width
    rowi = lax.broadcasted_iota(jnp.int32, (direct, width), 0)
    lanei = lax.broadcasted_iota(jnp.int32, (direct, width), 1)
    panels = []
    for blk in range(rows // direct):
        r0 = blk * direct
        qb = q[r0:r0 + direct]
        gb = g_cum[r0:r0 + direct]
        kb = kin[r0:r0 + direct]
        lane0 = r0 % width
        ag = jnp.zeros((direct, width), F32)
        for s in range(direct):
            e = jnp.exp(jnp.minimum(gb - gb[s:s + 1, :], 0.0))
            col = jnp.sum(qb * e * kb[s:s + 1, :], axis=1, keepdims=True)
            ag = jnp.where(lanei == lane0 + s, jnp.where(rowi >= s, col, 0.0), ag)
        grp = r0 // width
        pieces = [jnp.zeros((direct, width), F32)] * grp + [ag]
        pieces += [jnp.zeros((direct, width), F32)] * (ngroups - grp - 1)
        panels.append(pieces[0] if ngroups == 1 else jnp.concatenate(pieces, axis=1))
    a = panels[0] if len(panels) == 1 else jnp.concatenate(panels, axis=0)

    if levels:
        ri = lax.broadcasted_iota(jnp.int32, (rows, rows), 0)
        ci = lax.broadcasted_iota(jnp.int32, (rows, rows), 1)
        rowid = lax.broadcasted_iota(jnp.int32, (rows, 1), 0)
    for block, sub_size in levels:
        nsub = block // sub_size
        bshift = block.bit_length() - 1
        sshift = sub_size.bit_length() - 1
        row_sub = (rowid & (block - 1)) >> sshift
        q_parts, k_parts = [], []
        for j in range(1, nsub):
            refs = []
            for b0 in range(0, rows, block):
                r = b0 + j * sub_size - 1
                refs.append(jnp.broadcast_to(g_cum[r:r + 1, :], (block, HG_DK)))
            g_ref = refs[0] if len(refs) == 1 else jnp.concatenate(refs, axis=0)
            qt = q * jnp.exp(jnp.minimum(g_cum - g_ref, 0.0))
            q_parts.append(jnp.where(row_sub == j, qt, 0.0).astype(BF16))
            k_parts.append((kin * jnp.exp(jnp.minimum(g_ref - g_cum, 0.0))).astype(BF16))
        qcat = q_parts[0] if len(q_parts) == 1 else jnp.concatenate(q_parts, axis=1)
        kcat = k_parts[0] if len(k_parts) == 1 else jnp.concatenate(k_parts, axis=1)
        a_lvl = _dot_nt(qcat, kcat)
        same_block = (ri >> bshift) == (ci >> bshift)
        earlier_sub = ((ci & (block - 1)) >> sshift) < ((ri & (block - 1)) >> sshift)
        a = a + jnp.where(same_block, jnp.where(earlier_sub, a_lvl, 0.0), 0.0)
    return a


def _hgrn_kernel(*refs, layer, rows, seq_len, direct, levels, zero_init):
    if zero_init:
        q_ref, f_ref, i_ref, g_ref, lbl_ref, ng_ref, bb_ref, s_ref = refs
        s0_ref = s_ref
    else:
        q_ref, f_ref, i_ref, g_ref, lbl_ref, ng_ref, s0_ref, bb_ref, s_ref = refs
    nseq = rows // seq_len

    if zero_init:
        @pl.when(pl.program_id(2) == 0)
        def _init():
            s_ref[...] = jnp.zeros(s_ref.shape, F32)

    lg = lbl_ref[...]
    ex = jnp.exp(lg - jnp.max(lg, axis=0, keepdims=True))
    lb = jnp.sum(ex[0:layer + 1, :], axis=0, keepdims=True) / jnp.sum(ex, axis=0, keepdims=True)

    f = lb + (1.0 - lb) * _sigmoid(f_ref[...])
    lf = jnp.log(f)
    kin = 1.0 - f
    same, lower, _ = _seq_masks(rows, seq_len)
    g_cum = _dot01_right(lower.astype(BF16), lf)
    q = q_ref[...].astype(F32)
    if nseq == 1:
        g_end = jnp.broadcast_to(g_cum[rows - 1:rows, :], (rows, HG_DK))
    else:
        g_end = _dot01_right(same.astype(BF16), lf)
    qg = (q * jnp.exp(g_cum)).astype(BF16)
    kg = kin * jnp.exp(g_end - g_cum)
    ib = i_ref[...]

    a = _hgrn_intra(q, kin, g_cum, rows, direct, levels)
    o = jnp.dot(a.astype(BF16), ib, preferred_element_type=F32)

    def decay_cols(row):
        col = jnp.broadcast_to(jnp.exp(row), (HG_DK, HG_DK)).T
        return jnp.concatenate([col] * (HG_DV // HG_DK), axis=1)

    if nseq == 1:
        s_prev = s0_ref[0, 0]
        o = o + jnp.dot(qg, s_prev.astype(BF16), preferred_element_type=F32)
        s_ref[0, 0] = decay_cols(g_end[0:1, :]) * s_prev + _dot_tn(kg.astype(BF16), ib)
    else:
        shift = seq_len.bit_length() - 1
        row_seq = lax.broadcasted_iota(jnp.int32, (rows, 1), 0) >> shift
        for j in range(nseq):
            in_seq = row_seq == j
            s_prev = s0_ref[j, 0]
            o_j = jnp.dot(qg, s_prev.astype(BF16), preferred_element_type=F32)
            o = o + jnp.where(in_seq, o_j, 0.0)
            kg_j = jnp.where(in_seq, kg, 0.0).astype(BF16)
            s_ref[j, 0] = (decay_cols(g_end[j * seq_len:j * seq_len + 1, :]) * s_prev
                           + _dot_tn(kg_j, ib))

    o = o * lax.rsqrt(jnp.mean(o * o, axis=1, keepdims=True) + LN_EPS)
    gate = g_ref[...].astype(F32)
    bb_ref[...] = (o * ng_ref[...] * (gate * _sigmoid(gate))).astype(BF16)


def _hgrn(proj, proj_f, lb_logits, norm_g, s0, *, layer, batch, seq, rows, seq_len, direct, levels):
    tokens = batch * seq
    zero_init = s0 is None
    nseq = rows // seq_len
    nchunk = seq // seq_len if nseq == 1 else 1
    ngroup = tokens // (rows * nchunk)

    def row_blk(g, h, c):
        return g * nchunk + c

    in_specs = [
        pl.BlockSpec((rows, HG_DK), lambda g, h, c: (row_blk(g, h, c), _P_HG_Q // HG_DK + h)),
        pl.BlockSpec((rows, HG_DK), lambda g, h, c: (row_blk(g, h, c), h)),
        pl.BlockSpec((rows, HG_DV), lambda g, h, c: (row_blk(g, h, c), _P_HG_I // HG_DV + h)),
        pl.BlockSpec((rows, HG_DV), lambda g, h, c: (row_blk(g, h, c), _P_HG_G // HG_DV + h)),
        pl.BlockSpec((DEPTH + 1, HG_DK), lambda g, h, c: (0, h)),
        pl.BlockSpec((1, HG_DV), lambda g, h, c: (0, h)),
    ]
    args = [proj, proj_f, proj, proj, lb_logits, norm_g]
    state_spec = pl.BlockSpec((nseq, 1, HG_DK, HG_DV), lambda g, h, c: (g, h, 0, 0))
    if not zero_init:
        in_specs.append(state_spec)
        args.append(s0)
    return pl.pallas_call(
        functools.partial(_hgrn_kernel, layer=layer, rows=rows, seq_len=seq_len, direct=direct,
                          levels=levels, zero_init=zero_init),
        grid=(ngroup, HG_HEADS, nchunk),
        in_specs=in_specs,
        out_specs=[pl.BlockSpec((rows, HG_DV), lambda g, h, c: (row_blk(g, h, c), h)), state_spec],
        out_shape=[jax.ShapeDtypeStruct((tokens, HG_V), BF16),
                   jax.ShapeDtypeStruct((batch, HG_HEADS, HG_DK, HG_DV), F32)],
        compiler_params=pltpu.CompilerParams(
            dimension_semantics=("parallel", "parallel", "arbitrary")),
        name="hgrn2",
    )(*args)


def _merge_kernel(ba_ref, wa_ref, bb_ref, wb_ref, ga_ref, gb_ref, o_ref):
    ya = jnp.dot(ba_ref[...], wa_ref[...], preferred_element_type=F32)
    yb = jnp.dot(bb_ref[...], wb_ref[...], preferred_element_type=F32)
    ga = _sigmoid(ga_ref[...].astype(F32))
    gb = _sigmoid(gb_ref[...].astype(F32))
    o_ref[...] = (ga * ya + gb * yb).astype(o_ref.dtype)


def _merge(branch_a, w_a, branch_b, w_b, proj, tm, tn):
    m = branch_a.shape[0]
    ga_blk = _P_GATE_A // tn
    gb_blk = _P_GATE_B // tn
    return pl.pallas_call(
        _merge_kernel,
        grid=(m // tm, D_MODEL // tn),
        in_specs=[pl.BlockSpec((tm, ML_V), lambda i, j: (i, 0)),
                  pl.BlockSpec((ML_V, tn), lambda i, j: (0, j)),
                  pl.BlockSpec((tm, HG_V), lambda i, j: (i, 0)),
                  pl.BlockSpec((HG_V, tn), lambda i, j: (0, j)),
                  pl.BlockSpec((tm, tn), lambda i, j: (i, ga_blk + j)),
                  pl.BlockSpec((tm, tn), lambda i, j: (i, gb_blk + j))],
        out_specs=pl.BlockSpec((tm, tn), lambda i, j: (i, j)),
        out_shape=jax.ShapeDtypeStruct((m, D_MODEL), BF16),
        compiler_params=pltpu.CompilerParams(dimension_semantics=("parallel", "arbitrary")),
        name="merge",
    )(branch_a, w_a, branch_b, w_b, proj, proj)


def _layernorm_rows(z, g, b):
    mu = jnp.mean(z, axis=1, keepdims=True)
    zc = z - mu
    var = jnp.mean(zc * zc, axis=1, keepdims=True)
    return zc * lax.rsqrt(var + LN_EPS) * g + b


def _outproj_kernel(mg_ref, w_ref, x_ref, g_ref, b_ref, x1_ref, x1b_ref):
    mix = jnp.dot(mg_ref[...], w_ref[...], preferred_element_type=F32)
    x1 = _layernorm_rows(DEEPNORM_ALPHA * x_ref[...] + mix, g_ref[...], b_ref[...])
    x1_ref[...] = x1
    x1b_ref[...] = x1.astype(BF16)


def _outproj(merged, w_out, x, ln_g, ln_b, tm):
    m = x.shape[0]
    row = lambda i: (i, 0)
    const = lambda i: (0, 0)
    return pl.pallas_call(
        _outproj_kernel,
        grid=(m // tm,),
        in_specs=[pl.BlockSpec((tm, D_MODEL), row),
                  pl.BlockSpec((D_MODEL, D_MODEL), const),
                  pl.BlockSpec((tm, D_MODEL), row),
                  pl.BlockSpec((1, D_MODEL), const),
                  pl.BlockSpec((1, D_MODEL), const)],
        out_specs=[pl.BlockSpec((tm, D_MODEL), row), pl.BlockSpec((tm, D_MODEL), row)],
        out_shape=[jax.ShapeDtypeStruct((m, D_MODEL), F32),
                   jax.ShapeDtypeStruct((m, D_MODEL), BF16)],
        compiler_params=pltpu.CompilerParams(dimension_semantics=("parallel",)),
        name="out_proj_ln",
    )(merged, w_out, x, ln_g, ln_b)


def _mlp_kernel(x1b_ref, wu_ref, wd_ref, x1_ref, g_ref, b_ref, y_ref, acc_ref):
    f = pl.program_id(1)

    @pl.when(f == 0)
    def _init():
        acc_ref[...] = jnp.zeros(acc_ref.shape, F32)

    hid = jnp.maximum(jnp.dot(x1b_ref[...], wu_ref[...], preferred_element_type=F32), 0.0)
    hid = (hid * hid).astype(BF16)
    acc_ref[...] += jnp.dot(hid, wd_ref[...], preferred_element_type=F32)

    @pl.when(f == pl.num_programs(1) - 1)
    def _finish():
        z = DEEPNORM_ALPHA * x1_ref[...] + acc_ref[...]
        y_ref[...] = _layernorm_rows(z, g_ref[...], b_ref[...])


def _mlp(x1b, w_up, w_down, x1, ln_g, ln_b, tm, tf):
    m = x1.shape[0]
    return pl.pallas_call(
        _mlp_kernel,
        grid=(m // tm, D_FF // tf),
        in_specs=[pl.BlockSpec((tm, D_MODEL), lambda i, f: (i, 0)),
                  pl.BlockSpec((D_MODEL, tf), lambda i, f: (0, f)),
                  pl.BlockSpec((tf, D_MODEL), lambda i, f: (f, 0)),
                  pl.BlockSpec((tm, D_MODEL), lambda i, f: (i, 0)),
                  pl.BlockSpec((1, D_MODEL), lambda i, f: (0, 0)),
                  pl.BlockSpec((1, D_MODEL), lambda i, f: (0, 0))],
        out_specs=pl.BlockSpec((tm, D_MODEL), lambda i, f: (i, 0)),
        out_shape=jax.ShapeDtypeStruct((m, D_MODEL), F32),
        scratch_shapes=[pltpu.VMEM((tm, D_MODEL), F32)],
        compiler_params=pltpu.CompilerParams(dimension_semantics=("parallel", "arbitrary")),
        name="mlp_ln",
    )(x1b, w_up, w_down, x1, ln_g, ln_b)


def _layer(x, state, params, *, layer, batch, seq, rec_cfg):
    (lb_logits, w_main, w_f32part, brow, bcol, ml_norm_g, hg_norm_g, w_a, w_b, w_out,
     ln1_g, ln1_b, w_up, w_down, ln2_g, ln2_b) = params
    tokens = batch * seq
    x2 = x.reshape(tokens, D_MODEL)
    xb = x2.astype(BF16)
    tm = min(tokens, 1024)
    proj = _matmul(xb, w_main, BF16, tm, 1024)
    proj_f = _matmul(xb, w_f32part, F32, tm, _PF_WIDTH)

    rows, seq_len = rec_cfg["rows"], rec_cfg["seq_len"]
    gates_col = proj_f[:, HG_K:]
    if state is None:
        c0 = n0 = s0 = None
    else:
        c0, n0, m0, s0 = state
        n0 = n0.reshape(batch, ML_HEADS, 1, ML_DK)
        m_rows = jnp.repeat(m0, seq, axis=0)
        gates_col = lax.dynamic_update_slice(gates_col, m_rows, (0, 2 * ML_HEADS))
    gates_row = gates_col[:, :8].reshape(tokens // rows, rows, 8).transpose(0, 2, 1)

    branch_a, c_new, n_new, m_all = _mlstm(
        proj, gates_col, gates_row, brow, bcol, ml_norm_g, c0, n0,
        batch=batch, seq=seq, rows=rows, seq_len=seq_len)
    branch_b, s_new = _hgrn(
        proj, proj_f, lb_logits, hg_norm_g, s0,
        layer=layer, batch=batch, seq=seq, rows=rows, seq_len=seq_len,
        direct=rec_cfg["direct"], levels=rec_cfg["levels"])

    merged = _merge(branch_a, w_a, branch_b, w_b, proj, tm, 512)
    x1, x1b = _outproj(merged, w_out, x2, ln1_g, ln1_b, 512)
    y = _mlp(x1b, w_up, w_down, x1, ln2_g, ln2_b, 512, 1024)

    m_new = m_all[:, seq - 1::seq, 0].T
    return (y.reshape(batch, seq, D_MODEL), c_new, n_new.reshape(batch, ML_HEADS, ML_DK), m_new, s_new)


_PROMPT_CFG = dict(rows=256, seq_len=256, direct=16, levels=((64, 16), (256, 64)))
_SAMPLE_CFG = dict(rows=64, seq_len=8, direct=8, levels=())


def kernel(x_prompt, x_sample, state_mlstm_C, state_mlstm_n, state_mlstm_m, state_hgrn_S,
           hg_lb_logits, w_in, b_ig, b_fg, ml_norm_g, hg_norm_g, w_branch_a, w_branch_b, w_out,
           ln1_g, ln1_b, w_up, w_down, ln2_g, ln2_b):
    batch_p, seq_p, _ = x_prompt.shape
    batch_s, seq_s, _ = x_sample.shape
    xp, xs = x_prompt, x_sample
    outs_p, outs_s = [], []
    for l in range(DEPTH):
        w = w_in[l]
        w_main = jnp.concatenate(
            [w[:, _OFF_ML_V:_OFF_ML_I], w[:, _OFF_ML_O:_OFF_HG_Q], w[:, _OFF_HG_I:_OFF_HG_G],
             w[:, _OFF_HG_G:_OFF_GATE_A], w[:, _OFF_GATE_A:_OFF_GATE_B], w[:, _OFF_GATE_B:D_IN],
             w[:, _OFF_ML_Q:_OFF_ML_K], w[:, _OFF_ML_K:_OFF_ML_V], w[:, _OFF_HG_Q:_OFF_HG_F]],
            axis=1).astype(BF16)
        w_f32part = jnp.concatenate(
            [w[:, _OFF_HG_F:_OFF_HG_I], w[:, _OFF_ML_I:_OFF_ML_O],
             jnp.zeros((D_MODEL, LANES - 2 * ML_HEADS), w.dtype)], axis=1).astype(BF16)
        gate_bias = jnp.concatenate([b_ig[l], b_fg[l]]).astype(F32)
        brow = jnp.zeros((1, LANES), F32).at[0, :2 * ML_HEADS].set(gate_bias)
        bcol = gate_bias.reshape(2 * ML_HEADS, 1)
        params = (hg_lb_logits.astype(F32), w_main, w_f32part, brow, bcol,
                  ml_norm_g[l].reshape(1, ML_V).astype(F32), hg_norm_g[l].reshape(1, HG_V).astype(F32),
                  w_branch_a[l].astype(BF16), w_branch_b[l].astype(BF16), w_out[l].astype(BF16),
                  ln1_g[l].reshape(1, D_MODEL), ln1_b[l].reshape(1, D_MODEL),
                  w_up[l].astype(BF16), w_down[l].astype(BF16),
                  ln2_g[l].reshape(1, D_MODEL), ln2_b[l].reshape(1, D_MODEL))
        xp, *st_p = _layer(xp, None, params, layer=l, batch=batch_p, seq=seq_p,
                           rec_cfg=_PROMPT_CFG)
        state_s = (state_mlstm_C[l], state_mlstm_n[l], state_mlstm_m[l], state_hgrn_S[l])
        xs, *st_s = _layer(xs, state_s, params, layer=l, batch=batch_s, seq=seq_s,
                           rec_cfg=_SAMPLE_CFG)
        outs_p.append(st_p)
        outs_s.append(st_s)
    stack = lambda outs, k: jnp.stack([o[k] for o in outs])
    return (xp, xs,
            stack(outs_p, 0), stack(outs_p, 1), stack(outs_p, 2), stack(outs_p, 3),
            stack(outs_s, 0), stack(outs_s, 1), stack(outs_s, 2), stack(outs_s, 3))
```

```python
import functools

import jax
import jax.numpy as jnp
from jax import lax
from jax.experimental import pallas as pl
from jax.experimental.pallas import tpu as pltpu

F32 = jnp.float32
BF16 = jnp.bfloat16

D_MODEL = 2048
DEPTH = 1
ML_HEADS, ML_DK, ML_DV = 4, 256, 512
HG_HEADS, HG_DK, HG_DV = 8, 128, 256
ML_QK = ML_HEADS * ML_DK
ML_V = ML_HEADS * ML_DV
HG_K = HG_HEADS * HG_DK
HG_V = HG_HEADS * HG_DV
D_FF = 4 * D_MODEL
LN_EPS = 1e-5
DEEPNORM_ALPHA = (2.0 * DEPTH) ** 0.25
ML_K_SCALE = ML_DK ** -0.5
LANES = 128

_OFF_ML_Q = 0
_OFF_ML_K = _OFF_ML_Q + ML_QK
_OFF_ML_V = _OFF_ML_K + ML_QK
_OFF_ML_I = _OFF_ML_V + ML_V
_OFF_ML_F = _OFF_ML_I + ML_HEADS
_OFF_ML_O = _OFF_ML_F + ML_HEADS
_OFF_HG_Q = _OFF_ML_O + ML_V
_OFF_HG_F = _OFF_HG_Q + HG_K
_OFF_HG_I = _OFF_HG_F + HG_K
_OFF_HG_G = _OFF_HG_I + HG_V
_OFF_GATE_A = _OFF_HG_G + HG_V
_OFF_GATE_B = _OFF_GATE_A + D_MODEL
D_IN = _OFF_GATE_B + D_MODEL

_N_GATES = 2 * ML_HEADS
_P_ML_Q = _OFF_ML_Q
_P_ML_K = _OFF_ML_K
_P_ML_V = _OFF_ML_V
_P_ML_O = _OFF_ML_O - _N_GATES
_P_HG_Q = _OFF_HG_Q - _N_GATES
_P_HG_F = _OFF_HG_F - _N_GATES
_P_HG_I = _OFF_HG_I - _N_GATES
_P_HG_G = _OFF_HG_G - _N_GATES
_P_GATE_A = _OFF_GATE_A - _N_GATES
_P_GATE_B = _OFF_GATE_B - _N_GATES
_P_WIDTH = D_IN - _N_GATES

_PROJ_TM, _PROJ_TN = 1024, 1024
_MERGE_TM, _MERGE_TN = 1024, 512
_OUT_TM = 512
_MLP_TM, _MLP_TF = 512, 1024


def _sigmoid(x):
    return 1.0 / (1.0 + jnp.exp(-x))


def _log_sigmoid(x):
    return jnp.minimum(x, 0.0) - jnp.log1p(jnp.exp(-jnp.abs(x)))


def _split3(x):
    hi = x.astype(BF16)
    r = x - hi.astype(F32)
    mid = r.astype(BF16)
    lo = (r - mid.astype(F32)).astype(BF16)
    return hi, mid, lo


def _dot01_right(t01, x):
    hi, mid, lo = _split3(x)
    d = lambda a: jnp.dot(t01, a, preferred_element_type=F32)
    return (d(hi) + d(mid)) + d(lo)


def _dot01_left(x, t01):
    hi, mid, lo = _split3(x)
    d = lambda a: jnp.dot(a, t01, preferred_element_type=F32)
    return (d(hi) + d(mid)) + d(lo)


def _dot_nt(a, b):
    return lax.dot_general(a, b, (((1,), (1,)), ((), ())), preferred_element_type=F32)


def _dot_tn(a, b):
    return lax.dot_general(a, b, (((0,), (0,)), ((), ())), preferred_element_type=F32)


def _seq_masks(rows, seq_len):
    ri = lax.broadcasted_iota(jnp.int32, (rows, rows), 0)
    ci = lax.broadcasted_iota(jnp.int32, (rows, rows), 1)
    lower = jnp.where(ci <= ri, 1.0, 0.0)
    upper = jnp.where(ri <= ci, 1.0, 0.0)
    if seq_len == rows:
        return jnp.ones((rows, rows), F32), lower, upper
    shift = seq_len.bit_length() - 1
    same = jnp.where((ri >> shift) == (ci >> shift), 1.0, 0.0)
    return same, same * lower, same * upper


def _inproj_kernel(x_ref, wt_ref, wg_ref, o_ref, f_ref, g_ref, gr_ref, wbf_sc, *, n_f32):
    @pl.when(pl.program_id(1) == 0)
    def _cast_weights():
        wbf_sc[...] = wt_ref[...].astype(BF16)

    x = x_ref[...]
    acc = _dot_nt(x, wbf_sc[...])
    o_ref[...] = acc.astype(BF16)

    @pl.when(pl.program_id(0) == n_f32)
    def _f32_outputs():
        f_ref[...] = acc
        wg = wg_ref[...].astype(BF16)
        g_ref[...] = _dot_nt(x, wg)
        gr_ref[0] = _dot_nt(wg, x)[0:8, :]


def _inproj(xb, w_t, w_gates):
    tokens, k = xb.shape
    tm, tn = _PROJ_TM, _PROJ_TN
    n_m = tokens // tm
    n_gate_tile = _OFF_ML_I // tn
    n_f32 = _P_HG_F // tn

    def w_rows(n, m):
        return (pl.multiple_of(n * tn + jnp.where(n >= n_gate_tile, _N_GATES, 0), _N_GATES), 0)

    def parked(n, m):
        return jnp.where(n < n_f32, 0, jnp.where(n == n_f32, m, n_m - 1))

    return pl.pallas_call(
        functools.partial(_inproj_kernel, n_f32=n_f32),
        grid=(_P_WIDTH // tn, n_m),
        in_specs=[pl.BlockSpec((tm, k), lambda n, m: (m, 0)),
                  pl.BlockSpec((pl.Element(tn), pl.Element(k)), w_rows),
                  pl.BlockSpec((LANES, k), lambda n, m: (0, 0))],
        out_specs=[pl.BlockSpec((tm, tn), lambda n, m: (m, n)),
                   pl.BlockSpec((tm, HG_K), lambda n, m: (parked(n, m), 0)),
                   pl.BlockSpec((tm, LANES), lambda n, m: (parked(n, m), 0)),
                   pl.BlockSpec((1, 8, tm), lambda n, m: (parked(n, m), 0, 0))],
        out_shape=[jax.ShapeDtypeStruct((tokens, _P_WIDTH), BF16),
                   jax.ShapeDtypeStruct((tokens, HG_K), F32),
                   jax.ShapeDtypeStruct((tokens, LANES), F32),
                   jax.ShapeDtypeStruct((n_m, 8, tm), F32)],
        scratch_shapes=[pltpu.VMEM((tn, k), BF16)],
        compiler_params=pltpu.CompilerParams(dimension_semantics=("arbitrary", "arbitrary")),
        name="in_proj",
    )(xb, w_t, w_gates)


def _mlstm_kernel(*refs, rows, seq_len, zero_init):
    if zero_init:
        (q_ref, k_ref, v_ref, og_ref, gc_ref, gr_ref, brow_ref, bcol_ref, ng_ref,
         ba_ref, c_ref, n_ref, mrow_ref, m_sc) = refs
        c0_ref, n0_ref = c_ref, n_ref
    else:
        (q_ref, k_ref, v_ref, og_ref, gc_ref, gr_ref, brow_ref, bcol_ref, ng_ref, c0_ref, n0_ref, _,
         ba_ref, c_ref, n_ref, mrow_ref) = refs
    nseq = rows // seq_len
    head = pl.program_id(1)

    if zero_init:
        @pl.when(pl.program_id(2) == 0)
        def _init():
            c_ref[...] = jnp.zeros(c_ref.shape, F32)
            n_ref[...] = jnp.zeros(n_ref.shape, F32)
            m_sc[...] = jnp.zeros(m_sc.shape, F32)

    same, lower, upper = _seq_masks(rows, seq_len)
    causal = lower > 0.5
    lane = lax.broadcasted_iota(jnp.int32, (rows, LANES), 1)
    sub = lax.broadcasted_iota(jnp.int32, (8, rows), 0)

    def sel_lane(x, idx):
        return jnp.sum(jnp.where(lane == idx, x, 0.0), axis=1, keepdims=True)

    def sel_sub(x, idx):
        return jnp.sum(jnp.where(sub == idx, x, 0.0), axis=0, keepdims=True)

    gc_raw = gc_ref[...]
    gc = gc_raw + brow_ref[...]
    lf_cols = _log_sigmoid(gc)
    b_cols = _dot01_right(lower.astype(BF16), lf_cols)
    ig_col = sel_lane(gc, head)
    b_col = sel_lane(b_cols, head + ML_HEADS)
    if zero_init:
        m_prev = jnp.broadcast_to(m_sc[0:1, 0:1], (rows, 1))
    else:
        m_prev = sel_lane(gc_raw, head + 2 * ML_HEADS)

    gr = gr_ref[0] + bcol_ref[...]
    b_rows = _dot01_left(_log_sigmoid(gr), upper.astype(BF16))
    ig_row = sel_sub(gr, head)
    b_row = sel_sub(b_rows, head + ML_HEADS)

    logd = jnp.where(causal, (b_col - b_row) + ig_row, -jnp.inf)
    m_t = jnp.maximum(b_col + m_prev, jnp.max(logd, axis=1, keepdims=True))
    d = jnp.exp(logd - m_t)
    w_inter = jnp.exp(b_col + m_prev - m_t)

    qb = q_ref[...]
    kb = k_ref[...]
    vb = v_ref[...]
    s = _dot_nt(qb, kb) * (d * ML_K_SCALE)
    num = jnp.dot(s.astype(BF16), vb, preferred_element_type=F32)
    den = jnp.sum(s, axis=1, keepdims=True)

    if nseq == 1:
        b_end = b_col[rows - 1:rows, :]
        m_new = m_t[rows - 1:rows, :]
    else:
        b_end = sel_lane(_dot01_right(same.astype(BF16), lf_cols), head + ML_HEADS)
        ci = lax.broadcasted_iota(jnp.int32, (rows, rows), 1)
        last = same * jnp.where((ci & (seq_len - 1)) == seq_len - 1, 1.0, 0.0)
        m_new = _dot01_right(last.astype(BF16), jnp.broadcast_to(m_t, (rows, LANES)))[:, 0:1]
    w_end = jnp.exp(b_end - b_col + ig_col - m_new)
    decay = jnp.exp(b_end + m_prev - m_new)

    qf = qb.astype(F32)
    kw = (w_end * ML_K_SCALE) * kb.astype(F32)

    if nseq == 1:
        c_prev = c0_ref[0, 0]
        n_prev = n0_ref[0, 0]
        q_c = jnp.dot(qb, c_prev.astype(BF16), preferred_element_type=F32)
        q_n = jnp.sum(qf * n_prev, axis=1, keepdims=True)
        dec = decay[0:1, :]
        c_ref[0, 0] = dec * c_prev + _dot_tn(kw.astype(BF16), vb)
        n_ref[0, 0] = dec * n_prev + jnp.sum(kw, axis=0, keepdims=True)
    else:
        shift = seq_len.bit_length() - 1
        row_seq = lax.broadcasted_iota(jnp.int32, (rows, 1), 0) >> shift
        q_c = jnp.zeros((rows, ML_DV), F32)
        q_n = jnp.zeros((rows, 1), F32)
        for j in range(nseq):
            in_seq = row_seq == j
            c_prev = c0_ref[j, 0]
            n_prev = n0_ref[j, 0]
            q_c = jnp.where(in_seq, jnp.dot(qb, c_prev.astype(BF16), preferred_element_type=F32), q_c)
            q_n = jnp.where(in_seq, jnp.sum(qf * n_prev, axis=1, keepdims=True), q_n)
            kw_j = jnp.where(in_seq, kw, 0.0)
            dec = decay[j * seq_len:j * seq_len + 1, :]
            c_ref[j, 0] = dec * c_prev + _dot_tn(kw_j.astype(BF16), vb)
            n_ref[j, 0] = dec * n_prev + jnp.sum(kw_j, axis=0, keepdims=True)

    num = num + w_inter * q_c
    den = den + w_inter * q_n
    hh = num / jnp.maximum(jnp.abs(den), jnp.exp(-m_t))
    mu = jnp.mean(hh, axis=1, keepdims=True)
    xc = hh - mu
    var = jnp.mean(xc * xc, axis=1, keepdims=True)
    hn = xc * lax.rsqrt(var + LN_EPS) * ng_ref[...]
    ba_ref[...] = (hn * _sigmoid(og_ref[...].astype(F32))).astype(BF16)
    mrow_ref[0] = jnp.broadcast_to(m_t, (rows, LANES))
    if zero_init:
        m_sc[...] = jnp.broadcast_to(m_new, m_sc.shape)


def _mlstm(proj, gates_col, gates_row, brow, bcol, norm_g, state, branch, *, total_tokens, row0,
           batch, seq, rows, seq_len):
    tokens = batch * seq
    zero_init = state is None
    nseq = rows // seq_len
    nchunk = seq // seq_len if nseq == 1 else 1
    ngroup = tokens // (rows * nchunk)
    bq, bv = ML_DK, ML_DV
    rb0 = row0 // rows
    per_row_tile = gates_row.shape[2] // rows

    def local(g, c):
        return g * nchunk + c

    in_specs = [
        pl.BlockSpec((rows, bq), lambda g, h, c: (rb0 + local(g, c), _P_ML_Q // bq + h)),
        pl.BlockSpec((rows, bq), lambda g, h, c: (rb0 + local(g, c), _P_ML_K // bq + h)),
        pl.BlockSpec((rows, bv), lambda g, h, c: (rb0 + local(g, c), _P_ML_V // bv + h)),
        pl.BlockSpec((rows, bv), lambda g, h, c: (rb0 + local(g, c), _P_ML_O // bv + h)),
        pl.BlockSpec((rows, LANES), lambda g, h, c: (local(g, c), 0)),
        pl.BlockSpec((1, 8, rows),
                     lambda g, h, c: (local(g, c) // per_row_tile, 0, local(g, c) % per_row_tile)),
        pl.BlockSpec((1, LANES), lambda g, h, c: (0, 0)),
        pl.BlockSpec((8, 1), lambda g, h, c: (0, 0)),
        pl.BlockSpec((1, bv), lambda g, h, c: (0, h)),
    ]
    args = [proj, proj, proj, proj, gates_col, gates_row, brow, bcol, norm_g]
    state_specs = [
        pl.BlockSpec((nseq, 1, ML_DK, ML_DV), lambda g, h, c: (g, h, 0, 0)),
        pl.BlockSpec((nseq, 1, 1, ML_DK), lambda g, h, c: (g, h, 0, 0)),
    ]
    scratch = []
    aliases = {}
    if zero_init:
        scratch = [pltpu.VMEM((1, LANES), F32)]
    else:
        in_specs += state_specs + [pl.BlockSpec(memory_space=pl.ANY)]
        args += [state[0], state[1], branch]
        aliases = {len(args) - 1: 0}
    out_specs = [
        pl.BlockSpec((rows, bv), lambda g, h, c: (rb0 + local(g, c), h)),
        *state_specs,
        pl.BlockSpec((1, rows, LANES), lambda g, h, c: (h, local(g, c), 0)),
    ]
    out_shape = [
        jax.ShapeDtypeStruct((total_tokens, ML_V), BF16),
        jax.ShapeDtypeStruct((batch, ML_HEADS, ML_DK, ML_DV), F32),
        jax.ShapeDtypeStruct((batch, ML_HEADS, 1, ML_DK), F32),
        jax.ShapeDtypeStruct((ML_HEADS, tokens, LANES), F32),
    ]
    return pl.pallas_call(
        functools.partial(_mlstm_kernel, rows=rows, seq_len=seq_len, zero_init=zero_init),
        grid=(ngroup, ML_HEADS, nchunk),
        in_specs=in_specs,
        out_specs=out_specs,
        out_shape=out_shape,
        scratch_shapes=scratch,
        input_output_aliases=aliases,
        compiler_params=pltpu.CompilerParams(
            dimension_semantics=("parallel", "parallel", "arbitrary")),
        name="mlstm",
    )(*args)


def _hgrn_intra(q, kin, g_cum, rows, direct, levels):
    width = min(rows, LANES)
    ngroups = rows // width
    rowi = lax.broadcasted_iota(jnp.int32, (direct, width), 0)
    lanei = lax.broadcasted_iota(jnp.int32, (direct, width), 1)
    panels = []
    for blk in range(rows // direct):
        r0 = blk * direct
        qb = q[r0:r0 + direct]
        gb = g_cum[r0:r0 + direct]
        kb = kin[r0:r0 + direct]
        lane0 = r0 % width
        ag = jnp.zeros((direct, width), F32)
        for s in range(direct):
            e = jnp.exp(jnp.minimum(gb - gb[s:s + 1, :], 0.0))
            col = jnp.sum(qb * e * kb[s:s + 1, :], axis=1, keepdims=True)
            ag = jnp.where(lanei == lane0 + s, jnp.where(rowi >= s, col, 0.0), ag)
        grp = r0 // width
        pieces = [jnp.zeros((direct, width), F32)] * grp + [ag]
        pieces += [jnp.zeros((direct, width), F32)] * (ngroups - grp - 1)
        panels.append(pieces[0] if ngroups == 1 else jnp.concatenate(pieces, axis=1))
    a = panels[0] if len(panels) == 1 else jnp.concatenate(panels, axis=0)

    if levels:
        ri = lax.broadcasted_iota(jnp.int32, (rows, rows), 0)
        ci = lax.broadcasted_iota(jnp.int32, (rows, rows), 1)
        rowid = lax.broadcasted_iota(jnp.int32, (rows, 1), 0)
    for block, sub_size in levels:
        nsub = block // sub_size
        bshift = block.bit_length() - 1
        sshift = sub_size.bit_length() - 1
        row_sub = (rowid & (block - 1)) >> sshift
        q_parts, k_parts = [], []
        for j in range(1, nsub):
            refs = []
            for b0 in range(0, rows, block):
                r = b0 + j * sub_size - 1
                refs.append(jnp.broadcast_to(g_cum[r:r + 1, :], (block, HG_DK)))
            g_ref = refs[0] if len(refs) == 1 else jnp.concatenate(refs, axis=0)
            qt = q * jnp.exp(jnp.minimum(g_cum - g_ref, 0.0))
            q_parts.append(jnp.where(row_sub == j, qt, 0.0).astype(BF16))
            k_parts.append((kin * jnp.exp(jnp.minimum(g_ref - g_cum, 0.0))).astype(BF16))
        qcat = q_parts[0] if len(q_parts) == 1 else jnp.concatenate(q_parts, axis=1)
        kcat = k_parts[0] if len(k_parts) == 1 else jnp.concatenate(k_parts, axis=1)
        a_lvl = _dot_nt(qcat, kcat)
        same_block = (ri >> bshift) == (ci >> bshift)
        earlier_sub = ((ci & (block - 1)) >> sshift) < ((ri & (block - 1)) >> sshift)
        a = a + jnp.where(same_block, jnp.where(earlier_sub, a_lvl, 0.0), 0.0)
    return a


def _hgrn_kernel(*refs, layer, rows, seq_len, direct, levels, zero_init):
    if zero_init:
        q_ref, f_ref, i_ref, g_ref, lbl_ref, ng_ref, bb_ref, s_ref = refs
        s0_ref = s_ref
    else:
        q_ref, f_ref, i_ref, g_ref, lbl_ref, ng_ref, s0_ref, _, bb_ref, s_ref = refs
    nseq = rows // seq_len

    if zero_init:
        @pl.when(pl.program_id(2) == 0)
        def _init():
            s_ref[...] = jnp.zeros(s_ref.shape, F32)

    lg = lbl_ref[...]
    ex = jnp.exp(lg - jnp.max(lg, axis=0, keepdims=True))
    lb = jnp.sum(ex[0:layer + 1, :], axis=0, keepdims=True) / jnp.sum(ex, axis=0, keepdims=True)

    f = lb + (1.0 - lb) * _sigmoid(f_ref[...])
    lf = jnp.log(f)
    kin = 1.0 - f
    same, lower, _ = _seq_masks(rows, seq_len)
    g_cum = _dot01_right(lower.astype(BF16), lf)
    q = q_ref[...].astype(F32)
    if nseq == 1:
        g_end = jnp.broadcast_to(g_cum[rows - 1:rows, :], (rows, HG_DK))
    else:
        g_end = _dot01_right(same.astype(BF16), lf)
    qg = (q * jnp.exp(g_cum)).astype(BF16)
    kg = kin * jnp.exp(g_end - g_cum)
    ib = i_ref[...]

    a = _hgrn_intra(q, kin, g_cum, rows, direct, levels)
    o = jnp.dot(a.astype(BF16), ib, preferred_element_type=F32)

    def decay_cols(row):
        col = jnp.broadcast_to(jnp.exp(row), (HG_DK, HG_DK)).T
        return jnp.concatenate([col] * (HG_DV // HG_DK), axis=1)

    if nseq == 1:
        s_prev = s0_ref[0, 0]
        o = o + jnp.dot(qg, s_prev.astype(BF16), preferred_element_type=F32)
        s_ref[0, 0] = decay_cols(g_end[0:1, :]) * s_prev + _dot_tn(kg.astype(BF16), ib)
    else:
        shift = seq_len.bit_length() - 1
        row_seq = lax.broadcasted_iota(jnp.int32, (rows, 1), 0) >> shift
        for j in range(nseq):
            in_seq = row_seq == j
            s_prev = s0_ref[j, 0]
            o_j = jnp.dot(qg, s_prev.astype(BF16), preferred_element_type=F32)
            o = o + jnp.where(in_seq, o_j, 0.0)
            kg_j = jnp.where(in_seq, kg, 0.0).astype(BF16)
            s_ref[j, 0] = (decay_cols(g_end[j * seq_len:j * seq_len + 1, :]) * s_prev
                           + _dot_tn(kg_j, ib))

    o = o * lax.rsqrt(jnp.mean(o * o, axis=1, keepdims=True) + LN_EPS)
    gate = g_ref[...].astype(F32)
    bb_ref[...] = (o * ng_ref[...] * (gate * _sigmoid(gate))).astype(BF16)


def _hgrn(proj, hg_f, lb_logits, norm_g, s0, branch, *, layer, total_tokens, row0, batch, seq,
          rows, seq_len, direct, levels):
    tokens = batch * seq
    zero_init = s0 is None
    nseq = rows // seq_len
    nchunk = seq // seq_len if nseq == 1 else 1
    ngroup = tokens // (rows * nchunk)
    rb0 = row0 // rows

    def row_blk(g, h, c):
        return rb0 + g * nchunk + c

    in_specs = [
        pl.BlockSpec((rows, HG_DK), lambda g, h, c: (row_blk(g, h, c), _P_HG_Q // HG_DK + h)),
        pl.BlockSpec((rows, HG_DK), lambda g, h, c: (row_blk(g, h, c), h)),
        pl.BlockSpec((rows, HG_DV), lambda g, h, c: (row_blk(g, h, c), _P_HG_I // HG_DV + h)),
        pl.BlockSpec((rows, HG_DV), lambda g, h, c: (row_blk(g, h, c), _P_HG_G // HG_DV + h)),
        pl.BlockSpec((DEPTH + 1, HG_DK), lambda g, h, c: (0, h)),
        pl.BlockSpec((1, HG_DV), lambda g, h, c: (0, h)),
    ]
    args = [proj, hg_f, proj, proj, lb_logits, norm_g]
    state_spec = pl.BlockSpec((nseq, 1, HG_DK, HG_DV), lambda g, h, c: (g, h, 0, 0))
    aliases = {}
    if not zero_init:
        in_specs += [state_spec, pl.BlockSpec(memory_space=pl.ANY)]
        args += [s0, branch]
        aliases = {len(args) - 1: 0}
    return pl.pallas_call(
        functools.partial(_hgrn_kernel, layer=layer, rows=rows, seq_len=seq_len, direct=direct,
                          levels=levels, zero_init=zero_init),
        grid=(ngroup, HG_HEADS, nchunk),
        in_specs=in_specs,
        out_specs=[pl.BlockSpec((rows, HG_DV), lambda g, h, c: (row_blk(g, h, c), h)), state_spec],
        out_shape=[jax.ShapeDtypeStruct((total_tokens, HG_V), BF16),
                   jax.ShapeDtypeStruct((batch, HG_HEADS, HG_DK, HG_DV), F32)],
        input_output_aliases=aliases,
        compiler_params=pltpu.CompilerParams(
            dimension_semantics=("parallel", "parallel", "arbitrary")),
        name="hgrn2",
    )(*args)


def _merge_kernel(ba_ref, wa_ref, bb_ref, wb_ref, ga_ref, gb_ref, o_ref, wa_sc, wb_sc):
    @pl.when(pl.program_id(1) == 0)
    def _cast_weights():
        wa_sc[...] = wa_ref[...].astype(BF16)
        wb_sc[...] = wb_ref[...].astype(BF16)

    ya = jnp.dot(ba_ref[...], wa_sc[...], preferred_element_type=F32)
    yb = jnp.dot(bb_ref[...], wb_sc[...], preferred_element_type=F32)
    ga = _sigmoid(ga_ref[...].astype(F32))
    gb = _sigmoid(gb_ref[...].astype(F32))
    o_ref[...] = (ga * ya + gb * yb).astype(o_ref.dtype)


def _merge(branch_a, w_a, branch_b, w_b, proj):
    m = branch_a.shape[0]
    tm, tn = _MERGE_TM, _MERGE_TN
    ga_blk = _P_GATE_A // tn
    gb_blk = _P_GATE_B // tn
    return pl.pallas_call(
        _merge_kernel,
        grid=(D_MODEL // tn, m // tm),
        in_specs=[pl.BlockSpec((tm, ML_V), lambda j, i: (i, 0)),
                  pl.BlockSpec((ML_V, tn), lambda j, i: (0, j)),
                  pl.BlockSpec((tm, HG_V), lambda j, i: (i, 0)),
                  pl.BlockSpec((HG_V, tn), lambda j, i: (0, j)),
                  pl.BlockSpec((tm, tn), lambda j, i: (i, ga_blk + j)),
                  pl.BlockSpec((tm, tn), lambda j, i: (i, gb_blk + j))],
        out_specs=pl.BlockSpec((tm, tn), lambda j, i: (i, j)),
        out_shape=jax.ShapeDtypeStruct((m, D_MODEL), BF16),
        scratch_shapes=[pltpu.VMEM((ML_V, tn), BF16), pltpu.VMEM((HG_V, tn), BF16)],
        compiler_params=pltpu.CompilerParams(dimension_semantics=("parallel", "arbitrary")),
        name="merge",
    )(branch_a, w_a, branch_b, w_b, proj, proj)


def _layernorm_rows(z, g, b):
    mu = jnp.mean(z, axis=1, keepdims=True)
    zc = z - mu
    var = jnp.mean(zc * zc, axis=1, keepdims=True)
    return zc * lax.rsqrt(var + LN_EPS) * g + b


def _outproj_kernel(mg_ref, w_ref, xp_ref, xs_ref, g_ref, b_ref, x1_ref, x1b_ref, *, prompt_tiles):
    mix = jnp.dot(mg_ref[...], w_ref[...], preferred_element_type=F32)
    x = jnp.where(pl.program_id(0) < prompt_tiles, xp_ref[...], xs_ref[...])
    x1 = _layernorm_rows(DEEPNORM_ALPHA * x + mix, g_ref[...], b_ref[...])
    x1_ref[...] = x1
    x1b_ref[...] = x1.astype(BF16)


def _outproj(merged, w_out, x_prompt, x_sample, ln_g, ln_b):
    tm = _OUT_TM
    m = merged.shape[0]
    p_tiles = x_prompt.shape[0] // tm
    row = lambda i: (i, 0)
    const = lambda i: (0, 0)
    return pl.pallas_call(
        functools.partial(_outproj_kernel, prompt_tiles=p_tiles),
        grid=(m // tm,),
        in_specs=[pl.BlockSpec((tm, D_MODEL), row),
                  pl.BlockSpec((D_MODEL, D_MODEL), const),
                  pl.BlockSpec((tm, D_MODEL), lambda i: (jnp.minimum(i, p_tiles - 1), 0)),
                  pl.BlockSpec((tm, D_MODEL), lambda i: (jnp.maximum(i - p_tiles, 0), 0)),
                  pl.BlockSpec((1, D_MODEL), const),
                  pl.BlockSpec((1, D_MODEL), const)],
        out_specs=[pl.BlockSpec((tm, D_MODEL), row), pl.BlockSpec((tm, D_MODEL), row)],
        out_shape=[jax.ShapeDtypeStruct((m, D_MODEL), F32),
                   jax.ShapeDtypeStruct((m, D_MODEL), BF16)],
        compiler_params=pltpu.CompilerParams(dimension_semantics=("arbitrary",)),
        name="out_proj_ln",
    )(merged, w_out, x_prompt, x_sample, ln_g, ln_b)


def _mlp_kernel(x1b_ref, wu_ref, wd_ref, x1_ref, g_ref, b_ref, yp_ref, ys_ref, acc_ref, *,
                prompt_tiles):
    i = pl.program_id(0)
    f = pl.program_id(1)
    last = pl.num_programs(1) - 1

    @pl.when(f == 0)
    def _init():
        acc_ref[...] = jnp.zeros(acc_ref.shape, F32)

    hid = jnp.maximum(jnp.dot(x1b_ref[...], wu_ref[...], preferred_element_type=F32), 0.0)
    hid = (hid * hid).astype(BF16)
    acc_ref[...] += jnp.dot(hid, wd_ref[...], preferred_element_type=F32)

    def finish(y_ref):
        z = DEEPNORM_ALPHA * x1_ref[...] + acc_ref[...]
        y_ref[...] = _layernorm_rows(z, g_ref[...], b_ref[...])

    @pl.when((f == last) & (i < prompt_tiles))
    def _finish_prompt():
        finish(yp_ref)

    @pl.when((f == last) & (i >= prompt_tiles))
    def _finish_sample():
        finish(ys_ref)


def _mlp(x1b, w_up, w_down, x1, ln_g, ln_b, prompt_tokens):
    tm, tf = _MLP_TM, _MLP_TF
    m = x1.shape[0]
    p_tiles = prompt_tokens // tm
    return pl.pallas_call(
        functools.partial(_mlp_kernel, prompt_tiles=p_tiles),
        grid=(m // tm, D_FF // tf),
        in_specs=[pl.BlockSpec((tm, D_MODEL), lambda i, f: (i, 0)),
                  pl.BlockSpec((D_MODEL, tf), lambda i, f: (0, f)),
                  pl.BlockSpec((tf, D_MODEL), lambda i, f: (f, 0)),
                  pl.BlockSpec((tm, D_MODEL), lambda i, f: (i, 0)),
                  pl.BlockSpec((1, D_MODEL), lambda i, f: (0, 0)),
                  pl.BlockSpec((1, D_MODEL), lambda i, f: (0, 0))],
        out_specs=[pl.BlockSpec((tm, D_MODEL), lambda i, f: (jnp.minimum(i, p_tiles - 1), 0)),
                   pl.BlockSpec((tm, D_MODEL), lambda i, f: (jnp.maximum(i - p_tiles, 0), 0))],
        out_shape=[jax.ShapeDtypeStruct((prompt_tokens, D_MODEL), F32),
                   jax.ShapeDtypeStruct((m - prompt_tokens, D_MODEL), F32)],
        scratch_shapes=[pltpu.VMEM((tm, D_MODEL), F32)],
        compiler_params=pltpu.CompilerParams(dimension_semantics=("arbitrary", "arbitrary")),
        name="mlp_ln",
    )(x1b, w_up, w_down, x1, ln_g, ln_b)


_PROMPT_CFG = dict(rows=256, seq_len=256)
_PROMPT_HGRN = dict(direct=16, levels=((64, 16), (256, 64)))
_SAMPLE_CFG = dict(rows=64, seq_len=8)
_SAMPLE_HGRN = dict(direct=8, levels=())


def kernel(x_prompt, x_sample, state_mlstm_C, state_mlstm_n, state_mlstm_m, state_hgrn_S,
           hg_lb_logits, w_in, b_ig, b_fg, ml_norm_g, hg_norm_g, w_branch_a, w_branch_b, w_out,
           ln1_g, ln1_b, w_up, w_down, ln2_g, ln2_b):
    batch_p, seq_p, _ = x_prompt.shape
    batch_s, seq_s, _ = x_sample.shape
    tok_p, tok_s = batch_p * seq_p, batch_s * seq_s
    total = tok_p + tok_s
    group_p = dict(total_tokens=total, row0=0, batch=batch_p, seq=seq_p, **_PROMPT_CFG)
    group_s = dict(total_tokens=total, row0=tok_p, batch=batch_s, seq=seq_s, **_SAMPLE_CFG)
    xp = x_prompt.reshape(tok_p, D_MODEL)
    xs = x_sample.reshape(tok_s, D_MODEL)
    lb_logits = hg_lb_logits.astype(F32)
    states_p, states_s = [], []
    for l in range(DEPTH):
        xb = jnp.concatenate([xp.astype(BF16), xs.astype(BF16)], axis=0)
        w_t = jnp.swapaxes(w_in[l], 0, 1)
        w_gates = jnp.zeros((LANES, D_MODEL), F32).at[:_N_GATES].set(w_t[_OFF_ML_I:_OFF_ML_O])
        proj, hg_f, gates, gates_gm = _inproj(xb, w_t, w_gates)

        gate_bias = jnp.concatenate([b_ig[l], b_fg[l]]).astype(F32)
        brow = jnp.zeros((1, LANES), F32).at[0, :_N_GATES].set(gate_bias)
        bcol = gate_bias.reshape(_N_GATES, 1)
        ml_g = ml_norm_g[l].reshape(1, ML_V).astype(F32)
        hg_g = hg_norm_g[l].reshape(1, HG_V).astype(F32)

        m_rows = jnp.repeat(state_mlstm_m[l].astype(F32), seq_s, axis=0)
        gates_s = gates[tok_p:].at[:, _N_GATES:_N_GATES + ML_HEADS].set(m_rows)
        rows_s = _SAMPLE_CFG["rows"]
        gm_s = gates_gm[tok_p // _PROJ_TM:]
        gm_s = gm_s.reshape(-1, 8, _PROJ_TM // rows_s, rows_s).transpose(0, 2, 1, 3)
        gm_s = gm_s.reshape(tok_s // rows_s, 8, rows_s)
        state_ml = (state_mlstm_C[l].astype(F32),
                    state_mlstm_n[l].astype(F32).reshape(batch_s, ML_HEADS, 1, ML_DK))

        branch_a, c_p, n_p, m_all_p = _mlstm(proj, gates, gates_gm, brow, bcol, ml_g, None, None,
                                             **group_p)
        branch_a, c_s, n_s, m_all_s = _mlstm(proj, gates_s, gm_s, brow, bcol, ml_g, state_ml,
                                             branch_a, **group_s)
        branch_b, s_p = _hgrn(proj, hg_f, lb_logits, hg_g, None, None, layer=l,
                              **group_p, **_PROMPT_HGRN)
        branch_b, s_s = _hgrn(proj, hg_f, lb_logits, hg_g, state_hgrn_S[l].astype(F32), branch_b,
                              layer=l, **group_s, **_SAMPLE_HGRN)

        merged = _merge(branch_a, w_branch_a[l], branch_b, w_branch_b[l], proj)
        x1, x1b = _outproj(merged, w_out[l].astype(BF16), xp, xs,
                           ln1_g[l].reshape(1, D_MODEL), ln1_b[l].reshape(1, D_MODEL))
        xp, xs = _mlp(x1b, w_up[l].astype(BF16), w_down[l].astype(BF16), x1,
                      ln2_g[l].reshape(1, D_MODEL), ln2_b[l].reshape(1, D_MODEL), tok_p)

        states_p.append((c_p, n_p.reshape(batch_p, ML_HEADS, ML_DK),
                         m_all_p[:, seq_p - 1::seq_p, 0].T, s_p))
        states_s.append((c_s, n_s.reshape(batch_s, ML_HEADS, ML_DK),
                         m_all_s[:, seq_s - 1::seq_s, 0].T, s_s))
    stack = lambda states, k: jnp.stack([s[k] for s in states])
    return (xp.reshape(batch_p, seq_p, D_MODEL), xs.reshape(batch_s, seq_s, D_MODEL),
            stack(states_p, 0), stack(states_p, 1), stack(states_p, 2), stack(states_p, 3),
            stack(states_s, 0), stack(states_s, 1), stack(states_s, 2), stack(states_s, 3))
```

```python
import functools
import math

import numpy as np
import jax
import jax.numpy as jnp
from jax import lax
from jax.experimental import pallas as pl
from jax.experimental.pallas import tpu as pltpu

F32 = jnp.float32
BF16 = jnp.bfloat16

D_MODEL = 2048
DEPTH = 1
ML_HEADS, ML_DK, ML_DV = 4, 256, 512
HG_HEADS, HG_DK, HG_DV = 8, 128, 256
ML_QK = ML_HEADS * ML_DK
ML_V = ML_HEADS * ML_DV
HG_K = HG_HEADS * HG_DK
HG_V = HG_HEADS * HG_DV
D_FF = 4 * D_MODEL
LN_EPS = 1e-5
DEEPNORM_ALPHA = (2.0 * DEPTH) ** 0.25
ML_K_SCALE = ML_DK ** -0.5
LOG2_E = math.log2(math.e)
LANES = 128

_OFF_ML_Q = 0
_OFF_ML_K = _OFF_ML_Q + ML_QK
_OFF_ML_V = _OFF_ML_K + ML_QK
_OFF_ML_I = _OFF_ML_V + ML_V
_OFF_ML_F = _OFF_ML_I + ML_HEADS
_OFF_ML_O = _OFF_ML_F + ML_HEADS
_OFF_HG_Q = _OFF_ML_O + ML_V
_OFF_HG_F = _OFF_HG_Q + HG_K
_OFF_HG_I = _OFF_HG_F + HG_K
_OFF_HG_G = _OFF_HG_I + HG_V
_OFF_GATE_A = _OFF_HG_G + HG_V
_OFF_GATE_B = _OFF_GATE_A + D_MODEL
D_IN = _OFF_GATE_B + D_MODEL

_N_GATES = 2 * ML_HEADS
_P_ML_Q = _OFF_ML_Q
_P_ML_K = _OFF_ML_K
_P_ML_V = _OFF_ML_V
_P_ML_O = _OFF_ML_O - _N_GATES
_P_HG_Q = _OFF_HG_Q - _N_GATES
_P_HG_F = _OFF_HG_F - _N_GATES
_P_HG_I = _OFF_HG_I - _N_GATES
_P_HG_G = _OFF_HG_G - _N_GATES
_P_GATE_A = _OFF_GATE_A - _N_GATES
_P_GATE_B = _OFF_GATE_B - _N_GATES
_P_WIDTH = D_IN - _N_GATES

_PROJ_TM, _PROJ_TN = 1024, 1024
_MERGE_TM, _MERGE_TN = 1024, 512
_OUT_TM = 512
_MLP_TM, _MLP_TF = 1024, 512


def _sigmoid(x):
    return 1.0 / (1.0 + jnp.exp(-x))


def _log_sigmoid(x):
    return jnp.minimum(x, 0.0) - jnp.log1p(jnp.exp(-jnp.abs(x)))


def _split2(x):
    hi = x.astype(BF16)
    lo = (x - hi.astype(F32)).astype(BF16)
    return hi, lo


def _dot01_right(t01, x):
    hi, lo = _split2(x)
    return (jnp.dot(t01, hi, preferred_element_type=F32)
            + jnp.dot(t01, lo, preferred_element_type=F32))


def _dot01_left(x, t01):
    hi, lo = _split2(x)
    return (jnp.dot(hi, t01, preferred_element_type=F32)
            + jnp.dot(lo, t01, preferred_element_type=F32))


def _dot_nt(a, b):
    return lax.dot_general(a, b, (((1,), (1,)), ((), ())), preferred_element_type=F32)


def _dot_tn(a, b):
    return lax.dot_general(a, b, (((0,), (0,)), ((), ())), preferred_element_type=F32)


def _seq_masks(rows, seq_len):
    ri = lax.broadcasted_iota(jnp.int32, (rows, rows), 0)
    ci = lax.broadcasted_iota(jnp.int32, (rows, rows), 1)
    lower = jnp.where(ci <= ri, 1.0, 0.0)
    upper = jnp.where(ri <= ci, 1.0, 0.0)
    if seq_len == rows:
        return jnp.ones((rows, rows), F32), lower, upper
    shift = seq_len.bit_length() - 1
    same = jnp.where((ri >> shift) == (ci >> shift), 1.0, 0.0)
    return same, same * lower, same * upper


def _inproj_kernel(x_ref, wt_hbm, o_ref, f_ref, g_ref, gr_ref, wbuf, wbf_sc, wg_sc, sem, gsem, *,
                   n_f32, n_gate_tile):
    n = pl.program_id(0)
    m = pl.program_id(1)
    tn = wbf_sc.shape[0]

    def tile_copy(tile, slot):
        start = pl.multiple_of(tile * tn + jnp.where(tile >= n_gate_tile, _N_GATES, 0), _N_GATES)
        return pltpu.make_async_copy(wt_hbm.at[pl.ds(start, tn), :], wbuf.at[slot], sem.at[slot])

    def gate_copy():
        return pltpu.make_async_copy(wt_hbm.at[pl.ds(_OFF_ML_I, _N_GATES), :],
                                     wg_sc.at[pl.ds(0, _N_GATES), :], gsem.at[0])

    @pl.when(m == 0)
    def _next_weight_tile():
        slot = n % 2

        @pl.when(n == 0)
        def _first():
            tile_copy(0, 0).start()
            wg_sc[_N_GATES:, :] = jnp.zeros((LANES - _N_GATES, wg_sc.shape[1]), F32)
            gate_copy().start()
            gate_copy().wait()

        tile_copy(n, slot).wait()

        @pl.when(n + 1 < pl.num_programs(0))
        def _prefetch():
            tile_copy(n + 1, 1 - slot).start()

        wbf_sc[...] = wbuf[slot].astype(BF16)

    x = x_ref[...]
    acc = _dot_nt(x, wbf_sc[...])
    o_ref[...] = acc.astype(BF16)

    @pl.when(n == n_f32)
    def _f32_outputs():
        f_ref[...] = acc
        wg = wg_sc[...].astype(BF16)
        g_ref[...] = _dot_nt(x, wg)
        gr_ref[0] = _dot_nt(wg, x)[0:_N_GATES, :]


def _inproj(xb, w_t):
    tokens, k = xb.shape
    tm, tn = _PROJ_TM, _PROJ_TN
    n_m = tokens // tm
    n_f32 = _P_HG_F // tn

    def parked(n, m):
        return jnp.where(n < n_f32, 0, jnp.where(n == n_f32, m, n_m - 1))

    return pl.pallas_call(
        functools.partial(_inproj_kernel, n_f32=n_f32, n_gate_tile=_OFF_ML_I // tn),
        grid=(_P_WIDTH // tn, n_m),
        in_specs=[pl.BlockSpec((tm, k), lambda n, m: (m, 0)),
                  pl.BlockSpec(memory_space=pl.ANY)],
        out_specs=[pl.BlockSpec((tm, tn), lambda n, m: (m, n)),
                   pl.BlockSpec((tm, HG_K), lambda n, m: (parked(n, m), 0)),
                   pl.BlockSpec((tm, LANES), lambda n, m: (parked(n, m), 0)),
                   pl.BlockSpec((1, 8, tm), lambda n, m: (parked(n, m), 0, 0))],
        out_shape=[jax.ShapeDtypeStruct((tokens, _P_WIDTH), BF16),
                   jax.ShapeDtypeStruct((tokens, HG_K), F32),
                   jax.ShapeDtypeStruct((tokens, LANES), F32),
                   jax.ShapeDtypeStruct((n_m, 8, tm), F32)],
        scratch_shapes=[pltpu.VMEM((2, tn, k), F32),
                        pltpu.VMEM((tn, k), BF16),
                        pltpu.VMEM((LANES, k), F32),
                        pltpu.SemaphoreType.DMA((2,)),
                        pltpu.SemaphoreType.DMA((1,))],
        compiler_params=pltpu.CompilerParams(dimension_semantics=("arbitrary", "arbitrary")),
        name="in_proj",
    )(xb, w_t)


def _mlstm_kernel(*refs, rows, seq_len, zero_init):
    if zero_init:
        (q_ref, k_ref, v_ref, og_ref, gc_ref, gr_ref, brow_ref, bcol_ref, ng_ref,
         ba_ref, c_ref, n_ref, mrow_ref, m_sc) = refs
        c0_ref, n0_ref = c_ref, n_ref
    else:
        (q_ref, k_ref, v_ref, og_ref, gc_ref, gr_ref, brow_ref, bcol_ref, ng_ref, c0_ref, n0_ref,
         ba_ref, c_ref, n_ref, mrow_ref) = refs
    nseq = rows // seq_len
    head = pl.program_id(1)

    if zero_init:
        @pl.when(pl.program_id(2) == 0)
        def _init():
            c_ref[...] = jnp.zeros(c_ref.shape, F32)
            n_ref[...] = jnp.zeros(n_ref.shape, F32)
            m_sc[...] = jnp.zeros(m_sc.shape, F32)

    same, lower, upper = _seq_masks(rows, seq_len)
    causal = lower > 0.5
    lane = lax.broadcasted_iota(jnp.int32, (rows, LANES), 1)
    sub = lax.broadcasted_iota(jnp.int32, (8, rows), 0)

    def sel_lane(x, idx):
        return jnp.sum(jnp.where(lane == idx, x, 0.0), axis=1, keepdims=True)

    def sel_sub(x, idx):
        return jnp.sum(jnp.where(sub == idx, x, 0.0), axis=0, keepdims=True)

    gc_raw = gc_ref[...]
    gc = gc_raw + brow_ref[...]
    lf_cols = _log_sigmoid(gc)
    b_cols = _dot01_right(lower.astype(BF16), lf_cols)
    ig_col = sel_lane(gc, head)
    b_col = sel_lane(b_cols, head + ML_HEADS)
    if zero_init:
        m_prev = jnp.broadcast_to(m_sc[0:1, 0:1], (rows, 1))
    else:
        m_prev = sel_lane(gc_raw, head + 2 * ML_HEADS)

    gr = gr_ref[0] + bcol_ref[...]
    b_rows = _dot01_left(_log_sigmoid(gr), upper.astype(BF16))
    ig_row = sel_sub(gr, head)
    b_row = sel_sub(b_rows, head + ML_HEADS)

    logd = jnp.where(causal, (b_col - b_row) + ig_row, -jnp.inf)
    m_t = jnp.maximum(b_col + m_prev, jnp.max(logd, axis=1, keepdims=True))
    d = jnp.exp(logd - m_t)
    w_inter = jnp.exp(b_col + m_prev - m_t)

    qb = q_ref[...]
    kb = k_ref[...]
    vb = v_ref[...]
    s = _dot_nt(qb, kb) * (d * ML_K_SCALE)
    num = jnp.dot(s.astype(BF16), vb, preferred_element_type=F32)
    den = jnp.sum(s, axis=1, keepdims=True)

    if nseq == 1:
        b_end = b_col[rows - 1:rows, :]
        m_new = m_t[rows - 1:rows, :]
    else:
        b_end = sel_lane(_dot01_right(same.astype(BF16), lf_cols), head + ML_HEADS)
        ci = lax.broadcasted_iota(jnp.int32, (rows, rows), 1)
        last = same * jnp.where((ci & (seq_len - 1)) == seq_len - 1, 1.0, 0.0)
        m_new = _dot01_right(last.astype(BF16), jnp.broadcast_to(m_t, (rows, LANES)))[:, 0:1]
    w_end = jnp.exp(b_end - b_col + ig_col - m_new)
    decay = jnp.exp(b_end + m_prev - m_new)

    qf = qb.astype(F32)
    kw = (w_end * ML_K_SCALE) * kb.astype(F32)

    if nseq == 1:
        c_prev = c0_ref[0, 0]
        n_prev = n0_ref[0, 0]
        q_c = jnp.dot(qb, c_prev.astype(BF16), preferred_element_type=F32)
        q_n = jnp.sum(qf * n_prev, axis=1, keepdims=True)
        dec = decay[0:1, :]
        c_ref[0, 0] = dec * c_prev + _dot_tn(kw.astype(BF16), vb)
        n_ref[0, 0] = dec * n_prev + jnp.sum(kw, axis=0, keepdims=True)
    else:
        shift = seq_len.bit_length() - 1
        row_seq = lax.broadcasted_iota(jnp.int32, (rows, 1), 0) >> shift
        q_c = jnp.zeros((rows, ML_DV), F32)
        q_n = jnp.zeros((rows, 1), F32)
        for j in range(nseq):
            in_seq = row_seq == j
            c_prev = c0_ref[j, 0]
            n_prev = n0_ref[j, 0]
            q_c = jnp.where(in_seq, jnp.dot(qb, c_prev.astype(BF16), preferred_element_type=F32), q_c)
            q_n = jnp.where(in_seq, jnp.sum(qf * n_prev, axis=1, keepdims=True), q_n)
            kw_j = jnp.where(in_seq, kw, 0.0)
            dec = decay[j * seq_len:j * seq_len + 1, :]
            c_ref[j, 0] = dec * c_prev + _dot_tn(kw_j.astype(BF16), vb)
            n_ref[j, 0] = dec * n_prev + jnp.sum(kw_j, axis=0, keepdims=True)

    num = num + w_inter * q_c
    den = den + w_inter * q_n
    hh = num / jnp.maximum(jnp.abs(den), jnp.exp(-m_t))
    mu = jnp.mean(hh, axis=1, keepdims=True)
    xc = hh - mu
    var = jnp.mean(xc * xc, axis=1, keepdims=True)
    hn = xc * lax.rsqrt(var + LN_EPS) * ng_ref[...]
    ba_ref[...] = (hn * _sigmoid(og_ref[...].astype(F32))).astype(BF16)
    mrow_ref[0] = jnp.broadcast_to(m_t, (rows, LANES))
    if zero_init:
        m_sc[...] = jnp.broadcast_to(m_new, m_sc.shape)


def _mlstm(proj, gates_col, gates_row, brow, bcol, norm_g, state, *, row0, batch, seq, rows, seq_len):
    tokens = batch * seq
    zero_init = state is None
    nseq = rows // seq_len
    nchunk = seq // seq_len if nseq == 1 else 1
    ngroup = tokens // (rows * nchunk)
    bq, bv = ML_DK, ML_DV
    rb0 = row0 // rows
    per_row_tile = gates_row.shape[2] // rows

    def local(g, c):
        return g * nchunk + c

    in_specs = [
        pl.BlockSpec((rows, bq), lambda g, h, c: (rb0 + local(g, c), _P_ML_Q // bq + h)),
        pl.BlockSpec((rows, bq), lambda g, h, c: (rb0 + local(g, c), _P_ML_K // bq + h)),
        pl.BlockSpec((rows, bv), lambda g, h, c: (rb0 + local(g, c), _P_ML_V // bv + h)),
        pl.BlockSpec((rows, bv), lambda g, h, c: (rb0 + local(g, c), _P_ML_O // bv + h)),
        pl.BlockSpec((rows, LANES), lambda g, h, c: (local(g, c), 0)),
        pl.BlockSpec((1, 8, rows),
                     lambda g, h, c: (local(g, c) // per_row_tile, 0, local(g, c) % per_row_tile)),
        pl.BlockSpec((1, LANES), lambda g, h, c: (0, 0)),
        pl.BlockSpec((8, 1), lambda g, h, c: (0, 0)),
        pl.BlockSpec((1, bv), lambda g, h, c: (0, h)),
    ]
    args = [proj, proj, proj, proj, gates_col, gates_row, brow, bcol, norm_g]
    state_specs = [
        pl.BlockSpec((nseq, 1, ML_DK, ML_DV), lambda g, h, c: (g, h, 0, 0)),
        pl.BlockSpec((nseq, 1, 1, ML_DK), lambda g, h, c: (g, h, 0, 0)),
    ]
    scratch = []
    if zero_init:
        scratch = [pltpu.VMEM((1, LANES), F32)]
    else:
        in_specs += state_specs
        args += [state[0], state[1]]
    out_specs = [
        pl.BlockSpec((rows, bv), lambda g, h, c: (local(g, c), h)),
        *state_specs,
        pl.BlockSpec((1, rows, LANES), lambda g, h, c: (h, local(g, c), 0)),
    ]
    out_shape = [
        jax.ShapeDtypeStruct((tokens, ML_V), BF16),
        jax.ShapeDtypeStruct((batch, ML_HEADS, ML_DK, ML_DV), F32),
        jax.ShapeDtypeStruct((batch, ML_HEADS, 1, ML_DK), F32),
        jax.ShapeDtypeStruct((ML_HEADS, tokens, LANES), F32),
    ]
    return pl.pallas_call(
        functools.partial(_mlstm_kernel, rows=rows, seq_len=seq_len, zero_init=zero_init),
        grid=(ngroup, ML_HEADS, nchunk),
        in_specs=in_specs,
        out_specs=out_specs,
        out_shape=out_shape,
        scratch_shapes=scratch,
        compiler_params=pltpu.CompilerParams(
            dimension_semantics=("parallel", "parallel", "arbitrary")),
        name="mlstm",
    )(*args)


def _hgrn_level_ids(rows, seq_len, direct, levels):
    t = np.arange(rows)[:, None]
    s = np.arange(rows)[None, :]
    ids = np.full((rows, rows), -1, np.int32)
    count = ((t // direct == s // direct) & (s <= t)).astype(np.int32)
    for idx, (block, sub_size) in enumerate(levels):
        owned = (t // block == s // block) & ((s % block) // sub_size < (t % block) // sub_size)
        ids[owned] = idx
        count += owned
    wanted = (t // seq_len == s // seq_len) & (s <= t)
    assert np.array_equal(count, wanted.astype(np.int32)), (rows, seq_len, direct, levels)
    return ids


def _hgrn_intra(q, kin, g2, gk2, level_ids, rows, direct, levels):
    ngroups = rows // LANES
    rowi = lax.broadcasted_iota(jnp.int32, (direct, LANES), 0)
    lanei = lax.broadcasted_iota(jnp.int32, (direct, LANES), 1)
    keep = [jnp.where(lanei == s, rowi, -1) >= s for s in range(direct)]
    zero_group = jnp.zeros((direct, LANES), F32)
    prods = []
    for blk in range(rows // direct):
        r0 = blk * direct
        qb = q[r0:r0 + direct]
        gb = g2[r0:r0 + direct]
        gkb = gk2[r0:r0 + direct]
        for s in range(direct):
            prods.append(qb * jnp.exp2(gb - gkb[s:s + 1, :]))
    sums = jnp.dot(jnp.concatenate(prods, axis=0).astype(BF16), jnp.ones((HG_DK, LANES), BF16),
                   preferred_element_type=F32)
    panels = []
    for blk in range(rows // direct):
        r0 = blk * direct
        ag = zero_group
        for s in range(direct):
            p0 = (blk * direct + s) * direct
            ag = jnp.where(keep[s], sums[p0:p0 + direct], ag)
        lane0 = r0 % LANES
        if lane0:
            ag = pltpu.roll(ag, lane0, axis=1)
        grp = r0 // LANES
        pieces = [zero_group] * grp + [ag] + [zero_group] * (ngroups - grp - 1)
        panels.append(pieces[0] if ngroups == 1 else jnp.concatenate(pieces, axis=1))
    a = panels[0] if len(panels) == 1 else jnp.concatenate(panels, axis=0)

    if levels:
        rowid = lax.broadcasted_iota(jnp.int32, (rows, 1), 0)
    for idx, (block, sub_size) in enumerate(levels):
        sshift = sub_size.bit_length() - 1
        row_sub = (rowid & (block - 1)) >> sshift
        q_parts, k_parts = [], []
        for j in range(1, block // sub_size):
            refs = []
            for b0 in range(0, rows, block):
                r = b0 + j * sub_size - 1
                refs.append(jnp.broadcast_to(g2[r:r + 1, :], (block, HG_DK)))
            g_ref = refs[0] if len(refs) == 1 else jnp.concatenate(refs, axis=0)
            e = jnp.exp2(-jnp.abs(g2 - g_ref))
            q_parts.append(jnp.where(row_sub == j, q * e, 0.0).astype(BF16))
            k_parts.append((kin * e).astype(BF16))
        qcat = q_parts[0] if len(q_parts) == 1 else jnp.concatenate(q_parts, axis=1)
        kcat = k_parts[0] if len(k_parts) == 1 else jnp.concatenate(k_parts, axis=1)
        a = jnp.where(level_ids == idx, _dot_nt(qcat, kcat), a)
    return a


def _hgrn_kernel(*refs, layer, rows, seq_len, heads, direct, levels, zero_init):
    if zero_init:
        q_ref, f_ref, i_ref, g_ref, lbl_ref, ng_ref, lvl_ref, bb_ref, s_ref = refs
        s0_ref = s_ref
    else:
        q_ref, f_ref, i_ref, g_ref, lbl_ref, ng_ref, lvl_ref, s0_ref, bb_ref, s_ref = refs
    nseq = rows // seq_len

    if zero_init:
        @pl.when(pl.program_id(2) == 0)
        def _init():
            s_ref[...] = jnp.zeros(s_ref.shape, F32)

    same, lower, _ = _seq_masks(rows, seq_len)
    lower_b = lower.astype(BF16)
    same_b = same.astype(BF16)
    level_ids = lvl_ref[...]
    if nseq > 1:
        shift = seq_len.bit_length() - 1
        row_seq = lax.broadcasted_iota(jnp.int32, (rows, 1), 0) >> shift

    def decay_cols(row):
        col = jnp.broadcast_to(jnp.exp2(row), (HG_DK, HG_DK)).T
        return jnp.concatenate([col] * (HG_DV // HG_DK), axis=1)

    for hh in range(heads):
        ks = slice(hh * HG_DK, (hh + 1) * HG_DK)
        vs = slice(hh * HG_DV, (hh + 1) * HG_DV)
        lg = lbl_ref[:, ks]
        ex = jnp.exp(lg - jnp.max(lg, axis=0, keepdims=True))
        lb = (jnp.sum(ex[0:layer + 1, :], axis=0, keepdims=True)
              / jnp.sum(ex, axis=0, keepdims=True))

        f = lb + (1.0 - lb) * _sigmoid(f_ref[:, ks])
        kin = 1.0 - f
        lf2 = jnp.log(f) * LOG2_E
        g2 = _dot01_right(lower_b, lf2)
        gk2 = g2 - jnp.log(kin) * LOG2_E
        q = q_ref[:, ks].astype(F32)
        if nseq == 1:
            g_end = jnp.broadcast_to(g2[rows - 1:rows, :], (rows, HG_DK))
        else:
            g_end = _dot01_right(same_b, lf2)
        qg = (q * jnp.exp2(g2)).astype(BF16)
        kg = kin * jnp.exp2(g_end - g2)
        ib = i_ref[:, vs]

        a = _hgrn_intra(q, kin, g2, gk2, level_ids, rows, direct, levels)
        o = jnp.dot(a.astype(BF16), ib, preferred_element_type=F32)

        if nseq == 1:
            s_prev = s0_ref[0, hh]
            o = o + jnp.dot(qg, s_prev.astype(BF16), preferred_element_type=F32)
            s_ref[0, hh] = decay_cols(g_end[0:1, :]) * s_prev + _dot_tn(kg.astype(BF16), ib)
        else:
            for j in range(nseq):
                in_seq = row_seq == j
                s_prev = s0_ref[j, hh]
                o_j = jnp.dot(qg, s_prev.astype(BF16), preferred_element_type=F32)
                o = o + jnp.where(in_seq, o_j, 0.0)
                kg_j = jnp.where(in_seq, kg, 0.0).astype(BF16)
                s_ref[j, hh] = (decay_cols(g_end[j * seq_len:j * seq_len + 1, :]) * s_prev
                                + _dot_tn(kg_j, ib))

        o = o * lax.rsqrt(jnp.mean(o * o, axis=1, keepdims=True) + LN_EPS)
        gate = g_ref[:, vs].astype(F32)
        bb_ref[:, vs] = (o * ng_ref[:, vs] * (gate * _sigmoid(gate))).astype(BF16)


def _hgrn(proj, hg_f, lb_logits, norm_g, s0, *, layer, row0, batch, seq, rows, seq_len, heads,
          direct, levels):
    tokens = batch * seq
    zero_init = s0 is None
    nseq = rows // seq_len
    nchunk = seq // seq_len if nseq == 1 else 1
    ngroup = tokens // (rows * nchunk)
    rb0 = row0 // rows

    def row_blk(g, h, c):
        return rb0 + g * nchunk + c

    wk, wv = heads * HG_DK, heads * HG_DV
    in_specs = [
        pl.BlockSpec((rows, wk), lambda g, h, c: (row_blk(g, h, c), _P_HG_Q // wk + h)),
        pl.BlockSpec((rows, wk), lambda g, h, c: (row_blk(g, h, c), h)),
        pl.BlockSpec((rows, wv), lambda g, h, c: (row_blk(g, h, c), _P_HG_I // wv + h)),
        pl.BlockSpec((rows, wv), lambda g, h, c: (row_blk(g, h, c), _P_HG_G // wv + h)),
        pl.BlockSpec((DEPTH + 1, wk), lambda g, h, c: (0, h)),
        pl.BlockSpec((1, wv), lambda g, h, c: (0, h)),
        pl.BlockSpec((rows, rows), lambda g, h, c: (0, 0)),
    ]
    level_ids = jnp.asarray(_hgrn_level_ids(rows, seq_len, direct, levels))
    args = [proj, hg_f, proj, proj, lb_logits, norm_g, level_ids]
    state_spec = pl.BlockSpec((nseq, heads, HG_DK, HG_DV), lambda g, h, c: (g, h, 0, 0))
    if not zero_init:
        in_specs.append(state_spec)
        args.append(s0)
    return pl.pallas_call(
        functools.partial(_hgrn_kernel, layer=layer, rows=rows, seq_len=seq_len, heads=heads,
                          direct=direct, levels=levels, zero_init=zero_init),
        grid=(ngroup, HG_HEADS // heads, nchunk),
        in_specs=in_specs,
        out_specs=[pl.BlockSpec((rows, wv), lambda g, h, c: (g * nchunk + c, h)), state_spec],
        out_shape=[jax.ShapeDtypeStruct((tokens, HG_V), BF16),
                   jax.ShapeDtypeStruct((batch, HG_HEADS, HG_DK, HG_DV), F32)],
        compiler_params=pltpu.CompilerParams(
            dimension_semantics=("parallel", "parallel", "arbitrary")),
        name="hgrn2",
    )(*args)


def _merge_kernel(ba_ref, wa_ref, bb_ref, wb_ref, ga_ref, gb_ref, o_ref, wa_sc, wb_sc):
    @pl.when(pl.program_id(1) == 0)
    def _cast_weights():
        wa_sc[...] = wa_ref[...].astype(BF16)
        wb_sc[...] = wb_ref[...].astype(BF16)

    ya = jnp.dot(ba_ref[...], wa_sc[...], preferred_element_type=F32)
    yb = jnp.dot(bb_ref[...], wb_sc[...], preferred_element_type=F32)
    ga = _sigmoid(ga_ref[...].astype(F32))
    gb = _sigmoid(gb_ref[...].astype(F32))
    o_ref[...] = (ga * ya + gb * yb).astype(o_ref.dtype)


def _merge(branch_a, w_a, branch_b, w_b, proj, row0):
    m = branch_a.shape[0]
    tm, tn = _MERGE_TM, _MERGE_TN
    rb0 = row0 // tm
    ga_blk = _P_GATE_A // tn
    gb_blk = _P_GATE_B // tn
    return pl.pallas_call(
        _merge_kernel,
        grid=(D_MODEL // tn, m // tm),
        in_specs=[pl.BlockSpec((tm, ML_V), lambda j, i: (i, 0)),
                  pl.BlockSpec((ML_V, tn), lambda j, i: (0, j)),
                  pl.BlockSpec((tm, HG_V), lambda j, i: (i, 0)),
                  pl.BlockSpec((HG_V, tn), lambda j, i: (0, j)),
                  pl.BlockSpec((tm, tn), lambda j, i: (rb0 + i, ga_blk + j)),
                  pl.BlockSpec((tm, tn), lambda j, i: (rb0 + i, gb_blk + j))],
        out_specs=pl.BlockSpec((tm, tn), lambda j, i: (i, j)),
        out_shape=jax.ShapeDtypeStruct((m, D_MODEL), BF16),
        scratch_shapes=[pltpu.VMEM((ML_V, tn), BF16), pltpu.VMEM((HG_V, tn), BF16)],
        compiler_params=pltpu.CompilerParams(dimension_semantics=("parallel", "arbitrary")),
        name="merge",
    )(branch_a, w_a, branch_b, w_b, proj, proj)


def _layernorm_rows(z, g, b):
    mu = jnp.mean(z, axis=1, keepdims=True)
    zc = z - mu
    var = jnp.mean(zc * zc, axis=1, keepdims=True)
    return zc * lax.rsqrt(var + LN_EPS) * g + b


def _outproj_kernel(mgp_ref, mgs_ref, w_ref, xp_ref, xs_ref, g_ref, b_ref, x1_ref, x1b_ref, *,
                    prompt_tiles):
    in_prompt = pl.program_id(0) < prompt_tiles
    mg = jnp.where(in_prompt, mgp_ref[...], mgs_ref[...])
    mix = jnp.dot(mg, w_ref[...], preferred_element_type=F32)
    x = jnp.where(in_prompt, xp_ref[...], xs_ref[...])
    x1 = _layernorm_rows(DEEPNORM_ALPHA * x + mix, g_ref[...], b_ref[...])
    x1_ref[...] = x1
    x1b_ref[...] = x1.astype(BF16)


def _outproj(merged_p, merged_s, w_out, x_prompt, x_sample, ln_g, ln_b):
    tm = _OUT_TM
    m = x_prompt.shape[0] + x_sample.shape[0]
    p_tiles = x_prompt.shape[0] // tm
    row = lambda i: (i, 0)
    const = lambda i: (0, 0)
    prompt_row = lambda i: (jnp.minimum(i, p_tiles - 1), 0)
    sample_row = lambda i: (jnp.maximum(i - p_tiles, 0), 0)
    return pl.pallas_call(
        functools.partial(_outproj_kernel, prompt_tiles=p_tiles),
        grid=(m // tm,),
        in_specs=[pl.BlockSpec((tm, D_MODEL), prompt_row),
                  pl.BlockSpec((tm, D_MODEL), sample_row),
                  pl.BlockSpec((D_MODEL, D_MODEL), const),
                  pl.BlockSpec((tm, D_MODEL), prompt_row),
                  pl.BlockSpec((tm, D_MODEL), sample_row),
                  pl.BlockSpec((1, D_MODEL), const),
                  pl.BlockSpec((1, D_MODEL), const)],
        out_specs=[pl.BlockSpec((tm, D_MODEL), row), pl.BlockSpec((tm, D_MODEL), row)],
        out_shape=[jax.ShapeDtypeStruct((m, D_MODEL), F32),
                   jax.ShapeDtypeStruct((m, D_MODEL), BF16)],
        compiler_params=pltpu.CompilerParams(dimension_semantics=("arbitrary",)),
        name="out_proj_ln",
    )(merged_p, merged_s, w_out, x_prompt, x_sample, ln_g, ln_b)


def _mlp_kernel(x1b_ref, wu_ref, wd_ref, x1_ref, g_ref, b_ref, y_ref):
    f = pl.program_id(1)

    @pl.when(f == 0)
    def _init():
        y_ref[...] = jnp.zeros(y_ref.shape, F32)

    hid = jnp.maximum(jnp.dot(x1b_ref[...], wu_ref[...], preferred_element_type=F32), 0.0)
    hid = (hid * hid).astype(BF16)
    y_ref[...] += jnp.dot(hid, wd_ref[...], preferred_element_type=F32)

    @pl.when(f == pl.num_programs(1) - 1)
    def _finish():
        z = DEEPNORM_ALPHA * x1_ref[...] + y_ref[...]
        y_ref[...] = _layernorm_rows(z, g_ref[...], b_ref[...])


def _mlp(x1b, w_up, w_down, x1, ln_g, ln_b, row0, tokens):
    tm, tf = _MLP_TM, _MLP_TF
    rb0 = row0 // tm
    return pl.pallas_call(
        _mlp_kernel,
        grid=(tokens // tm, D_FF // tf),
        in_specs=[pl.BlockSpec((tm, D_MODEL), lambda i, f: (rb0 + i, 0)),
                  pl.BlockSpec((D_MODEL, tf), lambda i, f: (0, f)),
                  pl.BlockSpec((tf, D_MODEL), lambda i, f: (f, 0)),
                  pl.BlockSpec((tm, D_MODEL), lambda i, f: (rb0 + i, 0)),
                  pl.BlockSpec((1, D_MODEL), lambda i, f: (0, 0)),
                  pl.BlockSpec((1, D_MODEL), lambda i, f: (0, 0))],
        out_specs=pl.BlockSpec((tm, D_MODEL), lambda i, f: (i, 0)),
        out_shape=jax.ShapeDtypeStruct((tokens, D_MODEL), F32),
        compiler_params=pltpu.CompilerParams(dimension_semantics=("parallel", "arbitrary")),
        name="mlp_ln",
    )(x1b, w_up, w_down, x1, ln_g, ln_b)


_PROMPT_ML = dict(rows=256, seq_len=256)
_PROMPT_HG = dict(rows=256, seq_len=256, heads=2, direct=8,
                  levels=((32, 8), (128, 32), (256, 128)))
_SAMPLE_ML = dict(rows=64, seq_len=8)
_SAMPLE_HG = dict(rows=128, seq_len=8, heads=2, direct=8, levels=())


def kernel(x_prompt, x_sample, state_mlstm_C, state_mlstm_n, state_mlstm_m, state_hgrn_S,
           hg_lb_logits, w_in, b_ig, b_fg, ml_norm_g, hg_norm_g, w_branch_a, w_branch_b, w_out,
           ln1_g, ln1_b, w_up, w_down, ln2_g, ln2_b):
    batch_p, seq_p, _ = x_prompt.shape
    batch_s, seq_s, _ = x_sample.shape
    tok_p, tok_s = batch_p * seq_p, batch_s * seq_s
    total = tok_p + tok_s
    group_p = dict(row0=0, batch=batch_p, seq=seq_p)
    group_s = dict(row0=tok_p, batch=batch_s, seq=seq_s)
    xp = x_prompt.reshape(tok_p, D_MODEL)
    xs = x_sample.reshape(tok_s, D_MODEL)
    lb_logits = hg_lb_logits.astype(F32)
    states_p, states_s = [], []
    for l in range(DEPTH):
        xb = jnp.concatenate([xp.astype(BF16), xs.astype(BF16)], axis=0)
        proj, hg_f, gates, gates_gm = _inproj(xb, jnp.swapaxes(w_in[l], 0, 1))

        gate_bias = jnp.concatenate([b_ig[l], b_fg[l]]).astype(F32)
        brow = jnp.zeros((1, LANES), F32).at[0, :_N_GATES].set(gate_bias)
        bcol = gate_bias.reshape(_N_GATES, 1)
        ml_g = ml_norm_g[l].reshape(1, ML_V).astype(F32)
        hg_g = hg_norm_g[l].reshape(1, HG_V).astype(F32)

        m_rows = jnp.repeat(state_mlstm_m[l].astype(F32), seq_s, axis=0)
        gates_s = gates[tok_p:].at[:, _N_GATES:_N_GATES + ML_HEADS].set(m_rows)
        rows_s = _SAMPLE_ML["rows"]
        gm_s = gates_gm[tok_p // _PROJ_TM:]
        gm_s = gm_s.reshape(-1, 8, _PROJ_TM // rows_s, rows_s).transpose(0, 2, 1, 3)
        gm_s = gm_s.reshape(tok_s // rows_s, 8, rows_s)
        state_ml = (state_mlstm_C[l].astype(F32),
                    state_mlstm_n[l].astype(F32).reshape(batch_s, ML_HEADS, 1, ML_DK))

        ba_p, c_p, n_p, m_all_p = _mlstm(proj, gates, gates_gm, brow, bcol, ml_g, None,
                                         **group_p, **_PROMPT_ML)
        ba_s, c_s, n_s, m_all_s = _mlstm(proj, gates_s, gm_s, brow, bcol, ml_g, state_ml,
                                         **group_s, **_SAMPLE_ML)
        bb_p, s_p = _hgrn(proj, hg_f, lb_logits, hg_g, None, layer=l, **group_p, **_PROMPT_HG)
        bb_s, s_s = _hgrn(proj, hg_f, lb_logits, hg_g, state_hgrn_S[l].astype(F32), layer=l,
                          **group_s, **_SAMPLE_HG)

        merged_p = _merge(ba_p, w_branch_a[l], bb_p, w_branch_b[l], proj, 0)
        merged_s = _merge(ba_s, w_branch_a[l], bb_s, w_branch_b[l], proj, tok_p)
        x1, x1b = _outproj(merged_p, merged_s, w_out[l].astype(BF16), xp, xs,
                           ln1_g[l].reshape(1, D_MODEL), ln1_b[l].reshape(1, D_MODEL))
        mlp_args = (x1b, w_up[l].astype(BF16), w_down[l].astype(BF16), x1,
                    ln2_g[l].reshape(1, D_MODEL), ln2_b[l].reshape(1, D_MODEL))
        xp = _mlp(*mlp_args, 0, tok_p)
        xs = _mlp(*mlp_args, tok_p, tok_s)

        states_p.append((c_p, n_p.reshape(batch_p, ML_HEADS, ML_DK),
                         m_all_p[:, seq_p - 1::seq_p, 0].T, s_p))
        states_s.append((c_s, n_s.reshape(batch_s, ML_HEADS, ML_DK),
                         m_all_s[:, seq_s - 1::seq_s, 0].T, s_s))
    stack = lambda states, k: jnp.stack([s[k] for s in states])
    return (xp.reshape(batch_p, seq_p, D_MODEL), xs.reshape(batch_s, seq_s, D_MODEL),
            stack(states_p, 0), stack(states_p, 1), stack(states_p, 2), stack(states_p, 3),
            stack(states_s, 0), stack(states_s, 1), stack(states_s, 2), stack(states_s, 3))
```

```python
import functools
import math

import numpy as np
import jax
import jax.numpy as jnp
from jax import lax
from jax.experimental import pallas as pl
from jax.experimental.pallas import tpu as pltpu

F32 = jnp.float32
BF16 = jnp.bfloat16

D_MODEL = 2048
DEPTH = 1
ML_HEADS, ML_DK, ML_DV = 4, 256, 512
HG_HEADS, HG_DK, HG_DV = 8, 128, 256
ML_QK = ML_HEADS * ML_DK
ML_V = ML_HEADS * ML_DV
HG_K = HG_HEADS * HG_DK
HG_V = HG_HEADS * HG_DV
D_FF = 4 * D_MODEL
LN_EPS = 1e-5
DEEPNORM_ALPHA = (2.0 * DEPTH) ** 0.25
ML_K_SCALE = ML_DK ** -0.5
LOG2_E = math.log2(math.e)
LANES = 128

_OFF_ML_Q = 0
_OFF_ML_K = _OFF_ML_Q + ML_QK
_OFF_ML_V = _OFF_ML_K + ML_QK
_OFF_ML_I = _OFF_ML_V + ML_V
_OFF_ML_F = _OFF_ML_I + ML_HEADS
_OFF_ML_O = _OFF_ML_F + ML_HEADS
_OFF_HG_Q = _OFF_ML_O + ML_V
_OFF_HG_F = _OFF_HG_Q + HG_K
_OFF_HG_I = _OFF_HG_F + HG_K
_OFF_HG_G = _OFF_HG_I + HG_V
_OFF_GATE_A = _OFF_HG_G + HG_V
_OFF_GATE_B = _OFF_GATE_A + D_MODEL
D_IN = _OFF_GATE_B + D_MODEL

_N_GATES = 2 * ML_HEADS
_P_ML_Q = _OFF_ML_Q
_P_ML_K = _OFF_ML_K
_P_ML_V = _OFF_ML_V
_P_ML_O = _OFF_ML_O - _N_GATES
_P_HG_Q = _OFF_HG_Q - _N_GATES
_P_HG_F = _OFF_HG_F - _N_GATES
_P_HG_I = _OFF_HG_I - _N_GATES
_P_HG_G = _OFF_HG_G - _N_GATES
_P_GATE_A = _OFF_GATE_A - _N_GATES
_P_GATE_B = _OFF_GATE_B - _N_GATES
_P_WIDTH = D_IN - _N_GATES

_PROJ_TM, _PROJ_TN = 1024, 1024
_MERGE_TM, _MERGE_TN = 1024, 512
_OUT_TM = 512
_MLP_TM, _MLP_TF = 1024, 512


def _sigmoid(x):
    return 1.0 / (1.0 + jnp.exp(-x))


def _log_sigmoid(x):
    return jnp.minimum(x, 0.0) - jnp.log1p(jnp.exp(-jnp.abs(x)))


def _split2(x):
    hi = x.astype(BF16)
    lo = (x - hi.astype(F32)).astype(BF16)
    return hi, lo


def _dot01_right(t01, x):
    hi, lo = _split2(x)
    return (jnp.dot(t01, hi, preferred_element_type=F32)
            + jnp.dot(t01, lo, preferred_element_type=F32))


def _dot01_left(x, t01):
    hi, lo = _split2(x)
    return (jnp.dot(hi, t01, preferred_element_type=F32)
            + jnp.dot(lo, t01, preferred_element_type=F32))


def _dot_nt(a, b):
    return lax.dot_general(a, b, (((1,), (1,)), ((), ())), preferred_element_type=F32)


def _dot_tn(a, b):
    return lax.dot_general(a, b, (((0,), (0,)), ((), ())), preferred_element_type=F32)


def _seq_masks(rows, seq_len):
    ri = lax.broadcasted_iota(jnp.int32, (rows, rows), 0)
    ci = lax.broadcasted_iota(jnp.int32, (rows, rows), 1)
    lower = jnp.where(ci <= ri, 1.0, 0.0)
    upper = jnp.where(ri <= ci, 1.0, 0.0)
    if seq_len == rows:
        return jnp.ones((rows, rows), F32), lower, upper
    shift = seq_len.bit_length() - 1
    same = jnp.where((ri >> shift) == (ci >> shift), 1.0, 0.0)
    return same, same * lower, same * upper


def _inproj_kernel(xp_ref, xs_ref, wt_hbm, *refs, n_f32, n_gate_tile, prompt_tiles, cast_steps):
    n_cast = (len(refs) - 9) // 2
    cast_in = refs[:n_cast]
    o_ref, f_ref, g_ref, gr_ref = refs[n_cast:n_cast + 4]
    cast_out = refs[n_cast + 4:2 * n_cast + 4]
    wbuf, wbf_sc, wg_sc, sem, gsem = refs[2 * n_cast + 4:]
    n = pl.program_id(0)
    m = pl.program_id(1)
    tn = wbf_sc.shape[1]

    @pl.when(m < cast_steps)
    def _cast_other_weights():
        for src, dst in zip(cast_in, cast_out):
            dst[...] = src[...].astype(BF16)

    def tile_copy(tile):
        start = pl.multiple_of(tile * tn + jnp.where(tile >= n_gate_tile, _N_GATES, 0), _N_GATES)
        return pltpu.make_async_copy(wt_hbm.at[pl.ds(start, tn), :], wbuf, sem.at[0])

    def gate_copy():
        return pltpu.make_async_copy(wt_hbm.at[pl.ds(_OFF_ML_I, _N_GATES), :],
                                     wg_sc.at[pl.ds(0, _N_GATES), :], gsem.at[0])

    @pl.when(m == 0)
    def _next_weight_tile():
        @pl.when(n == 0)
        def _first():
            tile_copy(0).start()
            wg_sc[_N_GATES:, :] = jnp.zeros((LANES - _N_GATES, wg_sc.shape[1]), F32)
            gate_copy().start()
            gate_copy().wait()

        tile_copy(n).wait()
        wbf_sc[...] = wbuf[...].T.astype(BF16)

        @pl.when(n + 1 < pl.num_programs(0))
        def _prefetch():
            tile_copy(n + 1).start()

    x = jnp.where(m < prompt_tiles, xp_ref[...], xs_ref[...])
    acc = jnp.dot(x, wbf_sc[...], preferred_element_type=F32)
    o_ref[...] = acc.astype(BF16)

    @pl.when(n == n_f32)
    def _f32_outputs():
        f_ref[...] = acc
        wg = wg_sc[...].astype(BF16)
        g_ref[...] = _dot_nt(x, wg)
        gr_ref[0] = _dot_nt(wg, x)[0:_N_GATES, :]


def _inproj(xp_b, xs_b, w_t, cast_weights):
    k = xp_b.shape[1]
    tm, tn = _PROJ_TM, _PROJ_TN
    p_tiles = xp_b.shape[0] // tm
    n_m = p_tiles + xs_b.shape[0] // tm
    tokens = n_m * tm
    n_n = _P_WIDTH // tn
    n_f32 = _P_HG_F // tn
    cast_steps = min(n_m, 8)

    def parked(n, m):
        return jnp.where(n < n_f32, 0, jnp.where(n == n_f32, m, n_m - 1))

    def cast_spec(w):
        return pl.BlockSpec((w.shape[0] // n_n, w.shape[1] // cast_steps),
                            lambda n, m: (n, jnp.minimum(m, cast_steps - 1)))

    cast_specs = [cast_spec(w) for w in cast_weights]
    outs = pl.pallas_call(
        functools.partial(_inproj_kernel, n_f32=n_f32, n_gate_tile=_OFF_ML_I // tn,
                          prompt_tiles=p_tiles, cast_steps=cast_steps),
        grid=(n_n, n_m),
        in_specs=[pl.BlockSpec((tm, k), lambda n, m: (jnp.minimum(m, p_tiles - 1), 0)),
                  pl.BlockSpec((tm, k), lambda n, m: (jnp.maximum(m - p_tiles, 0), 0)),
                  pl.BlockSpec(memory_space=pl.ANY),
                  *cast_specs],
        out_specs=[pl.BlockSpec((tm, tn), lambda n, m: (m, n)),
                   pl.BlockSpec((tm, HG_K), lambda n, m: (parked(n, m), 0)),
                   pl.BlockSpec((tm, LANES), lambda n, m: (parked(n, m), 0)),
                   pl.BlockSpec((1, 8, tm), lambda n, m: (parked(n, m), 0, 0)),
                   *cast_specs],
        out_shape=[jax.ShapeDtypeStruct((tokens, _P_WIDTH), BF16),
                   jax.ShapeDtypeStruct((tokens, HG_K), F32),
                   jax.ShapeDtypeStruct((tokens, LANES), F32),
                   jax.ShapeDtypeStruct((n_m, 8, tm), F32),
                   *[jax.ShapeDtypeStruct(w.shape, BF16) for w in cast_weights]],
        scratch_shapes=[pltpu.VMEM((tn, k), F32),
                        pltpu.VMEM((k, tn), BF16),
                        pltpu.VMEM((LANES, k), F32),
                        pltpu.SemaphoreType.DMA((1,)),
                        pltpu.SemaphoreType.DMA((1,))],
        compiler_params=pltpu.CompilerParams(dimension_semantics=("arbitrary", "arbitrary")),
        name="in_proj",
    )(xp_b, xs_b, w_t, *cast_weights)
    return outs[:4], outs[4:]


def _mlstm_kernel(*refs, rows, seq_len, heads, zero_init):
    if zero_init:
        (q_ref, k_ref, v_ref, og_ref, gc_ref, gr_ref, brow_ref, bcol_ref, ng_ref,
         ba_ref, c_ref, n_ref, mrow_ref, m_sc) = refs
        c0_ref, n0_ref = c_ref, n_ref
    else:
        (q_ref, k_ref, v_ref, og_ref, gc_ref, gr_ref, brow_ref, bcol_ref, ng_ref, c0_ref, n0_ref,
         ba_ref, c_ref, n_ref, mrow_ref) = refs
    nseq = rows // seq_len
    head0 = pl.program_id(1) * heads

    if zero_init:
        @pl.when(pl.program_id(2) == 0)
        def _init():
            c_ref[...] = jnp.zeros(c_ref.shape, F32)
            n_ref[...] = jnp.zeros(n_ref.shape, F32)
            m_sc[...] = jnp.zeros(m_sc.shape, F32)

    same, lower, upper = _seq_masks(rows, seq_len)
    causal = lower > 0.5
    lane = lax.broadcasted_iota(jnp.int32, (rows, LANES), 1)
    sub = lax.broadcasted_iota(jnp.int32, (8, rows), 0)

    def sel_lane(x, idx):
        return jnp.sum(jnp.where(lane == idx, x, 0.0), axis=1, keepdims=True)

    def sel_sub(x, idx):
        return jnp.sum(jnp.where(sub == idx, x, 0.0), axis=0, keepdims=True)

    gc_raw = gc_ref[...]
    gc = gc_raw + brow_ref[...]
    lf_cols = _log_sigmoid(gc)
    b_cols = _dot01_right(lower.astype(BF16), lf_cols)
    gr = gr_ref[0] + bcol_ref[...]
    b_rows = _dot01_left(_log_sigmoid(gr), upper.astype(BF16))
    if nseq > 1:
        b_ends = _dot01_right(same.astype(BF16), lf_cols)
        ci = lax.broadcasted_iota(jnp.int32, (rows, rows), 1)
        last = (same * jnp.where((ci & (seq_len - 1)) == seq_len - 1, 1.0, 0.0)).astype(BF16)
        shift = seq_len.bit_length() - 1
        row_seq = lax.broadcasted_iota(jnp.int32, (rows, 1), 0) >> shift

    for hh in range(heads):
        head = head0 + hh
        ks = slice(hh * ML_DK, (hh + 1) * ML_DK)
        vs = slice(hh * ML_DV, (hh + 1) * ML_DV)
        ig_col = sel_lane(gc, head)
        b_col = sel_lane(b_cols, head + ML_HEADS)
        if zero_init:
            m_prev = jnp.broadcast_to(m_sc[hh:hh + 1, 0:1], (rows, 1))
        else:
            m_prev = sel_lane(gc_raw, head + 2 * ML_HEADS)
        ig_row = sel_sub(gr, head)
        b_row = sel_sub(b_rows, head + ML_HEADS)

        logd = jnp.where(causal, (b_col - b_row) + ig_row, -jnp.inf)
        m_t = jnp.maximum(b_col + m_prev, jnp.max(logd, axis=1, keepdims=True))
        d = jnp.exp(logd - m_t)
        w_inter = jnp.exp(b_col + m_prev - m_t)

        qb = q_ref[:, ks]
        kb = k_ref[:, ks]
        vb = v_ref[:, vs]
        s = _dot_nt(qb, kb) * (d * ML_K_SCALE)
        num = jnp.dot(s.astype(BF16), vb, preferred_element_type=F32)
        den = jnp.sum(s, axis=1, keepdims=True)

        if nseq == 1:
            b_end = b_col[rows - 1:rows, :]
            m_new = m_t[rows - 1:rows, :]
        else:
            b_end = sel_lane(b_ends, head + ML_HEADS)
            m_new = _dot01_right(last, jnp.broadcast_to(m_t, (rows, LANES)))[:, 0:1]
        w_end = jnp.exp(b_end - b_col + ig_col - m_new)
        decay = jnp.exp(b_end + m_prev - m_new)

        qf = qb.astype(F32)
        kw = (w_end * ML_K_SCALE) * kb.astype(F32)

        if nseq == 1:
            c_prev = c0_ref[0, hh]
            n_prev = n0_ref[0, hh]
            q_c = jnp.dot(qb, c_prev.astype(BF16), preferred_element_type=F32)
            q_n = jnp.sum(qf * n_prev, axis=1, keepdims=True)
            dec = decay[0:1, :]
            c_ref[0, hh] = dec * c_prev + _dot_tn(kw.astype(BF16), vb)
            n_ref[0, hh] = dec * n_prev + jnp.sum(kw, axis=0, keepdims=True)
        else:
            q_c = jnp.zeros((rows, ML_DV), F32)
            q_n = jnp.zeros((rows, 1), F32)
            for j in range(nseq):
                in_seq = row_seq == j
                c_prev = c0_ref[j, hh]
                n_prev = n0_ref[j, hh]
                q_c = jnp.where(
                    in_seq, jnp.dot(qb, c_prev.astype(BF16), preferred_element_type=F32), q_c)
                q_n = jnp.where(in_seq, jnp.sum(qf * n_prev, axis=1, keepdims=True), q_n)
                kw_j = jnp.where(in_seq, kw, 0.0)
                dec = decay[j * seq_len:j * seq_len + 1, :]
                c_ref[j, hh] = dec * c_prev + _dot_tn(kw_j.astype(BF16), vb)
                n_ref[j, hh] = dec * n_prev + jnp.sum(kw_j, axis=0, keepdims=True)

        num = num + w_inter * q_c
        den = den + w_inter * q_n
        h_out = num / jnp.maximum(jnp.abs(den), jnp.exp(-m_t))
        mu = jnp.mean(h_out, axis=1, keepdims=True)
        xc = h_out - mu
        var = jnp.mean(xc * xc, axis=1, keepdims=True)
        hn = xc * lax.rsqrt(var + LN_EPS) * ng_ref[:, vs]
        ba_ref[:, vs] = (hn * _sigmoid(og_ref[:, vs].astype(F32))).astype(BF16)
        mrow_ref[hh] = jnp.broadcast_to(m_t, (rows, LANES))
        if zero_init:
            m_sc[hh:hh + 1, :] = jnp.broadcast_to(m_new, (1, LANES))


def _mlstm(proj, gates_col, gates_row, brow, bcol, norm_g, state, *, row0, batch, seq, rows, seq_len,
           heads):
    tokens = batch * seq
    zero_init = state is None
    nseq = rows // seq_len
    nchunk = seq // seq_len if nseq == 1 else 1
    ngroup = tokens // (rows * nchunk)
    bq, bv = heads * ML_DK, heads * ML_DV
    rb0 = row0 // rows
    per_row_tile = gates_row.shape[2] // rows

    def local(g, c):
        return g * nchunk + c

    in_specs = [
        pl.BlockSpec((rows, bq), lambda g, h, c: (rb0 + local(g, c), _P_ML_Q // bq + h)),
        pl.BlockSpec((rows, bq), lambda g, h, c: (rb0 + local(g, c), _P_ML_K // bq + h)),
        pl.BlockSpec((rows, bv), lambda g, h, c: (rb0 + local(g, c), _P_ML_V // bv + h)),
        pl.BlockSpec((rows, bv), lambda g, h, c: (rb0 + local(g, c), _P_ML_O // bv + h)),
        pl.BlockSpec((rows, LANES), lambda g, h, c: (local(g, c), 0)),
        pl.BlockSpec((1, 8, rows),
                     lambda g, h, c: (local(g, c) // per_row_tile, 0, local(g, c) % per_row_tile)),
        pl.BlockSpec((1, LANES), lambda g, h, c: (0, 0)),
        pl.BlockSpec((8, 1), lambda g, h, c: (0, 0)),
        pl.BlockSpec((1, bv), lambda g, h, c: (0, h)),
    ]
    args = [proj, proj, proj, proj, gates_col, gates_row, brow, bcol, norm_g]
    state_specs = [
        pl.BlockSpec((nseq, heads, ML_DK, ML_DV), lambda g, h, c: (g, h, 0, 0)),
        pl.BlockSpec((nseq, heads, 1, ML_DK), lambda g, h, c: (g, h, 0, 0)),
    ]
    scratch = []
    if zero_init:
        scratch = [pltpu.VMEM((8, LANES), F32)]
    else:
        in_specs += state_specs
        args += [state[0], state[1]]
    out_specs = [
        pl.BlockSpec((rows, bv), lambda g, h, c: (local(g, c), h)),
        *state_specs,
        pl.BlockSpec((heads, rows, LANES), lambda g, h, c: (h, local(g, c), 0)),
    ]
    out_shape = [
        jax.ShapeDtypeStruct((tokens, ML_V), BF16),
        jax.ShapeDtypeStruct((batch, ML_HEADS, ML_DK, ML_DV), F32),
        jax.ShapeDtypeStruct((batch, ML_HEADS, 1, ML_DK), F32),
        jax.ShapeDtypeStruct((ML_HEADS, tokens, LANES), F32),
    ]
    return pl.pallas_call(
        functools.partial(_mlstm_kernel, rows=rows, seq_len=seq_len, heads=heads,
                          zero_init=zero_init),
        grid=(ngroup, ML_HEADS // heads, nchunk),
        in_specs=in_specs,
        out_specs=out_specs,
        out_shape=out_shape,
        scratch_shapes=scratch,
        compiler_params=pltpu.CompilerParams(
            dimension_semantics=("parallel", "parallel", "arbitrary")),
        name="mlstm",
    )(*args)


def _hgrn_level_ids(rows, seq_len, direct, levels):
    t = np.arange(rows)[:, None]
    s = np.arange(rows)[None, :]
    ids = np.full((rows, rows), -1, np.int32)
    count = ((t // direct == s // direct) & (s <= t)).astype(np.int32)
    for idx, (block, sub_size) in enumerate(levels):
        owned = (t // block == s // block) & ((s % block) // sub_size < (t % block) // sub_size)
        ids[owned] = idx
        count += owned
    wanted = (t // seq_len == s // seq_len) & (s <= t)
    assert np.array_equal(count, wanted.astype(np.int32)), (rows, seq_len, direct, levels)
    return ids


def _hgrn_intra(q, kin, g2, gk2, level_ids, rows, direct, levels):
    ngroups = rows // LANES
    rowi = lax.broadcasted_iota(jnp.int32, (direct, LANES), 0)
    lanei = lax.broadcasted_iota(jnp.int32, (direct, LANES), 1)
    keep = [jnp.where(lanei == s, rowi, -1) >= s for s in range(direct)]
    zero_group = jnp.zeros((direct, LANES), F32)
    prods = []
    for blk in range(rows // direct):
        r0 = blk * direct
        qb = q[r0:r0 + direct]
        gb = g2[r0:r0 + direct]
        gkb = gk2[r0:r0 + direct]
        for s in range(direct):
            prods.append(qb * jnp.exp2(gb - gkb[s:s + 1, :]))
    sums = jnp.dot(jnp.concatenate(prods, axis=0).astype(BF16), jnp.ones((HG_DK, LANES), BF16),
                   preferred_element_type=F32)
    panels = []
    for blk in range(rows // direct):
        r0 = blk * direct
        ag = zero_group
        for s in range(direct):
            p0 = (blk * direct + s) * direct
            ag = jnp.where(keep[s], sums[p0:p0 + direct], ag)
        lane0 = r0 % LANES
        if lane0:
            ag = pltpu.roll(ag, lane0, axis=1)
        grp = r0 // LANES
        pieces = [zero_group] * grp + [ag] + [zero_group] * (ngroups - grp - 1)
        panels.append(pieces[0] if ngroups == 1 else jnp.concatenate(pieces, axis=1))
    a = panels[0] if len(panels) == 1 else jnp.concatenate(panels, axis=0)

    if levels:
        rowid = lax.broadcasted_iota(jnp.int32, (rows, 1), 0)
    for idx, (block, sub_size) in enumerate(levels):
        sshift = sub_size.bit_length() - 1
        row_sub = (rowid & (block - 1)) >> sshift
        q_parts, k_parts = [], []
        for j in range(1, block // sub_size):
            refs = []
            for b0 in range(0, rows, block):
                r = b0 + j * sub_size - 1
                refs.append(jnp.broadcast_to(g2[r:r + 1, :], (block, HG_DK)))
            g_ref = refs[0] if len(refs) == 1 else jnp.concatenate(refs, axis=0)
            e = jnp.exp2(-jnp.abs(g2 - g_ref))
            q_parts.append(jnp.where(row_sub == j, q * e, 0.0).astype(BF16))
            k_parts.append((kin * e).astype(BF16))
        qcat = q_parts[0] if len(q_parts) == 1 else jnp.concatenate(q_parts, axis=1)
        kcat = k_parts[0] if len(k_parts) == 1 else jnp.concatenate(k_parts, axis=1)
        a = jnp.where(level_ids == idx, _dot_nt(qcat, kcat), a)
    return a


def _hgrn_kernel(*refs, layer, rows, seq_len, heads, direct, levels, zero_init):
    if zero_init:
        q_ref, f_ref, i_ref, g_ref, lbl_ref, ng_ref, lvl_ref, bb_ref, s_ref = refs
        s0_ref = s_ref
    else:
        q_ref, f_ref, i_ref, g_ref, lbl_ref, ng_ref, lvl_ref, s0_ref, bb_ref, s_ref = refs
    nseq = rows // seq_len

    if zero_init:
        @pl.when(pl.program_id(2) == 0)
        def _init():
            s_ref[...] = jnp.zeros(s_ref.shape, F32)

    same, lower, _ = _seq_masks(rows, seq_len)
    lower_b = lower.astype(BF16)
    same_b = same.astype(BF16)
    level_ids = lvl_ref[...]
    if nseq > 1:
        shift = seq_len.bit_length() - 1
        row_seq = lax.broadcasted_iota(jnp.int32, (rows, 1), 0) >> shift

    def decay_cols(row):
        col = jnp.broadcast_to(jnp.exp2(row), (HG_DK, HG_DK)).T
        return jnp.concatenate([col] * (HG_DV // HG_DK), axis=1)

    for hh in range(heads):
        ks = slice(hh * HG_DK, (hh + 1) * HG_DK)
        vs = slice(hh * HG_DV, (hh + 1) * HG_DV)
        lg = lbl_ref[:, ks]
        ex = jnp.exp(lg - jnp.max(lg, axis=0, keepdims=True))
        lb = (jnp.sum(ex[0:layer + 1, :], axis=0, keepdims=True)
              / jnp.sum(ex, axis=0, keepdims=True))

        f = lb + (1.0 - lb) * _sigmoid(f_ref[:, ks])
        kin = 1.0 - f
        lf2 = jnp.log(f) * LOG2_E
        g2 = _dot01_right(lower_b, lf2)
        gk2 = g2 - jnp.log(kin) * LOG2_E
        q = q_ref[:, ks].astype(F32)
        if nseq == 1:
            g_end = jnp.broadcast_to(g2[rows - 1:rows, :], (rows, HG_DK))
        else:
            g_end = _dot01_right(same_b, lf2)
        qg = (q * jnp.exp2(g2)).astype(BF16)
        kg = kin * jnp.exp2(g_end - g2)
        ib = i_ref[:, vs]

        a = _hgrn_intra(q, kin, g2, gk2, level_ids, rows, direct, levels)
        o = jnp.dot(a.astype(BF16), ib, preferred_element_type=F32)

        if nseq == 1:
            s_prev = s0_ref[0, hh]
            o = o + jnp.dot(qg, s_prev.astype(BF16), preferred_element_type=F32)
            s_ref[0, hh] = decay_cols(g_end[0:1, :]) * s_prev + _dot_tn(kg.astype(BF16), ib)
        else:
            for j in range(nseq):
                in_seq = row_seq == j
                s_prev = s0_ref[j, hh]
                o_j = jnp.dot(qg, s_prev.astype(BF16), preferred_element_type=F32)
                o = o + jnp.where(in_seq, o_j, 0.0)
                kg_j = jnp.where(in_seq, kg, 0.0).astype(BF16)
                s_ref[j, hh] = (decay_cols(g_end[j * seq_len:j * seq_len + 1, :]) * s_prev
                                + _dot_tn(kg_j, ib))

        o = o * lax.rsqrt(jnp.mean(o * o, axis=1, keepdims=True) + LN_EPS)
        gate = g_ref[:, vs].astype(F32)
        bb_ref[:, vs] = (o * ng_ref[:, vs] * (gate * _sigmoid(gate))).astype(BF16)


def _hgrn(proj, hg_f, lb_logits, norm_g, s0, *, layer, row0, batch, seq, rows, seq_len, heads,
          direct, levels):
    tokens = batch * seq
    zero_init = s0 is None
    nseq = rows // seq_len
    nchunk = seq // seq_len if nseq == 1 else 1
    ngroup = tokens // (rows * nchunk)
    rb0 = row0 // rows

    def row_blk(g, h, c):
        return rb0 + g * nchunk + c

    wk, wv = heads * HG_DK, heads * HG_DV
    in_specs = [
        pl.BlockSpec((rows, wk), lambda g, h, c: (row_blk(g, h, c), _P_HG_Q // wk + h)),
        pl.BlockSpec((rows, wk), lambda g, h, c: (row_blk(g, h, c), h)),
        pl.BlockSpec((rows, wv), lambda g, h, c: (row_blk(g, h, c), _P_HG_I // wv + h)),
        pl.BlockSpec((rows, wv), lambda g, h, c: (row_blk(g, h, c), _P_HG_G // wv + h)),
        pl.BlockSpec((DEPTH + 1, wk), lambda g, h, c: (0, h)),
        pl.BlockSpec((1, wv), lambda g, h, c: (0, h)),
        pl.BlockSpec((rows, rows), lambda g, h, c: (0, 0)),
    ]
    level_ids = jnp.asarray(_hgrn_level_ids(rows, seq_len, direct, levels))
    args = [proj, hg_f, proj, proj, lb_logits, norm_g, level_ids]
    state_spec = pl.BlockSpec((nseq, heads, HG_DK, HG_DV), lambda g, h, c: (g, h, 0, 0))
    if not zero_init:
        in_specs.append(state_spec)
        args.append(s0)
    return pl.pallas_call(
        functools.partial(_hgrn_kernel, layer=layer, rows=rows, seq_len=seq_len, heads=heads,
                          direct=direct, levels=levels, zero_init=zero_init),
        grid=(ngroup, HG_HEADS // heads, nchunk),
        in_specs=in_specs,
        out_specs=[pl.BlockSpec((rows, wv), lambda g, h, c: (g * nchunk + c, h)), state_spec],
        out_shape=[jax.ShapeDtypeStruct((tokens, HG_V), BF16),
                   jax.ShapeDtypeStruct((batch, HG_HEADS, HG_DK, HG_DV), F32)],
        compiler_params=pltpu.CompilerParams(
            dimension_semantics=("parallel", "parallel", "arbitrary")),
        name="hgrn2",
    )(*args)


def _merge_kernel(ba_ref, wa_ref, bb_ref, wb_ref, ga_ref, gb_ref, o_ref, wa_sc, wb_sc):
    @pl.when(pl.program_id(1) == 0)
    def _cast_weights():
        wa_sc[...] = wa_ref[...].astype(BF16)
        wb_sc[...] = wb_ref[...].astype(BF16)

    ya = jnp.dot(ba_ref[...], wa_sc[...], preferred_element_type=F32)
    yb = jnp.dot(bb_ref[...], wb_sc[...], preferred_element_type=F32)
    ga = _sigmoid(ga_ref[...].astype(F32))
    gb = _sigmoid(gb_ref[...].astype(F32))
    o_ref[...] = (ga * ya + gb * yb).astype(o_ref.dtype)


def _merge(branch_a, w_a, branch_b, w_b, proj, row0):
    m = branch_a.shape[0]
    tm, tn = _MERGE_TM, _MERGE_TN
    rb0 = row0 // tm
    ga_blk = _P_GATE_A // tn
    gb_blk = _P_GATE_B // tn
    return pl.pallas_call(
        _merge_kernel,
        grid=(D_MODEL // tn, m // tm),
        in_specs=[pl.BlockSpec((tm, ML_V), lambda j, i: (i, 0)),
                  pl.BlockSpec((ML_V, tn), lambda j, i: (0, j)),
                  pl.BlockSpec((tm, HG_V), lambda j, i: (i, 0)),
                  pl.BlockSpec((HG_V, tn), lambda j, i: (0, j)),
                  pl.BlockSpec((tm, tn), lambda j, i: (rb0 + i, ga_blk + j)),
                  pl.BlockSpec((tm, tn), lambda j, i: (rb0 + i, gb_blk + j))],
        out_specs=pl.BlockSpec((tm, tn), lambda j, i: (i, j)),
        out_shape=jax.ShapeDtypeStruct((m, D_MODEL), BF16),
        scratch_shapes=[pltpu.VMEM((ML_V, tn), BF16), pltpu.VMEM((HG_V, tn), BF16)],
        compiler_params=pltpu.CompilerParams(dimension_semantics=("parallel", "arbitrary")),
        name="merge",
    )(branch_a, w_a, branch_b, w_b, proj, proj)


def _layernorm_rows(z, g, b):
    mu = jnp.mean(z, axis=1, keepdims=True)
    zc = z - mu
    var = jnp.mean(zc * zc, axis=1, keepdims=True)
    return zc * lax.rsqrt(var + LN_EPS) * g + b


def _outproj_kernel(mgp_ref, mgs_ref, w_ref, xp_ref, xs_ref, g_ref, b_ref, x1_ref, x1b_ref, *,
                    prompt_tiles):
    in_prompt = pl.program_id(0) < prompt_tiles
    mg = jnp.where(in_prompt, mgp_ref[...], mgs_ref[...])
    mix = jnp.dot(mg, w_ref[...], preferred_element_type=F32)
    x = jnp.where(in_prompt, xp_ref[...], xs_ref[...])
    x1 = _layernorm_rows(DEEPNORM_ALPHA * x + mix, g_ref[...], b_ref[...])
    x1_ref[...] = x1
    x1b_ref[...] = x1.astype(BF16)


def _outproj(merged_p, merged_s, w_out, x_prompt, x_sample, ln_g, ln_b):
    tm = _OUT_TM
    m = x_prompt.shape[0] + x_sample.shape[0]
    p_tiles = x_prompt.shape[0] // tm
    row = lambda i: (i, 0)
    const = lambda i: (0, 0)
    prompt_row = lambda i: (jnp.minimum(i, p_tiles - 1), 0)
    sample_row = lambda i: (jnp.maximum(i - p_tiles, 0), 0)
    return pl.pallas_call(
        functools.partial(_outproj_kernel, prompt_tiles=p_tiles),
        grid=(m // tm,),
        in_specs=[pl.BlockSpec((tm, D_MODEL), prompt_row),
                  pl.BlockSpec((tm, D_MODEL), sample_row),
                  pl.BlockSpec((D_MODEL, D_MODEL), const),
                  pl.BlockSpec((tm, D_MODEL), prompt_row),
                  pl.BlockSpec((tm, D_MODEL), sample_row),
                  pl.BlockSpec((1, D_MODEL), const),
                  pl.BlockSpec((1, D_MODEL), const)],
        out_specs=[pl.BlockSpec((tm, D_MODEL), row), pl.BlockSpec((tm, D_MODEL), row)],
        out_shape=[jax.ShapeDtypeStruct((m, D_MODEL), F32),
                   jax.ShapeDtypeStruct((m, D_MODEL), BF16)],
        compiler_params=pltpu.CompilerParams(dimension_semantics=("arbitrary",)),
        name="out_proj_ln",
    )(merged_p, merged_s, w_out, x_prompt, x_sample, ln_g, ln_b)


def _mlp_kernel(x1b_ref, wu_ref, wd_ref, x1_ref, g_ref, b_ref, y_ref):
    f = pl.program_id(1)

    @pl.when(f == 0)
    def _init():
        y_ref[...] = jnp.zeros(y_ref.shape, F32)

    hid = jnp.maximum(jnp.dot(x1b_ref[...], wu_ref[...], preferred_element_type=F32), 0.0)
    hid = (hid * hid).astype(BF16)
    y_ref[...] += jnp.dot(hid, wd_ref[...], preferred_element_type=F32)

    @pl.when(f == pl.num_programs(1) - 1)
    def _finish():
        z = DEEPNORM_ALPHA * x1_ref[...] + y_ref[...]
        y_ref[...] = _layernorm_rows(z, g_ref[...], b_ref[...])


def _mlp(x1b, w_up, w_down, x1, ln_g, ln_b, row0, tokens):
    tm, tf = _MLP_TM, _MLP_TF
    rb0 = row0 // tm
    return pl.pallas_call(
        _mlp_kernel,
        grid=(tokens // tm, D_FF // tf),
        in_specs=[pl.BlockSpec((tm, D_MODEL), lambda i, f: (rb0 + i, 0)),
                  pl.BlockSpec((D_MODEL, tf), lambda i, f: (0, f)),
                  pl.BlockSpec((tf, D_MODEL), lambda i, f: (f, 0)),
                  pl.BlockSpec((tm, D_MODEL), lambda i, f: (rb0 + i, 0)),
                  pl.BlockSpec((1, D_MODEL), lambda i, f: (0, 0)),
                  pl.BlockSpec((1, D_MODEL), lambda i, f: (0, 0))],
        out_specs=pl.BlockSpec((tm, D_MODEL), lambda i, f: (i, 0)),
        out_shape=jax.ShapeDtypeStruct((tokens, D_MODEL), F32),
        compiler_params=pltpu.CompilerParams(dimension_semantics=("parallel", "arbitrary")),
        name="mlp_ln",
    )(x1b, w_up, w_down, x1, ln_g, ln_b)


_PROMPT_ML = dict(rows=256, seq_len=256, heads=4)
_PROMPT_HG = dict(rows=256, seq_len=256, heads=4, direct=8,
                  levels=((32, 8), (128, 32), (256, 128)))
_SAMPLE_ML = dict(rows=64, seq_len=8, heads=1)
_SAMPLE_HG = dict(rows=128, seq_len=8, heads=2, direct=8, levels=())


def kernel(x_prompt, x_sample, state_mlstm_C, state_mlstm_n, state_mlstm_m, state_hgrn_S,
           hg_lb_logits, w_in, b_ig, b_fg, ml_norm_g, hg_norm_g, w_branch_a, w_branch_b, w_out,
           ln1_g, ln1_b, w_up, w_down, ln2_g, ln2_b):
    batch_p, seq_p, _ = x_prompt.shape
    batch_s, seq_s, _ = x_sample.shape
    tok_p, tok_s = batch_p * seq_p, batch_s * seq_s
    total = tok_p + tok_s
    group_p = dict(row0=0, batch=batch_p, seq=seq_p)
    group_s = dict(row0=tok_p, batch=batch_s, seq=seq_s)
    xp = x_prompt.reshape(tok_p, D_MODEL)
    xs = x_sample.reshape(tok_s, D_MODEL)
    lb_logits = hg_lb_logits.astype(F32)
    states_p, states_s = [], []
    for l in range(DEPTH):
        (proj, hg_f, gates, gates_gm), (w_out_b, w_up_b, w_down_b) = _inproj(
            xp.astype(BF16), xs.astype(BF16), jnp.swapaxes(w_in[l], 0, 1),
            (w_out[l], w_up[l], w_down[l]))

        gate_bias = jnp.concatenate([b_ig[l], b_fg[l]]).astype(F32)
        brow = jnp.zeros((1, LANES), F32).at[0, :_N_GATES].set(gate_bias)
        bcol = gate_bias.reshape(_N_GATES, 1)
        ml_g = ml_norm_g[l].reshape(1, ML_V).astype(F32)
        hg_g = hg_norm_g[l].reshape(1, HG_V).astype(F32)

        m_rows = jnp.repeat(state_mlstm_m[l].astype(F32), seq_s, axis=0)
        gates_s = gates[tok_p:].at[:, _N_GATES:_N_GATES + ML_HEADS].set(m_rows)
        rows_s = _SAMPLE_ML["rows"]
        gm_s = gates_gm[tok_p // _PROJ_TM:]
        gm_s = gm_s.reshape(-1, 8, _PROJ_TM // rows_s, rows_s).transpose(0, 2, 1, 3)
        gm_s = gm_s.reshape(tok_s // rows_s, 8, rows_s)
        state_ml = (state_mlstm_C[l].astype(F32),
                    state_mlstm_n[l].astype(F32).reshape(batch_s, ML_HEADS, 1, ML_DK))

        ba_p, c_p, n_p, m_all_p = _mlstm(proj, gates, gates_gm, brow, bcol, ml_g, None,
                                         **group_p, **_PROMPT_ML)
        ba_s, c_s, n_s, m_all_s = _mlstm(proj, gates_s, gm_s, brow, bcol, ml_g, state_ml,
                                         **group_s, **_SAMPLE_ML)
        bb_p, s_p = _hgrn(proj, hg_f, lb_logits, hg_g, None, layer=l, **group_p, **_PROMPT_HG)
        bb_s, s_s = _hgrn(proj, hg_f, lb_logits, hg_g, state_hgrn_S[l].astype(F32), layer=l,
                          **group_s, **_SAMPLE_HG)

        merged_p = _merge(ba_p, w_branch_a[l], bb_p, w_branch_b[l], proj, 0)
        merged_s = _merge(ba_s, w_branch_a[l], bb_s, w_branch_b[l], proj, tok_p)
        x1, x1b = _outproj(merged_p, merged_s, w_out_b, xp, xs,
                           ln1_g[l].reshape(1, D_MODEL), ln1_b[l].reshape(1, D_MODEL))
        mlp_args = (x1b, w_up_b, w_down_b, x1,
                    ln2_g[l].reshape(1, D_MODEL), ln2_b[l].reshape(1, D_MODEL))
        xp = _mlp(*mlp_args, 0, tok_p)
        xs = _mlp(*mlp_args, tok_p, tok_s)

        states_p.append((c_p, n_p.reshape(batch_p, ML_HEADS, ML_DK),
                         m_all_p[:, seq_p - 1::seq_p, 0].T, s_p))
        states_s.append((c_s, n_s.reshape(batch_s, ML_HEADS, ML_DK),
                         m_all_s[:, seq_s - 1::seq_s, 0].T, s_s))
    stack = lambda states, k: jnp.stack([s[k] for s in states])
    return (xp.reshape(batch_p, seq_p, D_MODEL), xs.reshape(batch_s, seq_s, D_MODEL),
            stack(states_p, 0), stack(states_p, 1), stack(states_p, 2), stack(states_p, 3),
            stack(states_s, 0), stack(states_s, 1), stack(states_s, 2), stack(states_s, 3))
```

```python
import functools
import math

import numpy as np
import jax
import jax.numpy as jnp
from jax import lax
from jax.experimental import pallas as pl
from jax.experimental.pallas import tpu as pltpu

F32 = jnp.float32
BF16 = jnp.bfloat16

D_MODEL = 2048
DEPTH = 1
ML_HEADS, ML_DK, ML_DV = 4, 256, 512
HG_HEADS, HG_DK, HG_DV = 8, 128, 256
ML_QK = ML_HEADS * ML_DK
ML_V = ML_HEADS * ML_DV
HG_K = HG_HEADS * HG_DK
HG_V = HG_HEADS * HG_DV
D_FF = 4 * D_MODEL
LN_EPS = 1e-5
DEEPNORM_ALPHA = (2.0 * DEPTH) ** 0.25
ML_K_SCALE = ML_DK ** -0.5
LOG2_E = math.log2(math.e)
LANES = 128

_OFF_ML_Q = 0
_OFF_ML_K = _OFF_ML_Q + ML_QK
_OFF_ML_V = _OFF_ML_K + ML_QK
_OFF_ML_I = _OFF_ML_V + ML_V
_OFF_ML_F = _OFF_ML_I + ML_HEADS
_OFF_ML_O = _OFF_ML_F + ML_HEADS
_OFF_HG_Q = _OFF_ML_O + ML_V
_OFF_HG_F = _OFF_HG_Q + HG_K
_OFF_HG_I = _OFF_HG_F + HG_K
_OFF_HG_G = _OFF_HG_I + HG_V
_OFF_GATE_A = _OFF_HG_G + HG_V
_OFF_GATE_B = _OFF_GATE_A + D_MODEL
D_IN = _OFF_GATE_B + D_MODEL

_N_GATES = 2 * ML_HEADS
_P_ML_Q = _OFF_ML_Q
_P_ML_K = _OFF_ML_K
_P_ML_V = _OFF_ML_V
_P_ML_O = _OFF_ML_O - _N_GATES
_P_HG_Q = _OFF_HG_Q - _N_GATES
_P_HG_F = _OFF_HG_F - _N_GATES
_P_HG_I = _OFF_HG_I - _N_GATES
_P_HG_G = _OFF_HG_G - _N_GATES
_P_GATE_A = _OFF_GATE_A - _N_GATES
_P_GATE_B = _OFF_GATE_B - _N_GATES
_P_WIDTH = D_IN - _N_GATES

_PROJ_TM, _PROJ_TN = 1024, 1024
_MERGE_TM, _MERGE_TN = 1024, 512
_OUT_TM = 512
_MLP_TM, _MLP_TF = 1024, 512


def _sigmoid(x):
    return 1.0 / (1.0 + jnp.exp(-x))


def _log_sigmoid(x):
    return jnp.minimum(x, 0.0) - jnp.log1p(jnp.exp(-jnp.abs(x)))


def _split2(x):
    hi = x.astype(BF16)
    lo = (x - hi.astype(F32)).astype(BF16)
    return hi, lo


def _dot01_right(t01, x):
    hi, lo = _split2(x)
    return (jnp.dot(t01, hi, preferred_element_type=F32)
            + jnp.dot(t01, lo, preferred_element_type=F32))


def _dot01_left(x, t01):
    hi, lo = _split2(x)
    return (jnp.dot(hi, t01, preferred_element_type=F32)
            + jnp.dot(lo, t01, preferred_element_type=F32))


def _dot_nt(a, b):
    return lax.dot_general(a, b, (((1,), (1,)), ((), ())), preferred_element_type=F32)


def _dot_tn(a, b):
    return lax.dot_general(a, b, (((0,), (0,)), ((), ())), preferred_element_type=F32)


def _seq_masks(rows, seq_len):
    ri = lax.broadcasted_iota(jnp.int32, (rows, rows), 0)
    ci = lax.broadcasted_iota(jnp.int32, (rows, rows), 1)
    lower = jnp.where(ci <= ri, 1.0, 0.0)
    upper = jnp.where(ri <= ci, 1.0, 0.0)
    if seq_len == rows:
        return jnp.ones((rows, rows), F32), lower, upper
    shift = seq_len.bit_length() - 1
    same = jnp.where((ri >> shift) == (ci >> shift), 1.0, 0.0)
    return same, same * lower, same * upper


def _inproj_kernel(xp_ref, xs_ref, wt_hbm, *refs, n_f32, n_gate_tile, prompt_tiles, cast_steps):
    n_cast = (len(refs) - 9) // 2
    cast_in = refs[:n_cast]
    o_ref, f_ref, g_ref, gr_ref = refs[n_cast:n_cast + 4]
    cast_out = refs[n_cast + 4:2 * n_cast + 4]
    wbuf, wbf_sc, wg_sc, sem, gsem = refs[2 * n_cast + 4:]
    n = pl.program_id(0)
    m = pl.program_id(1)
    tn = wbf_sc.shape[1]

    @pl.when(m < cast_steps)
    def _cast_other_weights():
        for src, dst in zip(cast_in, cast_out):
            dst[...] = src[...].astype(BF16)

    def tile_copy(tile):
        start = pl.multiple_of(tile * tn + jnp.where(tile >= n_gate_tile, _N_GATES, 0), _N_GATES)
        return pltpu.make_async_copy(wt_hbm.at[pl.ds(start, tn), :], wbuf, sem.at[0])

    def gate_copy():
        return pltpu.make_async_copy(wt_hbm.at[pl.ds(_OFF_ML_I, _N_GATES), :],
                                     wg_sc.at[pl.ds(0, _N_GATES), :], gsem.at[0])

    @pl.when(m == 0)
    def _next_weight_tile():
        @pl.when(n == 0)
        def _first():
            tile_copy(0).start()
            wg_sc[_N_GATES:, :] = jnp.zeros((LANES - _N_GATES, wg_sc.shape[1]), F32)
            gate_copy().start()
            gate_copy().wait()

        tile_copy(n).wait()
        wbf_sc[...] = wbuf[...].T.astype(BF16)

        @pl.when(n + 1 < pl.num_programs(0))
        def _prefetch():
            tile_copy(n + 1).start()

    x = jnp.where(m < prompt_tiles, xp_ref[...], xs_ref[...])
    acc = jnp.dot(x, wbf_sc[...], preferred_element_type=F32)
    o_ref[...] = acc.astype(BF16)

    @pl.when(n == n_f32)
    def _f32_outputs():
        f_ref[...] = acc
        wg = wg_sc[...].astype(BF16)
        g_ref[...] = _dot_nt(x, wg)
        gr_ref[0] = _dot_nt(wg, x)[0:_N_GATES, :]


def _inproj(xp_b, xs_b, w_t, cast_weights):
    k = xp_b.shape[1]
    tm, tn = _PROJ_TM, _PROJ_TN
    p_tiles = xp_b.shape[0] // tm
    n_m = p_tiles + xs_b.shape[0] // tm
    tokens = n_m * tm
    n_n = _P_WIDTH // tn
    n_f32 = _P_HG_F // tn
    cast_steps = min(n_m, 8)

    def parked(n, m):
        return jnp.where(n < n_f32, 0, jnp.where(n == n_f32, m, n_m - 1))

    def cast_spec(w):
        return pl.BlockSpec((w.shape[0] // n_n, w.shape[1] // cast_steps),
                            lambda n, m: (n, jnp.minimum(m, cast_steps - 1)))

    cast_specs = [cast_spec(w) for w in cast_weights]
    outs = pl.pallas_call(
        functools.partial(_inproj_kernel, n_f32=n_f32, n_gate_tile=_OFF_ML_I // tn,
                          prompt_tiles=p_tiles, cast_steps=cast_steps),
        grid=(n_n, n_m),
        in_specs=[pl.BlockSpec((tm, k), lambda n, m: (jnp.minimum(m, p_tiles - 1), 0)),
                  pl.BlockSpec((tm, k), lambda n, m: (jnp.maximum(m - p_tiles, 0), 0)),
                  pl.BlockSpec(memory_space=pl.ANY),
                  *cast_specs],
        out_specs=[pl.BlockSpec((tm, tn), lambda n, m: (m, n)),
                   pl.BlockSpec((tm, HG_K), lambda n, m: (parked(n, m), 0)),
                   pl.BlockSpec((tm, LANES), lambda n, m: (parked(n, m), 0)),
                   pl.BlockSpec((1, 8, tm), lambda n, m: (parked(n, m), 0, 0)),
                   *cast_specs],
        out_shape=[jax.ShapeDtypeStruct((tokens, _P_WIDTH), BF16),
                   jax.ShapeDtypeStruct((tokens, HG_K), F32),
                   jax.ShapeDtypeStruct((tokens, LANES), F32),
                   jax.ShapeDtypeStruct((n_m, 8, tm), F32),
                   *[jax.ShapeDtypeStruct(w.shape, BF16) for w in cast_weights]],
        scratch_shapes=[pltpu.VMEM((tn, k), F32),
                        pltpu.VMEM((k, tn), BF16),
                        pltpu.VMEM((LANES, k), F32),
                        pltpu.SemaphoreType.DMA((1,)),
                        pltpu.SemaphoreType.DMA((1,))],
        compiler_params=pltpu.CompilerParams(dimension_semantics=("arbitrary", "arbitrary")),
        name="in_proj",
    )(xp_b, xs_b, w_t, *cast_weights)
    return outs[:4], outs[4:]


def _mlstm_kernel(*refs, rows, seq_len, heads, zero_init, head0=None):
    if zero_init:
        (q_ref, k_ref, v_ref, og_ref, gc_ref, gr_ref, brow_ref, bcol_ref, ng_ref,
         ba_ref, c_ref, n_ref, mrow_ref, m_sc) = refs
        c0_ref, n0_ref = c_ref, n_ref
    else:
        (q_ref, k_ref, v_ref, og_ref, gc_ref, gr_ref, brow_ref, bcol_ref, ng_ref, c0_ref, n0_ref,
         ba_ref, c_ref, n_ref, mrow_ref) = refs
    nseq = rows // seq_len
    if head0 is None:
        head0 = pl.program_id(1) * heads

    if zero_init:
        @pl.when(pl.program_id(2) == 0)
        def _init():
            c_ref[...] = jnp.zeros(c_ref.shape, F32)
            n_ref[...] = jnp.zeros(n_ref.shape, F32)
            m_sc[...] = jnp.zeros(m_sc.shape, F32)

    same, lower, upper = _seq_masks(rows, seq_len)
    causal = lower > 0.5
    lane = lax.broadcasted_iota(jnp.int32, (rows, LANES), 1)
    sub = lax.broadcasted_iota(jnp.int32, (8, rows), 0)

    def sel_lane(x, idx):
        return jnp.sum(jnp.where(lane == idx, x, 0.0), axis=1, keepdims=True)

    def sel_sub(x, idx):
        return jnp.sum(jnp.where(sub == idx, x, 0.0), axis=0, keepdims=True)

    gc_raw = gc_ref[...]
    gc = gc_raw + brow_ref[...]
    lf_cols = _log_sigmoid(gc)
    b_cols = _dot01_right(lower.astype(BF16), lf_cols)
    gr = gr_ref[0] + bcol_ref[...]
    b_rows = _dot01_left(_log_sigmoid(gr), upper.astype(BF16))
    if nseq > 1:
        b_ends = _dot01_right(same.astype(BF16), lf_cols)
        ci = lax.broadcasted_iota(jnp.int32, (rows, rows), 1)
        last = (same * jnp.where((ci & (seq_len - 1)) == seq_len - 1, 1.0, 0.0)).astype(BF16)
        shift = seq_len.bit_length() - 1
        row_seq = lax.broadcasted_iota(jnp.int32, (rows, 1), 0) >> shift

    for hh in range(heads):
        head = head0 + hh
        ks = slice(hh * ML_DK, (hh + 1) * ML_DK)
        vs = slice(hh * ML_DV, (hh + 1) * ML_DV)
        ig_col = sel_lane(gc, head)
        b_col = sel_lane(b_cols, head + ML_HEADS)
        if zero_init:
            m_prev = jnp.broadcast_to(m_sc[hh:hh + 1, 0:1], (rows, 1))
        else:
            m_prev = sel_lane(gc_raw, head + 2 * ML_HEADS)
        ig_row = sel_sub(gr, head)
        b_row = sel_sub(b_rows, head + ML_HEADS)

        logd = jnp.where(causal, (b_col - b_row) + ig_row, -jnp.inf)
        m_t = jnp.maximum(b_col + m_prev, jnp.max(logd, axis=1, keepdims=True))
        d = jnp.exp(logd - m_t)
        w_inter = jnp.exp(b_col + m_prev - m_t)

        qb = q_ref[:, ks].astype(BF16)
        kb = k_ref[:, ks].astype(BF16)
        vb = v_ref[:, vs].astype(BF16)
        s = _dot_nt(qb, kb) * (d * ML_K_SCALE)
        num = jnp.dot(s.astype(BF16), vb, preferred_element_type=F32)
        den = jnp.sum(s, axis=1, keepdims=True)

        if nseq == 1:
            b_end = b_col[rows - 1:rows, :]
            m_new = m_t[rows - 1:rows, :]
        else:
            b_end = sel_lane(b_ends, head + ML_HEADS)
            m_new = _dot01_right(last, jnp.broadcast_to(m_t, (rows, LANES)))[:, 0:1]
        w_end = jnp.exp(b_end - b_col + ig_col - m_new)
        decay = jnp.exp(b_end + m_prev - m_new)

        qf = qb.astype(F32)
        kw = (w_end * ML_K_SCALE) * kb.astype(F32)

        if nseq == 1:
            c_prev = c0_ref[0, hh]
            n_prev = n0_ref[0, hh]
            q_c = jnp.dot(qb, c_prev.astype(BF16), preferred_element_type=F32)
            q_n = jnp.sum(qf * n_prev, axis=1, keepdims=True)
            dec = decay[0:1, :]
            c_ref[0, hh] = dec * c_prev + _dot_tn(kw.astype(BF16), vb)
            n_ref[0, hh] = dec * n_prev + jnp.sum(kw, axis=0, keepdims=True)
        else:
            q_c = jnp.zeros((rows, ML_DV), F32)
            q_n = jnp.zeros((rows, 1), F32)
            for j in range(nseq):
                in_seq = row_seq == j
                c_prev = c0_ref[j, hh]
                n_prev = n0_ref[j, hh]
                q_c = jnp.where(
                    in_seq, jnp.dot(qb, c_prev.astype(BF16), preferred_element_type=F32), q_c)
                q_n = jnp.where(in_seq, jnp.sum(qf * n_prev, axis=1, keepdims=True), q_n)
                kw_j = jnp.where(in_seq, kw, 0.0)
                dec = decay[j * seq_len:j * seq_len + 1, :]
                c_ref[j, hh] = dec * c_prev + _dot_tn(kw_j.astype(BF16), vb)
                n_ref[j, hh] = dec * n_prev + jnp.sum(kw_j, axis=0, keepdims=True)

        num = num + w_inter * q_c
        den = den + w_inter * q_n
        h_out = num / jnp.maximum(jnp.abs(den), jnp.exp(-m_t))
        mu = jnp.mean(h_out, axis=1, keepdims=True)
        xc = h_out - mu
        var = jnp.mean(xc * xc, axis=1, keepdims=True)
        hn = xc * lax.rsqrt(var + LN_EPS) * ng_ref[:, vs]
        ba_ref[:, vs] = (hn * _sigmoid(og_ref[:, vs].astype(F32))).astype(ba_ref.dtype)
        mrow_ref[hh] = jnp.broadcast_to(m_t, (rows, LANES))
        if zero_init:
            m_sc[hh:hh + 1, :] = jnp.broadcast_to(m_new, (1, LANES))


def _mlstm_parts(proj, gates_col, gates_row, brow, bcol, norm_g, state, *, row0, batch, seq, rows,
                 seq_len, heads, index=None):
    tokens = batch * seq
    zero_init = state is None
    nseq = rows // seq_len
    nchunk = seq // seq_len if nseq == 1 else 1
    ngroup = tokens // (rows * nchunk)
    bq, bv = heads * ML_DK, heads * ML_DV
    rb0 = row0 // rows
    per_row_tile = gates_row.shape[2] // rows
    head0 = None
    if index is None:
        index = lambda g, h, c: (g * nchunk + c, h, g)
    else:
        assert heads == ML_HEADS
        head0 = 0
    row = lambda *ids: index(*ids)[0]
    hblk = lambda *ids: index(*ids)[1]
    sblk = lambda *ids: index(*ids)[2]

    in_specs = [
        pl.BlockSpec((rows, bq), lambda *ids: (rb0 + row(*ids), _P_ML_Q // bq + hblk(*ids))),
        pl.BlockSpec((rows, bq), lambda *ids: (rb0 + row(*ids), _P_ML_K // bq + hblk(*ids))),
        pl.BlockSpec((rows, bv), lambda *ids: (rb0 + row(*ids), _P_ML_V // bv + hblk(*ids))),
        pl.BlockSpec((rows, bv), lambda *ids: (rb0 + row(*ids), _P_ML_O // bv + hblk(*ids))),
        pl.BlockSpec((rows, LANES), lambda *ids: (row(*ids), 0)),
        pl.BlockSpec((1, 8, rows),
                     lambda *ids: (row(*ids) // per_row_tile, 0, row(*ids) % per_row_tile)),
        pl.BlockSpec((1, LANES), lambda *ids: (0, 0)),
        pl.BlockSpec((8, 1), lambda *ids: (0, 0)),
        pl.BlockSpec((1, bv), lambda *ids: (0, hblk(*ids))),
    ]
    args = [proj, proj, proj, proj, gates_col, gates_row, brow, bcol, norm_g]
    state_specs = [
        pl.BlockSpec((nseq, heads, ML_DK, ML_DV), lambda *ids: (sblk(*ids), hblk(*ids), 0, 0)),
        pl.BlockSpec((nseq, heads, 1, ML_DK), lambda *ids: (sblk(*ids), hblk(*ids), 0, 0)),
    ]
    scratch = []
    if zero_init:
        scratch = [pltpu.VMEM((8, LANES), F32)]
    else:
        in_specs += state_specs
        args += [state[0], state[1]]
    out_specs = [
        pl.BlockSpec((rows, bv), lambda *ids: (row(*ids), hblk(*ids))),
        *state_specs,
        pl.BlockSpec((heads, rows, LANES), lambda *ids: (hblk(*ids), row(*ids), 0)),
    ]
    out_shape = [
        jax.ShapeDtypeStruct((tokens, ML_V), BF16),
        jax.ShapeDtypeStruct((batch, ML_HEADS, ML_DK, ML_DV), F32),
        jax.ShapeDtypeStruct((batch, ML_HEADS, 1, ML_DK), F32),
        jax.ShapeDtypeStruct((ML_HEADS, tokens, LANES), F32),
    ]
    return dict(
        kernel=functools.partial(_mlstm_kernel, rows=rows, seq_len=seq_len, heads=heads,
                                 zero_init=zero_init, head0=head0),
        grid=(ngroup, ML_HEADS // heads, nchunk),
        in_specs=in_specs, args=args, out_specs=out_specs, out_shape=out_shape, scratch=scratch)


def _run_parts(parts, name):
    n_in = [len(p["in_specs"]) for p in parts]
    n_out = [len(p["out_specs"]) for p in parts]
    n_scr = [len(p["scratch"]) for p in parts]

    def kernel(*refs):
        ins = refs[:sum(n_in)]
        outs = refs[sum(n_in):sum(n_in) + sum(n_out)]
        scr = refs[sum(n_in) + sum(n_out):]
        i = o = s = 0
        for p, ni, no, ns in zip(parts, n_in, n_out, n_scr):
            p["kernel"](*ins[i:i + ni], *outs[o:o + no], *scr[s:s + ns])
            i, o, s = i + ni, o + no, s + ns

    outs = pl.pallas_call(
        kernel,
        grid=parts[0]["grid"],
        in_specs=[s for p in parts for s in p["in_specs"]],
        out_specs=[s for p in parts for s in p["out_specs"]],
        out_shape=[s for p in parts for s in p["out_shape"]],
        scratch_shapes=[s for p in parts for s in p["scratch"]],
        compiler_params=pltpu.CompilerParams(
            dimension_semantics=("parallel", "parallel", "arbitrary")),
        name=name,
    )(*[a for p in parts for a in p["args"]])
    split, o = [], 0
    for no in n_out:
        split.append(outs[o:o + no])
        o += no
    return split


def _hgrn_level_ids(rows, seq_len, direct, levels):
    t = np.arange(rows)[:, None]
    s = np.arange(rows)[None, :]
    ids = np.full((rows, rows), -1, np.int32)
    count = ((t // direct == s // direct) & (s <= t)).astype(np.int32)
    for idx, (block, sub_size) in enumerate(levels):
        owned = (t // block == s // block) & ((s % block) // sub_size < (t % block) // sub_size)
        ids[owned] = idx
        count += owned
    wanted = (t // seq_len == s // seq_len) & (s <= t)
    assert np.array_equal(count, wanted.astype(np.int32)), (rows, seq_len, direct, levels)
    return ids


def _hgrn_intra(q, kin, g2, gk2, level_ids, rows, direct, levels):
    ngroups = max(rows // LANES, 1)
    rowi = lax.broadcasted_iota(jnp.int32, (direct, LANES), 0)
    lanei = lax.broadcasted_iota(jnp.int32, (direct, LANES), 1)
    keep = [jnp.where(lanei == s, rowi, -1) >= s for s in range(direct)]
    zero_group = jnp.zeros((direct, LANES), F32)
    prods = []
    for blk in range(rows // direct):
        r0 = blk * direct
        qb = q[r0:r0 + direct]
        gb = g2[r0:r0 + direct]
        gkb = gk2[r0:r0 + direct]
        for s in range(direct):
            prods.append(qb * jnp.exp2(gb - gkb[s:s + 1, :]))
    sums = jnp.dot(jnp.concatenate(prods, axis=0).astype(BF16), jnp.ones((HG_DK, LANES), BF16),
                   preferred_element_type=F32)
    panels = []
    for blk in range(rows // direct):
        r0 = blk * direct
        ag = zero_group
        for s in range(direct):
            p0 = (blk * direct + s) * direct
            ag = jnp.where(keep[s], sums[p0:p0 + direct], ag)
        lane0 = r0 % LANES
        if lane0:
            ag = pltpu.roll(ag, lane0, axis=1)
        grp = r0 // LANES
        pieces = [zero_group] * grp + [ag] + [zero_group] * (ngroups - grp - 1)
        panels.append(pieces[0] if ngroups == 1 else jnp.concatenate(pieces, axis=1))
    a = panels[0] if len(panels) == 1 else jnp.concatenate(panels, axis=0)
    if rows < LANES:
        a = a[:, :rows]

    if levels:
        rowid = lax.broadcasted_iota(jnp.int32, (rows, 1), 0)
    for idx, (block, sub_size) in enumerate(levels):
        sshift = sub_size.bit_length() - 1
        row_sub = (rowid & (block - 1)) >> sshift
        q_parts, k_parts = [], []
        for j in range(1, block // sub_size):
            refs = []
            for b0 in range(0, rows, block):
                r = b0 + j * sub_size - 1
                refs.append(jnp.broadcast_to(g2[r:r + 1, :], (block, HG_DK)))
            g_ref = refs[0] if len(refs) == 1 else jnp.concatenate(refs, axis=0)
            e = jnp.exp2(-jnp.abs(g2 - g_ref))
            q_parts.append(jnp.where(row_sub == j, q * e, 0.0).astype(BF16))
            k_parts.append((kin * e).astype(BF16))
        qcat = q_parts[0] if len(q_parts) == 1 else jnp.concatenate(q_parts, axis=1)
        kcat = k_parts[0] if len(k_parts) == 1 else jnp.concatenate(k_parts, axis=1)
        a = jnp.where(level_ids == idx, _dot_nt(qcat, kcat), a)
    return a


def _hgrn_kernel(*refs, layer, rows, seq_len, heads, direct, levels, zero_init):
    if zero_init:
        q_ref, f_ref, i_ref, g_ref, lbl_ref, ng_ref, lvl_ref, bb_ref, s_ref = refs
        s0_ref = s_ref
    else:
        q_ref, f_ref, i_ref, g_ref, lbl_ref, ng_ref, lvl_ref, s0_ref, bb_ref, s_ref = refs
    nseq = rows // seq_len

    if zero_init:
        @pl.when(pl.program_id(2) == 0)
        def _init():
            s_ref[...] = jnp.zeros(s_ref.shape, F32)

    same, lower, _ = _seq_masks(rows, seq_len)
    lower_b = lower.astype(BF16)
    same_b = same.astype(BF16)
    level_ids = lvl_ref[...]
    if nseq > 1:
        shift = seq_len.bit_length() - 1
        row_seq = lax.broadcasted_iota(jnp.int32, (rows, 1), 0) >> shift

    def decay_cols(row):
        col = jnp.broadcast_to(jnp.exp2(row), (HG_DK, HG_DK)).T
        return jnp.concatenate([col] * (HG_DV // HG_DK), axis=1)

    for hh in range(heads):
        ks = slice(hh * HG_DK, (hh + 1) * HG_DK)
        vs = slice(hh * HG_DV, (hh + 1) * HG_DV)
        lg = lbl_ref[:, ks]
        ex = jnp.exp(lg - jnp.max(lg, axis=0, keepdims=True))
        lb = (jnp.sum(ex[0:layer + 1, :], axis=0, keepdims=True)
              / jnp.sum(ex, axis=0, keepdims=True))

        f = lb + (1.0 - lb) * _sigmoid(f_ref[:, ks])
        kin = 1.0 - f
        lf2 = jnp.log(f) * LOG2_E
        g2 = _dot01_right(lower_b, lf2)
        gk2 = g2 - jnp.log(kin) * LOG2_E
        q = q_ref[:, ks].astype(F32)
        if nseq == 1:
            g_end = jnp.broadcast_to(g2[rows - 1:rows, :], (rows, HG_DK))
        else:
            g_end = _dot01_right(same_b, lf2)
        qg = (q * jnp.exp2(g2)).astype(BF16)
        kg = kin * jnp.exp2(g_end - g2)
        ib = i_ref[:, vs].astype(BF16)

        a = _hgrn_intra(q, kin, g2, gk2, level_ids, rows, direct, levels)
        o = jnp.dot(a.astype(BF16), ib, preferred_element_type=F32)

        if nseq == 1:
            s_prev = s0_ref[0, hh]
            o = o + jnp.dot(qg, s_prev.astype(BF16), preferred_element_type=F32)
            s_ref[0, hh] = decay_cols(g_end[0:1, :]) * s_prev + _dot_tn(kg.astype(BF16), ib)
        else:
            for j in range(nseq):
                in_seq = row_seq == j
                s_prev = s0_ref[j, hh]
                o_j = jnp.dot(qg, s_prev.astype(BF16), preferred_element_type=F32)
                o = o + jnp.where(in_seq, o_j, 0.0)
                kg_j = jnp.where(in_seq, kg, 0.0).astype(BF16)
                s_ref[j, hh] = (decay_cols(g_end[j * seq_len:j * seq_len + 1, :]) * s_prev
                                + _dot_tn(kg_j, ib))

        o = o * lax.rsqrt(jnp.mean(o * o, axis=1, keepdims=True) + LN_EPS)
        gate = g_ref[:, vs].astype(F32)
        bb_ref[:, vs] = (o * ng_ref[:, vs] * (gate * _sigmoid(gate))).astype(bb_ref.dtype)


def _hgrn_parts(proj, hg_f, lb_logits, norm_g, s0, *, layer, row0, batch, seq, rows, seq_len, heads,
                direct, levels, index=None):
    tokens = batch * seq
    zero_init = s0 is None
    nseq = rows // seq_len
    nchunk = seq // seq_len if nseq == 1 else 1
    ngroup = tokens // (rows * nchunk)
    rb0 = row0 // rows
    if index is None:
        index = lambda g, h, c: (g * nchunk + c, h, g)
    row = lambda *ids: index(*ids)[0]
    hblk = lambda *ids: index(*ids)[1]
    sblk = lambda *ids: index(*ids)[2]

    wk, wv = heads * HG_DK, heads * HG_DV
    in_specs = [
        pl.BlockSpec((rows, wk), lambda *ids: (rb0 + row(*ids), _P_HG_Q // wk + hblk(*ids))),
        pl.BlockSpec((rows, wk), lambda *ids: (rb0 + row(*ids), hblk(*ids))),
        pl.BlockSpec((rows, wv), lambda *ids: (rb0 + row(*ids), _P_HG_I // wv + hblk(*ids))),
        pl.BlockSpec((rows, wv), lambda *ids: (rb0 + row(*ids), _P_HG_G // wv + hblk(*ids))),
        pl.BlockSpec((DEPTH + 1, wk), lambda *ids: (0, hblk(*ids))),
        pl.BlockSpec((1, wv), lambda *ids: (0, hblk(*ids))),
        pl.BlockSpec((rows, rows), lambda *ids: (0, 0)),
    ]
    level_ids = jnp.asarray(_hgrn_level_ids(rows, seq_len, direct, levels))
    args = [proj, hg_f, proj, proj, lb_logits, norm_g, level_ids]
    state_spec = pl.BlockSpec((nseq, heads, HG_DK, HG_DV),
                              lambda *ids: (sblk(*ids), hblk(*ids), 0, 0))
    if not zero_init:
        in_specs.append(state_spec)
        args.append(s0)
    return dict(
        kernel=functools.partial(_hgrn_kernel, layer=layer, rows=rows, seq_len=seq_len, heads=heads,
                                 direct=direct, levels=levels, zero_init=zero_init),
        grid=(ngroup, HG_HEADS // heads, nchunk),
        in_specs=in_specs, args=args,
        out_specs=[pl.BlockSpec((rows, wv), lambda *ids: (row(*ids), hblk(*ids))), state_spec],
        out_shape=[jax.ShapeDtypeStruct((tokens, HG_V), BF16),
                   jax.ShapeDtypeStruct((batch, HG_HEADS, HG_DK, HG_DV), F32)],
        scratch=[])


def _merge_kernel(ba_ref, wa_ref, bb_ref, wb_ref, ga_ref, gb_ref, o_ref, wa_sc, wb_sc):
    @pl.when(pl.program_id(1) == 0)
    def _cast_weights():
        wa_sc[...] = wa_ref[...].astype(BF16)
        wb_sc[...] = wb_ref[...].astype(BF16)

    ya = jnp.dot(ba_ref[...], wa_sc[...], preferred_element_type=F32)
    yb = jnp.dot(bb_ref[...], wb_sc[...], preferred_element_type=F32)
    ga = _sigmoid(ga_ref[...].astype(F32))
    gb = _sigmoid(gb_ref[...].astype(F32))
    o_ref[...] = (ga * ya + gb * yb).astype(o_ref.dtype)


def _merge(branch_a, w_a, branch_b, w_b, proj, row0):
    m = branch_a.shape[0]
    tm, tn = _MERGE_TM, _MERGE_TN
    rb0 = row0 // tm
    ga_blk = _P_GATE_A // tn
    gb_blk = _P_GATE_B // tn
    return pl.pallas_call(
        _merge_kernel,
        grid=(D_MODEL // tn, m // tm),
        in_specs=[pl.BlockSpec((tm, ML_V), lambda j, i: (i, 0)),
                  pl.BlockSpec((ML_V, tn), lambda j, i: (0, j)),
                  pl.BlockSpec((tm, HG_V), lambda j, i: (i, 0)),
                  pl.BlockSpec((HG_V, tn), lambda j, i: (0, j)),
                  pl.BlockSpec((tm, tn), lambda j, i: (rb0 + i, ga_blk + j)),
                  pl.BlockSpec((tm, tn), lambda j, i: (rb0 + i, gb_blk + j))],
        out_specs=pl.BlockSpec((tm, tn), lambda j, i: (i, j)),
        out_shape=jax.ShapeDtypeStruct((m, D_MODEL), BF16),
        scratch_shapes=[pltpu.VMEM((ML_V, tn), BF16), pltpu.VMEM((HG_V, tn), BF16)],
        compiler_params=pltpu.CompilerParams(dimension_semantics=("parallel", "arbitrary")),
        name="merge",
    )(branch_a, w_a, branch_b, w_b, proj, proj)


def _layernorm_rows(z, g, b):
    mu = jnp.mean(z, axis=1, keepdims=True)
    zc = z - mu
    var = jnp.mean(zc * zc, axis=1, keepdims=True)
    return zc * lax.rsqrt(var + LN_EPS) * g + b


def _outproj_kernel(mg_ref, w_ref, x_ref, g_ref, b_ref, x1_ref, x1b_ref):
    mix = jnp.dot(mg_ref[...], w_ref[...], preferred_element_type=F32)
    x1 = _layernorm_rows(DEEPNORM_ALPHA * x_ref[...] + mix, g_ref[...], b_ref[...])
    x1_ref[...] = x1
    x1b_ref[...] = x1.astype(BF16)


def _outproj(merged, w_out, x, ln_g, ln_b):
    tm = _OUT_TM
    m = x.shape[0]
    row = lambda i: (i, 0)
    const = lambda i: (0, 0)
    return pl.pallas_call(
        _outproj_kernel,
        grid=(m // tm,),
        in_specs=[pl.BlockSpec((tm, D_MODEL), row),
                  pl.BlockSpec((D_MODEL, D_MODEL), const),
                  pl.BlockSpec((tm, D_MODEL), row),
                  pl.BlockSpec((1, D_MODEL), const),
                  pl.BlockSpec((1, D_MODEL), const)],
        out_specs=[pl.BlockSpec((tm, D_MODEL), row), pl.BlockSpec((tm, D_MODEL), row)],
        out_shape=[jax.ShapeDtypeStruct((m, D_MODEL), F32),
                   jax.ShapeDtypeStruct((m, D_MODEL), BF16)],
        compiler_params=pltpu.CompilerParams(dimension_semantics=("parallel",)),
        name="out_proj_ln",
    )(merged, w_out, x, ln_g, ln_b)


def _mlp_kernel(x1b_ref, wu_ref, wd_ref, x1_ref, g_ref, b_ref, y_ref):
    f = pl.program_id(1)

    @pl.when(f == 0)
    def _init():
        y_ref[...] = jnp.zeros(y_ref.shape, F32)

    hid = jnp.maximum(jnp.dot(x1b_ref[...], wu_ref[...], preferred_element_type=F32), 0.0)
    hid = (hid * hid).astype(BF16)
    y_ref[...] += jnp.dot(hid, wd_ref[...], preferred_element_type=F32)

    @pl.when(f == pl.num_programs(1) - 1)
    def _finish():
        z = DEEPNORM_ALPHA * x1_ref[...] + y_ref[...]
        y_ref[...] = _layernorm_rows(z, g_ref[...], b_ref[...])


def _mlp(x1b, w_up, w_down, x1, ln_g, ln_b):
    tm, tf = _MLP_TM, _MLP_TF
    tokens = x1.shape[0]
    return pl.pallas_call(
        _mlp_kernel,
        grid=(tokens // tm, D_FF // tf),
        in_specs=[pl.BlockSpec((tm, D_MODEL), lambda i, f: (i, 0)),
                  pl.BlockSpec((D_MODEL, tf), lambda i, f: (0, f)),
                  pl.BlockSpec((tf, D_MODEL), lambda i, f: (f, 0)),
                  pl.BlockSpec((tm, D_MODEL), lambda i, f: (i, 0)),
                  pl.BlockSpec((1, D_MODEL), lambda i, f: (0, 0)),
                  pl.BlockSpec((1, D_MODEL), lambda i, f: (0, 0))],
        out_specs=pl.BlockSpec((tm, D_MODEL), lambda i, f: (i, 0)),
        out_shape=jax.ShapeDtypeStruct((tokens, D_MODEL), F32),
        compiler_params=pltpu.CompilerParams(dimension_semantics=("parallel", "arbitrary")),
        name="mlp_ln",
    )(x1b, w_up, w_down, x1, ln_g, ln_b)


_PROMPT_ML = dict(rows=256, seq_len=256, heads=4)
_PROMPT_HG = dict(rows=256, seq_len=256, heads=4, direct=8,
                  levels=((32, 8), (128, 32), (256, 128)))
_SAMPLE_ML = dict(rows=16, seq_len=8, heads=4)
_SAMPLE_HG = dict(rows=32, seq_len=8, heads=8, direct=8, levels=())


def kernel(x_prompt, x_sample, state_mlstm_C, state_mlstm_n, state_mlstm_m, state_hgrn_S,
           hg_lb_logits, w_in, b_ig, b_fg, ml_norm_g, hg_norm_g, w_branch_a, w_branch_b, w_out,
           ln1_g, ln1_b, w_up, w_down, ln2_g, ln2_b):
    batch_p, seq_p, _ = x_prompt.shape
    batch_s, seq_s, _ = x_sample.shape
    tok_p, tok_s = batch_p * seq_p, batch_s * seq_s
    total = tok_p + tok_s
    group_p = dict(row0=0, batch=batch_p, seq=seq_p)
    group_s = dict(row0=tok_p, batch=batch_s, seq=seq_s)
    xp = x_prompt.reshape(tok_p, D_MODEL)
    xs = x_sample.reshape(tok_s, D_MODEL)
    lb_logits = hg_lb_logits.astype(F32)
    states_p, states_s = [], []
    for l in range(DEPTH):
        (proj, hg_f, gates, gates_gm), (w_out_b, w_up_b, w_down_b) = _inproj(
            xp.astype(BF16), xs.astype(BF16), jnp.swapaxes(w_in[l], 0, 1),
            (w_out[l], w_up[l], w_down[l]))

        gate_bias = jnp.concatenate([b_ig[l], b_fg[l]]).astype(F32)
        brow = jnp.zeros((1, LANES), F32).at[0, :_N_GATES].set(gate_bias)
        bcol = gate_bias.reshape(_N_GATES, 1)
        ml_g = ml_norm_g[l].reshape(1, ML_V).astype(F32)
        hg_g = hg_norm_g[l].reshape(1, HG_V).astype(F32)

        m_rows = jnp.repeat(state_mlstm_m[l].astype(F32), seq_s, axis=0)
        gates_s = gates[tok_p:].at[:, _N_GATES:_N_GATES + ML_HEADS].set(m_rows)
        rows_s = _SAMPLE_ML["rows"]
        gm_s = gates_gm[tok_p // _PROJ_TM:]
        gm_s = gm_s.reshape(-1, 8, _PROJ_TM // rows_s, rows_s).transpose(0, 2, 1, 3)
        gm_s = gm_s.reshape(tok_s // rows_s, 8, rows_s)
        state_ml = (state_mlstm_C[l].astype(F32),
                    state_mlstm_n[l].astype(F32).reshape(batch_s, ML_HEADS, 1, ML_DK))
        ln1 = (ln1_g[l].reshape(1, D_MODEL), ln1_b[l].reshape(1, D_MODEL))
        ln2 = (ln2_g[l].reshape(1, D_MODEL), ln2_b[l].reshape(1, D_MODEL))

        ml_p = _mlstm_parts(proj, gates, gates_gm, brow, bcol, ml_g, None, **group_p, **_PROMPT_ML)
        hg_p = _hgrn_parts(proj, hg_f, lb_logits, hg_g, None, layer=l, **group_p, **_PROMPT_HG)

        def flat_step(grid):
            def index(g, h, c):
                step = (g * grid[1] + h) * grid[2] + c
                return step, 0, step
            return index

        ml_s = _mlstm_parts(proj, gates_s, gm_s, brow, bcol, ml_g, state_ml, **group_s,
                            **_SAMPLE_ML, index=flat_step(hg_p["grid"]))
        hg_s = _hgrn_parts(proj, hg_f, lb_logits, hg_g, state_hgrn_S[l].astype(F32), layer=l,
                           **group_s, **_SAMPLE_HG, index=flat_step(ml_p["grid"]))
        assert math.prod(hg_p["grid"]) * _SAMPLE_ML["rows"] == tok_s
        assert math.prod(ml_p["grid"]) * _SAMPLE_HG["rows"] == tok_s
        (bb_p, s_p), (ba_s, c_s, n_s, m_all_s) = _run_parts([hg_p, ml_s], "hgrn2_prompt_mlstm_sample")
        (ba_p, c_p, n_p, m_all_p), (bb_s, s_s) = _run_parts([ml_p, hg_s], "mlstm_prompt_hgrn2_sample")

        merged_p = _merge(ba_p, w_branch_a[l], bb_p, w_branch_b[l], proj, 0)
        merged_s = _merge(ba_s, w_branch_a[l], bb_s, w_branch_b[l], proj, tok_p)
        x1_p, x1b_p = _outproj(merged_p, w_out_b, xp, *ln1)
        x1_s, x1b_s = _outproj(merged_s, w_out_b, xs, *ln1)
        xp = _mlp(x1b_p, w_up_b, w_down_b, x1_p, *ln2)
        xs = _mlp(x1b_s, w_up_b, w_down_b, x1_s, *ln2)

        states_p.append((c_p, n_p.reshape(batch_p, ML_HEADS, ML_DK),
                         m_all_p[:, seq_p - 1::seq_p, 0].T, s_p))
        states_s.append((c_s, n_s.reshape(batch_s, ML_HEADS, ML_DK),
                         m_all_s[:, seq_s - 1::seq_s, 0].T, s_s))
    stack = lambda states, k: jnp.stack([s[k] for s in states])
    return (xp.reshape(batch_p, seq_p, D_MODEL), xs.reshape(batch_s, seq_s, D_MODEL),
            stack(states_p, 0), stack(states_p, 1), stack(states_p, 2), stack(states_p, 3),
            stack(states_s, 0), stack(states_s, 1), stack(states_s, 2), stack(states_s, 3))
```

```python
import functools
import math

import numpy as np
import jax
import jax.numpy as jnp
from jax import lax
from jax.experimental import pallas as pl
from jax.experimental.pallas import tpu as pltpu

F32 = jnp.float32
BF16 = jnp.bfloat16

D_MODEL = 2048
DEPTH = 1
ML_HEADS, ML_DK, ML_DV = 4, 256, 512
HG_HEADS, HG_DK, HG_DV = 8, 128, 256
ML_QK = ML_HEADS * ML_DK
ML_V = ML_HEADS * ML_DV
HG_K = HG_HEADS * HG_DK
HG_V = HG_HEADS * HG_DV
D_FF = 4 * D_MODEL
LN_EPS = 1e-5
DEEPNORM_ALPHA = (2.0 * DEPTH) ** 0.25
ML_K_SCALE = ML_DK ** -0.5
LOG2_E = math.log2(math.e)
LANES = 128

_OFF_ML_Q = 0
_OFF_ML_K = _OFF_ML_Q + ML_QK
_OFF_ML_V = _OFF_ML_K + ML_QK
_OFF_ML_I = _OFF_ML_V + ML_V
_OFF_ML_F = _OFF_ML_I + ML_HEADS
_OFF_ML_O = _OFF_ML_F + ML_HEADS
_OFF_HG_Q = _OFF_ML_O + ML_V
_OFF_HG_F = _OFF_HG_Q + HG_K
_OFF_HG_I = _OFF_HG_F + HG_K
_OFF_HG_G = _OFF_HG_I + HG_V
_OFF_GATE_A = _OFF_HG_G + HG_V
_OFF_GATE_B = _OFF_GATE_A + D_MODEL
D_IN = _OFF_GATE_B + D_MODEL

_N_GATES = 2 * ML_HEADS
_P_ML_Q = _OFF_ML_Q
_P_ML_K = _OFF_ML_K
_P_ML_V = _OFF_ML_V
_P_ML_O = _OFF_ML_O - _N_GATES
_P_HG_Q = _OFF_HG_Q - _N_GATES
_P_HG_F = _OFF_HG_F - _N_GATES
_P_HG_I = _OFF_HG_I - _N_GATES
_P_HG_G = _OFF_HG_G - _N_GATES
_P_GATE_A = _OFF_GATE_A - _N_GATES
_P_GATE_B = _OFF_GATE_B - _N_GATES
_P_WIDTH = D_IN - _N_GATES

_PROJ_TM, _PROJ_TN = 1024, 1024
_MERGE_TM, _MERGE_TN = 512, 512
_MLP_TM, _MLP_TF = 1024, 512


def _sigmoid(x):
    return 1.0 / (1.0 + jnp.exp(-x))


def _log_sigmoid(x):
    return jnp.minimum(x, 0.0) - jnp.log1p(jnp.exp(-jnp.abs(x)))


def _split2(x):
    hi = x.astype(BF16)
    lo = (x - hi.astype(F32)).astype(BF16)
    return hi, lo


def _dot01_right(t01, x):
    hi, lo = _split2(x)
    return (jnp.dot(t01, hi, preferred_element_type=F32)
            + jnp.dot(t01, lo, preferred_element_type=F32))


def _dot01_left(x, t01):
    hi, lo = _split2(x)
    return (jnp.dot(hi, t01, preferred_element_type=F32)
            + jnp.dot(lo, t01, preferred_element_type=F32))


def _dot_nt(a, b):
    return lax.dot_general(a, b, (((1,), (1,)), ((), ())), preferred_element_type=F32)


def _dot_tn(a, b):
    return lax.dot_general(a, b, (((0,), (0,)), ((), ())), preferred_element_type=F32)


def _seq_masks(rows, seq_len):
    ri = lax.broadcasted_iota(jnp.int32, (rows, rows), 0)
    ci = lax.broadcasted_iota(jnp.int32, (rows, rows), 1)
    lower = jnp.where(ci <= ri, 1.0, 0.0)
    upper = jnp.where(ri <= ci, 1.0, 0.0)
    if seq_len == rows:
        return jnp.ones((rows, rows), F32), lower, upper
    shift = seq_len.bit_length() - 1
    same = jnp.where((ri >> shift) == (ci >> shift), 1.0, 0.0)
    return same, same * lower, same * upper


def _inproj_kernel(xp_ref, xs_ref, wt_hbm, *refs, n_f32, n_gate_tile, prompt_tiles, cast_steps):
    n_cast = (len(refs) - 9) // 2
    cast_in = refs[:n_cast]
    o_ref, f_ref, g_ref, gr_ref = refs[n_cast:n_cast + 4]
    cast_out = refs[n_cast + 4:2 * n_cast + 4]
    wbuf, wbf_sc, wg_sc, sem, gsem = refs[2 * n_cast + 4:]
    n = pl.program_id(0)
    m = pl.program_id(1)
    tn = wbf_sc.shape[1]

    @pl.when(m < cast_steps)
    def _cast_other_weights():
        for src, dst in zip(cast_in, cast_out):
            dst[...] = src[...].astype(BF16)

    def tile_copy(tile):
        start = pl.multiple_of(tile * tn + jnp.where(tile >= n_gate_tile, _N_GATES, 0), _N_GATES)
        return pltpu.make_async_copy(wt_hbm.at[pl.ds(start, tn), :], wbuf, sem.at[0])

    def gate_copy():
        return pltpu.make_async_copy(wt_hbm.at[pl.ds(_OFF_ML_I, _N_GATES), :],
                                     wg_sc.at[pl.ds(0, _N_GATES), :], gsem.at[0])

    @pl.when(m == 0)
    def _next_weight_tile():
        @pl.when(n == 0)
        def _first():
            tile_copy(0).start()
            wg_sc[_N_GATES:, :] = jnp.zeros((LANES - _N_GATES, wg_sc.shape[1]), F32)
            gate_copy().start()
            gate_copy().wait()

        tile_copy(n).wait()
        wbf_sc[...] = wbuf[...].T.astype(BF16)

        @pl.when(n + 1 < pl.num_programs(0))
        def _prefetch():
            tile_copy(n + 1).start()

    x = jnp.where(m < prompt_tiles, xp_ref[...], xs_ref[...])
    acc = jnp.dot(x, wbf_sc[...], preferred_element_type=F32)
    o_ref[...] = acc.astype(BF16)

    @pl.when(n == n_f32)
    def _f32_outputs():
        f_ref[...] = acc
        wg = wg_sc[...].astype(BF16)
        g_ref[...] = _dot_nt(x, wg)
        gr_ref[0] = _dot_nt(wg, x)[0:_N_GATES, :]


def _inproj(xp_b, xs_b, w_t, cast_weights):
    k = xp_b.shape[1]
    tm, tn = _PROJ_TM, _PROJ_TN
    p_tiles = xp_b.shape[0] // tm
    n_m = p_tiles + xs_b.shape[0] // tm
    tokens = n_m * tm
    n_n = _P_WIDTH // tn
    n_f32 = _P_HG_F // tn
    cast_steps = min(n_m, 8)

    def parked(n, m):
        return jnp.where(n < n_f32, 0, jnp.where(n == n_f32, m, n_m - 1))

    def cast_spec(w):
        return pl.BlockSpec((w.shape[0] // n_n, w.shape[1] // cast_steps),
                            lambda n, m: (n, jnp.minimum(m, cast_steps - 1)))

    cast_specs = [cast_spec(w) for w in cast_weights]
    outs = pl.pallas_call(
        functools.partial(_inproj_kernel, n_f32=n_f32, n_gate_tile=_OFF_ML_I // tn,
                          prompt_tiles=p_tiles, cast_steps=cast_steps),
        grid=(n_n, n_m),
        in_specs=[pl.BlockSpec((tm, k), lambda n, m: (jnp.minimum(m, p_tiles - 1), 0)),
                  pl.BlockSpec((tm, k), lambda n, m: (jnp.maximum(m - p_tiles, 0), 0)),
                  pl.BlockSpec(memory_space=pl.ANY),
                  *cast_specs],
        out_specs=[pl.BlockSpec((tm, tn), lambda n, m: (m, n)),
                   pl.BlockSpec((tm, HG_K), lambda n, m: (parked(n, m), 0)),
                   pl.BlockSpec((tm, LANES), lambda n, m: (parked(n, m), 0)),
                   pl.BlockSpec((1, 8, tm), lambda n, m: (parked(n, m), 0, 0)),
                   *cast_specs],
        out_shape=[jax.ShapeDtypeStruct((tokens, _P_WIDTH), BF16),
                   jax.ShapeDtypeStruct((tokens, HG_K), F32),
                   jax.ShapeDtypeStruct((tokens, LANES), F32),
                   jax.ShapeDtypeStruct((n_m, 8, tm), F32),
                   *[jax.ShapeDtypeStruct(w.shape, BF16) for w in cast_weights]],
        scratch_shapes=[pltpu.VMEM((tn, k), F32),
                        pltpu.VMEM((k, tn), BF16),
                        pltpu.VMEM((LANES, k), F32),
                        pltpu.SemaphoreType.DMA((1,)),
                        pltpu.SemaphoreType.DMA((1,))],
        compiler_params=pltpu.CompilerParams(dimension_semantics=("arbitrary", "arbitrary")),
        name="in_proj",
    )(xp_b, xs_b, w_t, *cast_weights)
    return outs[:4], outs[4:]


def _mlstm_kernel(*refs, rows, seq_len, heads, zero_init, head0=None):
    if zero_init:
        (q_ref, k_ref, v_ref, og_ref, gc_ref, gr_ref, brow_ref, bcol_ref, ng_ref,
         ba_ref, c_ref, n_ref, mrow_ref, m_sc) = refs
        c0_ref, n0_ref = c_ref, n_ref
    else:
        (q_ref, k_ref, v_ref, og_ref, gc_ref, gr_ref, brow_ref, bcol_ref, ng_ref, c0_ref, n0_ref,
         ba_ref, c_ref, n_ref, mrow_ref) = refs
    nseq = rows // seq_len
    if head0 is None:
        head0 = pl.program_id(1) * heads

    if zero_init:
        @pl.when(pl.program_id(2) == 0)
        def _init():
            c_ref[...] = jnp.zeros(c_ref.shape, F32)
            n_ref[...] = jnp.zeros(n_ref.shape, F32)
            m_sc[...] = jnp.zeros(m_sc.shape, F32)

    same, lower, upper = _seq_masks(rows, seq_len)
    causal = lower > 0.5
    lane = lax.broadcasted_iota(jnp.int32, (rows, LANES), 1)
    sub = lax.broadcasted_iota(jnp.int32, (8, rows), 0)

    def sel_lane(x, idx):
        return jnp.sum(jnp.where(lane == idx, x, 0.0), axis=1, keepdims=True)

    def sel_sub(x, idx):
        return jnp.sum(jnp.where(sub == idx, x, 0.0), axis=0, keepdims=True)

    gc_raw = gc_ref[...]
    gc = gc_raw + brow_ref[...]
    lf_cols = _log_sigmoid(gc)
    b_cols = _dot01_right(lower.astype(BF16), lf_cols)
    gr = gr_ref[0] + bcol_ref[...]
    b_rows = _dot01_left(_log_sigmoid(gr), upper.astype(BF16))
    if nseq > 1:
        b_ends = _dot01_right(same.astype(BF16), lf_cols)
        ci = lax.broadcasted_iota(jnp.int32, (rows, rows), 1)
        last = (same * jnp.where((ci & (seq_len - 1)) == seq_len - 1, 1.0, 0.0)).astype(BF16)
        shift = seq_len.bit_length() - 1
        row_seq = lax.broadcasted_iota(jnp.int32, (rows, 1), 0) >> shift

    for hh in range(heads):
        head = head0 + hh
        ks = slice(hh * ML_DK, (hh + 1) * ML_DK)
        vs = slice(hh * ML_DV, (hh + 1) * ML_DV)
        ig_col = sel_lane(gc, head)
        b_col = sel_lane(b_cols, head + ML_HEADS)
        if zero_init:
            m_prev = jnp.broadcast_to(m_sc[hh:hh + 1, 0:1], (rows, 1))
        else:
            m_prev = sel_lane(gc_raw, head + 2 * ML_HEADS)
        ig_row = sel_sub(gr, head)
        b_row = sel_sub(b_rows, head + ML_HEADS)

        logd = jnp.where(causal, (b_col - b_row) + ig_row, -jnp.inf)
        m_t = jnp.maximum(b_col + m_prev, jnp.max(logd, axis=1, keepdims=True))
        d = jnp.exp(logd - m_t)
        w_inter = jnp.exp(b_col + m_prev - m_t)

        qb = q_ref[:, ks].astype(BF16)
        kb = k_ref[:, ks].astype(BF16)
        vb = v_ref[:, vs].astype(BF16)
        s = _dot_nt(qb, kb) * (d * ML_K_SCALE)
        num = jnp.dot(s.astype(BF16), vb, preferred_element_type=F32)
        den = jnp.sum(s, axis=1, keepdims=True)

        if nseq == 1:
            b_end = b_col[rows - 1:rows, :]
            m_new = m_t[rows - 1:rows, :]
        else:
            b_end = sel_lane(b_ends, head + ML_HEADS)
            m_new = _dot01_right(last, jnp.broadcast_to(m_t, (rows, LANES)))[:, 0:1]
        w_end = jnp.exp(b_end - b_col + ig_col - m_new)
        decay = jnp.exp(b_end + m_prev - m_new)

        qf = qb.astype(F32)
        kw = (w_end * ML_K_SCALE) * kb.astype(F32)

        if nseq == 1:
            c_prev = c0_ref[0, hh]
            n_prev = n0_ref[0, hh]
            q_c = jnp.dot(qb, c_prev.astype(BF16), preferred_element_type=F32)
            q_n = jnp.sum(qf * n_prev, axis=1, keepdims=True)
            dec = decay[0:1, :]
            c_ref[0, hh] = dec * c_prev + _dot_tn(kw.astype(BF16), vb)
            n_ref[0, hh] = dec * n_prev + jnp.sum(kw, axis=0, keepdims=True)
        else:
            q_c = jnp.zeros((rows, ML_DV), F32)
            q_n = jnp.zeros((rows, 1), F32)
            for j in range(nseq):
                in_seq = row_seq == j
                c_prev = c0_ref[j, hh]
                n_prev = n0_ref[j, hh]
                q_c = jnp.where(
                    in_seq, jnp.dot(qb, c_prev.astype(BF16), preferred_element_type=F32), q_c)
                q_n = jnp.where(in_seq, jnp.sum(qf * n_prev, axis=1, keepdims=True), q_n)
                kw_j = jnp.where(in_seq, kw, 0.0)
                dec = decay[j * seq_len:j * seq_len + 1, :]
                c_ref[j, hh] = dec * c_prev + _dot_tn(kw_j.astype(BF16), vb)
                n_ref[j, hh] = dec * n_prev + jnp.sum(kw_j, axis=0, keepdims=True)

        num = num + w_inter * q_c
        den = den + w_inter * q_n
        h_out = num / jnp.maximum(jnp.abs(den), jnp.exp(-m_t))
        mu = jnp.mean(h_out, axis=1, keepdims=True)
        xc = h_out - mu
        var = jnp.mean(xc * xc, axis=1, keepdims=True)
        hn = xc * lax.rsqrt(var + LN_EPS) * ng_ref[:, vs]
        ba_ref[:, vs] = (hn * _sigmoid(og_ref[:, vs].astype(F32))).astype(ba_ref.dtype)
        mrow_ref[hh] = jnp.broadcast_to(m_t, (rows, LANES))
        if zero_init:
            m_sc[hh:hh + 1, :] = jnp.broadcast_to(m_new, (1, LANES))


def _mlstm_parts(proj, gates_col, gates_row, brow, bcol, norm_g, state, *, row0, batch, seq, rows,
                 seq_len, heads, index=None):
    tokens = batch * seq
    zero_init = state is None
    nseq = rows // seq_len
    nchunk = seq // seq_len if nseq == 1 else 1
    ngroup = tokens // (rows * nchunk)
    bq, bv = heads * ML_DK, heads * ML_DV
    rb0 = row0 // rows
    per_row_tile = gates_row.shape[2] // rows
    head0 = None
    if index is None:
        index = lambda g, h, c: (g * nchunk + c, h, g)
    else:
        assert heads == ML_HEADS
        head0 = 0
    row = lambda *ids: index(*ids)[0]
    hblk = lambda *ids: index(*ids)[1]
    sblk = lambda *ids: index(*ids)[2]

    in_specs = [
        pl.BlockSpec((rows, bq), lambda *ids: (rb0 + row(*ids), _P_ML_Q // bq + hblk(*ids))),
        pl.BlockSpec((rows, bq), lambda *ids: (rb0 + row(*ids), _P_ML_K // bq + hblk(*ids))),
        pl.BlockSpec((rows, bv), lambda *ids: (rb0 + row(*ids), _P_ML_V // bv + hblk(*ids))),
        pl.BlockSpec((rows, bv), lambda *ids: (rb0 + row(*ids), _P_ML_O // bv + hblk(*ids))),
        pl.BlockSpec((rows, LANES), lambda *ids: (row(*ids), 0)),
        pl.BlockSpec((1, 8, rows),
                     lambda *ids: (row(*ids) // per_row_tile, 0, row(*ids) % per_row_tile)),
        pl.BlockSpec((1, LANES), lambda *ids: (0, 0)),
        pl.BlockSpec((8, 1), lambda *ids: (0, 0)),
        pl.BlockSpec((1, bv), lambda *ids: (0, hblk(*ids))),
    ]
    args = [proj, proj, proj, proj, gates_col, gates_row, brow, bcol, norm_g]
    state_specs = [
        pl.BlockSpec((nseq, heads, ML_DK, ML_DV), lambda *ids: (sblk(*ids), hblk(*ids), 0, 0)),
        pl.BlockSpec((nseq, heads, 1, ML_DK), lambda *ids: (sblk(*ids), hblk(*ids), 0, 0)),
    ]
    scratch = []
    if zero_init:
        scratch = [pltpu.VMEM((8, LANES), F32)]
    else:
        in_specs += state_specs
        args += [state[0], state[1]]
    out_specs = [
        pl.BlockSpec((rows, bv), lambda *ids: (row(*ids), hblk(*ids))),
        *state_specs,
        pl.BlockSpec((heads, rows, LANES), lambda *ids: (hblk(*ids), row(*ids), 0)),
    ]
    out_shape = [
        jax.ShapeDtypeStruct((tokens, ML_V), BF16),
        jax.ShapeDtypeStruct((batch, ML_HEADS, ML_DK, ML_DV), F32),
        jax.ShapeDtypeStruct((batch, ML_HEADS, 1, ML_DK), F32),
        jax.ShapeDtypeStruct((ML_HEADS, tokens, LANES), F32),
    ]
    return dict(
        kernel=functools.partial(_mlstm_kernel, rows=rows, seq_len=seq_len, heads=heads,
                                 zero_init=zero_init, head0=head0),
        grid=(ngroup, ML_HEADS // heads, nchunk),
        in_specs=in_specs, args=args, out_specs=out_specs, out_shape=out_shape, scratch=scratch)


def _run_parts(parts, name):
    n_in = [len(p["in_specs"]) for p in parts]
    n_out = [len(p["out_specs"]) for p in parts]
    n_scr = [len(p["scratch"]) for p in parts]

    def kernel(*refs):
        ins = refs[:sum(n_in)]
        outs = refs[sum(n_in):sum(n_in) + sum(n_out)]
        scr = refs[sum(n_in) + sum(n_out):]
        i = o = s = 0
        for p, ni, no, ns in zip(parts, n_in, n_out, n_scr):
            p["kernel"](*ins[i:i + ni], *outs[o:o + no], *scr[s:s + ns])
            i, o, s = i + ni, o + no, s + ns

    outs = pl.pallas_call(
        kernel,
        grid=parts[0]["grid"],
        in_specs=[s for p in parts for s in p["in_specs"]],
        out_specs=[s for p in parts for s in p["out_specs"]],
        out_shape=[s for p in parts for s in p["out_shape"]],
        scratch_shapes=[s for p in parts for s in p["scratch"]],
        compiler_params=pltpu.CompilerParams(
            dimension_semantics=("parallel", "parallel", "arbitrary")),
        name=name,
    )(*[a for p in parts for a in p["args"]])
    split, o = [], 0
    for no in n_out:
        split.append(outs[o:o + no])
        o += no
    return split


def _hgrn_level_ids(rows, seq_len, direct, levels):
    t = np.arange(rows)[:, None]
    s = np.arange(rows)[None, :]
    ids = np.full((rows, rows), -1, np.int32)
    count = ((t // direct == s // direct) & (s <= t)).astype(np.int32)
    for idx, (block, sub_size) in enumerate(levels):
        owned = (t // block == s // block) & ((s % block) // sub_size < (t % block) // sub_size)
        ids[owned] = idx
        count += owned
    wanted = (t // seq_len == s // seq_len) & (s <= t)
    assert np.array_equal(count, wanted.astype(np.int32)), (rows, seq_len, direct, levels)
    return ids


def _hgrn_intra(q, kin, g2, gk2, level_ids, rows, direct, levels):
    ngroups = max(rows // LANES, 1)
    rowi = lax.broadcasted_iota(jnp.int32, (direct, LANES), 0)
    lanei = lax.broadcasted_iota(jnp.int32, (direct, LANES), 1)
    keep = [jnp.where(lanei == s, rowi, -1) >= s for s in range(direct)]
    zero_group = jnp.zeros((direct, LANES), F32)
    prods = []
    for blk in range(rows // direct):
        r0 = blk * direct
        qb = q[r0:r0 + direct]
        gb = g2[r0:r0 + direct]
        gkb = gk2[r0:r0 + direct]
        for s in range(direct):
            prods.append(qb * jnp.exp2(gb - gkb[s:s + 1, :]))
    sums = jnp.dot(jnp.concatenate(prods, axis=0).astype(BF16), jnp.ones((HG_DK, LANES), BF16),
                   preferred_element_type=F32)
    panels = []
    for blk in range(rows // direct):
        r0 = blk * direct
        ag = zero_group
        for s in range(direct):
            p0 = (blk * direct + s) * direct
            ag = jnp.where(keep[s], sums[p0:p0 + direct], ag)
        lane0 = r0 % LANES
        if lane0:
            ag = pltpu.roll(ag, lane0, axis=1)
        grp = r0 // LANES
        pieces = [zero_group] * grp + [ag] + [zero_group] * (ngroups - grp - 1)
        panels.append(pieces[0] if ngroups == 1 else jnp.concatenate(pieces, axis=1))
    a = panels[0] if len(panels) == 1 else jnp.concatenate(panels, axis=0)
    if rows < LANES:
        a = a[:, :rows]

    if levels:
        rowid = lax.broadcasted_iota(jnp.int32, (rows, 1), 0)
    for idx, (block, sub_size) in enumerate(levels):
        sshift = sub_size.bit_length() - 1
        row_sub = (rowid & (block - 1)) >> sshift
        q_parts, k_parts = [], []
        for j in range(1, block // sub_size):
            refs = []
            for b0 in range(0, rows, block):
                r = b0 + j * sub_size - 1
                refs.append(jnp.broadcast_to(g2[r:r + 1, :], (block, HG_DK)))
            g_ref = refs[0] if len(refs) == 1 else jnp.concatenate(refs, axis=0)
            e = jnp.exp2(-jnp.abs(g2 - g_ref))
            q_parts.append(jnp.where(row_sub == j, q * e, 0.0).astype(BF16))
            k_parts.append((kin * e).astype(BF16))
        qcat = q_parts[0] if len(q_parts) == 1 else jnp.concatenate(q_parts, axis=1)
        kcat = k_parts[0] if len(k_parts) == 1 else jnp.concatenate(k_parts, axis=1)
        a = jnp.where(level_ids == idx, _dot_nt(qcat, kcat), a)
    return a


def _hgrn_kernel(*refs, layer, rows, seq_len, heads, direct, levels, zero_init):
    if zero_init:
        q_ref, f_ref, i_ref, g_ref, lbl_ref, ng_ref, lvl_ref, bb_ref, s_ref = refs
        s0_ref = s_ref
    else:
        q_ref, f_ref, i_ref, g_ref, lbl_ref, ng_ref, lvl_ref, s0_ref, bb_ref, s_ref = refs
    nseq = rows // seq_len

    if zero_init:
        @pl.when(pl.program_id(2) == 0)
        def _init():
            s_ref[...] = jnp.zeros(s_ref.shape, F32)

    same, lower, _ = _seq_masks(rows, seq_len)
    lower_b = lower.astype(BF16)
    same_b = same.astype(BF16)
    level_ids = lvl_ref[...]
    if nseq > 1:
        shift = seq_len.bit_length() - 1
        row_seq = lax.broadcasted_iota(jnp.int32, (rows, 1), 0) >> shift

    def decay_cols(row):
        col = jnp.broadcast_to(jnp.exp2(row), (HG_DK, HG_DK)).T
        return jnp.concatenate([col] * (HG_DV // HG_DK), axis=1)

    for hh in range(heads):
        ks = slice(hh * HG_DK, (hh + 1) * HG_DK)
        vs = slice(hh * HG_DV, (hh + 1) * HG_DV)
        lg = lbl_ref[:, ks]
        ex = jnp.exp(lg - jnp.max(lg, axis=0, keepdims=True))
        lb = (jnp.sum(ex[0:layer + 1, :], axis=0, keepdims=True)
              / jnp.sum(ex, axis=0, keepdims=True))

        f = lb + (1.0 - lb) * _sigmoid(f_ref[:, ks])
        kin = 1.0 - f
        lf2 = jnp.log(f) * LOG2_E
        g2 = _dot01_right(lower_b, lf2)
        gk2 = g2 - jnp.log(kin) * LOG2_E
        q = q_ref[:, ks].astype(F32)
        if nseq == 1:
            g_end = jnp.broadcast_to(g2[rows - 1:rows, :], (rows, HG_DK))
        else:
            g_end = _dot01_right(same_b, lf2)
        qg = (q * jnp.exp2(g2)).astype(BF16)
        kg = kin * jnp.exp2(g_end - g2)
        ib = i_ref[:, vs].astype(BF16)

        a = _hgrn_intra(q, kin, g2, gk2, level_ids, rows, direct, levels)
        o = jnp.dot(a.astype(BF16), ib, preferred_element_type=F32)

        if nseq == 1:
            s_prev = s0_ref[0, hh]
            o = o + jnp.dot(qg, s_prev.astype(BF16), preferred_element_type=F32)
            s_ref[0, hh] = decay_cols(g_end[0:1, :]) * s_prev + _dot_tn(kg.astype(BF16), ib)
        else:
            for j in range(nseq):
                in_seq = row_seq == j
                s_prev = s0_ref[j, hh]
                o_j = jnp.dot(qg, s_prev.astype(BF16), preferred_element_type=F32)
                o = o + jnp.where(in_seq, o_j, 0.0)
                kg_j = jnp.where(in_seq, kg, 0.0).astype(BF16)
                s_ref[j, hh] = (decay_cols(g_end[j * seq_len:j * seq_len + 1, :]) * s_prev
                                + _dot_tn(kg_j, ib))

        o = o * lax.rsqrt(jnp.mean(o * o, axis=1, keepdims=True) + LN_EPS)
        gate = g_ref[:, vs].astype(F32)
        bb_ref[:, vs] = (o * ng_ref[:, vs] * (gate * _sigmoid(gate))).astype(bb_ref.dtype)


def _hgrn_parts(proj, hg_f, lb_logits, norm_g, s0, *, layer, row0, batch, seq, rows, seq_len, heads,
                direct, levels, index=None):
    tokens = batch * seq
    zero_init = s0 is None
    nseq = rows // seq_len
    nchunk = seq // seq_len if nseq == 1 else 1
    ngroup = tokens // (rows * nchunk)
    rb0 = row0 // rows
    if index is None:
        index = lambda g, h, c: (g * nchunk + c, h, g)
    row = lambda *ids: index(*ids)[0]
    hblk = lambda *ids: index(*ids)[1]
    sblk = lambda *ids: index(*ids)[2]

    wk, wv = heads * HG_DK, heads * HG_DV
    in_specs = [
        pl.BlockSpec((rows, wk), lambda *ids: (rb0 + row(*ids), _P_HG_Q // wk + hblk(*ids))),
        pl.BlockSpec((rows, wk), lambda *ids: (rb0 + row(*ids), hblk(*ids))),
        pl.BlockSpec((rows, wv), lambda *ids: (rb0 + row(*ids), _P_HG_I // wv + hblk(*ids))),
        pl.BlockSpec((rows, wv), lambda *ids: (rb0 + row(*ids), _P_HG_G // wv + hblk(*ids))),
        pl.BlockSpec((DEPTH + 1, wk), lambda *ids: (0, hblk(*ids))),
        pl.BlockSpec((1, wv), lambda *ids: (0, hblk(*ids))),
        pl.BlockSpec((rows, rows), lambda *ids: (0, 0)),
    ]
    level_ids = jnp.asarray(_hgrn_level_ids(rows, seq_len, direct, levels))
    args = [proj, hg_f, proj, proj, lb_logits, norm_g, level_ids]
    state_spec = pl.BlockSpec((nseq, heads, HG_DK, HG_DV),
                              lambda *ids: (sblk(*ids), hblk(*ids), 0, 0))
    if not zero_init:
        in_specs.append(state_spec)
        args.append(s0)
    return dict(
        kernel=functools.partial(_hgrn_kernel, layer=layer, rows=rows, seq_len=seq_len, heads=heads,
                                 direct=direct, levels=levels, zero_init=zero_init),
        grid=(ngroup, HG_HEADS // heads, nchunk),
        in_specs=in_specs, args=args,
        out_specs=[pl.BlockSpec((rows, wv), lambda *ids: (row(*ids), hblk(*ids))), state_spec],
        out_shape=[jax.ShapeDtypeStruct((tokens, HG_V), BF16),
                   jax.ShapeDtypeStruct((batch, HG_HEADS, HG_DK, HG_DV), F32)],
        scratch=[])


def _layernorm_rows(z, g, b):
    mu = jnp.mean(z, axis=1, keepdims=True)
    zc = z - mu
    var = jnp.mean(zc * zc, axis=1, keepdims=True)
    return zc * lax.rsqrt(var + LN_EPS) * g + b


def _merge_outproj_kernel(ba_ref, wa_ref, bb_ref, wb_ref, ga_ref, gb_ref, wo_ref, x_ref, g_ref, b_ref,
                          x1_ref, x1b_ref):
    j = pl.program_id(1)

    @pl.when(j == 0)
    def _init():
        x1_ref[...] = jnp.zeros(x1_ref.shape, F32)

    ya = jnp.dot(ba_ref[...], wa_ref[...], preferred_element_type=F32)
    yb = jnp.dot(bb_ref[...], wb_ref[...], preferred_element_type=F32)
    ga = _sigmoid(ga_ref[...].astype(F32))
    gb = _sigmoid(gb_ref[...].astype(F32))
    merged = (ga * ya + gb * yb).astype(BF16)
    x1_ref[...] += jnp.dot(merged, wo_ref[...], preferred_element_type=F32)

    @pl.when(j == pl.num_programs(1) - 1)
    def _finish():
        x1 = _layernorm_rows(DEEPNORM_ALPHA * x_ref[...] + x1_ref[...], g_ref[...], b_ref[...])
        x1_ref[...] = x1
        x1b_ref[...] = x1.astype(BF16)


def _merge_outproj(branch_a, w_a, branch_b, w_b, proj, row0, w_out, x, ln_g, ln_b):
    m = x.shape[0]
    tm, tn = _MERGE_TM, _MERGE_TN
    rb0 = row0 // tm
    ga_blk = _P_GATE_A // tn
    gb_blk = _P_GATE_B // tn
    row = lambda i, j: (i, 0)
    const = lambda i, j: (0, 0)
    return pl.pallas_call(
        _merge_outproj_kernel,
        grid=(m // tm, D_MODEL // tn),
        in_specs=[pl.BlockSpec((tm, ML_V), row),
                  pl.BlockSpec((ML_V, tn), lambda i, j: (0, j)),
                  pl.BlockSpec((tm, HG_V), row),
                  pl.BlockSpec((HG_V, tn), lambda i, j: (0, j)),
                  pl.BlockSpec((tm, tn), lambda i, j: (rb0 + i, ga_blk + j)),
                  pl.BlockSpec((tm, tn), lambda i, j: (rb0 + i, gb_blk + j)),
                  pl.BlockSpec((tn, D_MODEL), lambda i, j: (j, 0)),
                  pl.BlockSpec((tm, D_MODEL), row),
                  pl.BlockSpec((1, D_MODEL), const),
                  pl.BlockSpec((1, D_MODEL), const)],
        out_specs=[pl.BlockSpec((tm, D_MODEL), row), pl.BlockSpec((tm, D_MODEL), row)],
        out_shape=[jax.ShapeDtypeStruct((m, D_MODEL), F32),
                   jax.ShapeDtypeStruct((m, D_MODEL), BF16)],
        compiler_params=pltpu.CompilerParams(dimension_semantics=("parallel", "arbitrary")),
        name="merge_out_proj_ln",
    )(branch_a, w_a, branch_b, w_b, proj, proj, w_out, x, ln_g, ln_b)


def _mlp_kernel(x1b_ref, wu_ref, wd_ref, x1_ref, g_ref, b_ref, y_ref):
    f = pl.program_id(1)

    @pl.when(f == 0)
    def _init():
        y_ref[...] = jnp.zeros(y_ref.shape, F32)

    hid = jnp.maximum(jnp.dot(x1b_ref[...], wu_ref[...], preferred_element_type=F32), 0.0)
    hid = (hid * hid).astype(BF16)
    y_ref[...] += jnp.dot(hid, wd_ref[...], preferred_element_type=F32)

    @pl.when(f == pl.num_programs(1) - 1)
    def _finish():
        z = DEEPNORM_ALPHA * x1_ref[...] + y_ref[...]
        y_ref[...] = _layernorm_rows(z, g_ref[...], b_ref[...])


def _mlp(x1b, w_up, w_down, x1, ln_g, ln_b):
    tm, tf = _MLP_TM, _MLP_TF
    tokens = x1.shape[0]
    return pl.pallas_call(
        _mlp_kernel,
        grid=(tokens // tm, D_FF // tf),
        in_specs=[pl.BlockSpec((tm, D_MODEL), lambda i, f: (i, 0)),
                  pl.BlockSpec((D_MODEL, tf), lambda i, f: (0, f)),
                  pl.BlockSpec((tf, D_MODEL), lambda i, f: (f, 0)),
                  pl.BlockSpec((tm, D_MODEL), lambda i, f: (i, 0)),
                  pl.BlockSpec((1, D_MODEL), lambda i, f: (0, 0)),
                  pl.BlockSpec((1, D_MODEL), lambda i, f: (0, 0))],
        out_specs=pl.BlockSpec((tm, D_MODEL), lambda i, f: (i, 0)),
        out_shape=jax.ShapeDtypeStruct((tokens, D_MODEL), F32),
        compiler_params=pltpu.CompilerParams(dimension_semantics=("parallel", "arbitrary")),
        name="mlp_ln",
    )(x1b, w_up, w_down, x1, ln_g, ln_b)


_PROMPT_ML = dict(rows=256, seq_len=256, heads=4)
_PROMPT_HG = dict(rows=256, seq_len=256, heads=4, direct=8,
                  levels=((32, 8), (128, 32), (256, 128)))
_SAMPLE_ML = dict(rows=16, seq_len=8, heads=4)
_SAMPLE_HG = dict(rows=32, seq_len=8, heads=8, direct=8, levels=())


def kernel(x_prompt, x_sample, state_mlstm_C, state_mlstm_n, state_mlstm_m, state_hgrn_S,
           hg_lb_logits, w_in, b_ig, b_fg, ml_norm_g, hg_norm_g, w_branch_a, w_branch_b, w_out,
           ln1_g, ln1_b, w_up, w_down, ln2_g, ln2_b):
    batch_p, seq_p, _ = x_prompt.shape
    batch_s, seq_s, _ = x_sample.shape
    tok_p, tok_s = batch_p * seq_p, batch_s * seq_s
    total = tok_p + tok_s
    group_p = dict(row0=0, batch=batch_p, seq=seq_p)
    group_s = dict(row0=tok_p, batch=batch_s, seq=seq_s)
    xp = x_prompt.reshape(tok_p, D_MODEL)
    xs = x_sample.reshape(tok_s, D_MODEL)
    lb_logits = hg_lb_logits.astype(F32)
    states_p, states_s = [], []
    for l in range(DEPTH):
        (proj, hg_f, gates, gates_gm), (w_a_b, w_b_b, w_out_b, w_up_b, w_down_b) = _inproj(
            xp.astype(BF16), xs.astype(BF16), jnp.swapaxes(w_in[l], 0, 1),
            (w_branch_a[l], w_branch_b[l], w_out[l], w_up[l], w_down[l]))

        gate_bias = jnp.concatenate([b_ig[l], b_fg[l]]).astype(F32)
        brow = jnp.zeros((1, LANES), F32).at[0, :_N_GATES].set(gate_bias)
        bcol = gate_bias.reshape(_N_GATES, 1)
        ml_g = ml_norm_g[l].reshape(1, ML_V).astype(F32)
        hg_g = hg_norm_g[l].reshape(1, HG_V).astype(F32)

        m_rows = jnp.repeat(state_mlstm_m[l].astype(F32), seq_s, axis=0)
        gates_s = gates[tok_p:].at[:, _N_GATES:_N_GATES + ML_HEADS].set(m_rows)
        rows_s = _SAMPLE_ML["rows"]
        gm_s = gates_gm[tok_p // _PROJ_TM:]
        gm_s = gm_s.reshape(-1, 8, _PROJ_TM // rows_s, rows_s).transpose(0, 2, 1, 3)
        gm_s = gm_s.reshape(tok_s // rows_s, 8, rows_s)
        state_ml = (state_mlstm_C[l].astype(F32),
                    state_mlstm_n[l].astype(F32).reshape(batch_s, ML_HEADS, 1, ML_DK))
        ln1 = (ln1_g[l].reshape(1, D_MODEL), ln1_b[l].reshape(1, D_MODEL))
        ln2 = (ln2_g[l].reshape(1, D_MODEL), ln2_b[l].reshape(1, D_MODEL))

        ml_p = _mlstm_parts(proj, gates, gates_gm, brow, bcol, ml_g, None, **group_p, **_PROMPT_ML)
        hg_p = _hgrn_parts(proj, hg_f, lb_logits, hg_g, None, layer=l, **group_p, **_PROMPT_HG)

        def flat_step(grid):
            def index(g, h, c):
                step = (g * grid[1] + h) * grid[2] + c
                return step, 0, step
            return index

        ml_s = _mlstm_parts(proj, gates_s, gm_s, brow, bcol, ml_g, state_ml, **group_s,
                            **_SAMPLE_ML, index=flat_step(hg_p["grid"]))
        hg_s = _hgrn_parts(proj, hg_f, lb_logits, hg_g, state_hgrn_S[l].astype(F32), layer=l,
                           **group_s, **_SAMPLE_HG, index=flat_step(ml_p["grid"]))
        assert math.prod(hg_p["grid"]) * _SAMPLE_ML["rows"] == tok_s
        assert math.prod(ml_p["grid"]) * _SAMPLE_HG["rows"] == tok_s
        (bb_p, s_p), (ba_s, c_s, n_s, m_all_s) = _run_parts([hg_p, ml_s], "hgrn2_prompt_mlstm_sample")
        (ba_p, c_p, n_p, m_all_p), (bb_s, s_s) = _run_parts([ml_p, hg_s], "mlstm_prompt_hgrn2_sample")

        x1_p, x1b_p = _merge_outproj(ba_p, w_a_b, bb_p, w_b_b, proj, 0, w_out_b, xp, *ln1)
        x1_s, x1b_s = _merge_outproj(ba_s, w_a_b, bb_s, w_b_b, proj, tok_p, w_out_b, xs, *ln1)
        xp = _mlp(x1b_p, w_up_b, w_down_b, x1_p, *ln2)
        xs = _mlp(x1b_s, w_up_b, w_down_b, x1_s, *ln2)

        states_p.append((c_p, n_p.reshape(batch_p, ML_HEADS, ML_DK),
                         m_all_p[:, seq_p - 1::seq_p, 0].T, s_p))
        states_s.append((c_s, n_s.reshape(batch_s, ML_HEADS, ML_DK),
                         m_all_s[:, seq_s - 1::seq_s, 0].T, s_s))
    stack = lambda states, k: jnp.stack([s[k] for s in states])
    return (xp.reshape(batch_p, seq_p, D_MODEL), xs.reshape(batch_s, seq_s, D_MODEL),
            stack(states_p, 0), stack(states_p, 1), stack(states_p, 2), stack(states_p, 3),
            stack(states_s, 0), stack(states_s, 1), stack(states_s, 2), stack(states_s, 3))
```

```python
import functools
import math

import numpy as np
import jax
import jax.numpy as jnp
from jax import lax
from jax.experimental import pallas as pl
from jax.experimental.pallas import tpu as pltpu

F32 = jnp.float32
BF16 = jnp.bfloat16

D_MODEL = 2048
DEPTH = 1
ML_HEADS, ML_DK, ML_DV = 4, 256, 512
HG_HEADS, HG_DK, HG_DV = 8, 128, 256
ML_QK = ML_HEADS * ML_DK
ML_V = ML_HEADS * ML_DV
HG_K = HG_HEADS * HG_DK
HG_V = HG_HEADS * HG_DV
D_FF = 4 * D_MODEL
LN_EPS = 1e-5
DEEPNORM_ALPHA = (2.0 * DEPTH) ** 0.25
ML_K_SCALE = ML_DK ** -0.5
LOG2_E = math.log2(math.e)
LANES = 128

_OFF_ML_Q = 0
_OFF_ML_K = _OFF_ML_Q + ML_QK
_OFF_ML_V = _OFF_ML_K + ML_QK
_OFF_ML_I = _OFF_ML_V + ML_V
_OFF_ML_F = _OFF_ML_I + ML_HEADS
_OFF_ML_O = _OFF_ML_F + ML_HEADS
_OFF_HG_Q = _OFF_ML_O + ML_V
_OFF_HG_F = _OFF_HG_Q + HG_K
_OFF_HG_I = _OFF_HG_F + HG_K
_OFF_HG_G = _OFF_HG_I + HG_V
_OFF_GATE_A = _OFF_HG_G + HG_V
_OFF_GATE_B = _OFF_GATE_A + D_MODEL
D_IN = _OFF_GATE_B + D_MODEL

_N_GATES = 2 * ML_HEADS
_P_ML_Q = _OFF_ML_Q
_P_ML_K = _OFF_ML_K
_P_ML_V = _OFF_ML_V
_P_ML_O = _OFF_ML_O - _N_GATES
_P_HG_Q = _OFF_HG_Q - _N_GATES
_P_HG_F = _OFF_HG_F - _N_GATES
_P_HG_I = _OFF_HG_I - _N_GATES
_P_HG_G = _OFF_HG_G - _N_GATES
_P_GATE_A = _OFF_GATE_A - _N_GATES
_P_GATE_B = _OFF_GATE_B - _N_GATES
_P_WIDTH = D_IN - _N_GATES

_PROJ_TM, _PROJ_TN = 1024, 1024
_MERGE_TM, _MERGE_TN = 1024, 512
_OUT_TM = 512
_MLP_TM, _MLP_TF = 1024, 512


def _sigmoid(x):
    return 1.0 / (1.0 + jnp.exp(-x))


def _log_sigmoid(x):
    return jnp.minimum(x, 0.0) - jnp.log1p(jnp.exp(-jnp.abs(x)))


def _split2(x):
    hi = x.astype(BF16)
    lo = (x - hi.astype(F32)).astype(BF16)
    return hi, lo


def _dot01_right(t01, x):
    hi, lo = _split2(x)
    return (jnp.dot(t01, hi, preferred_element_type=F32)
            + jnp.dot(t01, lo, preferred_element_type=F32))


def _dot01_left(x, t01):
    hi, lo = _split2(x)
    return (jnp.dot(hi, t01, preferred_element_type=F32)
            + jnp.dot(lo, t01, preferred_element_type=F32))


def _dot_nt(a, b):
    return lax.dot_general(a, b, (((1,), (1,)), ((), ())), preferred_element_type=F32)


def _dot_tn(a, b):
    return lax.dot_general(a, b, (((0,), (0,)), ((), ())), preferred_element_type=F32)


def _seq_masks(rows, seq_len):
    ri = lax.broadcasted_iota(jnp.int32, (rows, rows), 0)
    ci = lax.broadcasted_iota(jnp.int32, (rows, rows), 1)
    lower = jnp.where(ci <= ri, 1.0, 0.0)
    upper = jnp.where(ri <= ci, 1.0, 0.0)
    if seq_len == rows:
        return jnp.ones((rows, rows), F32), lower, upper
    shift = seq_len.bit_length() - 1
    same = jnp.where((ri >> shift) == (ci >> shift), 1.0, 0.0)
    return same, same * lower, same * upper


def _inproj_kernel(xp_ref, xs_ref, wt_hbm, *refs, n_f32, n_gate_tile, prompt_tiles, cast_steps):
    n_cast = (len(refs) - 9) // 2
    cast_in = refs[:n_cast]
    o_ref, f_ref, g_ref, gr_ref = refs[n_cast:n_cast + 4]
    cast_out = refs[n_cast + 4:2 * n_cast + 4]
    wbuf, wbf_sc, wg_sc, sem, gsem = refs[2 * n_cast + 4:]
    n = pl.program_id(0)
    m = pl.program_id(1)
    tn = wbf_sc.shape[1]

    @pl.when(m < cast_steps)
    def _cast_other_weights():
        for src, dst in zip(cast_in, cast_out):
            dst[...] = src[...].astype(BF16)

    def tile_copy(tile):
        start = pl.multiple_of(tile * tn + jnp.where(tile >= n_gate_tile, _N_GATES, 0), _N_GATES)
        return pltpu.make_async_copy(wt_hbm.at[pl.ds(start, tn), :], wbuf, sem.at[0])

    def gate_copy():
        return pltpu.make_async_copy(wt_hbm.at[pl.ds(_OFF_ML_I, _N_GATES), :],
                                     wg_sc.at[pl.ds(0, _N_GATES), :], gsem.at[0])

    @pl.when(m == 0)
    def _next_weight_tile():
        @pl.when(n == 0)
        def _first():
            tile_copy(0).start()
            wg_sc[_N_GATES:, :] = jnp.zeros((LANES - _N_GATES, wg_sc.shape[1]), F32)
            gate_copy().start()
            gate_copy().wait()

        tile_copy(n).wait()
        wbf_sc[...] = wbuf[...].T.astype(BF16)

        @pl.when(n + 1 < pl.num_programs(0))
        def _prefetch():
            tile_copy(n + 1).start()

    x = jnp.where(m < prompt_tiles, xp_ref[...], xs_ref[...])
    acc = jnp.dot(x, wbf_sc[...], preferred_element_type=F32)
    o_ref[...] = acc.astype(BF16)

    @pl.when(n == n_f32)
    def _f32_outputs():
        f_ref[...] = acc
        wg = wg_sc[...].astype(BF16)
        gates = _dot_nt(x, wg)
        g_ref[...] = gates
        gr_ref[0] = gates.T[0:_N_GATES, :]


def _inproj(xp_b, xs_b, w_t, cast_weights):
    k = xp_b.shape[1]
    tm, tn = _PROJ_TM, _PROJ_TN
    p_tiles = xp_b.shape[0] // tm
    n_m = p_tiles + xs_b.shape[0] // tm
    tokens = n_m * tm
    n_n = _P_WIDTH // tn
    n_f32 = _P_HG_F // tn
    cast_steps = min(n_m, 8)

    def parked(n, m):
        return jnp.where(n < n_f32, 0, jnp.where(n == n_f32, m, n_m - 1))

    def cast_spec(w):
        return pl.BlockSpec((w.shape[0] // n_n, w.shape[1] // cast_steps),
                            lambda n, m: (n, jnp.minimum(m, cast_steps - 1)))

    cast_specs = [cast_spec(w) for w in cast_weights]
    outs = pl.pallas_call(
        functools.partial(_inproj_kernel, n_f32=n_f32, n_gate_tile=_OFF_ML_I // tn,
                          prompt_tiles=p_tiles, cast_steps=cast_steps),
        grid=(n_n, n_m),
        in_specs=[pl.BlockSpec((tm, k), lambda n, m: (jnp.minimum(m, p_tiles - 1), 0)),
                  pl.BlockSpec((tm, k), lambda n, m: (jnp.maximum(m - p_tiles, 0), 0)),
                  pl.BlockSpec(memory_space=pl.ANY),
                  *cast_specs],
        out_specs=[pl.BlockSpec((tm, tn), lambda n, m: (m, n)),
                   pl.BlockSpec((tm, HG_K), lambda n, m: (parked(n, m), 0)),
                   pl.BlockSpec((tm, LANES), lambda n, m: (parked(n, m), 0)),
                   pl.BlockSpec((1, 8, tm), lambda n, m: (parked(n, m), 0, 0)),
                   *cast_specs],
        out_shape=[jax.ShapeDtypeStruct((tokens, _P_WIDTH), BF16),
                   jax.ShapeDtypeStruct((tokens, HG_K), F32),
                   jax.ShapeDtypeStruct((tokens, LANES), F32),
                   jax.ShapeDtypeStruct((n_m, 8, tm), F32),
                   *[jax.ShapeDtypeStruct(w.shape, BF16) for w in cast_weights]],
        scratch_shapes=[pltpu.VMEM((tn, k), F32),
                        pltpu.VMEM((k, tn), BF16),
                        pltpu.VMEM((LANES, k), F32),
                        pltpu.SemaphoreType.DMA((1,)),
                        pltpu.SemaphoreType.DMA((1,))],
        compiler_params=pltpu.CompilerParams(dimension_semantics=("arbitrary", "arbitrary")),
        name="in_proj",
    )(xp_b, xs_b, w_t, *cast_weights)
    return outs[:4], outs[4:]


def _mlstm_kernel(*refs, rows, seq_len, heads, zero_init, head0=None):
    if zero_init:
        (q_ref, k_ref, v_ref, og_ref, gc_ref, gr_ref, brow_ref, bcol_ref, ng_ref,
         ba_ref, c_ref, n_ref, mrow_ref, m_sc) = refs
        c0_ref, n0_ref = c_ref, n_ref
    else:
        (q_ref, k_ref, v_ref, og_ref, gc_ref, gr_ref, brow_ref, bcol_ref, ng_ref, c0_ref, n0_ref,
         ba_ref, c_ref, n_ref, mrow_ref) = refs
    nseq = rows // seq_len
    if head0 is None:
        head0 = pl.program_id(1) * heads

    if zero_init:
        @pl.when(pl.program_id(2) == 0)
        def _init():
            c_ref[...] = jnp.zeros(c_ref.shape, F32)
            n_ref[...] = jnp.zeros(n_ref.shape, F32)
            m_sc[...] = jnp.zeros(m_sc.shape, F32)

    same, lower, upper = _seq_masks(rows, seq_len)
    causal = lower > 0.5
    lane = lax.broadcasted_iota(jnp.int32, (rows, LANES), 1)
    sub = lax.broadcasted_iota(jnp.int32, (8, rows), 0)

    def sel_lane(x, idx):
        return jnp.sum(jnp.where(lane == idx, x, 0.0), axis=1, keepdims=True)

    def sel_sub(x, idx):
        return jnp.sum(jnp.where(sub == idx, x, 0.0), axis=0, keepdims=True)

    gc_raw = gc_ref[...]
    gc = gc_raw + brow_ref[...]
    lf_cols = _log_sigmoid(gc)
    b_cols = _dot01_right(lower.astype(BF16), lf_cols)
    gr = gr_ref[0] + bcol_ref[...]
    b_rows = _dot01_left(_log_sigmoid(gr), upper.astype(BF16))
    if nseq > 1:
        b_ends = _dot01_right(same.astype(BF16), lf_cols)
        ci = lax.broadcasted_iota(jnp.int32, (rows, rows), 1)
        last = (same * jnp.where((ci & (seq_len - 1)) == seq_len - 1, 1.0, 0.0)).astype(BF16)
        shift = seq_len.bit_length() - 1
        row_seq = lax.broadcasted_iota(jnp.int32, (rows, 1), 0) >> shift

    for hh in range(heads):
        head = head0 + hh
        ks = slice(hh * ML_DK, (hh + 1) * ML_DK)
        vs = slice(hh * ML_DV, (hh + 1) * ML_DV)
        ig_col = sel_lane(gc, head)
        b_col = sel_lane(b_cols, head + ML_HEADS)
        if zero_init:
            m_prev = jnp.broadcast_to(m_sc[hh:hh + 1, 0:1], (rows, 1))
        else:
            m_prev = sel_lane(gc_raw, head + 2 * ML_HEADS)
        ig_row = sel_sub(gr, head)
        b_row = sel_sub(b_rows, head + ML_HEADS)

        logd = jnp.where(causal, (b_col - b_row) + ig_row, -jnp.inf)
        m_t = jnp.maximum(b_col + m_prev, jnp.max(logd, axis=1, keepdims=True))
        d = jnp.exp(logd - m_t)
        w_inter = jnp.exp(b_col + m_prev - m_t)

        qb = q_ref[:, ks].astype(BF16)
        kb = k_ref[:, ks].astype(BF16)
        vb = v_ref[:, vs].astype(BF16)
        s = _dot_nt(qb, kb) * (d * ML_K_SCALE)
        num = jnp.dot(s.astype(BF16), vb, preferred_element_type=F32)
        den = jnp.sum(s, axis=1, keepdims=True)

        if nseq == 1:
            b_end = b_col[rows - 1:rows, :]
            m_new = m_t[rows - 1:rows, :]
        else:
            b_end = sel_lane(b_ends, head + ML_HEADS)
            m_new = _dot01_right(last, jnp.broadcast_to(m_t, (rows, LANES)))[:, 0:1]
        w_end = jnp.exp(b_end - b_col + ig_col - m_new)
        decay = jnp.exp(b_end + m_prev - m_new)

        qf = qb.astype(F32)
        kw = (w_end * ML_K_SCALE) * kb.astype(F32)

        if nseq == 1:
            c_prev = c0_ref[0, hh]
            n_prev = n0_ref[0, hh]
            q_c = jnp.dot(qb, c_prev.astype(BF16), preferred_element_type=F32)
            q_n = jnp.sum(qf * n_prev, axis=1, keepdims=True)
            dec = decay[0:1, :]
            c_ref[0, hh] = dec * c_prev + _dot_tn(kw.astype(BF16), vb)
            n_ref[0, hh] = dec * n_prev + jnp.sum(kw, axis=0, keepdims=True)
        else:
            q_c = jnp.zeros((rows, ML_DV), F32)
            q_n = jnp.zeros((rows, 1), F32)
            for j in range(nseq):
                in_seq = row_seq == j
                c_prev = c0_ref[j, hh]
                n_prev = n0_ref[j, hh]
                q_c = jnp.where(
                    in_seq, jnp.dot(qb, c_prev.astype(BF16), preferred_element_type=F32), q_c)
                q_n = jnp.where(in_seq, jnp.sum(qf * n_prev, axis=1, keepdims=True), q_n)
                kw_j = jnp.where(in_seq, kw, 0.0)
                dec = decay[j * seq_len:j * seq_len + 1, :]
                c_ref[j, hh] = dec * c_prev + _dot_tn(kw_j.astype(BF16), vb)
                n_ref[j, hh] = dec * n_prev + jnp.sum(kw_j, axis=0, keepdims=True)

        num = num + w_inter * q_c
        den = den + w_inter * q_n
        h_out = num / jnp.maximum(jnp.abs(den), jnp.exp(-m_t))
        mu = jnp.mean(h_out, axis=1, keepdims=True)
        xc = h_out - mu
        var = jnp.mean(xc * xc, axis=1, keepdims=True)
        hn = xc * lax.rsqrt(var + LN_EPS) * ng_ref[:, vs]
        ba_ref[:, vs] = (hn * _sigmoid(og_ref[:, vs].astype(F32))).astype(ba_ref.dtype)
        mrow_ref[hh] = jnp.broadcast_to(m_t, (rows, LANES))
        if zero_init:
            m_sc[hh:hh + 1, :] = jnp.broadcast_to(m_new, (1, LANES))


def _mlstm_parts(proj, gates_col, gates_row, brow, bcol, norm_g, state, *, row0, batch, seq, rows,
                 seq_len, heads, index=None):
    tokens = batch * seq
    zero_init = state is None
    nseq = rows // seq_len
    nchunk = seq // seq_len if nseq == 1 else 1
    ngroup = tokens // (rows * nchunk)
    bq, bv = heads * ML_DK, heads * ML_DV
    rb0 = row0 // rows
    per_row_tile = gates_row.shape[2] // rows
    head0 = None
    if index is None:
        index = lambda g, h, c: (g * nchunk + c, h, g)
    else:
        assert heads == ML_HEADS
        head0 = 0
    row = lambda *ids: index(*ids)[0]
    hblk = lambda *ids: index(*ids)[1]
    sblk = lambda *ids: index(*ids)[2]

    in_specs = [
        pl.BlockSpec((rows, bq), lambda *ids: (rb0 + row(*ids), _P_ML_Q // bq + hblk(*ids))),
        pl.BlockSpec((rows, bq), lambda *ids: (rb0 + row(*ids), _P_ML_K // bq + hblk(*ids))),
        pl.BlockSpec((rows, bv), lambda *ids: (rb0 + row(*ids), _P_ML_V // bv + hblk(*ids))),
        pl.BlockSpec((rows, bv), lambda *ids: (rb0 + row(*ids), _P_ML_O // bv + hblk(*ids))),
        pl.BlockSpec((rows, LANES), lambda *ids: (row(*ids), 0)),
        pl.BlockSpec((1, 8, rows),
                     lambda *ids: (row(*ids) // per_row_tile, 0, row(*ids) % per_row_tile)),
        pl.BlockSpec((1, LANES), lambda *ids: (0, 0)),
        pl.BlockSpec((8, 1), lambda *ids: (0, 0)),
        pl.BlockSpec((1, bv), lambda *ids: (0, hblk(*ids))),
    ]
    args = [proj, proj, proj, proj, gates_col, gates_row, brow, bcol, norm_g]
    state_specs = [
        pl.BlockSpec((nseq, heads, ML_DK, ML_DV), lambda *ids: (sblk(*ids), hblk(*ids), 0, 0)),
        pl.BlockSpec((nseq, heads, 1, ML_DK), lambda *ids: (sblk(*ids), hblk(*ids), 0, 0)),
    ]
    scratch = []
    if zero_init:
        scratch = [pltpu.VMEM((8, LANES), F32)]
    else:
        in_specs += state_specs
        args += [state[0], state[1]]
    out_specs = [
        pl.BlockSpec((rows, bv), lambda *ids: (row(*ids), hblk(*ids))),
        *state_specs,
        pl.BlockSpec((heads, rows, LANES), lambda *ids: (hblk(*ids), row(*ids), 0)),
    ]
    out_shape = [
        jax.ShapeDtypeStruct((tokens, ML_V), BF16),
        jax.ShapeDtypeStruct((batch, ML_HEADS, ML_DK, ML_DV), F32),
        jax.ShapeDtypeStruct((batch, ML_HEADS, 1, ML_DK), F32),
        jax.ShapeDtypeStruct((ML_HEADS, tokens, LANES), F32),
    ]
    return dict(
        kernel=functools.partial(_mlstm_kernel, rows=rows, seq_len=seq_len, heads=heads,
                                 zero_init=zero_init, head0=head0),
        grid=(ngroup, ML_HEADS // heads, nchunk),
        in_specs=in_specs, args=args, out_specs=out_specs, out_shape=out_shape, scratch=scratch)


def _run_parts(parts, name):
    n_in = [len(p["in_specs"]) for p in parts]
    n_out = [len(p["out_specs"]) for p in parts]
    n_scr = [len(p["scratch"]) for p in parts]

    def kernel(*refs):
        ins = refs[:sum(n_in)]
        outs = refs[sum(n_in):sum(n_in) + sum(n_out)]
        scr = refs[sum(n_in) + sum(n_out):]
        i = o = s = 0
        for p, ni, no, ns in zip(parts, n_in, n_out, n_scr):
            run = functools.partial(p["kernel"], *ins[i:i + ni], *outs[o:o + no], *scr[s:s + ns])
            if p.get("when") is None:
                run()
            else:
                pl.when(p["when"](*[pl.program_id(a) for a in range(len(parts[0]["grid"]))]))(run)
            i, o, s = i + ni, o + no, s + ns

    outs = pl.pallas_call(
        kernel,
        grid=parts[0]["grid"],
        in_specs=[s for p in parts for s in p["in_specs"]],
        out_specs=[s for p in parts for s in p["out_specs"]],
        out_shape=[s for p in parts for s in p["out_shape"]],
        scratch_shapes=[s for p in parts for s in p["scratch"]],
        compiler_params=pltpu.CompilerParams(
            dimension_semantics=("parallel", "parallel", "arbitrary")),
        name=name,
    )(*[a for p in parts for a in p["args"]])
    split, o = [], 0
    for no in n_out:
        split.append(outs[o:o + no])
        o += no
    return split


def _hgrn_level_ids(rows, seq_len, direct, levels):
    t = np.arange(rows)[:, None]
    s = np.arange(rows)[None, :]
    ids = np.full((rows, rows), -1, np.int32)
    count = ((t // direct == s // direct) & (s <= t)).astype(np.int32)
    for idx, (block, sub_size) in enumerate(levels):
        owned = (t // block == s // block) & ((s % block) // sub_size < (t % block) // sub_size)
        ids[owned] = idx
        count += owned
    wanted = (t // seq_len == s // seq_len) & (s <= t)
    assert np.array_equal(count, wanted.astype(np.int32)), (rows, seq_len, direct, levels)
    return ids


def _hgrn_intra(q, kin, g2, gk2, level_ids, rows, direct, levels):
    ngroups = max(rows // LANES, 1)
    rowi = lax.broadcasted_iota(jnp.int32, (direct, LANES), 0)
    lanei = lax.broadcasted_iota(jnp.int32, (direct, LANES), 1)
    keep = [jnp.where(lanei == s, rowi, -1) >= s for s in range(direct)]
    zero_group = jnp.zeros((direct, LANES), F32)
    prods = []
    for blk in range(rows // direct):
        r0 = blk * direct
        qb = q[r0:r0 + direct]
        gb = g2[r0:r0 + direct]
        gkb = gk2[r0:r0 + direct]
        for s in range(direct):
            prods.append(qb * jnp.exp2(gb - gkb[s:s + 1, :]))
    sums = jnp.dot(jnp.concatenate(prods, axis=0).astype(BF16), jnp.ones((HG_DK, LANES), BF16),
                   preferred_element_type=F32)
    panels = []
    for blk in range(rows // direct):
        r0 = blk * direct
        ag = zero_group
        for s in range(direct):
            p0 = (blk * direct + s) * direct
            ag = jnp.where(keep[s], sums[p0:p0 + direct], ag)
        lane0 = r0 % LANES
        if lane0:
            ag = pltpu.roll(ag, lane0, axis=1)
        grp = r0 // LANES
        pieces = [zero_group] * grp + [ag] + [zero_group] * (ngroups - grp - 1)
        panels.append(pieces[0] if ngroups == 1 else jnp.concatenate(pieces, axis=1))
    a = panels[0] if len(panels) == 1 else jnp.concatenate(panels, axis=0)
    if rows < LANES:
        a = a[:, :rows]

    if levels:
        rowid = lax.broadcasted_iota(jnp.int32, (rows, 1), 0)
    for idx, (block, sub_size) in enumerate(levels):
        sshift = sub_size.bit_length() - 1
        row_sub = (rowid & (block - 1)) >> sshift
        q_parts, k_parts = [], []
        for j in range(1, block // sub_size):
            refs = []
            for b0 in range(0, rows, block):
                r = b0 + j * sub_size - 1
                refs.append(jnp.broadcast_to(g2[r:r + 1, :], (block, HG_DK)))
            g_ref = refs[0] if len(refs) == 1 else jnp.concatenate(refs, axis=0)
            e = jnp.exp2(-jnp.abs(g2 - g_ref))
            q_parts.append(jnp.where(row_sub == j, q * e, 0.0).astype(BF16))
            k_parts.append((kin * e).astype(BF16))
        qcat = q_parts[0] if len(q_parts) == 1 else jnp.concatenate(q_parts, axis=1)
        kcat = k_parts[0] if len(k_parts) == 1 else jnp.concatenate(k_parts, axis=1)
        a = jnp.where(level_ids == idx, _dot_nt(qcat, kcat), a)
    return a


def _hgrn_kernel(*refs, layer, rows, seq_len, heads, direct, levels, zero_init):
    if zero_init:
        q_ref, f_ref, i_ref, g_ref, lbl_ref, ng_ref, lvl_ref, bb_ref, s_ref = refs
        s0_ref = s_ref
    else:
        q_ref, f_ref, i_ref, g_ref, lbl_ref, ng_ref, lvl_ref, s0_ref, bb_ref, s_ref = refs
    nseq = rows // seq_len

    if zero_init:
        @pl.when(pl.program_id(2) == 0)
        def _init():
            s_ref[...] = jnp.zeros(s_ref.shape, F32)

    same, lower, _ = _seq_masks(rows, seq_len)
    lower_b = lower.astype(BF16)
    same_b = same.astype(BF16)
    level_ids = lvl_ref[...]
    if nseq > 1:
        shift = seq_len.bit_length() - 1
        row_seq = lax.broadcasted_iota(jnp.int32, (rows, 1), 0) >> shift

    def decay_cols(row):
        col = jnp.broadcast_to(jnp.exp2(row), (HG_DK, HG_DK)).T
        return jnp.concatenate([col] * (HG_DV // HG_DK), axis=1)

    for hh in range(heads):
        ks = slice(hh * HG_DK, (hh + 1) * HG_DK)
        vs = slice(hh * HG_DV, (hh + 1) * HG_DV)
        lg = lbl_ref[:, ks]
        ex = jnp.exp(lg - jnp.max(lg, axis=0, keepdims=True))
        lb = (jnp.sum(ex[0:layer + 1, :], axis=0, keepdims=True)
              / jnp.sum(ex, axis=0, keepdims=True))

        f = lb + (1.0 - lb) * _sigmoid(f_ref[:, ks])
        kin = 1.0 - f
        lf2 = jnp.log(f) * LOG2_E
        g2 = _dot01_right(lower_b, lf2)
        gk2 = g2 - jnp.log(kin) * LOG2_E
        q = q_ref[:, ks].astype(F32)
        if nseq == 1:
            g_end = jnp.broadcast_to(g2[rows - 1:rows, :], (rows, HG_DK))
        else:
            g_end = _dot01_right(same_b, lf2)
        qg = (q * jnp.exp2(g2)).astype(BF16)
        kg = kin * jnp.exp2(g_end - g2)
        ib = i_ref[:, vs].astype(BF16)

        a = _hgrn_intra(q, kin, g2, gk2, level_ids, rows, direct, levels)
        o = jnp.dot(a.astype(BF16), ib, preferred_element_type=F32)

        if nseq == 1:
            s_prev = s0_ref[0, hh]
            o = o + jnp.dot(qg, s_prev.astype(BF16), preferred_element_type=F32)
            s_ref[0, hh] = decay_cols(g_end[0:1, :]) * s_prev + _dot_tn(kg.astype(BF16), ib)
        else:
            for j in range(nseq):
                in_seq = row_seq == j
                s_prev = s0_ref[j, hh]
                o_j = jnp.dot(qg, s_prev.astype(BF16), preferred_element_type=F32)
                o = o + jnp.where(in_seq, o_j, 0.0)
                kg_j = jnp.where(in_seq, kg, 0.0).astype(BF16)
                s_ref[j, hh] = (decay_cols(g_end[j * seq_len:j * seq_len + 1, :]) * s_prev
                                + _dot_tn(kg_j, ib))

        o = o * lax.rsqrt(jnp.mean(o * o, axis=1, keepdims=True) + LN_EPS)
        gate = g_ref[:, vs].astype(F32)
        bb_ref[:, vs] = (o * ng_ref[:, vs] * (gate * _sigmoid(gate))).astype(bb_ref.dtype)


def _hgrn_parts(proj, hg_f, lb_logits, norm_g, s0, *, layer, row0, batch, seq, rows, seq_len, heads,
                direct, levels, index=None):
    tokens = batch * seq
    zero_init = s0 is None
    nseq = rows // seq_len
    nchunk = seq // seq_len if nseq == 1 else 1
    ngroup = tokens // (rows * nchunk)
    rb0 = row0 // rows
    if index is None:
        index = lambda g, h, c: (g * nchunk + c, h, g)
    row = lambda *ids: index(*ids)[0]
    hblk = lambda *ids: index(*ids)[1]
    sblk = lambda *ids: index(*ids)[2]

    wk, wv = heads * HG_DK, heads * HG_DV
    in_specs = [
        pl.BlockSpec((rows, wk), lambda *ids: (rb0 + row(*ids), _P_HG_Q // wk + hblk(*ids))),
        pl.BlockSpec((rows, wk), lambda *ids: (rb0 + row(*ids), hblk(*ids))),
        pl.BlockSpec((rows, wv), lambda *ids: (rb0 + row(*ids), _P_HG_I // wv + hblk(*ids))),
        pl.BlockSpec((rows, wv), lambda *ids: (rb0 + row(*ids), _P_HG_G // wv + hblk(*ids))),
        pl.BlockSpec((DEPTH + 1, wk), lambda *ids: (0, hblk(*ids))),
        pl.BlockSpec((1, wv), lambda *ids: (0, hblk(*ids))),
        pl.BlockSpec((rows, rows), lambda *ids: (0, 0)),
    ]
    level_ids = jnp.asarray(_hgrn_level_ids(rows, seq_len, direct, levels))
    args = [proj, hg_f, proj, proj, lb_logits, norm_g, level_ids]
    state_spec = pl.BlockSpec((nseq, heads, HG_DK, HG_DV),
                              lambda *ids: (sblk(*ids), hblk(*ids), 0, 0))
    if not zero_init:
        in_specs.append(state_spec)
        args.append(s0)
    return dict(
        kernel=functools.partial(_hgrn_kernel, layer=layer, rows=rows, seq_len=seq_len, heads=heads,
                                 direct=direct, levels=levels, zero_init=zero_init),
        grid=(ngroup, HG_HEADS // heads, nchunk),
        in_specs=in_specs, args=args,
        out_specs=[pl.BlockSpec((rows, wv), lambda *ids: (row(*ids), hblk(*ids))), state_spec],
        out_shape=[jax.ShapeDtypeStruct((tokens, HG_V), BF16),
                   jax.ShapeDtypeStruct((batch, HG_HEADS, HG_DK, HG_DV), F32)],
        scratch=[])


def _merge_kernel(ba_ref, wa_ref, bb_ref, wb_ref, ga_ref, gb_ref, o_ref, wa_sc, wb_sc):
    @pl.when(pl.program_id(1) == 0)
    def _cast_weights():
        wa_sc[...] = wa_ref[...].astype(BF16)
        wb_sc[...] = wb_ref[...].astype(BF16)

    ya = jnp.dot(ba_ref[...], wa_sc[...], preferred_element_type=F32)
    yb = jnp.dot(bb_ref[...], wb_sc[...], preferred_element_type=F32)
    ga = _sigmoid(ga_ref[...].astype(F32))
    gb = _sigmoid(gb_ref[...].astype(F32))
    o_ref[...] = (ga * ya + gb * yb).astype(o_ref.dtype)


def _merge(branch_a, w_a, branch_b, w_b, proj, row0):
    m = branch_a.shape[0]
    tm, tn = _MERGE_TM, _MERGE_TN
    rb0 = row0 // tm
    ga_blk = _P_GATE_A // tn
    gb_blk = _P_GATE_B // tn
    return pl.pallas_call(
        _merge_kernel,
        grid=(D_MODEL // tn, m // tm),
        in_specs=[pl.BlockSpec((tm, ML_V), lambda j, i: (i, 0)),
                  pl.BlockSpec((ML_V, tn), lambda j, i: (0, j)),
                  pl.BlockSpec((tm, HG_V), lambda j, i: (i, 0)),
                  pl.BlockSpec((HG_V, tn), lambda j, i: (0, j)),
                  pl.BlockSpec((tm, tn), lambda j, i: (rb0 + i, ga_blk + j)),
                  pl.BlockSpec((tm, tn), lambda j, i: (rb0 + i, gb_blk + j))],
        out_specs=pl.BlockSpec((tm, tn), lambda j, i: (i, j)),
        out_shape=jax.ShapeDtypeStruct((m, D_MODEL), BF16),
        scratch_shapes=[pltpu.VMEM((ML_V, tn), BF16), pltpu.VMEM((HG_V, tn), BF16)],
        compiler_params=pltpu.CompilerParams(dimension_semantics=("parallel", "arbitrary")),
        name="merge",
    )(branch_a, w_a, branch_b, w_b, proj, proj)


def _layernorm_rows(z, g, b):
    mu = jnp.mean(z, axis=1, keepdims=True)
    zc = z - mu
    var = jnp.mean(zc * zc, axis=1, keepdims=True)
    return zc * lax.rsqrt(var + LN_EPS) * g + b


def _outproj_kernel(mg_ref, w_ref, x_ref, g_ref, b_ref, x1_ref, x1b_ref):
    mix = jnp.dot(mg_ref[...], w_ref[...], preferred_element_type=F32)
    x1 = _layernorm_rows(DEEPNORM_ALPHA * x_ref[...] + mix, g_ref[...], b_ref[...])
    x1_ref[...] = x1
    x1b_ref[...] = x1.astype(BF16)


def _outproj(merged, w_out, x, ln_g, ln_b):
    tm = _OUT_TM
    m = x.shape[0]
    row = lambda i: (i, 0)
    const = lambda i: (0, 0)
    return pl.pallas_call(
        _outproj_kernel,
        grid=(m // tm,),
        in_specs=[pl.BlockSpec((tm, D_MODEL), row),
                  pl.BlockSpec((D_MODEL, D_MODEL), const),
                  pl.BlockSpec((tm, D_MODEL), row),
                  pl.BlockSpec((1, D_MODEL), const),
                  pl.BlockSpec((1, D_MODEL), const)],
        out_specs=[pl.BlockSpec((tm, D_MODEL), row), pl.BlockSpec((tm, D_MODEL), row)],
        out_shape=[jax.ShapeDtypeStruct((m, D_MODEL), F32),
                   jax.ShapeDtypeStruct((m, D_MODEL), BF16)],
        compiler_params=pltpu.CompilerParams(dimension_semantics=("parallel",)),
        name="out_proj_ln",
    )(merged, w_out, x, ln_g, ln_b)


def _mlp_kernel(x1b_ref, wu_ref, wd_ref, x1_ref, g_ref, b_ref, y_ref):
    f = pl.program_id(1)

    @pl.when(f == 0)
    def _init():
        y_ref[...] = jnp.zeros(y_ref.shape, F32)

    hid = jnp.maximum(jnp.dot(x1b_ref[...], wu_ref[...], preferred_element_type=F32), 0.0)
    hid = (hid * hid).astype(BF16)
    y_ref[...] += jnp.dot(hid, wd_ref[...], preferred_element_type=F32)

    @pl.when(f == pl.num_programs(1) - 1)
    def _finish():
        z = DEEPNORM_ALPHA * x1_ref[...] + y_ref[...]
        y_ref[...] = _layernorm_rows(z, g_ref[...], b_ref[...])


def _mlp(x1b, w_up, w_down, x1, ln_g, ln_b):
    tm, tf = _MLP_TM, _MLP_TF
    tokens = x1.shape[0]
    return pl.pallas_call(
        _mlp_kernel,
        grid=(tokens // tm, D_FF // tf),
        in_specs=[pl.BlockSpec((tm, D_MODEL), lambda i, f: (i, 0)),
                  pl.BlockSpec((D_MODEL, tf), lambda i, f: (0, f)),
                  pl.BlockSpec((tf, D_MODEL), lambda i, f: (f, 0)),
                  pl.BlockSpec((tm, D_MODEL), lambda i, f: (i, 0)),
                  pl.BlockSpec((1, D_MODEL), lambda i, f: (0, 0)),
                  pl.BlockSpec((1, D_MODEL), lambda i, f: (0, 0))],
        out_specs=pl.BlockSpec((tm, D_MODEL), lambda i, f: (i, 0)),
        out_shape=jax.ShapeDtypeStruct((tokens, D_MODEL), F32),
        compiler_params=pltpu.CompilerParams(dimension_semantics=("parallel", "arbitrary")),
        name="mlp_ln",
    )(x1b, w_up, w_down, x1, ln_g, ln_b)


_PROMPT_ML = dict(rows=256, seq_len=256, heads=4)
_PROMPT_HG = dict(rows=256, seq_len=256, heads=4, direct=8,
                  levels=((32, 8), (128, 32), (256, 128)))
_SAMPLE_ML = dict(rows=32, seq_len=8, heads=4)
_SAMPLE_HG = dict(rows=64, seq_len=8, heads=8, direct=8, levels=())


def kernel(x_prompt, x_sample, state_mlstm_C, state_mlstm_n, state_mlstm_m, state_hgrn_S,
           hg_lb_logits, w_in, b_ig, b_fg, ml_norm_g, hg_norm_g, w_branch_a, w_branch_b, w_out,
           ln1_g, ln1_b, w_up, w_down, ln2_g, ln2_b):
    batch_p, seq_p, _ = x_prompt.shape
    batch_s, seq_s, _ = x_sample.shape
    tok_p, tok_s = batch_p * seq_p, batch_s * seq_s
    total = tok_p + tok_s
    group_p = dict(row0=0, batch=batch_p, seq=seq_p)
    group_s = dict(row0=tok_p, batch=batch_s, seq=seq_s)
    xp = x_prompt.reshape(tok_p, D_MODEL)
    xs = x_sample.reshape(tok_s, D_MODEL)
    lb_logits = hg_lb_logits.astype(F32)
    states_p, states_s = [], []
    for l in range(DEPTH):
        (proj, hg_f, gates, gates_gm), (w_out_b, w_up_b, w_down_b) = _inproj(
            xp.astype(BF16), xs.astype(BF16), jnp.swapaxes(w_in[l], 0, 1),
            (w_out[l], w_up[l], w_down[l]))

        gate_bias = jnp.concatenate([b_ig[l], b_fg[l]]).astype(F32)
        brow = jnp.zeros((1, LANES), F32).at[0, :_N_GATES].set(gate_bias)
        bcol = gate_bias.reshape(_N_GATES, 1)
        ml_g = ml_norm_g[l].reshape(1, ML_V).astype(F32)
        hg_g = hg_norm_g[l].reshape(1, HG_V).astype(F32)

        m_rows = jnp.repeat(state_mlstm_m[l].astype(F32), seq_s, axis=0)
        gates_s = gates[tok_p:].at[:, _N_GATES:_N_GATES + ML_HEADS].set(m_rows)
        rows_s = _SAMPLE_ML["rows"]
        gm_s = gates_gm[tok_p // _PROJ_TM:]
        gm_s = gm_s.reshape(-1, 8, _PROJ_TM // rows_s, rows_s).transpose(0, 2, 1, 3)
        gm_s = gm_s.reshape(tok_s // rows_s, 8, rows_s)
        state_ml = (state_mlstm_C[l].astype(F32),
                    state_mlstm_n[l].astype(F32).reshape(batch_s, ML_HEADS, 1, ML_DK))
        ln1 = (ln1_g[l].reshape(1, D_MODEL), ln1_b[l].reshape(1, D_MODEL))
        ln2 = (ln2_g[l].reshape(1, D_MODEL), ln2_b[l].reshape(1, D_MODEL))

        ml_p = _mlstm_parts(proj, gates, gates_gm, brow, bcol, ml_g, None, **group_p, **_PROMPT_ML)
        hg_p = _hgrn_parts(proj, hg_f, lb_logits, hg_g, None, layer=l, **group_p, **_PROMPT_HG)

        def guest(parts_fn, host, rows, **kwargs):
            grid = host["grid"]
            every = math.prod(grid) * rows // tok_s
            assert every >= 1 and math.prod(grid) * rows == every * tok_s, (grid, rows)
            flat = lambda g, h, c: (g * grid[1] + h) * grid[2] + c
            parts = parts_fn(index=lambda g, h, c: (flat(g, h, c) // every, 0, flat(g, h, c) // every),
                             rows=rows, **kwargs)
            if every > 1:
                parts["when"] = lambda g, h, c: flat(g, h, c) % every == 0
            return parts

        ml_s = guest(functools.partial(_mlstm_parts, proj, gates_s, gm_s, brow, bcol, ml_g, state_ml,
                                       **group_s), hg_p, **_SAMPLE_ML)
        hg_s = guest(functools.partial(_hgrn_parts, proj, hg_f, lb_logits, hg_g,
                                       state_hgrn_S[l].astype(F32), layer=l, **group_s),
                     ml_p, **_SAMPLE_HG)
        (bb_p, s_p), (ba_s, c_s, n_s, m_all_s) = _run_parts([hg_p, ml_s], "hgrn2_prompt_mlstm_sample")
        (ba_p, c_p, n_p, m_all_p), (bb_s, s_s) = _run_parts([ml_p, hg_s], "mlstm_prompt_hgrn2_sample")

        merged_p = _merge(ba_p, w_branch_a[l], bb_p, w_branch_b[l], proj, 0)
        merged_s = _merge(ba_s, w_branch_a[l], bb_s, w_branch_b[l], proj, tok_p)
        x1_p, x1b_p = _outproj(merged_p, w_out_b, xp, *ln1)
        x1_s, x1b_s = _outproj(merged_s, w_out_b, xs, *ln1)
        xp = _mlp(x1b_p, w_up_b, w_down_b, x1_p, *ln2)
        xs = _mlp(x1b_s, w_up_b, w_down_b, x1_s, *ln2)

        states_p.append((c_p, n_p.reshape(batch_p, ML_HEADS, ML_DK),
                         m_all_p[:, seq_p - 1::seq_p, 0].T, s_p))
        states_s.append((c_s, n_s.reshape(batch_s, ML_HEADS, ML_DK),
                         m_all_s[:, seq_s - 1::seq_s, 0].T, s_s))
    stack = lambda states, k: jnp.stack([s[k] for s in states])
    return (xp.reshape(batch_p, seq_p, D_MODEL), xs.reshape(batch_s, seq_s, D_MODEL),
            stack(states_p, 0), stack(states_p, 1), stack(states_p, 2), stack(states_p, 3),
            stack(states_s, 0), stack(states_s, 1), stack(states_s, 2), stack(states_s, 3))
```

```python
import functools
import math

import numpy as np
import jax
import jax.numpy as jnp
from jax import lax
from jax.experimental import pallas as pl
from jax.experimental.pallas import tpu as pltpu

F32 = jnp.float32
BF16 = jnp.bfloat16

D_MODEL = 2048
DEPTH = 1
ML_HEADS, ML_DK, ML_DV = 4, 256, 512
HG_HEADS, HG_DK, HG_DV = 8, 128, 256
ML_QK = ML_HEADS * ML_DK
ML_V = ML_HEADS * ML_DV
HG_K = HG_HEADS * HG_DK
HG_V = HG_HEADS * HG_DV
D_FF = 4 * D_MODEL
LN_EPS = 1e-5
DEEPNORM_ALPHA = (2.0 * DEPTH) ** 0.25
ML_K_SCALE = ML_DK ** -0.5
LOG2_E = math.log2(math.e)
LANES = 128

_OFF_ML_Q = 0
_OFF_ML_K = _OFF_ML_Q + ML_QK
_OFF_ML_V = _OFF_ML_K + ML_QK
_OFF_ML_I = _OFF_ML_V + ML_V
_OFF_ML_F = _OFF_ML_I + ML_HEADS
_OFF_ML_O = _OFF_ML_F + ML_HEADS
_OFF_HG_Q = _OFF_ML_O + ML_V
_OFF_HG_F = _OFF_HG_Q + HG_K
_OFF_HG_I = _OFF_HG_F + HG_K
_OFF_HG_G = _OFF_HG_I + HG_V
_OFF_GATE_A = _OFF_HG_G + HG_V
_OFF_GATE_B = _OFF_GATE_A + D_MODEL
D_IN = _OFF_GATE_B + D_MODEL

_N_GATES = 2 * ML_HEADS
_P_ML_Q = _OFF_ML_Q
_P_ML_K = _OFF_ML_K
_P_ML_V = _OFF_ML_V
_P_ML_O = _OFF_ML_O - _N_GATES
_P_HG_Q = _OFF_HG_Q - _N_GATES
_P_HG_F = _OFF_HG_F - _N_GATES
_P_HG_I = _OFF_HG_I - _N_GATES
_P_HG_G = _OFF_HG_G - _N_GATES
_P_GATE_A = _OFF_GATE_A - _N_GATES
_P_GATE_B = _OFF_GATE_B - _N_GATES
_P_WIDTH = D_IN - _N_GATES

_PROJ_TM, _PROJ_TN = 1024, 1024
_MERGE_TM, _MERGE_TN = 1024, 1024
_OUT_TM = 512
_MLP_TM, _MLP_TF = 1024, 512


def _sigmoid(x):
    return 1.0 / (1.0 + jnp.exp(-x))


def _log_sigmoid(x):
    return jnp.minimum(x, 0.0) - jnp.log1p(jnp.exp(-jnp.abs(x)))


def _split2(x):
    hi = x.astype(BF16)
    lo = (x - hi.astype(F32)).astype(BF16)
    return hi, lo


def _dot01_right(t01, x):
    hi, lo = _split2(x)
    return (jnp.dot(t01, hi, preferred_element_type=F32)
            + jnp.dot(t01, lo, preferred_element_type=F32))


def _dot01_left(x, t01):
    hi, lo = _split2(x)
    return (jnp.dot(hi, t01, preferred_element_type=F32)
            + jnp.dot(lo, t01, preferred_element_type=F32))


def _dot_nt(a, b):
    return lax.dot_general(a, b, (((1,), (1,)), ((), ())), preferred_element_type=F32)


def _dot_tn(a, b):
    return lax.dot_general(a, b, (((0,), (0,)), ((), ())), preferred_element_type=F32)


def _seq_masks(rows, seq_len):
    ri = lax.broadcasted_iota(jnp.int32, (rows, rows), 0)
    ci = lax.broadcasted_iota(jnp.int32, (rows, rows), 1)
    lower = jnp.where(ci <= ri, 1.0, 0.0)
    upper = jnp.where(ri <= ci, 1.0, 0.0)
    if seq_len == rows:
        return jnp.ones((rows, rows), F32), lower, upper
    shift = seq_len.bit_length() - 1
    same = jnp.where((ri >> shift) == (ci >> shift), 1.0, 0.0)
    return same, same * lower, same * upper


def _inproj_kernel(xp_ref, xs_ref, wt_hbm, *refs, n_f32, n_gate_tile, prompt_tiles, cast_steps):
    n_cast = (len(refs) - 9) // 2
    cast_in = refs[:n_cast]
    o_ref, f_ref, g_ref, gr_ref = refs[n_cast:n_cast + 4]
    cast_out = refs[n_cast + 4:2 * n_cast + 4]
    wbuf, wbf_sc, wg_sc, sem, gsem = refs[2 * n_cast + 4:]
    n = pl.program_id(0)
    m = pl.program_id(1)
    tn = wbf_sc.shape[1]

    @pl.when(m < cast_steps)
    def _cast_other_weights():
        for src, dst in zip(cast_in, cast_out):
            dst[...] = src[...].astype(BF16)

    def tile_copy(tile):
        start = pl.multiple_of(tile * tn + jnp.where(tile >= n_gate_tile, _N_GATES, 0), _N_GATES)
        return pltpu.make_async_copy(wt_hbm.at[pl.ds(start, tn), :], wbuf, sem.at[0])

    def gate_copy():
        return pltpu.make_async_copy(wt_hbm.at[pl.ds(_OFF_ML_I, _N_GATES), :],
                                     wg_sc.at[pl.ds(0, _N_GATES), :], gsem.at[0])

    @pl.when(m == 0)
    def _next_weight_tile():
        @pl.when(n == 0)
        def _first():
            tile_copy(0).start()
            wg_sc[_N_GATES:, :] = jnp.zeros((LANES - _N_GATES, wg_sc.shape[1]), F32)
            gate_copy().start()
            gate_copy().wait()

        tile_copy(n).wait()
        wbf_sc[...] = wbuf[...].T.astype(BF16)

        @pl.when(n + 1 < pl.num_programs(0))
        def _prefetch():
            tile_copy(n + 1).start()

    x = jnp.where(m < prompt_tiles, xp_ref[...], xs_ref[...])
    acc = jnp.dot(x, wbf_sc[...], preferred_element_type=F32)
    o_ref[...] = acc.astype(BF16)

    @pl.when(n == n_f32)
    def _f32_outputs():
        f_ref[...] = acc
        wg = wg_sc[...].astype(BF16)
        gates = _dot_nt(x, wg)
        g_ref[...] = gates
        gr_ref[0] = gates.T[0:_N_GATES, :]


def _inproj(xp_b, xs_b, w_t, cast_weights):
    k = xp_b.shape[1]
    tm, tn = _PROJ_TM, _PROJ_TN
    p_tiles = xp_b.shape[0] // tm
    n_m = p_tiles + xs_b.shape[0] // tm
    tokens = n_m * tm
    n_n = _P_WIDTH // tn
    n_f32 = _P_HG_F // tn
    cast_steps = min(n_m, 8)

    def parked(n, m):
        return jnp.where(n < n_f32, 0, jnp.where(n == n_f32, m, n_m - 1))

    def cast_spec(w):
        return pl.BlockSpec((w.shape[0] // n_n, w.shape[1] // cast_steps),
                            lambda n, m: (n, jnp.minimum(m, cast_steps - 1)))

    cast_specs = [cast_spec(w) for w in cast_weights]
    outs = pl.pallas_call(
        functools.partial(_inproj_kernel, n_f32=n_f32, n_gate_tile=_OFF_ML_I // tn,
                          prompt_tiles=p_tiles, cast_steps=cast_steps),
        grid=(n_n, n_m),
        in_specs=[pl.BlockSpec((tm, k), lambda n, m: (jnp.minimum(m, p_tiles - 1), 0)),
                  pl.BlockSpec((tm, k), lambda n, m: (jnp.maximum(m - p_tiles, 0), 0)),
                  pl.BlockSpec(memory_space=pl.ANY),
                  *cast_specs],
        out_specs=[pl.BlockSpec((tm, tn), lambda n, m: (m, n)),
                   pl.BlockSpec((tm, HG_K), lambda n, m: (parked(n, m), 0)),
                   pl.BlockSpec((tm, LANES), lambda n, m: (parked(n, m), 0)),
                   pl.BlockSpec((1, 8, tm), lambda n, m: (parked(n, m), 0, 0)),
                   *cast_specs],
        out_shape=[jax.ShapeDtypeStruct((tokens, _P_WIDTH), BF16),
                   jax.ShapeDtypeStruct((tokens, HG_K), F32),
                   jax.ShapeDtypeStruct((tokens, LANES), F32),
                   jax.ShapeDtypeStruct((n_m, 8, tm), F32),
                   *[jax.ShapeDtypeStruct(w.shape, BF16) for w in cast_weights]],
        scratch_shapes=[pltpu.VMEM((tn, k), F32),
                        pltpu.VMEM((k, tn), BF16),
                        pltpu.VMEM((LANES, k), F32),
                        pltpu.SemaphoreType.DMA((1,)),
                        pltpu.SemaphoreType.DMA((1,))],
        compiler_params=pltpu.CompilerParams(dimension_semantics=("arbitrary", "arbitrary")),
        name="in_proj",
    )(xp_b, xs_b, w_t, *cast_weights)
    return outs[:4], outs[4:]


def _mlstm_kernel(*refs, rows, seq_len, heads, zero_init, head0=None):
    if zero_init:
        (q_ref, k_ref, v_ref, og_ref, gc_ref, gr_ref, brow_ref, bcol_ref, ng_ref,
         ba_ref, c_ref, n_ref, mrow_ref, m_sc) = refs
        c0_ref, n0_ref = c_ref, n_ref
    else:
        (q_ref, k_ref, v_ref, og_ref, gc_ref, gr_ref, brow_ref, bcol_ref, ng_ref, c0_ref, n0_ref,
         ba_ref, c_ref, n_ref, mrow_ref) = refs
    nseq = rows // seq_len
    if head0 is None:
        head0 = pl.program_id(1) * heads

    if zero_init:
        @pl.when(pl.program_id(2) == 0)
        def _init():
            c_ref[...] = jnp.zeros(c_ref.shape, F32)
            n_ref[...] = jnp.zeros(n_ref.shape, F32)
            m_sc[...] = jnp.zeros(m_sc.shape, F32)

    same, lower, upper = _seq_masks(rows, seq_len)
    causal = lower > 0.5
    lane = lax.broadcasted_iota(jnp.int32, (rows, LANES), 1)
    sub = lax.broadcasted_iota(jnp.int32, (8, rows), 0)

    def sel_lane(x, idx):
        return jnp.sum(jnp.where(lane == idx, x, 0.0), axis=1, keepdims=True)

    def sel_sub(x, idx):
        return jnp.sum(jnp.where(sub == idx, x, 0.0), axis=0, keepdims=True)

    gc_raw = gc_ref[...]
    gc = gc_raw + brow_ref[...]
    lf_cols = _log_sigmoid(gc)
    b_cols = _dot01_right(lower.astype(BF16), lf_cols)
    gr = gr_ref[0] + bcol_ref[...]
    b_rows = _dot01_left(_log_sigmoid(gr), upper.astype(BF16))
    if nseq > 1:
        b_ends = _dot01_right(same.astype(BF16), lf_cols)
        ci = lax.broadcasted_iota(jnp.int32, (rows, rows), 1)
        last = (same * jnp.where((ci & (seq_len - 1)) == seq_len - 1, 1.0, 0.0)).astype(BF16)
        shift = seq_len.bit_length() - 1
        row_seq = lax.broadcasted_iota(jnp.int32, (rows, 1), 0) >> shift

    for hh in range(heads):
        head = head0 + hh
        ks = slice(hh * ML_DK, (hh + 1) * ML_DK)
        vs = slice(hh * ML_DV, (hh + 1) * ML_DV)
        ig_col = sel_lane(gc, head)
        b_col = sel_lane(b_cols, head + ML_HEADS)
        if zero_init:
            m_prev = jnp.broadcast_to(m_sc[hh:hh + 1, 0:1], (rows, 1))
        else:
            m_prev = sel_lane(gc_raw, head + 2 * ML_HEADS)
        ig_row = sel_sub(gr, head)
        b_row = sel_sub(b_rows, head + ML_HEADS)

        logd = jnp.where(causal, (b_col - b_row) + ig_row, -jnp.inf)
        m_t = jnp.maximum(b_col + m_prev, jnp.max(logd, axis=1, keepdims=True))
        d = jnp.exp(logd - m_t)
        w_inter = jnp.exp(b_col + m_prev - m_t)

        qb = q_ref[:, ks].astype(BF16)
        kb = k_ref[:, ks].astype(BF16)
        vb = v_ref[:, vs].astype(BF16)
        s = _dot_nt(qb, kb) * (d * ML_K_SCALE)
        num = jnp.dot(s.astype(BF16), vb, preferred_element_type=F32)
        den = jnp.sum(s, axis=1, keepdims=True)

        if nseq == 1:
            b_end = b_col[rows - 1:rows, :]
            m_new = m_t[rows - 1:rows, :]
        else:
            b_end = sel_lane(b_ends, head + ML_HEADS)
            m_new = _dot01_right(last, jnp.broadcast_to(m_t, (rows, LANES)))[:, 0:1]
        w_end = jnp.exp(b_end - b_col + ig_col - m_new)
        decay = jnp.exp(b_end + m_prev - m_new)

        qf = qb.astype(F32)
        kw = (w_end * ML_K_SCALE) * kb.astype(F32)

        if nseq == 1:
            c_prev = c0_ref[0, hh]
            n_prev = n0_ref[0, hh]
            q_c = jnp.dot(qb, c_prev.astype(BF16), preferred_element_type=F32)
            q_n = jnp.sum(qf * n_prev, axis=1, keepdims=True)
            dec = decay[0:1, :]
            c_ref[0, hh] = dec * c_prev + _dot_tn(kw.astype(BF16), vb)
            n_ref[0, hh] = dec * n_prev + jnp.sum(kw, axis=0, keepdims=True)
        else:
            q_c = jnp.zeros((rows, ML_DV), F32)
            q_n = jnp.zeros((rows, 1), F32)
            for j in range(nseq):
                in_seq = row_seq == j
                c_prev = c0_ref[j, hh]
                n_prev = n0_ref[j, hh]
                q_c = jnp.where(
                    in_seq, jnp.dot(qb, c_prev.astype(BF16), preferred_element_type=F32), q_c)
                q_n = jnp.where(in_seq, jnp.sum(qf * n_prev, axis=1, keepdims=True), q_n)
                kw_j = jnp.where(in_seq, kw, 0.0)
                dec = decay[j * seq_len:j * seq_len + 1, :]
                c_ref[j, hh] = dec * c_prev + _dot_tn(kw_j.astype(BF16), vb)
                n_ref[j, hh] = dec * n_prev + jnp.sum(kw_j, axis=0, keepdims=True)

        num = num + w_inter * q_c
        den = den + w_inter * q_n
        h_out = num / jnp.maximum(jnp.abs(den), jnp.exp(-m_t))
        mu = jnp.mean(h_out, axis=1, keepdims=True)
        xc = h_out - mu
        var = jnp.mean(xc * xc, axis=1, keepdims=True)
        hn = xc * lax.rsqrt(var + LN_EPS) * ng_ref[:, vs]
        ba_ref[:, vs] = (hn * _sigmoid(og_ref[:, vs].astype(F32))).astype(ba_ref.dtype)
        mrow_ref[hh] = jnp.broadcast_to(m_t, (rows, LANES))
        if zero_init:
            m_sc[hh:hh + 1, :] = jnp.broadcast_to(m_new, (1, LANES))


def _mlstm_parts(proj, gates_col, gates_row, brow, bcol, norm_g, state, *, row0, batch, seq, rows,
                 seq_len, heads, index=None):
    tokens = batch * seq
    zero_init = state is None
    nseq = rows // seq_len
    nchunk = seq // seq_len if nseq == 1 else 1
    ngroup = tokens // (rows * nchunk)
    bq, bv = heads * ML_DK, heads * ML_DV
    rb0 = row0 // rows
    per_row_tile = gates_row.shape[2] // rows
    head0 = None
    if index is None:
        index = lambda g, h, c: (g * nchunk + c, h, g)
    else:
        assert heads == ML_HEADS
        head0 = 0
    row = lambda *ids: index(*ids)[0]
    hblk = lambda *ids: index(*ids)[1]
    sblk = lambda *ids: index(*ids)[2]

    in_specs = [
        pl.BlockSpec((rows, bq), lambda *ids: (rb0 + row(*ids), _P_ML_Q // bq + hblk(*ids))),
        pl.BlockSpec((rows, bq), lambda *ids: (rb0 + row(*ids), _P_ML_K // bq + hblk(*ids))),
        pl.BlockSpec((rows, bv), lambda *ids: (rb0 + row(*ids), _P_ML_V // bv + hblk(*ids))),
        pl.BlockSpec((rows, bv), lambda *ids: (rb0 + row(*ids), _P_ML_O // bv + hblk(*ids))),
        pl.BlockSpec((rows, LANES), lambda *ids: (row(*ids), 0)),
        pl.BlockSpec((1, 8, rows),
                     lambda *ids: (row(*ids) // per_row_tile, 0, row(*ids) % per_row_tile)),
        pl.BlockSpec((1, LANES), lambda *ids: (0, 0)),
        pl.BlockSpec((8, 1), lambda *ids: (0, 0)),
        pl.BlockSpec((1, bv), lambda *ids: (0, hblk(*ids))),
    ]
    args = [proj, proj, proj, proj, gates_col, gates_row, brow, bcol, norm_g]
    state_specs = [
        pl.BlockSpec((nseq, heads, ML_DK, ML_DV), lambda *ids: (sblk(*ids), hblk(*ids), 0, 0)),
        pl.BlockSpec((nseq, heads, 1, ML_DK), lambda *ids: (sblk(*ids), hblk(*ids), 0, 0)),
    ]
    scratch = []
    if zero_init:
        scratch = [pltpu.VMEM((8, LANES), F32)]
    else:
        in_specs += state_specs
        args += [state[0], state[1]]
    out_specs = [
        pl.BlockSpec((rows, bv), lambda *ids: (row(*ids), hblk(*ids))),
        *state_specs,
        pl.BlockSpec((heads, rows, LANES), lambda *ids: (hblk(*ids), row(*ids), 0)),
    ]
    out_shape = [
        jax.ShapeDtypeStruct((tokens, ML_V), BF16),
        jax.ShapeDtypeStruct((batch, ML_HEADS, ML_DK, ML_DV), F32),
        jax.ShapeDtypeStruct((batch, ML_HEADS, 1, ML_DK), F32),
        jax.ShapeDtypeStruct((ML_HEADS, tokens, LANES), F32),
    ]
    return dict(
        kernel=functools.partial(_mlstm_kernel, rows=rows, seq_len=seq_len, heads=heads,
                                 zero_init=zero_init, head0=head0),
        grid=(ngroup, ML_HEADS // heads, nchunk),
        in_specs=in_specs, args=args, out_specs=out_specs, out_shape=out_shape, scratch=scratch)


def _run_parts(parts, name):
    n_in = [len(p["in_specs"]) for p in parts]
    n_out = [len(p["out_specs"]) for p in parts]
    n_scr = [len(p["scratch"]) for p in parts]

    def kernel(*refs):
        ins = refs[:sum(n_in)]
        outs = refs[sum(n_in):sum(n_in) + sum(n_out)]
        scr = refs[sum(n_in) + sum(n_out):]
        i = o = s = 0
        for p, ni, no, ns in zip(parts, n_in, n_out, n_scr):
            run = functools.partial(p["kernel"], *ins[i:i + ni], *outs[o:o + no], *scr[s:s + ns])
            if p.get("when") is None:
                run()
            else:
                pl.when(p["when"](*[pl.program_id(a) for a in range(len(parts[0]["grid"]))]))(run)
            i, o, s = i + ni, o + no, s + ns

    outs = pl.pallas_call(
        kernel,
        grid=parts[0]["grid"],
        in_specs=[s for p in parts for s in p["in_specs"]],
        out_specs=[s for p in parts for s in p["out_specs"]],
        out_shape=[s for p in parts for s in p["out_shape"]],
        scratch_shapes=[s for p in parts for s in p["scratch"]],
        compiler_params=pltpu.CompilerParams(
            dimension_semantics=("parallel", "parallel", "arbitrary")),
        name=name,
    )(*[a for p in parts for a in p["args"]])
    split, o = [], 0
    for no in n_out:
        split.append(outs[o:o + no])
        o += no
    return split


def _hgrn_level_ids(rows, seq_len, direct, levels):
    t = np.arange(rows)[:, None]
    s = np.arange(rows)[None, :]
    ids = np.full((rows, rows), -1, np.int32)
    count = ((t // direct == s // direct) & (s <= t)).astype(np.int32)
    for idx, (block, sub_size) in enumerate(levels):
        owned = (t // block == s // block) & ((s % block) // sub_size < (t % block) // sub_size)
        ids[owned] = idx
        count += owned
    wanted = (t // seq_len == s // seq_len) & (s <= t)
    assert np.array_equal(count, wanted.astype(np.int32)), (rows, seq_len, direct, levels)
    return ids


def _hgrn_intra(q, kin, g2, gk2, level_ids, rows, direct, levels):
    ngroups = max(rows // LANES, 1)
    rowi = lax.broadcasted_iota(jnp.int32, (direct, LANES), 0)
    lanei = lax.broadcasted_iota(jnp.int32, (direct, LANES), 1)
    keep = [jnp.where(lanei == s, rowi, -1) >= s for s in range(direct)]
    zero_group = jnp.zeros((direct, LANES), F32)
    prods = []
    for blk in range(rows // direct):
        r0 = blk * direct
        qb = q[r0:r0 + direct]
        gb = g2[r0:r0 + direct]
        gkb = gk2[r0:r0 + direct]
        for s in range(direct):
            prods.append(qb * jnp.exp2(gb - gkb[s:s + 1, :]))
    sums = jnp.dot(jnp.concatenate(prods, axis=0).astype(BF16), jnp.ones((HG_DK, LANES), BF16),
                   preferred_element_type=F32)
    panels = []
    for blk in range(rows // direct):
        r0 = blk * direct
        ag = zero_group
        for s in range(direct):
            p0 = (blk * direct + s) * direct
            ag = jnp.where(keep[s], sums[p0:p0 + direct], ag)
        lane0 = r0 % LANES
        if lane0:
            ag = pltpu.roll(ag, lane0, axis=1)
        grp = r0 // LANES
        pieces = [zero_group] * grp + [ag] + [zero_group] * (ngroups - grp - 1)
        panels.append(pieces[0] if ngroups == 1 else jnp.concatenate(pieces, axis=1))
    a = panels[0] if len(panels) == 1 else jnp.concatenate(panels, axis=0)
    if rows < LANES:
        a = a[:, :rows]

    if levels:
        rowid = lax.broadcasted_iota(jnp.int32, (rows, 1), 0)
    for idx, (block, sub_size) in enumerate(levels):
        sshift = sub_size.bit_length() - 1
        row_sub = (rowid & (block - 1)) >> sshift
        q_parts, k_parts = [], []
        for j in range(1, block // sub_size):
            refs = []
            for b0 in range(0, rows, block):
                r = b0 + j * sub_size - 1
                refs.append(jnp.broadcast_to(g2[r:r + 1, :], (block, HG_DK)))
            g_ref = refs[0] if len(refs) == 1 else jnp.concatenate(refs, axis=0)
            e = jnp.exp2(-jnp.abs(g2 - g_ref))
            q_parts.append(jnp.where(row_sub == j, q * e, 0.0).astype(BF16))
            k_parts.append((kin * e).astype(BF16))
        qcat = q_parts[0] if len(q_parts) == 1 else jnp.concatenate(q_parts, axis=1)
        kcat = k_parts[0] if len(k_parts) == 1 else jnp.concatenate(k_parts, axis=1)
        a = jnp.where(level_ids == idx, _dot_nt(qcat, kcat), a)
    return a


def _hgrn_kernel(*refs, layer, rows, seq_len, heads, direct, levels, zero_init):
    if zero_init:
        q_ref, f_ref, i_ref, g_ref, lbl_ref, ng_ref, lvl_ref, bb_ref, s_ref = refs
        s0_ref = s_ref
    else:
        q_ref, f_ref, i_ref, g_ref, lbl_ref, ng_ref, lvl_ref, s0_ref, bb_ref, s_ref = refs
    nseq = rows // seq_len

    if zero_init:
        @pl.when(pl.program_id(2) == 0)
        def _init():
            s_ref[...] = jnp.zeros(s_ref.shape, F32)

    same, lower, _ = _seq_masks(rows, seq_len)
    lower_b = lower.astype(BF16)
    same_b = same.astype(BF16)
    level_ids = lvl_ref[...]
    if nseq > 1:
        shift = seq_len.bit_length() - 1
        row_seq = lax.broadcasted_iota(jnp.int32, (rows, 1), 0) >> shift

    def decay_cols(row):
        col = jnp.broadcast_to(jnp.exp2(row), (HG_DK, HG_DK)).T
        return jnp.concatenate([col] * (HG_DV // HG_DK), axis=1)

    for hh in range(heads):
        ks = slice(hh * HG_DK, (hh + 1) * HG_DK)
        vs = slice(hh * HG_DV, (hh + 1) * HG_DV)
        lg = lbl_ref[:, ks]
        ex = jnp.exp(lg - jnp.max(lg, axis=0, keepdims=True))
        lb = (jnp.sum(ex[0:layer + 1, :], axis=0, keepdims=True)
              / jnp.sum(ex, axis=0, keepdims=True))

        f = lb + (1.0 - lb) * _sigmoid(f_ref[:, ks])
        kin = 1.0 - f
        lf2 = jnp.log(f) * LOG2_E
        g2 = _dot01_right(lower_b, lf2)
        gk2 = g2 - jnp.log(kin) * LOG2_E
        q = q_ref[:, ks].astype(F32)
        if nseq == 1:
            g_end = jnp.broadcast_to(g2[rows - 1:rows, :], (rows, HG_DK))
        else:
            g_end = _dot01_right(same_b, lf2)
        qg = (q * jnp.exp2(g2)).astype(BF16)
        kg = kin * jnp.exp2(g_end - g2)
        ib = i_ref[:, vs].astype(BF16)

        a = _hgrn_intra(q, kin, g2, gk2, level_ids, rows, direct, levels)
        o = jnp.dot(a.astype(BF16), ib, preferred_element_type=F32)

        if nseq == 1:
            s_prev = s0_ref[0, hh]
            o = o + jnp.dot(qg, s_prev.astype(BF16), preferred_element_type=F32)
            s_ref[0, hh] = decay_cols(g_end[0:1, :]) * s_prev + _dot_tn(kg.astype(BF16), ib)
        else:
            for j in range(nseq):
                in_seq = row_seq == j
                s_prev = s0_ref[j, hh]
                o_j = jnp.dot(qg, s_prev.astype(BF16), preferred_element_type=F32)
                o = o + jnp.where(in_seq, o_j, 0.0)
                kg_j = jnp.where(in_seq, kg, 0.0).astype(BF16)
                s_ref[j, hh] = (decay_cols(g_end[j * seq_len:j * seq_len + 1, :]) * s_prev
                                + _dot_tn(kg_j, ib))

        o = o * lax.rsqrt(jnp.mean(o * o, axis=1, keepdims=True) + LN_EPS)
        gate = g_ref[:, vs].astype(F32)
        bb_ref[:, vs] = (o * ng_ref[:, vs] * (gate * _sigmoid(gate))).astype(bb_ref.dtype)


def _hgrn_parts(proj, hg_f, lb_logits, norm_g, s0, *, layer, row0, batch, seq, rows, seq_len, heads,
                direct, levels, index=None):
    tokens = batch * seq
    zero_init = s0 is None
    nseq = rows // seq_len
    nchunk = seq // seq_len if nseq == 1 else 1
    ngroup = tokens // (rows * nchunk)
    rb0 = row0 // rows
    if index is None:
        index = lambda g, h, c: (g * nchunk + c, h, g)
    row = lambda *ids: index(*ids)[0]
    hblk = lambda *ids: index(*ids)[1]
    sblk = lambda *ids: index(*ids)[2]

    wk, wv = heads * HG_DK, heads * HG_DV
    in_specs = [
        pl.BlockSpec((rows, wk), lambda *ids: (rb0 + row(*ids), _P_HG_Q // wk + hblk(*ids))),
        pl.BlockSpec((rows, wk), lambda *ids: (rb0 + row(*ids), hblk(*ids))),
        pl.BlockSpec((rows, wv), lambda *ids: (rb0 + row(*ids), _P_HG_I // wv + hblk(*ids))),
        pl.BlockSpec((rows, wv), lambda *ids: (rb0 + row(*ids), _P_HG_G // wv + hblk(*ids))),
        pl.BlockSpec((DEPTH + 1, wk), lambda *ids: (0, hblk(*ids))),
        pl.BlockSpec((1, wv), lambda *ids: (0, hblk(*ids))),
        pl.BlockSpec((rows, rows), lambda *ids: (0, 0)),
    ]
    level_ids = jnp.asarray(_hgrn_level_ids(rows, seq_len, direct, levels))
    args = [proj, hg_f, proj, proj, lb_logits, norm_g, level_ids]
    state_spec = pl.BlockSpec((nseq, heads, HG_DK, HG_DV),
                              lambda *ids: (sblk(*ids), hblk(*ids), 0, 0))
    if not zero_init:
        in_specs.append(state_spec)
        args.append(s0)
    return dict(
        kernel=functools.partial(_hgrn_kernel, layer=layer, rows=rows, seq_len=seq_len, heads=heads,
                                 direct=direct, levels=levels, zero_init=zero_init),
        grid=(ngroup, HG_HEADS // heads, nchunk),
        in_specs=in_specs, args=args,
        out_specs=[pl.BlockSpec((rows, wv), lambda *ids: (row(*ids), hblk(*ids))), state_spec],
        out_shape=[jax.ShapeDtypeStruct((tokens, HG_V), BF16),
                   jax.ShapeDtypeStruct((batch, HG_HEADS, HG_DK, HG_DV), F32)],
        scratch=[])


def _merge_kernel(ba_ref, wa_ref, bb_ref, wb_ref, ga_ref, gb_ref, o_ref):
    ya = jnp.dot(ba_ref[...], wa_ref[...], preferred_element_type=F32)
    yb = jnp.dot(bb_ref[...], wb_ref[...], preferred_element_type=F32)
    ga = _sigmoid(ga_ref[...].astype(F32))
    gb = _sigmoid(gb_ref[...].astype(F32))
    o_ref[...] = (ga * ya + gb * yb).astype(o_ref.dtype)


def _merge(branch_a, w_a, branch_b, w_b, proj, row0):
    m = branch_a.shape[0]
    tm, tn = _MERGE_TM, _MERGE_TN
    rb0 = row0 // tm
    ga_blk = _P_GATE_A // tn
    gb_blk = _P_GATE_B // tn
    return pl.pallas_call(
        _merge_kernel,
        grid=(D_MODEL // tn, m // tm),
        in_specs=[pl.BlockSpec((tm, ML_V), lambda j, i: (i, 0)),
                  pl.BlockSpec((ML_V, tn), lambda j, i: (0, j)),
                  pl.BlockSpec((tm, HG_V), lambda j, i: (i, 0)),
                  pl.BlockSpec((HG_V, tn), lambda j, i: (0, j)),
                  pl.BlockSpec((tm, tn), lambda j, i: (rb0 + i, ga_blk + j)),
                  pl.BlockSpec((tm, tn), lambda j, i: (rb0 + i, gb_blk + j))],
        out_specs=pl.BlockSpec((tm, tn), lambda j, i: (i, j)),
        out_shape=jax.ShapeDtypeStruct((m, D_MODEL), BF16),
        compiler_params=pltpu.CompilerParams(dimension_semantics=("parallel", "arbitrary")),
        name="merge",
    )(branch_a, w_a, branch_b, w_b, proj, proj)


def _layernorm_rows(z, g, b):
    mu = jnp.mean(z, axis=1, keepdims=True)
    zc = z - mu
    var = jnp.mean(zc * zc, axis=1, keepdims=True)
    return zc * lax.rsqrt(var + LN_EPS) * g + b


def _outproj_kernel(mg_ref, w_ref, x_ref, g_ref, b_ref, x1_ref, x1b_ref):
    mix = jnp.dot(mg_ref[...], w_ref[...], preferred_element_type=F32)
    x1 = _layernorm_rows(DEEPNORM_ALPHA * x_ref[...] + mix, g_ref[...], b_ref[...])
    x1_ref[...] = x1
    x1b_ref[...] = x1.astype(BF16)


def _outproj(merged, w_out, x, ln_g, ln_b):
    tm = _OUT_TM
    m = x.shape[0]
    row = lambda i: (i, 0)
    const = lambda i: (0, 0)
    return pl.pallas_call(
        _outproj_kernel,
        grid=(m // tm,),
        in_specs=[pl.BlockSpec((tm, D_MODEL), row),
                  pl.BlockSpec((D_MODEL, D_MODEL), const),
                  pl.BlockSpec((tm, D_MODEL), row),
                  pl.BlockSpec((1, D_MODEL), const),
                  pl.BlockSpec((1, D_MODEL), const)],
        out_specs=[pl.BlockSpec((tm, D_MODEL), row), pl.BlockSpec((tm, D_MODEL), row)],
        out_shape=[jax.ShapeDtypeStruct((m, D_MODEL), F32),
                   jax.ShapeDtypeStruct((m, D_MODEL), BF16)],
        compiler_params=pltpu.CompilerParams(dimension_semantics=("parallel",)),
        name="out_proj_ln",
    )(merged, w_out, x, ln_g, ln_b)


def _mlp_kernel(x1b_ref, wu_ref, wd_ref, x1_ref, g_ref, b_ref, y_ref):
    f = pl.program_id(1)

    @pl.when(f == 0)
    def _init():
        y_ref[...] = jnp.zeros(y_ref.shape, F32)

    hid = jnp.maximum(jnp.dot(x1b_ref[...], wu_ref[...], preferred_element_type=F32), 0.0)
    hid = (hid * hid).astype(BF16)
    y_ref[...] += jnp.dot(hid, wd_ref[...], preferred_element_type=F32)

    @pl.when(f == pl.num_programs(1) - 1)
    def _finish():
        z = DEEPNORM_ALPHA * x1_ref[...] + y_ref[...]
        y_ref[...] = _layernorm_rows(z, g_ref[...], b_ref[...])


def _mlp(x1b, w_up, w_down, x1, ln_g, ln_b):
    tm, tf = _MLP_TM, _MLP_TF
    tokens = x1.shape[0]
    return pl.pallas_call(
        _mlp_kernel,
        grid=(tokens // tm, D_FF // tf),
        in_specs=[pl.BlockSpec((tm, D_MODEL), lambda i, f: (i, 0)),
                  pl.BlockSpec((D_MODEL, tf), lambda i, f: (0, f)),
                  pl.BlockSpec((tf, D_MODEL), lambda i, f: (f, 0)),
                  pl.BlockSpec((tm, D_MODEL), lambda i, f: (i, 0)),
                  pl.BlockSpec((1, D_MODEL), lambda i, f: (0, 0)),
                  pl.BlockSpec((1, D_MODEL), lambda i, f: (0, 0))],
        out_specs=pl.BlockSpec((tm, D_MODEL), lambda i, f: (i, 0)),
        out_shape=jax.ShapeDtypeStruct((tokens, D_MODEL), F32),
        compiler_params=pltpu.CompilerParams(dimension_semantics=("parallel", "arbitrary")),
        name="mlp_ln",
    )(x1b, w_up, w_down, x1, ln_g, ln_b)


_PROMPT_ML = dict(rows=256, seq_len=256, heads=4)
_PROMPT_HG = dict(rows=256, seq_len=256, heads=4, direct=8,
                  levels=((16, 8), (32, 16), (64, 32), (128, 64), (256, 128)))
_SAMPLE_ML = dict(rows=32, seq_len=8, heads=4)
_SAMPLE_HG = dict(rows=32, seq_len=8, heads=8, direct=8, levels=())


def kernel(x_prompt, x_sample, state_mlstm_C, state_mlstm_n, state_mlstm_m, state_hgrn_S,
           hg_lb_logits, w_in, b_ig, b_fg, ml_norm_g, hg_norm_g, w_branch_a, w_branch_b, w_out,
           ln1_g, ln1_b, w_up, w_down, ln2_g, ln2_b):
    batch_p, seq_p, _ = x_prompt.shape
    batch_s, seq_s, _ = x_sample.shape
    tok_p, tok_s = batch_p * seq_p, batch_s * seq_s
    total = tok_p + tok_s
    group_p = dict(row0=0, batch=batch_p, seq=seq_p)
    group_s = dict(row0=tok_p, batch=batch_s, seq=seq_s)
    xp = x_prompt.reshape(tok_p, D_MODEL)
    xs = x_sample.reshape(tok_s, D_MODEL)
    lb_logits = hg_lb_logits.astype(F32)
    states_p, states_s = [], []
    for l in range(DEPTH):
        (proj, hg_f, gates, gates_gm), (w_a_b, w_b_b, w_out_b, w_up_b, w_down_b) = _inproj(
            xp.astype(BF16), xs.astype(BF16), jnp.swapaxes(w_in[l], 0, 1),
            (w_branch_a[l], w_branch_b[l], w_out[l], w_up[l], w_down[l]))

        gate_bias = jnp.concatenate([b_ig[l], b_fg[l]]).astype(F32)
        brow = jnp.zeros((1, LANES), F32).at[0, :_N_GATES].set(gate_bias)
        bcol = gate_bias.reshape(_N_GATES, 1)
        ml_g = ml_norm_g[l].reshape(1, ML_V).astype(F32)
        hg_g = hg_norm_g[l].reshape(1, HG_V).astype(F32)

        m_rows = jnp.repeat(state_mlstm_m[l].astype(F32), seq_s, axis=0)
        gates_s = gates[tok_p:].at[:, _N_GATES:_N_GATES + ML_HEADS].set(m_rows)
        rows_s = _SAMPLE_ML["rows"]
        gm_s = gates_gm[tok_p // _PROJ_TM:]
        gm_s = gm_s.reshape(-1, 8, _PROJ_TM // rows_s, rows_s).transpose(0, 2, 1, 3)
        gm_s = gm_s.reshape(tok_s // rows_s, 8, rows_s)
        state_ml = (state_mlstm_C[l].astype(F32),
                    state_mlstm_n[l].astype(F32).reshape(batch_s, ML_HEADS, 1, ML_DK))
        ln1 = (ln1_g[l].reshape(1, D_MODEL), ln1_b[l].reshape(1, D_MODEL))
        ln2 = (ln2_g[l].reshape(1, D_MODEL), ln2_b[l].reshape(1, D_MODEL))

        ml_p = _mlstm_parts(proj, gates, gates_gm, brow, bcol, ml_g, None, **group_p, **_PROMPT_ML)
        hg_p = _hgrn_parts(proj, hg_f, lb_logits, hg_g, None, layer=l, **group_p, **_PROMPT_HG)

        def guest(parts_fn, host, rows, **kwargs):
            grid = host["grid"]
            every = math.prod(grid) * rows // tok_s
            assert every >= 1 and math.prod(grid) * rows == every * tok_s, (grid, rows)
            flat = lambda g, h, c: (g * grid[1] + h) * grid[2] + c
            parts = parts_fn(index=lambda g, h, c: (flat(g, h, c) // every, 0, flat(g, h, c) // every),
                             rows=rows, **kwargs)
            if every > 1:
                parts["when"] = lambda g, h, c: flat(g, h, c) % every == 0
            return parts

        ml_s = guest(functools.partial(_mlstm_parts, proj, gates_s, gm_s, brow, bcol, ml_g, state_ml,
                                       **group_s), hg_p, **_SAMPLE_ML)
        hg_s = guest(functools.partial(_hgrn_parts, proj, hg_f, lb_logits, hg_g,
                                       state_hgrn_S[l].astype(F32), layer=l, **group_s),
                     ml_p, **_SAMPLE_HG)
        (bb_p, s_p), (ba_s, c_s, n_s, m_all_s) = _run_parts([hg_p, ml_s], "hgrn2_prompt_mlstm_sample")
        (ba_p, c_p, n_p, m_all_p), (bb_s, s_s) = _run_parts([ml_p, hg_s], "mlstm_prompt_hgrn2_sample")

        merged_p = _merge(ba_p, w_a_b, bb_p, w_b_b, proj, 0)
        merged_s = _merge(ba_s, w_a_b, bb_s, w_b_b, proj, tok_p)
        x1_p, x1b_p = _outproj(merged_p, w_out_b, xp, *ln1)
        x1_s, x1b_s = _outproj(merged_s, w_out_b, xs, *ln1)
        xp = _mlp(x1b_p, w_up_b, w_down_b, x1_p, *ln2)
        xs = _mlp(x1b_s, w_up_b, w_down_b, x1_s, *ln2)

        states_p.append((c_p, n_p.reshape(batch_p, ML_HEADS, ML_DK),
                         m_all_p[:, seq_p - 1::seq_p, 0].T, s_p))
        states_s.append((c_s, n_s.reshape(batch_s, ML_HEADS, ML_DK),
                         m_all_s[:, seq_s - 1::seq_s, 0].T, s_s))
    stack = lambda states, k: jnp.stack([s[k] for s in states])
    return (xp.reshape(batch_p, seq_p, D_MODEL), xs.reshape(batch_s, seq_s, D_MODEL),
            stack(states_p, 0), stack(states_p, 1), stack(states_p, 2), stack(states_p, 3),
            stack(states_s, 0), stack(states_s, 1), stack(states_s, 2), stack(states_s, 3))
```

```python
import functools
import math

import numpy as np
import jax
import jax.numpy as jnp
from jax import lax
from jax.experimental import pallas as pl
from jax.experimental.pallas import tpu as pltpu

F32 = jnp.float32
BF16 = jnp.bfloat16

D_MODEL = 2048
DEPTH = 1
ML_HEADS, ML_DK, ML_DV = 4, 256, 512
HG_HEADS, HG_DK, HG_DV = 8, 128, 256
ML_QK = ML_HEADS * ML_DK
ML_V = ML_HEADS * ML_DV
HG_K = HG_HEADS * HG_DK
HG_V = HG_HEADS * HG_DV
D_FF = 4 * D_MODEL
LN_EPS = 1e-5
DEEPNORM_ALPHA = (2.0 * DEPTH) ** 0.25
ML_K_SCALE = ML_DK ** -0.5
LOG2_E = math.log2(math.e)
LANES = 128

_OFF_ML_Q = 0
_OFF_ML_K = _OFF_ML_Q + ML_QK
_OFF_ML_V = _OFF_ML_K + ML_QK
_OFF_ML_I = _OFF_ML_V + ML_V
_OFF_ML_F = _OFF_ML_I + ML_HEADS
_OFF_ML_O = _OFF_ML_F + ML_HEADS
_OFF_HG_Q = _OFF_ML_O + ML_V
_OFF_HG_F = _OFF_HG_Q + HG_K
_OFF_HG_I = _OFF_HG_F + HG_K
_OFF_HG_G = _OFF_HG_I + HG_V
_OFF_GATE_A = _OFF_HG_G + HG_V
_OFF_GATE_B = _OFF_GATE_A + D_MODEL
D_IN = _OFF_GATE_B + D_MODEL

_N_GATES = 2 * ML_HEADS
_P_ML_Q = _OFF_ML_Q
_P_ML_K = _OFF_ML_K
_P_ML_V = _OFF_ML_V
_P_ML_O = _OFF_ML_O - _N_GATES
_P_HG_Q = _OFF_HG_Q - _N_GATES
_P_HG_F = _OFF_HG_F - _N_GATES
_P_HG_I = _OFF_HG_I - _N_GATES
_P_HG_G = _OFF_HG_G - _N_GATES
_P_GATE_A = _OFF_GATE_A - _N_GATES
_P_GATE_B = _OFF_GATE_B - _N_GATES
_P_WIDTH = D_IN - _N_GATES

_PROJ_TM, _PROJ_TN = 1024, 1024
_MERGE_TM, _MERGE_TN = 1024, 1024
_OUT_TM = 512
_MLP_TM, _MLP_TF = 1024, 512


def _sigmoid(x):
    return 1.0 / (1.0 + jnp.exp(-x))


def _log_sigmoid(x):
    return jnp.minimum(x, 0.0) - jnp.log1p(jnp.exp(-jnp.abs(x)))


def _split2(x):
    hi = x.astype(BF16)
    lo = (x - hi.astype(F32)).astype(BF16)
    return hi, lo


def _dot01_right(t01, x):
    hi, lo = _split2(x)
    return (jnp.dot(t01, hi, preferred_element_type=F32)
            + jnp.dot(t01, lo, preferred_element_type=F32))


def _dot01_left(x, t01):
    hi, lo = _split2(x)
    return (jnp.dot(hi, t01, preferred_element_type=F32)
            + jnp.dot(lo, t01, preferred_element_type=F32))


def _dot_nt(a, b):
    return lax.dot_general(a, b, (((1,), (1,)), ((), ())), preferred_element_type=F32)


def _dot_tn(a, b):
    return lax.dot_general(a, b, (((0,), (0,)), ((), ())), preferred_element_type=F32)


def _seq_masks(rows, seq_len):
    ri = lax.broadcasted_iota(jnp.int32, (rows, rows), 0)
    ci = lax.broadcasted_iota(jnp.int32, (rows, rows), 1)
    lower = jnp.where(ci <= ri, 1.0, 0.0)
    upper = jnp.where(ri <= ci, 1.0, 0.0)
    if seq_len == rows:
        return jnp.ones((rows, rows), F32), lower, upper
    shift = seq_len.bit_length() - 1
    same = jnp.where((ri >> shift) == (ci >> shift), 1.0, 0.0)
    return same, same * lower, same * upper


def _inproj_kernel(xp_ref, xs_ref, wt_hbm, *refs, n_f32, n_gate_tile, prompt_tiles, cast_steps):
    n_cast = (len(refs) - 9) // 2
    cast_in = refs[:n_cast]
    o_ref, f_ref, g_ref, gr_ref = refs[n_cast:n_cast + 4]
    cast_out = refs[n_cast + 4:2 * n_cast + 4]
    wbuf, wbf_sc, wg_sc, sem, gsem = refs[2 * n_cast + 4:]
    n = pl.program_id(0)
    m = pl.program_id(1)
    tn = wbf_sc.shape[1]

    @pl.when(m < cast_steps)
    def _cast_other_weights():
        for src, dst in zip(cast_in, cast_out):
            dst[...] = src[...].astype(BF16)

    def tile_copy(tile):
        start = pl.multiple_of(tile * tn + jnp.where(tile >= n_gate_tile, _N_GATES, 0), _N_GATES)
        return pltpu.make_async_copy(wt_hbm.at[pl.ds(start, tn), :], wbuf, sem.at[0])

    def gate_copy():
        return pltpu.make_async_copy(wt_hbm.at[pl.ds(_OFF_ML_I, _N_GATES), :],
                                     wg_sc.at[pl.ds(0, _N_GATES), :], gsem.at[0])

    @pl.when(m == 0)
    def _next_weight_tile():
        @pl.when(n == 0)
        def _first():
            tile_copy(0).start()
            wg_sc[_N_GATES:, :] = jnp.zeros((LANES - _N_GATES, wg_sc.shape[1]), F32)
            gate_copy().start()
            gate_copy().wait()

        tile_copy(n).wait()
        wbf_sc[...] = wbuf[...].T.astype(BF16)

        @pl.when(n + 1 < pl.num_programs(0))
        def _prefetch():
            tile_copy(n + 1).start()

    x = jnp.where(m < prompt_tiles, xp_ref[...], xs_ref[...])
    acc = jnp.dot(x, wbf_sc[...], preferred_element_type=F32)
    o_ref[...] = acc.astype(BF16)

    @pl.when(n == n_f32)
    def _f32_outputs():
        f_ref[...] = acc
        wg = wg_sc[...].astype(BF16)
        gates = _dot_nt(x, wg)
        g_ref[...] = gates
        gr_ref[0] = gates.T[0:_N_GATES, :]


def _inproj(xp_b, xs_b, w_t, cast_weights):
    k = xp_b.shape[1]
    tm, tn = _PROJ_TM, _PROJ_TN
    p_tiles = xp_b.shape[0] // tm
    n_m = p_tiles + xs_b.shape[0] // tm
    tokens = n_m * tm
    n_n = _P_WIDTH // tn
    n_f32 = _P_HG_F // tn
    cast_steps = min(n_m, 8)

    def parked(n, m):
        return jnp.where(n < n_f32, 0, jnp.where(n == n_f32, m, n_m - 1))

    def cast_spec(w):
        return pl.BlockSpec((w.shape[0] // n_n, w.shape[1] // cast_steps),
                            lambda n, m: (n, jnp.minimum(m, cast_steps - 1)))

    cast_specs = [cast_spec(w) for w in cast_weights]
    outs = pl.pallas_call(
        functools.partial(_inproj_kernel, n_f32=n_f32, n_gate_tile=_OFF_ML_I // tn,
                          prompt_tiles=p_tiles, cast_steps=cast_steps),
        grid=(n_n, n_m),
        in_specs=[pl.BlockSpec((tm, k), lambda n, m: (jnp.minimum(m, p_tiles - 1), 0)),
                  pl.BlockSpec((tm, k), lambda n, m: (jnp.maximum(m - p_tiles, 0), 0)),
                  pl.BlockSpec(memory_space=pl.ANY),
                  *cast_specs],
        out_specs=[pl.BlockSpec((tm, tn), lambda n, m: (m, n)),
                   pl.BlockSpec((tm, HG_K), lambda n, m: (parked(n, m), 0)),
                   pl.BlockSpec((tm, LANES), lambda n, m: (parked(n, m), 0)),
                   pl.BlockSpec((1, 8, tm), lambda n, m: (parked(n, m), 0, 0)),
                   *cast_specs],
        out_shape=[jax.ShapeDtypeStruct((tokens, _P_WIDTH), BF16),
                   jax.ShapeDtypeStruct((tokens, HG_K), F32),
                   jax.ShapeDtypeStruct((tokens, LANES), F32),
                   jax.ShapeDtypeStruct((n_m, 8, tm), F32),
                   *[jax.ShapeDtypeStruct(w.shape, BF16) for w in cast_weights]],
        scratch_shapes=[pltpu.VMEM((tn, k), F32),
                        pltpu.VMEM((k, tn), BF16),
                        pltpu.VMEM((LANES, k), F32),
                        pltpu.SemaphoreType.DMA((1,)),
                        pltpu.SemaphoreType.DMA((1,))],
        compiler_params=pltpu.CompilerParams(dimension_semantics=("arbitrary", "arbitrary")),
        name="in_proj",
    )(xp_b, xs_b, w_t, *cast_weights)
    return outs[:4], outs[4:]


def _mlstm_steps(*refs, rows, seq_len, heads, zero_init, head0=None):
    if zero_init:
        (q_ref, k_ref, v_ref, og_ref, gc_ref, gr_ref, brow_ref, bcol_ref, ng_ref,
         ba_ref, c_ref, n_ref, mrow_ref, m_sc) = refs
        c0_ref, n0_ref = c_ref, n_ref
    else:
        (q_ref, k_ref, v_ref, og_ref, gc_ref, gr_ref, brow_ref, bcol_ref, ng_ref, c0_ref, n0_ref,
         ba_ref, c_ref, n_ref, mrow_ref) = refs
    nseq = rows // seq_len
    if head0 is None:
        head0 = pl.program_id(1) * heads

    if zero_init:
        @pl.when(pl.program_id(2) == 0)
        def _init():
            c_ref[...] = jnp.zeros(c_ref.shape, F32)
            n_ref[...] = jnp.zeros(n_ref.shape, F32)
            m_sc[...] = jnp.zeros(m_sc.shape, F32)

    same, lower, upper = _seq_masks(rows, seq_len)
    causal = lower > 0.5
    lane = lax.broadcasted_iota(jnp.int32, (rows, LANES), 1)
    sub = lax.broadcasted_iota(jnp.int32, (8, rows), 0)

    def sel_lane(x, idx):
        return jnp.sum(jnp.where(lane == idx, x, 0.0), axis=1, keepdims=True)

    def sel_sub(x, idx):
        return jnp.sum(jnp.where(sub == idx, x, 0.0), axis=0, keepdims=True)

    gc_raw = gc_ref[...]
    gc = gc_raw + brow_ref[...]
    lf_cols = _log_sigmoid(gc)
    b_cols = _dot01_right(lower.astype(BF16), lf_cols)
    gr = gr_ref[0] + bcol_ref[...]
    b_rows = _dot01_left(_log_sigmoid(gr), upper.astype(BF16))
    if nseq > 1:
        b_ends = _dot01_right(same.astype(BF16), lf_cols)
        ci = lax.broadcasted_iota(jnp.int32, (rows, rows), 1)
        last = (same * jnp.where((ci & (seq_len - 1)) == seq_len - 1, 1.0, 0.0)).astype(BF16)
        shift = seq_len.bit_length() - 1
        row_seq = lax.broadcasted_iota(jnp.int32, (rows, 1), 0) >> shift

    for hh in range(heads):
        head = head0 + hh
        ks = slice(hh * ML_DK, (hh + 1) * ML_DK)
        vs = slice(hh * ML_DV, (hh + 1) * ML_DV)
        ig_col = sel_lane(gc, head)
        b_col = sel_lane(b_cols, head + ML_HEADS)
        if zero_init:
            m_prev = jnp.broadcast_to(m_sc[hh:hh + 1, 0:1], (rows, 1))
        else:
            m_prev = sel_lane(gc_raw, head + 2 * ML_HEADS)
        ig_row = sel_sub(gr, head)
        b_row = sel_sub(b_rows, head + ML_HEADS)

        logd = jnp.where(causal, (b_col - b_row) + ig_row, -jnp.inf)
        m_t = jnp.maximum(b_col + m_prev, jnp.max(logd, axis=1, keepdims=True))
        d = jnp.exp(logd - m_t)
        w_inter = jnp.exp(b_col + m_prev - m_t)
        yield

        qb = q_ref[:, ks].astype(BF16)
        kb = k_ref[:, ks].astype(BF16)
        vb = v_ref[:, vs].astype(BF16)
        s = _dot_nt(qb, kb) * (d * ML_K_SCALE)
        num = jnp.dot(s.astype(BF16), vb, preferred_element_type=F32)
        den = jnp.sum(s, axis=1, keepdims=True)

        if nseq == 1:
            b_end = b_col[rows - 1:rows, :]
            m_new = m_t[rows - 1:rows, :]
        else:
            b_end = sel_lane(b_ends, head + ML_HEADS)
            m_new = _dot01_right(last, jnp.broadcast_to(m_t, (rows, LANES)))[:, 0:1]
        w_end = jnp.exp(b_end - b_col + ig_col - m_new)
        decay = jnp.exp(b_end + m_prev - m_new)

        qf = qb.astype(F32)
        kw = (w_end * ML_K_SCALE) * kb.astype(F32)
        yield

        if nseq == 1:
            c_prev = c0_ref[0, hh]
            n_prev = n0_ref[0, hh]
            q_c = jnp.dot(qb, c_prev.astype(BF16), preferred_element_type=F32)
            q_n = jnp.sum(qf * n_prev, axis=1, keepdims=True)
            dec = decay[0:1, :]
            c_ref[0, hh] = dec * c_prev + _dot_tn(kw.astype(BF16), vb)
            n_ref[0, hh] = dec * n_prev + jnp.sum(kw, axis=0, keepdims=True)
        else:
            q_c = jnp.zeros((rows, ML_DV), F32)
            q_n = jnp.zeros((rows, 1), F32)
            for j in range(nseq):
                in_seq = row_seq == j
                c_prev = c0_ref[j, hh]
                n_prev = n0_ref[j, hh]
                q_c = jnp.where(
                    in_seq, jnp.dot(qb, c_prev.astype(BF16), preferred_element_type=F32), q_c)
                q_n = jnp.where(in_seq, jnp.sum(qf * n_prev, axis=1, keepdims=True), q_n)
                kw_j = jnp.where(in_seq, kw, 0.0)
                dec = decay[j * seq_len:j * seq_len + 1, :]
                c_ref[j, hh] = dec * c_prev + _dot_tn(kw_j.astype(BF16), vb)
                n_ref[j, hh] = dec * n_prev + jnp.sum(kw_j, axis=0, keepdims=True)
                yield
        if nseq == 1:
            yield

        num = num + w_inter * q_c
        den = den + w_inter * q_n
        h_out = num / jnp.maximum(jnp.abs(den), jnp.exp(-m_t))
        mu =jnp.mean(h_out, axis=1, keepdims=True)
        xc = h_out - mu
        var = jnp.mean(xc * xc, axis=1, keepdims=True)
        hn = xc * lax.rsqrt(var + LN_EPS) * ng_ref[:, vs]
        ba_ref[:, vs] = (hn * _sigmoid(og_ref[:, vs].astype(F32))).astype(ba_ref.dtype)
        mrow_ref[hh] = jnp.broadcast_to(m_t, (rows, LANES))
        if zero_init:
            m_sc[hh:hh + 1, :] = jnp.broadcast_to(m_new, (1, LANES))
        yield


def _mlstm_parts(proj, gates_col, gates_row, brow, bcol, norm_g, state, *, row0, batch, seq, rows,
                 seq_len, heads, index=None):
    tokens = batch * seq
    zero_init = state is None
    nseq = rows // seq_len
    nchunk = seq // seq_len if nseq == 1 else 1
    ngroup = tokens // (rows * nchunk)
    bq, bv = heads * ML_DK, heads * ML_DV
    rb0 = row0 // rows
    per_row_tile = gates_row.shape[2] // rows
    head0 = None
    if index is None:
        index = lambda g, h, c: (g * nchunk + c, h, g)
    else:
        assert heads == ML_HEADS
        head0 = 0
    row = lambda *ids: index(*ids)[0]
    hblk = lambda *ids: index(*ids)[1]
    sblk = lambda *ids: index(*ids)[2]

    in_specs = [
        pl.BlockSpec((rows, bq), lambda *ids: (rb0 + row(*ids), _P_ML_Q // bq + hblk(*ids))),
        pl.BlockSpec((rows, bq), lambda *ids: (rb0 + row(*ids), _P_ML_K // bq + hblk(*ids))),
        pl.BlockSpec((rows, bv), lambda *ids: (rb0 + row(*ids), _P_ML_V // bv + hblk(*ids))),
        pl.BlockSpec((rows, bv), lambda *ids: (rb0 + row(*ids), _P_ML_O // bv + hblk(*ids))),
        pl.BlockSpec((rows, LANES), lambda *ids: (row(*ids), 0)),
        pl.BlockSpec((1, 8, rows),
                     lambda *ids: (row(*ids) // per_row_tile, 0, row(*ids) % per_row_tile)),
        pl.BlockSpec((1, LANES), lambda *ids: (0, 0)),
        pl.BlockSpec((8, 1), lambda *ids: (0, 0)),
        pl.BlockSpec((1, bv), lambda *ids: (0, hblk(*ids))),
    ]
    args = [proj, proj, proj, proj, gates_col, gates_row, brow, bcol, norm_g]
    state_specs = [
        pl.BlockSpec((nseq, heads, ML_DK, ML_DV), lambda *ids: (sblk(*ids), hblk(*ids), 0, 0)),
        pl.BlockSpec((nseq, heads, 1, ML_DK), lambda *ids: (sblk(*ids), hblk(*ids), 0, 0)),
    ]
    scratch = []
    if zero_init:
        scratch = [pltpu.VMEM((8, LANES), F32)]
    else:
        in_specs += state_specs
        args += [state[0], state[1]]
    out_specs = [
        pl.BlockSpec((rows, bv), lambda *ids: (row(*ids), hblk(*ids))),
        *state_specs,
        pl.BlockSpec((heads, rows, LANES), lambda *ids: (hblk(*ids), row(*ids), 0)),
    ]
    out_shape = [
        jax.ShapeDtypeStruct((tokens, ML_V), BF16),
        jax.ShapeDtypeStruct((batch, ML_HEADS, ML_DK, ML_DV), F32),
        jax.ShapeDtypeStruct((batch, ML_HEADS, 1, ML_DK), F32),
        jax.ShapeDtypeStruct((ML_HEADS, tokens, LANES), F32),
    ]
    return dict(
        steps=functools.partial(_mlstm_steps, rows=rows, seq_len=seq_len, heads=heads,
                                zero_init=zero_init, head0=head0),
        n_steps=heads * (3 + nseq),
        grid=(ngroup, ML_HEADS // heads, nchunk),
        in_specs=in_specs, args=args, out_specs=out_specs, out_shape=out_shape, scratch=scratch)


def _run_parts(parts, name):
    n_in = [len(p["in_specs"]) for p in parts]
    n_out = [len(p["out_specs"]) for p in parts]
    n_scr = [len(p["scratch"]) for p in parts]

    def kernel(*refs):
        ins = refs[:sum(n_in)]
        outs = refs[sum(n_in):sum(n_in) + sum(n_out)]
        scr = refs[sum(n_in) + sum(n_out):]
        gens = []
        i = o = s = 0
        for p, ni, no, ns in zip(parts, n_in, n_out, n_scr):
            gens.append(p["steps"](*ins[i:i + ni], *outs[o:o + no], *scr[s:s + ns]))
            i, o, s = i + ni, o + no, s + ns
        order = sorted(((j + 0.5) / p["n_steps"], k) for k, p in enumerate(parts)
                       for j in range(p["n_steps"]))
        for _, k in order:
            next(gens[k], None)
        for gen in gens:
            for _ in gen:
                pass

    outs = pl.pallas_call(
        kernel,
        grid=parts[0]["grid"],
        in_specs=[s for p in parts for s in p["in_specs"]],
        out_specs=[s for p in parts for s in p["out_specs"]],
        out_shape=[s for p in parts for s in p["out_shape"]],
        scratch_shapes=[s for p in parts for s in p["scratch"]],
        compiler_params=pltpu.CompilerParams(
            dimension_semantics=("parallel", "parallel", "arbitrary")),
        name=name,
    )(*[a for p in parts for a in p["args"]])
    split, o = [], 0
    for no in n_out:
        split.append(outs[o:o + no])
        o += no
    return split


def _hgrn_level_ids(rows, seq_len, direct, levels):
    t = np.arange(rows)[:, None]
    s = np.arange(rows)[None, :]
    ids = np.full((rows, rows), -1, np.int32)
    count = ((t // direct == s // direct) & (s <= t)).astype(np.int32)
    for idx, (block, sub_size) in enumerate(levels):
        owned = (t // block == s // block) & ((s % block) // sub_size < (t % block) // sub_size)
        ids[owned] = idx
        count += owned
    wanted = (t // seq_len == s // seq_len) & (s <= t)
    assert np.array_equal(count, wanted.astype(np.int32)), (rows, seq_len, direct, levels)
    return ids


_INTRA_YIELD_BLOCKS = 32


def _hgrn_intra_yields(rows, direct, levels):
    return 2 + len(levels) + 2 * (max(rows // direct // _INTRA_YIELD_BLOCKS, 1) - 1)


def _hgrn_intra(q, kin, g2, gk2, level_ids, rows, direct, levels):
    ngroups = max(rows // LANES, 1)
    rowi = lax.broadcasted_iota(jnp.int32, (direct, LANES), 0)
    lanei = lax.broadcasted_iota(jnp.int32, (direct, LANES), 1)
    keep = [jnp.where(lanei == s, rowi, -1) >= s for s in range(direct)]
    zero_group = jnp.zeros((direct, LANES), F32)
    prods = []
    for blk in range(rows // direct):
        r0 = blk * direct
        qb = q[r0:r0 + direct]
        gb = g2[r0:r0 + direct]
        gkb = gk2[r0:r0 + direct]
        for s in range(direct):
            prods.append(qb * jnp.exp2(gb - gkb[s:s + 1, :]))
        if (blk + 1) % _INTRA_YIELD_BLOCKS == 0 and blk + 1 < rows // direct:
            yield
    sums = jnp.dot(jnp.concatenate(prods, axis=0).astype(BF16), jnp.ones((HG_DK, LANES), BF16),
                   preferred_element_type=F32)
    yield
    panels = []
    for blk in range(rows // direct):
        r0 = blk * direct
        ag = zero_group
        for s in range(direct):
            p0 = (blk * direct + s) * direct
            ag = jnp.where(keep[s], sums[p0:p0 + direct], ag)
        lane0 = r0 % LANES
        if lane0:
            ag = pltpu.roll(ag, lane0, axis=1)
        grp = r0 // LANES
        pieces = [zero_group] * grp + [ag] + [zero_group] * (ngroups - grp - 1)
        panels.append(pieces[0] if ngroups == 1 else jnp.concatenate(pieces, axis=1))
        if (blk + 1) % _INTRA_YIELD_BLOCKS == 0 and blk + 1 < rows // direct:
            yield
    a = panels[0] if len(panels) == 1 else jnp.concatenate(panels, axis=0)
    if rows < LANES:
        a = a[:, :rows]
    yield

    if levels:
        rowid = lax.broadcasted_iota(jnp.int32, (rows, 1), 0)
    for idx, (block, sub_size) in enumerate(levels):
        sshift = sub_size.bit_length() - 1
        row_sub = (rowid & (block - 1)) >> sshift
        q_parts, k_parts = [], []
        for j in range(1, block // sub_size):
            refs = []
            for b0 in range(0, rows, block):
                r = b0 + j * sub_size - 1
                refs.append(jnp.broadcast_to(g2[r:r + 1, :], (block, HG_DK)))
            g_ref = refs[0] if len(refs) == 1 else jnp.concatenate(refs, axis=0)
            e = jnp.exp2(-jnp.abs(g2 - g_ref))
            q_parts.append(jnp.where(row_sub == j, q * e, 0.0).astype(BF16))
            k_parts.append((kin * e).astype(BF16))
        qcat = q_parts[0] if len(q_parts) == 1 else jnp.concatenate(q_parts, axis=1)
        kcat = k_parts[0] if len(k_parts) == 1 else jnp.concatenate(k_parts, axis=1)
        a = jnp.where(level_ids == idx, _dot_nt(qcat, kcat), a)
        yield
    return a


def _hgrn_steps(*refs, layer, rows, seq_len, heads, direct, levels, zero_init):
    if zero_init:
        q_ref, f_ref, i_ref, g_ref, lbl_ref, ng_ref, lvl_ref, bb_ref, s_ref = refs
        s0_ref = s_ref
    else:
        q_ref, f_ref, i_ref, g_ref, lbl_ref, ng_ref, lvl_ref, s0_ref, bb_ref, s_ref = refs
    nseq = rows // seq_len

    if zero_init:
        @pl.when(pl.program_id(2) == 0)
        def _init():
            s_ref[...] = jnp.zeros(s_ref.shape, F32)

    same, lower, _ = _seq_masks(rows, seq_len)
    lower_b = lower.astype(BF16)
    same_b = same.astype(BF16)
    level_ids = lvl_ref[...]
    if nseq > 1:
        shift = seq_len.bit_length() - 1
        row_seq = lax.broadcasted_iota(jnp.int32, (rows, 1), 0) >> shift

    def decay_cols(row):
        col = jnp.broadcast_to(jnp.exp2(row), (HG_DK, HG_DK)).T
        return jnp.concatenate([col] * (HG_DV // HG_DK), axis=1)

    for hh in range(heads):
        ks = slice(hh * HG_DK, (hh + 1) * HG_DK)
        vs = slice(hh * HG_DV, (hh + 1) * HG_DV)
        lg = lbl_ref[:, ks]
        ex = jnp.exp(lg - jnp.max(lg, axis=0, keepdims=True))
        lb = (jnp.sum(ex[0:layer + 1, :], axis=0, keepdims=True)
              / jnp.sum(ex, axis=0, keepdims=True))

        f = lb + (1.0 - lb) * _sigmoid(f_ref[:, ks])
        kin = 1.0 - f
        lf2 = jnp.log(f) * LOG2_E
        g2 = _dot01_right(lower_b, lf2)
        gk2 = g2 - jnp.log(kin) * LOG2_E
        yield
        q = q_ref[:, ks].astype(F32)
        if nseq == 1:
            g_end = jnp.broadcast_to(g2[rows - 1:rows, :], (rows, HG_DK))
        else:
            g_end = _dot01_right(same_b, lf2)
        qg = (q * jnp.exp2(g2)).astype(BF16)
        kg = kin * jnp.exp2(g_end - g2)
        ib = i_ref[:, vs].astype(BF16)
        yield

        a = yield from _hgrn_intra(q, kin, g2, gk2, level_ids, rows, direct, levels)
        o = jnp.dot(a.astype(BF16), ib, preferred_element_type=F32)

        if nseq == 1:
            s_prev = s0_ref[0, hh]
            o = o + jnp.dot(qg, s_prev.astype(BF16), preferred_element_type=F32)
            s_ref[0, hh] = decay_cols(g_end[0:1, :]) * s_prev + _dot_tn(kg.astype(BF16), ib)
            yield
        else:
            for j in range(nseq):
                in_seq = row_seq == j
                s_prev = s0_ref[j, hh]
                o_j = jnp.dot(qg, s_prev.astype(BF16), preferred_element_type=F32)
                o = o + jnp.where(in_seq, o_j, 0.0)
                kg_j = jnp.where(in_seq, kg, 0.0).astype(BF16)
                s_ref[j, hh] = (decay_cols(g_end[j * seq_len:j * seq_len + 1, :]) * s_prev
                                + _dot_tn(kg_j, ib))
                yield

        o = o * lax.rsqrt(jnp.mean(o * o, axis=1, keepdims=True) + LN_EPS)
        gate = g_ref[:, vs].astype(F32)
        bb_ref[:, vs] = (o * ng_ref[:, vs] * (gate * _sigmoid(gate))).astype(bb_ref.dtype)
        yield


def _hgrn_parts(proj, hg_f, lb_logits, norm_g, s0, *, layer, row0, batch, seq, rows, seq_len, heads,
                direct, levels, index=None):
    tokens = batch * seq
    zero_init = s0 is None
    nseq = rows // seq_len
    nchunk = seq // seq_len if nseq == 1 else 1
    ngroup = tokens // (rows * nchunk)
    rb0 = row0 // rows
    if index is None:
        index = lambda g, h, c: (g * nchunk + c, h, g)
    row = lambda *ids: index(*ids)[0]
    hblk = lambda *ids: index(*ids)[1]
    sblk = lambda *ids: index(*ids)[2]

    wk, wv = heads * HG_DK, heads * HG_DV
    in_specs = [
        pl.BlockSpec((rows, wk), lambda *ids: (rb0 + row(*ids), _P_HG_Q // wk + hblk(*ids))),
        pl.BlockSpec((rows, wk), lambda *ids: (rb0 + row(*ids), hblk(*ids))),
        pl.BlockSpec((rows, wv), lambda *ids: (rb0 + row(*ids), _P_HG_I // wv + hblk(*ids))),
        pl.BlockSpec((rows, wv), lambda *ids: (rb0 + row(*ids), _P_HG_G // wv + hblk(*ids))),
        pl.BlockSpec((DEPTH + 1, wk), lambda *ids: (0, hblk(*ids))),
        pl.BlockSpec((1, wv), lambda *ids: (0, hblk(*ids))),
        pl.BlockSpec((rows, rows), lambda *ids: (0, 0)),
    ]
    level_ids = jnp.asarray(_hgrn_level_ids(rows, seq_len, direct, levels))
    args = [proj, hg_f, proj, proj, lb_logits, norm_g, level_ids]
    state_spec = pl.BlockSpec((nseq, heads, HG_DK, HG_DV),
                              lambda *ids: (sblk(*ids), hblk(*ids), 0, 0))
    if not zero_init:
        in_specs.append(state_spec)
        args.append(s0)
    return dict(
        steps=functools.partial(_hgrn_steps, layer=layer, rows=rows, seq_len=seq_len, heads=heads,
                                direct=direct, levels=levels, zero_init=zero_init),
        n_steps=heads * (3 + _hgrn_intra_yields(rows, direct, levels) + nseq),
        grid=(ngroup, HG_HEADS // heads, nchunk),
        in_specs=in_specs, args=args,
        out_specs=[pl.BlockSpec((rows, wv), lambda *ids: (row(*ids), hblk(*ids))), state_spec],
        out_shape=[jax.ShapeDtypeStruct((tokens, HG_V), BF16),
                   jax.ShapeDtypeStruct((batch, HG_HEADS, HG_DK, HG_DV), F32)],
        scratch=[])


def _merge_kernel(ba_ref, wa_ref, bb_ref, wb_ref, ga_ref, gb_ref, o_ref):
    ya = jnp.dot(ba_ref[...], wa_ref[...], preferred_element_type=F32)
    yb = jnp.dot(bb_ref[...], wb_ref[...], preferred_element_type=F32)
    ga = _sigmoid(ga_ref[...].astype(F32))
    gb = _sigmoid(gb_ref[...].astype(F32))
    o_ref[...] = (ga * ya + gb * yb).astype(o_ref.dtype)


def _merge(branch_a, w_a, branch_b, w_b, proj, row0):
    m = branch_a.shape[0]
    tm, tn = _MERGE_TM, _MERGE_TN
    rb0 = row0 // tm
    ga_blk = _P_GATE_A // tn
    gb_blk = _P_GATE_B // tn
    return pl.pallas_call(
        _merge_kernel,
        grid=(D_MODEL // tn, m // tm),
        in_specs=[pl.BlockSpec((tm, ML_V), lambda j, i: (i, 0)),
                  pl.BlockSpec((ML_V, tn), lambda j, i: (0, j)),
                  pl.BlockSpec((tm, HG_V), lambda j, i: (i, 0)),
                  pl.BlockSpec((HG_V, tn), lambda j, i: (0, j)),
                  pl.BlockSpec((tm, tn), lambda j, i: (rb0 + i, ga_blk + j)),
                  pl.BlockSpec((tm, tn), lambda j, i: (rb0 + i, gb_blk + j))],
        out_specs=pl.BlockSpec((tm, tn), lambda j, i: (i, j)),
        out_shape=jax.ShapeDtypeStruct((m, D_MODEL), BF16),
        compiler_params=pltpu.CompilerParams(dimension_semantics=("parallel", "arbitrary")),
        name="merge",
    )(branch_a, w_a, branch_b, w_b, proj, proj)


def _layernorm_rows(z, g, b):
    mu = jnp.mean(z, axis=1, keepdims=True)
    zc = z - mu
    var = jnp.mean(zc * zc, axis=1, keepdims=True)
    return zc * lax.rsqrt(var + LN_EPS) * g + b


def _outproj_kernel(mg_ref, w_ref, x_ref, g_ref, b_ref, x1_ref, x1b_ref):
    mix = jnp.dot(mg_ref[...], w_ref[...], preferred_element_type=F32)
    x1 = _layernorm_rows(DEEPNORM_ALPHA * x_ref[...] + mix, g_ref[...], b_ref[...])
    x1_ref[...] = x1
    x1b_ref[...] = x1.astype(BF16)


def _outproj(merged, w_out, x, ln_g, ln_b):
    tm = _OUT_TM
    m = x.shape[0]
    row = lambda i: (i, 0)
    const = lambda i: (0, 0)
    return pl.pallas_call(
        _outproj_kernel,
        grid=(m // tm,),
        in_specs=[pl.BlockSpec((tm, D_MODEL), row),
                  pl.BlockSpec((D_MODEL, D_MODEL), const),
                  pl.BlockSpec((tm, D_MODEL), row),
                  pl.BlockSpec((1, D_MODEL), const),
                  pl.BlockSpec((1, D_MODEL), const)],
        out_specs=[pl.BlockSpec((tm, D_MODEL), row), pl.BlockSpec((tm, D_MODEL), row)],
        out_shape=[jax.ShapeDtypeStruct((m, D_MODEL), F32),
                   jax.ShapeDtypeStruct((m, D_MODEL), BF16)],
        compiler_params=pltpu.CompilerParams(dimension_semantics=("parallel",)),
        name="out_proj_ln",
    )(merged, w_out, x, ln_g, ln_b)


def _mlp_kernel(x1b_ref, wu_ref, wd_ref, x1_ref, g_ref, b_ref, y_ref):
    f = pl.program_id(1)

    @pl.when(f == 0)
    def _init():
        y_ref[...] = jnp.zeros(y_ref.shape, F32)

    hid = jnp.maximum(jnp.dot(x1b_ref[...], wu_ref[...], preferred_element_type=F32), 0.0)
    hid = (hid * hid).astype(BF16)
    y_ref[...] += jnp.dot(hid, wd_ref[...], preferred_element_type=F32)

    @pl.when(f == pl.num_programs(1) - 1)
    def _finish():
        z = DEEPNORM_ALPHA * x1_ref[...] + y_ref[...]
        y_ref[...] = _layernorm_rows(z, g_ref[...], b_ref[...])


def _mlp(x1b, w_up, w_down, x1, ln_g, ln_b):
    tm, tf = _MLP_TM, _MLP_TF
    tokens = x1.shape[0]
    return pl.pallas_call(
        _mlp_kernel,
        grid=(tokens // tm, D_FF // tf),
        in_specs=[pl.BlockSpec((tm, D_MODEL), lambda i, f: (i, 0)),
                  pl.BlockSpec((D_MODEL, tf), lambda i, f: (0, f)),
                  pl.BlockSpec((tf, D_MODEL), lambda i, f: (f, 0)),
                  pl.BlockSpec((tm, D_MODEL), lambda i, f: (i, 0)),
                  pl.BlockSpec((1, D_MODEL), lambda i, f: (0, 0)),
                  pl.BlockSpec((1, D_MODEL), lambda i, f: (0, 0))],
        out_specs=pl.BlockSpec((tm, D_MODEL), lambda i, f: (i, 0)),
        out_shape=jax.ShapeDtypeStruct((tokens, D_MODEL), F32),
        compiler_params=pltpu.CompilerParams(dimension_semantics=("parallel", "arbitrary")),
        name="mlp_ln",
    )(x1b, w_up, w_down, x1, ln_g, ln_b)


_PROMPT_ML = dict(rows=256, seq_len=256, heads=4)
_PROMPT_HG = dict(rows=256, seq_len=256, heads=4, direct=8,
                  levels=((16, 8), (32, 16), (64, 32), (128, 64), (256, 128)))
_SAMPLE_ML = dict(rows=16, seq_len=8, heads=4)
_SAMPLE_HG = dict(rows=32, seq_len=8, heads=8, direct=8, levels=())


def kernel(x_prompt, x_sample, state_mlstm_C, state_mlstm_n, state_mlstm_m, state_hgrn_S,
           hg_lb_logits, w_in, b_ig, b_fg, ml_norm_g, hg_norm_g, w_branch_a, w_branch_b, w_out,
           ln1_g, ln1_b, w_up, w_down, ln2_g, ln2_b):
    batch_p, seq_p, _ = x_prompt.shape
    batch_s, seq_s, _ = x_sample.shape
    tok_p, tok_s = batch_p * seq_p, batch_s * seq_s
    total = tok_p + tok_s
    group_p = dict(row0=0, batch=batch_p, seq=seq_p)
    group_s = dict(row0=tok_p, batch=batch_s, seq=seq_s)
    xp = x_prompt.reshape(tok_p, D_MODEL)
    xs = x_sample.reshape(tok_s, D_MODEL)
    lb_logits = hg_lb_logits.astype(F32)
    states_p, states_s = [], []
    for l in range(DEPTH):
        (proj, hg_f, gates, gates_gm), (w_a_b, w_b_b, w_out_b, w_up_b, w_down_b) = _inproj(
            xp.astype(BF16), xs.astype(BF16), jnp.swapaxes(w_in[l], 0, 1),
            (w_branch_a[l], w_branch_b[l], w_out[l], w_up[l], w_down[l]))

        gate_bias = jnp.concatenate([b_ig[l], b_fg[l]]).astype(F32)
        brow = jnp.zeros((1, LANES), F32).at[0, :_N_GATES].set(gate_bias)
        bcol = gate_bias.reshape(_N_GATES, 1)
        ml_g = ml_norm_g[l].reshape(1, ML_V).astype(F32)
        hg_g = hg_norm_g[l].reshape(1, HG_V).astype(F32)

        m_rows = jnp.repeat(state_mlstm_m[l].astype(F32), seq_s, axis=0)
        gates_s = gates[tok_p:].at[:, _N_GATES:_N_GATES + ML_HEADS].set(m_rows)
        rows_s = _SAMPLE_ML["rows"]
        gm_s = gates_gm[tok_p // _PROJ_TM:]
        gm_s = gm_s.reshape(-1, 8, _PROJ_TM // rows_s, rows_s).transpose(0, 2, 1, 3)
        gm_s = gm_s.reshape(tok_s // rows_s, 8, rows_s)
        state_ml = (state_mlstm_C[l].astype(F32),
                    state_mlstm_n[l].astype(F32).reshape(batch_s, ML_HEADS, 1, ML_DK))
        ln1 = (ln1_g[l].reshape(1, D_MODEL), ln1_b[l].reshape(1, D_MODEL))
        ln2 = (ln2_g[l].reshape(1, D_MODEL), ln2_b[l].reshape(1, D_MODEL))

        ml_p = _mlstm_parts(proj, gates, gates_gm, brow, bcol, ml_g, None, **group_p, **_PROMPT_ML)
        hg_p = _hgrn_parts(proj, hg_f, lb_logits, hg_g, None, layer=l, **group_p, **_PROMPT_HG)

        def guest(parts_fn, host, **kwargs):
            grid = host["grid"]
            assert math.prod(grid) * kwargs["rows"] == tok_s, (grid, kwargs["rows"])
            flat = lambda g, h, c: (g * grid[1] + h) * grid[2] + c
            return parts_fn(index=lambda g, h, c: (flat(g, h, c), 0, flat(g, h, c)), **kwargs)

        ml_s = guest(functools.partial(_mlstm_parts, proj, gates_s, gm_s, brow, bcol, ml_g, state_ml,
                                       **group_s), hg_p, **_SAMPLE_ML)
        hg_s = guest(functools.partial(_hgrn_parts, proj, hg_f, lb_logits, hg_g,
                                       state_hgrn_S[l].astype(F32), layer=l, **group_s),
                     ml_p, **_SAMPLE_HG)
        (bb_p, s_p), (ba_s, c_s, n_s, m_all_s) = _run_parts([hg_p, ml_s], "hgrn2_prompt_mlstm_sample")
        (ba_p, c_p, n_p, m_all_p), (bb_s, s_s) = _run_parts([ml_p, hg_s], "mlstm_prompt_hgrn2_sample")

        merged_p = _merge(ba_p, w_a_b, bb_p, w_b_b, proj, 0)
        merged_s = _merge(ba_s, w_a_b, bb_s, w_b_b, proj, tok_p)
        x1_p, x1b_p = _outproj(merged_p, w_out_b, xp, *ln1)
        x1_s, x1b_s = _outproj(merged_s, w_out_b, xs, *ln1)
        xp = _mlp(x1b_p, w_up_b, w_down_b, x1_p, *ln2)
        xs = _mlp(x1b_s, w_up_b, w_down_b, x1_s, *ln2)

        states_p.append((c_p, n_p.reshape(batch_p, ML_HEADS, ML_DK),
                         m_all_p[:, seq_p - 1::seq_p, 0].T, s_p))
        states_s.append((c_s, n_s.reshape(batch_s, ML_HEADS, ML_DK),
                         m_all_s[:, seq_s - 1::seq_s, 0].T, s_s))
    stack = lambda states, k: jnp.stack([s[k] for s in states])
    return (xp.reshape(batch_p, seq_p, D_MODEL), xs.reshape(batch_s, seq_s, D_MODEL),
            stack(states_p, 0), stack(states_p, 1), stack(states_p, 2), stack(states_p, 3),
            stack(states_s, 0), stack(states_s, 1), stack(states_s, 2), stack(states_s, 3))
```

```python
import functools
import math

import numpy as np
import jax
import jax.numpy as jnp
from jax import lax
from jax.experimental import pallas as pl
from jax.experimental.pallas import tpu as pltpu

F32 = jnp.float32
BF16 = jnp.bfloat16

D_MODEL = 2048
DEPTH = 1
ML_HEADS, ML_DK, ML_DV = 4, 256, 512
HG_HEADS, HG_DK, HG_DV = 8, 128, 256
ML_QK = ML_HEADS * ML_DK
ML_V = ML_HEADS * ML_DV
HG_K = HG_HEADS * HG_DK
HG_V = HG_HEADS * HG_DV
D_FF = 4 * D_MODEL
LN_EPS = 1e-5
DEEPNORM_ALPHA = (2.0 * DEPTH) ** 0.25
ML_K_SCALE = ML_DK ** -0.5
LOG2_E = math.log2(math.e)
LANES = 128

_OFF_ML_Q = 0
_OFF_ML_K = _OFF_ML_Q + ML_QK
_OFF_ML_V = _OFF_ML_K + ML_QK
_OFF_ML_I = _OFF_ML_V + ML_V
_OFF_ML_F = _OFF_ML_I + ML_HEADS
_OFF_ML_O = _OFF_ML_F + ML_HEADS
_OFF_HG_Q = _OFF_ML_O + ML_V
_OFF_HG_F = _OFF_HG_Q + HG_K
_OFF_HG_I = _OFF_HG_F + HG_K
_OFF_HG_G = _OFF_HG_I + HG_V
_OFF_GATE_A = _OFF_HG_G + HG_V
_OFF_GATE_B = _OFF_GATE_A + D_MODEL
D_IN = _OFF_GATE_B + D_MODEL

_N_GATES = 2 * ML_HEADS
_P_ML_Q = _OFF_ML_Q
_P_ML_K = _OFF_ML_K
_P_ML_V = _OFF_ML_V
_P_ML_O = _OFF_ML_O - _N_GATES
_P_HG_Q = _OFF_HG_Q - _N_GATES
_P_HG_F = _OFF_HG_F - _N_GATES
_P_HG_I = _OFF_HG_I - _N_GATES
_P_HG_G = _OFF_HG_G - _N_GATES
_P_GATE_A = _OFF_GATE_A - _N_GATES
_P_GATE_B = _OFF_GATE_B - _N_GATES
_P_WIDTH = D_IN - _N_GATES

_PROJ_TM, _PROJ_TN = 1024, 1024
_MERGE_TM, _MERGE_TN = 1024, 1024
_OUT_TM = 512
_MLP_TM, _MLP_TF = 1024, 512


def _sigmoid(x):
    return 1.0 / (1.0 + jnp.exp(-x))


def _log_sigmoid(x):
    return jnp.minimum(x, 0.0) - jnp.log1p(jnp.exp(-jnp.abs(x)))


def _split2(x):
    hi = x.astype(BF16)
    lo = (x - hi.astype(F32)).astype(BF16)
    return hi, lo


def _dot01_right(t01, x):
    hi, lo = _split2(x)
    return (jnp.dot(t01, hi, preferred_element_type=F32)
            + jnp.dot(t01, lo, preferred_element_type=F32))


def _dot01_left(x, t01):
    hi, lo = _split2(x)
    return (jnp.dot(hi, t01, preferred_element_type=F32)
            + jnp.dot(lo, t01, preferred_element_type=F32))


def _dot_nt(a, b):
    return lax.dot_general(a, b, (((1,), (1,)), ((), ())), preferred_element_type=F32)


def _dot_tn(a, b):
    return lax.dot_general(a, b, (((0,), (0,)), ((), ())), preferred_element_type=F32)


def _interleaved(gens, together):
    done = object()
    for g0 in range(0, len(gens), together):
        alive = gens[g0:g0 + together]
        while alive:
            alive = [g for g in alive if next(g, done) is not done]
            if alive:
                yield


def _seq_masks(rows, seq_len):
    ri = lax.broadcasted_iota(jnp.int32, (rows, rows), 0)
    ci = lax.broadcasted_iota(jnp.int32, (rows, rows), 1)
    lower = jnp.where(ci <= ri, 1.0, 0.0)
    upper = jnp.where(ri <= ci, 1.0, 0.0)
    if seq_len == rows:
        return jnp.ones((rows, rows), F32), lower, upper
    shift = seq_len.bit_length() - 1
    same = jnp.where((ri >> shift) == (ci >> shift), 1.0, 0.0)
    return same, same * lower, same * upper


def _inproj_kernel(xp_ref, xs_ref, wt_hbm, *refs, n_f32, n_gate_tile, prompt_tiles, cast_steps):
    n_cast = (len(refs) - 9) // 2
    cast_in = refs[:n_cast]
    o_ref, f_ref, g_ref, gr_ref = refs[n_cast:n_cast + 4]
    cast_out = refs[n_cast + 4:2 * n_cast + 4]
    wbuf, wbf_sc, wg_sc, sem, gsem = refs[2 * n_cast + 4:]
    n = pl.program_id(0)
    m = pl.program_id(1)
    tn = wbf_sc.shape[1]

    @pl.when(m < cast_steps)
    def _cast_other_weights():
        for src, dst in zip(cast_in, cast_out):
            dst[...] = src[...].astype(BF16)

    def tile_copy(tile):
        start = pl.multiple_of(tile * tn + jnp.where(tile >= n_gate_tile, _N_GATES, 0), _N_GATES)
        return pltpu.make_async_copy(wt_hbm.at[pl.ds(start, tn), :], wbuf, sem.at[0])

    def gate_copy():
        return pltpu.make_async_copy(wt_hbm.at[pl.ds(_OFF_ML_I, _N_GATES), :],
                                     wg_sc.at[pl.ds(0, _N_GATES), :], gsem.at[0])

    @pl.when(m == 0)
    def _next_weight_tile():
        @pl.when(n == 0)
        def _first():
            tile_copy(0).start()
            wg_sc[_N_GATES:, :] = jnp.zeros((LANES - _N_GATES, wg_sc.shape[1]), F32)
            gate_copy().start()
            gate_copy().wait()

        tile_copy(n).wait()
        wbf_sc[...] = wbuf[...].T.astype(BF16)

        @pl.when(n + 1 < pl.num_programs(0))
        def _prefetch():
            tile_copy(n + 1).start()

    x = jnp.where(m < prompt_tiles, xp_ref[...], xs_ref[...])
    acc = jnp.dot(x, wbf_sc[...], preferred_element_type=F32)
    o_ref[...] = acc.astype(BF16)

    @pl.when(n == n_f32)
    def _f32_outputs():
        f_ref[...] = acc
        wg = wg_sc[...].astype(BF16)
        gates = _dot_nt(x, wg)
        g_ref[...] = gates
        gr_ref[0] = gates.T[0:_N_GATES, :]


def _inproj(xp_b, xs_b, w_t, cast_weights):
    k = xp_b.shape[1]
    tm, tn = _PROJ_TM, _PROJ_TN
    p_tiles = xp_b.shape[0] // tm
    n_m = p_tiles + xs_b.shape[0] // tm
    tokens = n_m * tm
    n_n = _P_WIDTH // tn
    n_f32 = _P_HG_F // tn
    cast_steps = min(n_m, 8)

    def parked(n, m):
        return jnp.where(n < n_f32, 0, jnp.where(n == n_f32, m, n_m - 1))

    def cast_spec(w):
        return pl.BlockSpec((w.shape[0] // n_n, w.shape[1] // cast_steps),
                            lambda n, m: (n, jnp.minimum(m, cast_steps - 1)))

    cast_specs = [cast_spec(w) for w in cast_weights]
    outs = pl.pallas_call(
        functools.partial(_inproj_kernel, n_f32=n_f32, n_gate_tile=_OFF_ML_I // tn,
                          prompt_tiles=p_tiles, cast_steps=cast_steps),
        grid=(n_n, n_m),
        in_specs=[pl.BlockSpec((tm, k), lambda n, m: (jnp.minimum(m, p_tiles - 1), 0)),
                  pl.BlockSpec((tm, k), lambda n, m: (jnp.maximum(m - p_tiles, 0), 0)),
                  pl.BlockSpec(memory_space=pl.ANY),
                  *cast_specs],
        out_specs=[pl.BlockSpec((tm, tn), lambda n, m: (m, n)),
                   pl.BlockSpec((tm, HG_K), lambda n, m: (parked(n, m), 0)),
                   pl.BlockSpec((tm, LANES), lambda n, m: (parked(n, m), 0)),
                   pl.BlockSpec((1, 8, tm), lambda n, m: (parked(n, m), 0, 0)),
                   *cast_specs],
        out_shape=[jax.ShapeDtypeStruct((tokens, _P_WIDTH), BF16),
                   jax.ShapeDtypeStruct((tokens, HG_K), F32),
                   jax.ShapeDtypeStruct((tokens, LANES), F32),
                   jax.ShapeDtypeStruct((n_m, 8, tm), F32),
                   *[jax.ShapeDtypeStruct(w.shape, BF16) for w in cast_weights]],
        scratch_shapes=[pltpu.VMEM((tn, k), F32),
                        pltpu.VMEM((k, tn), BF16),
                        pltpu.VMEM((LANES, k), F32),
                        pltpu.SemaphoreType.DMA((1,)),
                        pltpu.SemaphoreType.DMA((1,))],
        compiler_params=pltpu.CompilerParams(dimension_semantics=("arbitrary", "arbitrary")),
        name="in_proj",
    )(xp_b, xs_b, w_t, *cast_weights)
    return outs[:4], outs[4:]


def _mlstm_steps(*refs, rows, seq_len, heads, together, zero_init, head0=None):
    if zero_init:
        (q_ref, k_ref, v_ref, og_ref, gc_ref, gr_ref, brow_ref, bcol_ref, ng_ref,
         ba_ref, c_ref, n_ref, mrow_ref, m_sc) = refs
        c0_ref, n0_ref = c_ref, n_ref
    else:
        (q_ref, k_ref, v_ref, og_ref, gc_ref, gr_ref, brow_ref, bcol_ref, ng_ref, c0_ref, n0_ref,
         ba_ref, c_ref, n_ref, mrow_ref) = refs
    nseq = rows // seq_len
    if head0 is None:
        head0 = pl.program_id(1) * heads

    if zero_init:
        @pl.when(pl.program_id(2) == 0)
        def _init():
            c_ref[...] = jnp.zeros(c_ref.shape, F32)
            n_ref[...] = jnp.zeros(n_ref.shape, F32)
            m_sc[...] = jnp.zeros(m_sc.shape, F32)

    same, lower, upper = _seq_masks(rows, seq_len)
    causal = lower > 0.5
    lane = lax.broadcasted_iota(jnp.int32, (rows, LANES), 1)
    sub = lax.broadcasted_iota(jnp.int32, (8, rows), 0)

    def sel_lane(x, idx):
        return jnp.sum(jnp.where(lane == idx, x, 0.0), axis=1, keepdims=True)

    def sel_sub(x, idx):
        return jnp.sum(jnp.where(sub == idx, x, 0.0), axis=0, keepdims=True)

    gc_raw = gc_ref[...]
    gc = gc_raw + brow_ref[...]
    lf_cols = _log_sigmoid(gc)
    b_cols = _dot01_right(lower.astype(BF16), lf_cols)
    gr = gr_ref[0] + bcol_ref[...]
    b_rows = _dot01_left(_log_sigmoid(gr), upper.astype(BF16))
    if nseq > 1:
        b_ends = _dot01_right(same.astype(BF16), lf_cols)
        ci = lax.broadcasted_iota(jnp.int32, (rows, rows), 1)
        last = (same * jnp.where((ci & (seq_len - 1)) == seq_len - 1, 1.0, 0.0)).astype(BF16)
        shift = seq_len.bit_length() - 1
        row_seq = lax.broadcasted_iota(jnp.int32, (rows, 1), 0) >> shift

    def head_steps(hh):
        head = head0 + hh
        ks = slice(hh * ML_DK, (hh + 1) * ML_DK)
        vs = slice(hh * ML_DV, (hh + 1) * ML_DV)
        ig_col = sel_lane(gc, head)
        b_col = sel_lane(b_cols, head + ML_HEADS)
        if zero_init:
            m_prev = jnp.broadcast_to(m_sc[hh:hh + 1, 0:1], (rows, 1))
        else:
            m_prev = sel_lane(gc_raw, head + 2 * ML_HEADS)
        ig_row = sel_sub(gr, head)
        b_row = sel_sub(b_rows, head + ML_HEADS)

        logd = jnp.where(causal, (b_col - b_row) + ig_row, -jnp.inf)
        m_t = jnp.maximum(b_col + m_prev, jnp.max(logd, axis=1, keepdims=True))
        d = jnp.exp(logd - m_t)
        w_inter = jnp.exp(b_col + m_prev - m_t)
        yield

        qb = q_ref[:, ks].astype(BF16)
        kb = k_ref[:, ks].astype(BF16)
        vb = v_ref[:, vs].astype(BF16)
        s = _dot_nt(qb, kb) * (d * ML_K_SCALE)
        num = jnp.dot(s.astype(BF16), vb, preferred_element_type=F32)
        den = jnp.sum(s, axis=1, keepdims=True)

        if nseq == 1:
            b_end = b_col[rows - 1:rows, :]
            m_new = m_t[rows - 1:rows, :]
        else:
            b_end = sel_lane(b_ends, head + ML_HEADS)
            m_new = _dot01_right(last, jnp.broadcast_to(m_t, (rows, LANES)))[:, 0:1]
        w_end = jnp.exp(b_end - b_col + ig_col - m_new)
        decay = jnp.exp(b_end + m_prev - m_new)

        qf = qb.astype(F32)
        kw = (w_end * ML_K_SCALE) * kb.astype(F32)
        yield

        if nseq == 1:
            c_prev = c0_ref[0, hh]
            n_prev = n0_ref[0, hh]
            q_c = jnp.dot(qb, c_prev.astype(BF16), preferred_element_type=F32)
            q_n = jnp.sum(qf * n_prev, axis=1, keepdims=True)
            dec = decay[0:1, :]
            c_ref[0, hh] = dec * c_prev + _dot_tn(kw.astype(BF16), vb)
            n_ref[0, hh] = dec * n_prev + jnp.sum(kw, axis=0, keepdims=True)
        else:
            q_c = jnp.zeros((rows, ML_DV), F32)
            q_n = jnp.zeros((rows, 1), F32)
            for j in range(nseq):
                in_seq = row_seq == j
                c_prev = c0_ref[j, hh]
                n_prev = n0_ref[j, hh]
                q_c = jnp.where(
                    in_seq, jnp.dot(qb, c_prev.astype(BF16), preferred_element_type=F32), q_c)
                q_n = jnp.where(in_seq, jnp.sum(qf * n_prev, axis=1, keepdims=True), q_n)
                kw_j = jnp.where(in_seq, kw, 0.0)
                dec = decay[j * seq_len:j * seq_len + 1, :]
                c_ref[j, hh] = dec * c_prev + _dot_tn(kw_j.astype(BF16), vb)
                n_ref[j, hh] = dec * n_prev + jnp.sum(kw_j, axis=0, keepdims=True)
                yield
        if nseq == 1:
            yield

        num = num + w_inter * q_c
        den = den + w_inter * q_n
        h_out = num / jnp.maximum(jnp.abs(den), jnp.exp(-m_t))
        mu =jnp.mean(h_out, axis=1, keepdims=True)
        xc = h_out - mu
        var = jnp.mean(xc * xc, axis=1, keepdims=True)
        hn = xc * lax.rsqrt(var + LN_EPS) * ng_ref[:, vs]
        ba_ref[:, vs] = (hn * _sigmoid(og_ref[:, vs].astype(F32))).astype(ba_ref.dtype)
        mrow_ref[hh] = jnp.broadcast_to(m_t, (rows, LANES))
        if zero_init:
            m_sc[hh:hh + 1, :] = jnp.broadcast_to(m_new, (1, LANES))
        yield

    yield from _interleaved([head_steps(hh) for hh in range(heads)], together)


def _mlstm_parts(proj, gates_col, gates_row, brow, bcol, norm_g, state, *, row0, batch, seq, rows,
                 seq_len, heads, together, index=None):
    tokens = batch * seq
    zero_init = state is None
    nseq = rows // seq_len
    nchunk = seq // seq_len if nseq == 1 else 1
    ngroup = tokens // (rows * nchunk)
    bq, bv = heads * ML_DK, heads * ML_DV
    rb0 = row0 // rows
    per_row_tile = gates_row.shape[2] // rows
    head0 = None
    if index is None:
        index = lambda g, h, c: (g * nchunk + c, h, g)
    else:
        assert heads == ML_HEADS
        head0 = 0
    row = lambda *ids: index(*ids)[0]
    hblk = lambda *ids: index(*ids)[1]
    sblk = lambda *ids: index(*ids)[2]

    in_specs = [
        pl.BlockSpec((rows, bq), lambda *ids: (rb0 + row(*ids), _P_ML_Q // bq + hblk(*ids))),
        pl.BlockSpec((rows, bq), lambda *ids: (rb0 + row(*ids), _P_ML_K // bq + hblk(*ids))),
        pl.BlockSpec((rows, bv), lambda *ids: (rb0 + row(*ids), _P_ML_V // bv + hblk(*ids))),
        pl.BlockSpec((rows, bv), lambda *ids: (rb0 + row(*ids), _P_ML_O // bv + hblk(*ids))),
        pl.BlockSpec((rows, LANES), lambda *ids: (row(*ids), 0)),
        pl.BlockSpec((1, 8, rows),
                     lambda *ids: (row(*ids) // per_row_tile, 0, row(*ids) % per_row_tile)),
        pl.BlockSpec((1, LANES), lambda *ids: (0, 0)),
        pl.BlockSpec((8, 1), lambda *ids: (0, 0)),
        pl.BlockSpec((1, bv), lambda *ids: (0, hblk(*ids))),
    ]
    args = [proj, proj, proj, proj, gates_col, gates_row, brow, bcol, norm_g]
    state_specs = [
        pl.BlockSpec((nseq, heads, ML_DK, ML_DV), lambda *ids: (sblk(*ids), hblk(*ids), 0, 0)),
        pl.BlockSpec((nseq, heads, 1, ML_DK), lambda *ids: (sblk(*ids), hblk(*ids), 0, 0)),
    ]
    scratch = []
    if zero_init:
        scratch = [pltpu.VMEM((8, LANES), F32)]
    else:
        in_specs += state_specs
        args += [state[0], state[1]]
    out_specs = [
        pl.BlockSpec((rows, bv), lambda *ids: (row(*ids), hblk(*ids))),
        *state_specs,
        pl.BlockSpec((heads, rows, LANES), lambda *ids: (hblk(*ids), row(*ids), 0)),
    ]
    out_shape = [
        jax.ShapeDtypeStruct((tokens, ML_V), BF16),
        jax.ShapeDtypeStruct((batch, ML_HEADS, ML_DK, ML_DV), F32),
        jax.ShapeDtypeStruct((batch, ML_HEADS, 1, ML_DK), F32),
        jax.ShapeDtypeStruct((ML_HEADS, tokens, LANES), F32),
    ]
    return dict(
        steps=functools.partial(_mlstm_steps, rows=rows, seq_len=seq_len, heads=heads,
                                together=together, zero_init=zero_init, head0=head0),
        n_steps=heads // together * (3 + nseq),
        grid=(ngroup, ML_HEADS // heads, nchunk),
        in_specs=in_specs, args=args, out_specs=out_specs, out_shape=out_shape, scratch=scratch)


def _run_parts(parts, name):
    n_in = [len(p["in_specs"]) for p in parts]
    n_out = [len(p["out_specs"]) for p in parts]
    n_scr = [len(p["scratch"]) for p in parts]

    def kernel(*refs):
        ins = refs[:sum(n_in)]
        outs = refs[sum(n_in):sum(n_in) + sum(n_out)]
        scr = refs[sum(n_in) + sum(n_out):]
        gens = []
        i = o = s = 0
        for p, ni, no, ns in zip(parts, n_in, n_out, n_scr):
            gens.append(p["steps"](*ins[i:i + ni], *outs[o:o + no], *scr[s:s + ns]))
            i, o, s = i + ni, o + no, s + ns
        order = sorted(((j + 0.5) / p["n_steps"], k) for k, p in enumerate(parts)
                       for j in range(p["n_steps"]))
        for _, k in order:
            next(gens[k], None)
        for gen in gens:
            for _ in gen:
                pass

    outs = pl.pallas_call(
        kernel,
        grid=parts[0]["grid"],
        in_specs=[s for p in parts for s in p["in_specs"]],
        out_specs=[s for p in parts for s in p["out_specs"]],
        out_shape=[s for p in parts for s in p["out_shape"]],
        scratch_shapes=[s for p in parts for s in p["scratch"]],
        compiler_params=pltpu.CompilerParams(
            dimension_semantics=("parallel", "parallel", "arbitrary")),
        name=name,
    )(*[a for p in parts for a in p["args"]])
    split, o = [], 0
    for no in n_out:
        split.append(outs[o:o + no])
        o += no
    return split


def _hgrn_level_ids(rows, seq_len, direct, levels):
    t = np.arange(rows)[:, None]
    s = np.arange(rows)[None, :]
    ids = np.full((rows, rows), -1, np.int32)
    count = ((t // direct == s // direct) & (s <= t)).astype(np.int32)
    for idx, (block, sub_size) in enumerate(levels):
        owned = (t // block == s // block) & ((s % block) // sub_size < (t % block) // sub_size)
        ids[owned] = idx
        count += owned
    wanted = (t // seq_len == s // seq_len) & (s <= t)
    assert np.array_equal(count, wanted.astype(np.int32)), (rows, seq_len, direct, levels)
    return ids


_INTRA_YIELD_BLOCKS = 32


def _hgrn_intra_yields(rows, direct, levels):
    return 2 + len(levels) + 2 * (max(rows // direct // _INTRA_YIELD_BLOCKS, 1) - 1)


def _hgrn_intra(q, kin, g2, gk2, level_ids, rows, direct, levels):
    ngroups = max(rows // LANES, 1)
    rowi = lax.broadcasted_iota(jnp.int32, (direct, LANES), 0)
    lanei = lax.broadcasted_iota(jnp.int32, (direct, LANES), 1)
    keep = [jnp.where(lanei == s, rowi, -1) >= s for s in range(direct)]
    zero_group = jnp.zeros((direct, LANES), F32)
    prods = []
    for blk in range(rows // direct):
        r0 = blk * direct
        qb = q[r0:r0 + direct]
        gb = g2[r0:r0 + direct]
        gkb = gk2[r0:r0 + direct]
        for s in range(direct):
            prods.append(qb * jnp.exp2(gb - gkb[s:s + 1, :]))
        if (blk + 1) % _INTRA_YIELD_BLOCKS == 0 and blk + 1 < rows // direct:
            yield
    sums = jnp.dot(jnp.concatenate(prods, axis=0).astype(BF16), jnp.ones((HG_DK, LANES), BF16),
                   preferred_element_type=F32)
    yield
    panels = []
    for blk in range(rows // direct):
        r0 = blk * direct
        ag = zero_group
        for s in range(direct):
            p0 = (blk * direct + s) * direct
            ag = jnp.where(keep[s], sums[p0:p0 + direct], ag)
        lane0 = r0 % LANES
        if lane0:
            ag = pltpu.roll(ag, lane0, axis=1)
        grp = r0 // LANES
        pieces = [zero_group] * grp + [ag] + [zero_group] * (ngroups - grp - 1)
        panels.append(pieces[0] if ngroups == 1 else jnp.concatenate(pieces, axis=1))
        if (blk + 1) % _INTRA_YIELD_BLOCKS == 0 and blk + 1 < rows // direct:
            yield
    a = panels[0] if len(panels) == 1 else jnp.concatenate(panels, axis=0)
    if rows < LANES:
        a = a[:, :rows]
    yield

    if levels:
        rowid = lax.broadcasted_iota(jnp.int32, (rows, 1), 0)
    for idx, (block, sub_size) in enumerate(levels):
        sshift = sub_size.bit_length() - 1
        row_sub = (rowid & (block - 1)) >> sshift
        q_parts, k_parts = [], []
        for j in range(1, block // sub_size):
            refs = []
            for b0 in range(0, rows, block):
                r = b0 + j * sub_size - 1
                refs.append(jnp.broadcast_to(g2[r:r + 1, :], (block, HG_DK)))
            g_ref = refs[0] if len(refs) == 1 else jnp.concatenate(refs, axis=0)
            e = jnp.exp2(-jnp.abs(g2 - g_ref))
            q_parts.append(jnp.where(row_sub == j, q * e, 0.0).astype(BF16))
            k_parts.append((kin * e).astype(BF16))
        qcat = q_parts[0] if len(q_parts) == 1 else jnp.concatenate(q_parts, axis=1)
        kcat = k_parts[0] if len(k_parts) == 1 else jnp.concatenate(k_parts, axis=1)
        a = jnp.where(level_ids == idx, _dot_nt(qcat, kcat), a)
        yield
    return a


def _hgrn_steps(*refs, layer, rows, seq_len, heads, together, direct, levels, zero_init):
    if zero_init:
        q_ref, f_ref, i_ref, g_ref, lbl_ref, ng_ref, lvl_ref, bb_ref, s_ref = refs
        s0_ref = s_ref
    else:
        q_ref, f_ref, i_ref, g_ref, lbl_ref, ng_ref, lvl_ref, s0_ref, bb_ref, s_ref = refs
    nseq = rows // seq_len

    if zero_init:
        @pl.when(pl.program_id(2) == 0)
        def _init():
            s_ref[...] = jnp.zeros(s_ref.shape, F32)

    same, lower, _ = _seq_masks(rows, seq_len)
    lower_b = lower.astype(BF16)
    same_b = same.astype(BF16)
    level_ids = lvl_ref[...]
    if nseq > 1:
        shift = seq_len.bit_length() - 1
        row_seq = lax.broadcasted_iota(jnp.int32, (rows, 1), 0) >> shift

    def decay_cols(row):
        col = jnp.broadcast_to(jnp.exp2(row), (HG_DK, HG_DK)).T
        return jnp.concatenate([col] * (HG_DV // HG_DK), axis=1)

    def head_steps(hh):
        ks = slice(hh * HG_DK, (hh + 1) * HG_DK)
        vs = slice(hh * HG_DV, (hh + 1) * HG_DV)
        lg = lbl_ref[:, ks]
        ex = jnp.exp(lg - jnp.max(lg, axis=0, keepdims=True))
        lb = (jnp.sum(ex[0:layer + 1, :], axis=0, keepdims=True)
              / jnp.sum(ex, axis=0, keepdims=True))

        f = lb + (1.0 - lb) * _sigmoid(f_ref[:, ks])
        kin = 1.0 - f
        lf2 = jnp.log(f) * LOG2_E
        g2 = _dot01_right(lower_b, lf2)
        gk2 = g2 - jnp.log(kin) * LOG2_E
        yield
        q = q_ref[:, ks].astype(F32)
        if nseq == 1:
            g_end = jnp.broadcast_to(g2[rows - 1:rows, :], (rows, HG_DK))
        else:
            g_end = _dot01_right(same_b, lf2)
        qg = (q * jnp.exp2(g2)).astype(BF16)
        kg = kin * jnp.exp2(g_end - g2)
        ib = i_ref[:, vs].astype(BF16)
        yield

        a = yield from _hgrn_intra(q, kin, g2, gk2, level_ids, rows, direct, levels)
        o = jnp.dot(a.astype(BF16), ib, preferred_element_type=F32)

        if nseq == 1:
            s_prev = s0_ref[0, hh]
            o = o + jnp.dot(qg, s_prev.astype(BF16), preferred_element_type=F32)
            s_ref[0, hh] = decay_cols(g_end[0:1, :]) * s_prev + _dot_tn(kg.astype(BF16), ib)
            yield
        else:
            for j in range(nseq):
                in_seq = row_seq == j
                s_prev = s0_ref[j, hh]
                o_j = jnp.dot(qg, s_prev.astype(BF16), preferred_element_type=F32)
                o = o + jnp.where(in_seq, o_j, 0.0)
                kg_j = jnp.where(in_seq, kg, 0.0).astype(BF16)
                s_ref[j, hh] = (decay_cols(g_end[j * seq_len:j * seq_len + 1, :]) * s_prev
                                + _dot_tn(kg_j, ib))
                yield

        o = o * lax.rsqrt(jnp.mean(o * o, axis=1, keepdims=True) + LN_EPS)
        gate = g_ref[:, vs].astype(F32)
        bb_ref[:, vs] = (o * ng_ref[:, vs] * (gate * _sigmoid(gate))).astype(bb_ref.dtype)
        yield

    yield from _interleaved([head_steps(hh) for hh in range(heads)], together)


def _hgrn_parts(proj, hg_f, lb_logits, norm_g, s0, *, layer, row0, batch, seq, rows, seq_len, heads,
                together, direct, levels, index=None):
    tokens = batch * seq
    zero_init = s0 is None
    nseq = rows // seq_len
    nchunk = seq // seq_len if nseq == 1 else 1
    ngroup = tokens // (rows * nchunk)
    rb0 = row0 // rows
    if index is None:
        index = lambda g, h, c: (g * nchunk + c, h, g)
    row = lambda *ids: index(*ids)[0]
    hblk = lambda *ids: index(*ids)[1]
    sblk = lambda *ids: index(*ids)[2]

    wk, wv = heads * HG_DK, heads * HG_DV
    in_specs = [
        pl.BlockSpec((rows, wk), lambda *ids: (rb0 + row(*ids), _P_HG_Q // wk + hblk(*ids))),
        pl.BlockSpec((rows, wk), lambda *ids: (rb0 + row(*ids), hblk(*ids))),
        pl.BlockSpec((rows, wv), lambda *ids: (rb0 + row(*ids), _P_HG_I // wv + hblk(*ids))),
        pl.BlockSpec((rows, wv), lambda *ids: (rb0 + row(*ids), _P_HG_G // wv + hblk(*ids))),
        pl.BlockSpec((DEPTH + 1, wk), lambda *ids: (0, hblk(*ids))),
        pl.BlockSpec((1, wv), lambda *ids: (0, hblk(*ids))),
        pl.BlockSpec((rows, rows), lambda *ids: (0, 0)),
    ]
    level_ids = jnp.asarray(_hgrn_level_ids(rows, seq_len, direct, levels))
    args = [proj, hg_f, proj, proj, lb_logits, norm_g, level_ids]
    state_spec = pl.BlockSpec((nseq, heads, HG_DK, HG_DV),
                              lambda *ids: (sblk(*ids), hblk(*ids), 0, 0))
    if not zero_init:
        in_specs.append(state_spec)
        args.append(s0)
    return dict(
        steps=functools.partial(_hgrn_steps, layer=layer, rows=rows, seq_len=seq_len, heads=heads,
                                together=together, direct=direct, levels=levels,
                                zero_init=zero_init),
        n_steps=heads // together * (3 + _hgrn_intra_yields(rows, direct, levels) + nseq),
        grid=(ngroup, HG_HEADS // heads, nchunk),
        in_specs=in_specs, args=args,
        out_specs=[pl.BlockSpec((rows, wv), lambda *ids: (row(*ids), hblk(*ids))), state_spec],
        out_shape=[jax.ShapeDtypeStruct((tokens, HG_V), BF16),
                   jax.ShapeDtypeStruct((batch, HG_HEADS, HG_DK, HG_DV), F32)],
        scratch=[])


def _merge_kernel(ba_ref, wa_ref, bb_ref, wb_ref, ga_ref, gb_ref, o_ref):
    ya = jnp.dot(ba_ref[...], wa_ref[...], preferred_element_type=F32)
    yb = jnp.dot(bb_ref[...], wb_ref[...], preferred_element_type=F32)
    ga = _sigmoid(ga_ref[...].astype(F32))
    gb = _sigmoid(gb_ref[...].astype(F32))
    o_ref[...] = (ga * ya + gb * yb).astype(o_ref.dtype)


def _merge(branch_a, w_a, branch_b, w_b, proj, row0):
    m = branch_a.shape[0]
    tm, tn = _MERGE_TM, _MERGE_TN
    rb0 = row0 // tm
    ga_blk = _P_GATE_A // tn
    gb_blk = _P_GATE_B // tn
    return pl.pallas_call(
        _merge_kernel,
        grid=(D_MODEL // tn, m // tm),
        in_specs=[pl.BlockSpec((tm, ML_V), lambda j, i: (i, 0)),
                  pl.BlockSpec((ML_V, tn), lambda j, i: (0, j)),
                  pl.BlockSpec((tm, HG_V), lambda j, i: (i, 0)),
                  pl.BlockSpec((HG_V, tn), lambda j, i: (0, j)),
                  pl.BlockSpec((tm, tn), lambda j, i: (rb0 + i, ga_blk + j)),
                  pl.BlockSpec((tm, tn), lambda j, i: (rb0 + i, gb_blk + j))],
        out_specs=pl.BlockSpec((tm, tn), lambda j, i: (i, j)),
        out_shape=jax.ShapeDtypeStruct((m, D_MODEL), BF16),
        compiler_params=pltpu.CompilerParams(dimension_semantics=("parallel", "arbitrary")),
        name="merge",
    )(branch_a, w_a, branch_b, w_b, proj, proj)


def _layernorm_rows(z, g, b):
    mu = jnp.mean(z, axis=1, keepdims=True)
    zc = z - mu
    var = jnp.mean(zc * zc, axis=1, keepdims=True)
    return zc * lax.rsqrt(var + LN_EPS) * g + b


def _outproj_kernel(mg_ref, w_ref, x_ref, g_ref, b_ref, x1_ref):
    mix = jnp.dot(mg_ref[...], w_ref[...], preferred_element_type=F32)
    x1_ref[...] = _layernorm_rows(DEEPNORM_ALPHA * x_ref[...] + mix, g_ref[...], b_ref[...])


def _outproj(merged, w_out, x, ln_g, ln_b):
    tm = _OUT_TM
    m = x.shape[0]
    row = lambda i: (i, 0)
    const = lambda i: (0, 0)
    return pl.pallas_call(
        _outproj_kernel,
        grid=(m // tm,),
        in_specs=[pl.BlockSpec((tm, D_MODEL), row),
                  pl.BlockSpec((D_MODEL, D_MODEL), const),
                  pl.BlockSpec((tm, D_MODEL), row),
                  pl.BlockSpec((1, D_MODEL), const),
                  pl.BlockSpec((1, D_MODEL), const)],
        out_specs=pl.BlockSpec((tm, D_MODEL), row),
        out_shape=jax.ShapeDtypeStruct((m, D_MODEL), F32),
        compiler_params=pltpu.CompilerParams(dimension_semantics=("parallel",)),
        name="out_proj_ln",
    )(merged, w_out, x, ln_g, ln_b)


def _mlp_kernel(wu_ref, wd_ref, x1_ref, g_ref, b_ref, y_ref, x1b_sc):
    f = pl.program_id(1)

    @pl.when(f == 0)
    def _init():
        y_ref[...] = jnp.zeros(y_ref.shape, F32)
        x1b_sc[...] = x1_ref[...].astype(BF16)

    hid = jnp.maximum(jnp.dot(x1b_sc[...], wu_ref[...], preferred_element_type=F32), 0.0)
    hid = (hid * hid).astype(BF16)
    y_ref[...] += jnp.dot(hid, wd_ref[...], preferred_element_type=F32)

    @pl.when(f == pl.num_programs(1) - 1)
    def _finish():
        z = DEEPNORM_ALPHA * x1_ref[...] + y_ref[...]
        y_ref[...] = _layernorm_rows(z, g_ref[...], b_ref[...])


def _mlp(w_up, w_down, x1, ln_g, ln_b):
    tm, tf = _MLP_TM, _MLP_TF
    tokens = x1.shape[0]
    return pl.pallas_call(
        _mlp_kernel,
        grid=(tokens // tm, D_FF // tf),
        in_specs=[pl.BlockSpec((D_MODEL, tf), lambda i, f: (0, f)),
                  pl.BlockSpec((tf, D_MODEL), lambda i, f: (f, 0)),
                  pl.BlockSpec((tm, D_MODEL), lambda i, f: (i, 0)),
                  pl.BlockSpec((1, D_MODEL), lambda i, f: (0, 0)),
                  pl.BlockSpec((1, D_MODEL), lambda i, f: (0, 0))],
        out_specs=pl.BlockSpec((tm, D_MODEL), lambda i, f: (i, 0)),
        out_shape=jax.ShapeDtypeStruct((tokens, D_MODEL), F32),
        scratch_shapes=[pltpu.VMEM((tm, D_MODEL), BF16)],
        compiler_params=pltpu.CompilerParams(dimension_semantics=("parallel", "arbitrary")),
        name="mlp_ln",
    )(w_up, w_down, x1, ln_g, ln_b)


_PROMPT_ML = dict(rows=256, seq_len=256, heads=4, together=1)
_PROMPT_HG = dict(rows=256, seq_len=256, heads=4, together=2, direct=8,
                  levels=((16, 8), (32, 16), (64, 32), (128, 64), (256, 128)))
_SAMPLE_ML = dict(rows=16, seq_len=8, heads=4, together=4)
_SAMPLE_HG = dict(rows=32, seq_len=8, heads=8, together=1, direct=8, levels=())


def kernel(x_prompt, x_sample, state_mlstm_C, state_mlstm_n, state_mlstm_m, state_hgrn_S,
           hg_lb_logits, w_in, b_ig, b_fg, ml_norm_g, hg_norm_g, w_branch_a, w_branch_b, w_out,
           ln1_g, ln1_b, w_up, w_down, ln2_g, ln2_b):
    batch_p, seq_p, _ = x_prompt.shape
    batch_s, seq_s, _ = x_sample.shape
    tok_p, tok_s = batch_p * seq_p, batch_s * seq_s
    total = tok_p + tok_s
    group_p = dict(row0=0, batch=batch_p, seq=seq_p)
    group_s = dict(row0=tok_p, batch=batch_s, seq=seq_s)
    xp = x_prompt.reshape(tok_p, D_MODEL)
    xs = x_sample.reshape(tok_s, D_MODEL)
    lb_logits = hg_lb_logits.astype(F32)
    states_p, states_s = [], []
    for l in range(DEPTH):
        (proj, hg_f, gates, gates_gm), (w_a_b, w_b_b, w_out_b, w_up_b, w_down_b) = _inproj(
            xp.astype(BF16), xs.astype(BF16), jnp.swapaxes(w_in[l], 0, 1),
            (w_branch_a[l], w_branch_b[l], w_out[l], w_up[l], w_down[l]))

        gate_bias = jnp.concatenate([b_ig[l], b_fg[l]]).astype(F32)
        brow = jnp.zeros((1, LANES), F32).at[0, :_N_GATES].set(gate_bias)
        bcol = gate_bias.reshape(_N_GATES, 1)
        ml_g = ml_norm_g[l].reshape(1, ML_V).astype(F32)
        hg_g = hg_norm_g[l].reshape(1, HG_V).astype(F32)

        m_rows = jnp.repeat(state_mlstm_m[l].astype(F32), seq_s, axis=0)
        gates_s = gates[tok_p:].at[:, _N_GATES:_N_GATES + ML_HEADS].set(m_rows)
        rows_s = _SAMPLE_ML["rows"]
        gm_s = gates_gm[tok_p // _PROJ_TM:]
        gm_s = gm_s.reshape(-1, 8, _PROJ_TM // rows_s, rows_s).transpose(0, 2, 1, 3)
        gm_s = gm_s.reshape(tok_s // rows_s, 8, rows_s)
        state_ml = (state_mlstm_C[l].astype(F32),
                    state_mlstm_n[l].astype(F32).reshape(batch_s, ML_HEADS, 1, ML_DK))
        ln1 = (ln1_g[l].reshape(1, D_MODEL), ln1_b[l].reshape(1, D_MODEL))
        ln2 = (ln2_g[l].reshape(1, D_MODEL), ln2_b[l].reshape(1, D_MODEL))

        ml_p = _mlstm_parts(proj, gates, gates_gm, brow, bcol, ml_g, None, **group_p, **_PROMPT_ML)
        hg_p = _hgrn_parts(proj, hg_f, lb_logits, hg_g, None, layer=l, **group_p, **_PROMPT_HG)

        def guest(parts_fn, host, **kwargs):
            grid = host["grid"]
            assert math.prod(grid) * kwargs["rows"] == tok_s, (grid, kwargs["rows"])
            flat = lambda g, h, c: (g * grid[1] + h) * grid[2] + c
            return parts_fn(index=lambda g, h, c: (flat(g, h, c), 0, flat(g, h, c)), **kwargs)

        ml_s = guest(functools.partial(_mlstm_parts, proj, gates_s, gm_s, brow, bcol, ml_g, state_ml,
                                       **group_s), hg_p, **_SAMPLE_ML)
        hg_s = guest(functools.partial(_hgrn_parts, proj, hg_f, lb_logits, hg_g,
                                       state_hgrn_S[l].astype(F32), layer=l, **group_s),
                     ml_p, **_SAMPLE_HG)
        (bb_p, s_p), (ba_s, c_s, n_s, m_all_s) = _run_parts([hg_p, ml_s], "hgrn2_prompt_mlstm_sample")
        (ba_p, c_p, n_p, m_all_p), (bb_s, s_s) = _run_parts([ml_p, hg_s], "mlstm_prompt_hgrn2_sample")

        merged_p = _merge(ba_p, w_a_b, bb_p, w_b_b, proj, 0)
        merged_s = _merge(ba_s, w_a_b, bb_s, w_b_b, proj, tok_p)
        x1_p = _outproj(merged_p, w_out_b, xp, *ln1)
        x1_s = _outproj(merged_s, w_out_b, xs, *ln1)
        xp = _mlp(w_up_b, w_down_b, x1_p, *ln2)
        xs = _mlp(w_up_b, w_down_b, x1_s, *ln2)

        states_p.append((c_p, n_p.reshape(batch_p, ML_HEADS, ML_DK),
                         m_all_p[:, seq_p - 1::seq_p, 0].T, s_p))
        states_s.append((c_s, n_s.reshape(batch_s, ML_HEADS, ML_DK),
                         m_all_s[:, seq_s - 1::seq_s, 0].T, s_s))
    stack = lambda states, k: jnp.stack([s[k] for s in states])
    return (xp.reshape(batch_p, seq_p, D_MODEL), xs.reshape(batch_s, seq_s, D_MODEL),
            stack(states_p, 0), stack(states_p, 1), stack(states_p, 2), stack(states_p, 3),
            stack(states_s, 0), stack(states_s, 1), stack(states_s, 2), stack(states_s, 3))
```

```python
import functools
import math

import numpy as np
import jax
import jax.numpy as jnp
from jax import lax
from jax.experimental import pallas as pl
from jax.experimental.pallas import tpu as pltpu

F32 = jnp.float32
BF16 = jnp.bfloat16

D_MODEL = 2048
DEPTH = 1
ML_HEADS, ML_DK, ML_DV = 4, 256, 512
HG_HEADS, HG_DK, HG_DV = 8, 128, 256
ML_QK = ML_HEADS * ML_DK
ML_V = ML_HEADS * ML_DV
HG_K = HG_HEADS * HG_DK
HG_V = HG_HEADS * HG_DV
D_FF = 4 * D_MODEL
LN_EPS = 1e-5
DEEPNORM_ALPHA = (2.0 * DEPTH) ** 0.25
ML_K_SCALE = ML_DK ** -0.5
LANES = 128
SUBLANES = 8

_OFF_ML_Q = 0
_OFF_ML_K = _OFF_ML_Q + ML_QK
_OFF_ML_V = _OFF_ML_K + ML_QK
_OFF_ML_I = _OFF_ML_V + ML_V
_OFF_ML_F = _OFF_ML_I + ML_HEADS
_OFF_ML_O = _OFF_ML_F + ML_HEADS
_OFF_HG_Q = _OFF_ML_O + ML_V
_OFF_HG_F = _OFF_HG_Q + HG_K
_OFF_HG_I = _OFF_HG_F + HG_K
_OFF_HG_G = _OFF_HG_I + HG_V
_OFF_GATE_A = _OFF_HG_G + HG_V
_OFF_GATE_B = _OFF_GATE_A + D_MODEL
D_IN = _OFF_GATE_B + D_MODEL

_N_GATES = 2 * ML_HEADS
_P_ML_Q = _OFF_ML_Q
_P_ML_K = _OFF_ML_K
_P_ML_V = _OFF_ML_V
_P_ML_O = _OFF_ML_O - _N_GATES
_P_HG_Q = _OFF_HG_Q - _N_GATES
_P_HG_F = _OFF_HG_F - _N_GATES
_P_HG_I = _OFF_HG_I - _N_GATES
_P_HG_G = _OFF_HG_G - _N_GATES
_P_GATE_A = _OFF_GATE_A - _N_GATES
_P_GATE_B = _OFF_GATE_B - _N_GATES
_P_WIDTH = D_IN - _N_GATES

_PROJ_TM, _PROJ_TN = 1024, 1024
_CAST_COL_BLOCKS = 8
_MERGE_TM, _MERGE_TN = 1024, 1024
_OUT_TM = 512
_MLP_TM, _MLP_TF = 1024, 512


def _sigmoid(x):
    return 1.0 / (1.0 + jnp.exp(-x))


def _log_sigmoid(x):
    return jnp.minimum(x, 0.0) - jnp.log1p(jnp.exp(-jnp.abs(x)))


def _split2(x):
    hi = x.astype(BF16)
    lo = (x - hi.astype(F32)).astype(BF16)
    return hi, lo


def _dot01_right(t01, x):
    hi, lo = _split2(x)
    return (jnp.dot(t01, hi, preferred_element_type=F32)
            + jnp.dot(t01, lo, preferred_element_type=F32))


def _dot01_left(x, t01):
    hi, lo = _split2(x)
    return (jnp.dot(hi, t01, preferred_element_type=F32)
            + jnp.dot(lo, t01, preferred_element_type=F32))


def _dot_nt(a, b):
    return lax.dot_general(a, b, (((1,), (1,)), ((), ())), preferred_element_type=F32)


def _dot_tn(a, b):
    return lax.dot_general(a, b, (((0,), (0,)), ((), ())), preferred_element_type=F32)


def _interleaved(gens, together):
    done = object()
    for g0 in range(0, len(gens), together):
        alive = gens[g0:g0 + together]
        while alive:
            alive = [g for g in alive if next(g, done) is not done]
            if alive:
                yield


def _seq_masks(rows, seq_len):
    ri = lax.broadcasted_iota(jnp.int32, (rows, rows), 0)
    ci = lax.broadcasted_iota(jnp.int32, (rows, rows), 1)
    lower = jnp.where(ci <= ri, 1.0, 0.0)
    upper = jnp.where(ri <= ci, 1.0, 0.0)
    if seq_len == rows:
        return jnp.ones((rows, rows), F32), lower, upper
    shift = seq_len.bit_length() - 1
    same = jnp.where((ri >> shift) == (ci >> shift), 1.0, 0.0)
    return same, same * lower, same * upper


def _inproj_kernel(xp_ref, xs_ref, wt_hbm, *refs, n_f32, n_gate_tile, prompt_tiles, cast_steps):
    n_cast = (len(refs) - 9) // 2
    cast_in = refs[:n_cast]
    o_ref, f_ref, g_ref, gr_ref = refs[n_cast:n_cast + 4]
    cast_out = refs[n_cast + 4:2 * n_cast + 4]
    wbuf, wbf_sc, wg_sc, sem, gsem = refs[2 * n_cast + 4:]
    n = pl.program_id(0)
    m = pl.program_id(1)
    tn = wbf_sc.shape[1]

    @pl.when(m < cast_steps)
    def _cast_other_weights():
        for src, dst in zip(cast_in, cast_out):
            dst[...] = src[...].astype(BF16)

    def tile_copy(tile):
        start = pl.multiple_of(tile * tn + jnp.where(tile >= n_gate_tile, _N_GATES, 0), _N_GATES)
        return pltpu.make_async_copy(wt_hbm.at[pl.ds(start, tn), :], wbuf, sem.at[0])

    def gate_copy():
        return pltpu.make_async_copy(wt_hbm.at[pl.ds(_OFF_ML_I, _N_GATES), :],
                                     wg_sc.at[pl.ds(0, _N_GATES), :], gsem.at[0])

    @pl.when(m == 0)
    def _next_weight_tile():
        @pl.when(n == 0)
        def _first():
            tile_copy(0).start()
            wg_sc[_N_GATES:, :] = jnp.zeros((LANES - _N_GATES, wg_sc.shape[1]), F32)
            gate_copy().start()
            gate_copy().wait()

        tile_copy(n).wait()
        wbf_sc[...] = wbuf[...].T.astype(BF16)

        @pl.when(n + 1 < pl.num_programs(0))
        def _prefetch():
            tile_copy(n + 1).start()

    x = jnp.where(m < prompt_tiles, xp_ref[...], xs_ref[...])
    acc = jnp.dot(x, wbf_sc[...], preferred_element_type=F32)
    o_ref[...] = acc.astype(BF16)

    @pl.when(n == n_f32)
    def _f32_outputs():
        f_ref[...] = acc
        wg = wg_sc[...].astype(BF16)
        gates = _dot_nt(x, wg)
        g_ref[...] = gates
        gr_ref[0] = gates.T[0:_N_GATES, :]


def _inproj(xp_b, xs_b, w_t, cast_weights):
    k = xp_b.shape[1]
    tm, tn = _PROJ_TM, _PROJ_TN
    p_tiles = xp_b.shape[0] // tm
    n_m = p_tiles + xs_b.shape[0] // tm
    tokens = n_m * tm
    n_n = _P_WIDTH // tn
    n_f32 = _P_HG_F // tn
    cast_steps = min(n_m, _CAST_COL_BLOCKS)

    def parked(n, m):
        return jnp.where(n < n_f32, 0, jnp.where(n == n_f32, m, n_m - 1))

    def cast_spec(w):
        return pl.BlockSpec((w.shape[0] // n_n, w.shape[1] // cast_steps),
                            lambda n, m: (n, jnp.minimum(m, cast_steps - 1)))

    cast_specs = [cast_spec(w) for w in cast_weights]
    outs = pl.pallas_call(
        functools.partial(_inproj_kernel, n_f32=n_f32, n_gate_tile=_OFF_ML_I // tn,
                          prompt_tiles=p_tiles, cast_steps=cast_steps),
        grid=(n_n, n_m),
        in_specs=[pl.BlockSpec((tm, k), lambda n, m: (jnp.minimum(m, p_tiles - 1), 0)),
                  pl.BlockSpec((tm, k), lambda n, m: (jnp.maximum(m - p_tiles, 0), 0)),
                  pl.BlockSpec(memory_space=pl.ANY),
                  *cast_specs],
        out_specs=[pl.BlockSpec((tm, tn), lambda n, m: (m, n)),
                   pl.BlockSpec((tm, HG_K), lambda n, m: (parked(n, m), 0)),
                   pl.BlockSpec((tm, LANES), lambda n, m: (parked(n, m), 0)),
                   pl.BlockSpec((1, _N_GATES, tm), lambda n, m: (parked(n, m), 0, 0)),
                   *cast_specs],
        out_shape=[jax.ShapeDtypeStruct((tokens, _P_WIDTH), BF16),
                   jax.ShapeDtypeStruct((tokens, HG_K), F32),
                   jax.ShapeDtypeStruct((tokens, LANES), F32),
                   jax.ShapeDtypeStruct((n_m, 8, tm), F32),
                   *[jax.ShapeDtypeStruct(w.shape, BF16) for w in cast_weights]],
        scratch_shapes=[pltpu.VMEM((tn, k), F32),
                        pltpu.VMEM((k, tn), BF16),
                        pltpu.VMEM((LANES, k), F32),
                        pltpu.SemaphoreType.DMA((1,)),
                        pltpu.SemaphoreType.DMA((1,))],
        compiler_params=pltpu.CompilerParams(dimension_semantics=("arbitrary", "arbitrary")),
        name="in_proj",
    )(xp_b, xs_b, w_t, *cast_weights)
    return outs[:4], outs[4:]


def _mlstm_steps(*refs, rows, seq_len, heads, together, zero_init, head0=None):
    if zero_init:
        (q_ref, k_ref, v_ref, og_ref, gc_ref, gr_ref, brow_ref, bcol_ref, ng_ref,
         ba_ref, c_ref, n_ref, mrow_ref, m_sc) = refs
        c0_ref, n0_ref = c_ref, n_ref
    else:
        (q_ref, k_ref, v_ref, og_ref, gc_ref, gr_ref, brow_ref, bcol_ref, ng_ref, c0_ref, n0_ref,
         ba_ref, c_ref, n_ref, mrow_ref) = refs
    nseq = rows // seq_len
    if head0 is None:
        head0 = pl.program_id(1) * heads

    if zero_init:
        @pl.when(pl.program_id(2) == 0)
        def _init():
            c_ref[...] = jnp.zeros(c_ref.shape, F32)
            n_ref[...] = jnp.zeros(n_ref.shape, F32)
            m_sc[...] = jnp.zeros(m_sc.shape, F32)

    same, lower, upper = _seq_masks(rows, seq_len)
    causal = lower > 0.5
    lane = lax.broadcasted_iota(jnp.int32, (rows, LANES), 1)
    sub = lax.broadcasted_iota(jnp.int32, (_N_GATES, rows), 0)

    def sel_lane(x, idx):
        return jnp.sum(jnp.where(lane == idx, x, 0.0), axis=1, keepdims=True)

    def sel_sub(x, idx):
        return jnp.sum(jnp.where(sub == idx, x, 0.0), axis=0, keepdims=True)

    gc_raw = gc_ref[...]
    gc = gc_raw + brow_ref[...]
    lf_cols = _log_sigmoid(gc)
    b_cols = _dot01_right(lower.astype(BF16), lf_cols)
    gr = gr_ref[0] + bcol_ref[...]
    b_rows = _dot01_left(_log_sigmoid(gr), upper.astype(BF16))
    if nseq > 1:
        b_ends = _dot01_right(same.astype(BF16), lf_cols)
        ci = lax.broadcasted_iota(jnp.int32, (rows, rows), 1)
        last = (same * jnp.where((ci & (seq_len - 1)) == seq_len - 1, 1.0, 0.0)).astype(BF16)
        shift = seq_len.bit_length() - 1
        row_seq = lax.broadcasted_iota(jnp.int32, (rows, 1), 0) >> shift

    def head_steps(hh):
        head = head0 + hh
        ks = slice(hh * ML_DK, (hh + 1) * ML_DK)
        vs = slice(hh * ML_DV, (hh + 1) * ML_DV)
        ig_col = sel_lane(gc, head)
        b_col = sel_lane(b_cols, head + ML_HEADS)
        if zero_init:
            m_prev = jnp.broadcast_to(m_sc[hh:hh + 1, 0:1], (rows, 1))
        else:
            m_prev = sel_lane(gc_raw, head + 2 * ML_HEADS)
        ig_row = sel_sub(gr, head)
        b_row = sel_sub(b_rows, head + ML_HEADS)

        logd = jnp.where(causal, (b_col - b_row) + ig_row, -jnp.inf)
        m_t = jnp.maximum(b_col + m_prev, jnp.max(logd, axis=1, keepdims=True))
        d = jnp.exp(logd - m_t)
        w_inter = jnp.exp(b_col + m_prev - m_t)
        yield

        qb = q_ref[:, ks].astype(BF16)
        kb = k_ref[:, ks].astype(BF16)
        vb = v_ref[:, vs].astype(BF16)
        s = _dot_nt(qb, kb) * (d * ML_K_SCALE)
        num = jnp.dot(s.astype(BF16), vb, preferred_element_type=F32)
        den = jnp.sum(s, axis=1, keepdims=True)

        if nseq == 1:
            b_end = b_col[rows - 1:rows, :]
            m_new = m_t[rows - 1:rows, :]
        else:
            b_end = sel_lane(b_ends, head + ML_HEADS)
            m_new = _dot01_right(last, jnp.broadcast_to(m_t, (rows, LANES)))[:, 0:1]
        w_end = jnp.exp(b_end - b_col + ig_col - m_new)
        decay = jnp.exp(b_end + m_prev - m_new)

        qf = qb.astype(F32)
        kw = (w_end * ML_K_SCALE) * kb.astype(F32)
        yield

        if nseq == 1:
            c_prev = c0_ref[0, hh]
            n_prev = n0_ref[0, hh]
            q_c = jnp.dot(qb, c_prev.astype(BF16), preferred_element_type=F32)
            q_n = jnp.sum(qf * n_prev, axis=1, keepdims=True)
            dec = decay[0:1, :]
            c_ref[0, hh] = dec * c_prev + _dot_tn(kw.astype(BF16), vb)
            n_ref[0, hh] = dec * n_prev + jnp.sum(kw, axis=0, keepdims=True)
        else:
            q_c = jnp.zeros((rows, ML_DV), F32)
            q_n = jnp.zeros((rows, 1), F32)
            for j in range(nseq):
                in_seq = row_seq == j
                c_prev = c0_ref[j, hh]
                n_prev = n0_ref[j, hh]
                q_c = jnp.where(
                    in_seq, jnp.dot(qb, c_prev.astype(BF16), preferred_element_type=F32), q_c)
                q_n = jnp.where(in_seq, jnp.sum(qf * n_prev, axis=1, keepdims=True), q_n)
                kw_j = jnp.where(in_seq, kw, 0.0)
                dec = decay[j * seq_len:j * seq_len + 1, :]
                c_ref[j, hh] = dec * c_prev + _dot_tn(kw_j.astype(BF16), vb)
                n_ref[j, hh] = dec * n_prev + jnp.sum(kw_j, axis=0, keepdims=True)
                yield
        if nseq == 1:
            yield

        num = num + w_inter * q_c
        den = den + w_inter * q_n
        h_out = num / jnp.maximum(jnp.abs(den), jnp.exp(-m_t))
        mu =jnp.mean(h_out, axis=1, keepdims=True)
        xc = h_out - mu
        var = jnp.mean(xc * xc, axis=1, keepdims=True)
        hn = xc * lax.rsqrt(var + LN_EPS) * ng_ref[:, vs]
        ba_ref[:, vs] = (hn * _sigmoid(og_ref[:, vs].astype(F32))).astype(ba_ref.dtype)
        mrow_ref[hh] = jnp.broadcast_to(m_t, (rows, LANES))
        if zero_init:
            m_sc[hh:hh + 1, :] = jnp.broadcast_to(m_new, (1, LANES))
        yield

    yield from _interleaved([head_steps(hh) for hh in range(heads)], together)


def _mlstm_parts(proj, gates_col, gates_row, brow, bcol, norm_g, state, *, row0, batch, seq, rows,
                 seq_len, heads, together, index=None):
    tokens = batch * seq
    zero_init = state is None
    nseq = rows // seq_len
    nchunk = seq // seq_len if nseq == 1 else 1
    ngroup = tokens // (rows * nchunk)
    bq, bv = heads * ML_DK, heads * ML_DV
    rb0 = row0 // rows
    per_row_tile = gates_row.shape[2] // rows
    head0 = None
    if index is None:
        index = lambda g, h, c: (g * nchunk + c, h, g)
    else:
        assert heads == ML_HEADS
        head0 = 0
    row = lambda *ids: index(*ids)[0]
    hblk = lambda *ids: index(*ids)[1]
    sblk = lambda *ids: index(*ids)[2]

    in_specs = [
        pl.BlockSpec((rows, bq), lambda *ids: (rb0 + row(*ids), _P_ML_Q // bq + hblk(*ids))),
        pl.BlockSpec((rows, bq), lambda *ids: (rb0 + row(*ids), _P_ML_K // bq + hblk(*ids))),
        pl.BlockSpec((rows, bv), lambda *ids: (rb0 + row(*ids), _P_ML_V // bv + hblk(*ids))),
        pl.BlockSpec((rows, bv), lambda *ids: (rb0 + row(*ids), _P_ML_O // bv + hblk(*ids))),
        pl.BlockSpec((rows, LANES), lambda *ids: (row(*ids), 0)),
        pl.BlockSpec((1, _N_GATES, rows),
                     lambda *ids: (row(*ids) // per_row_tile, 0, row(*ids) % per_row_tile)),
        pl.BlockSpec((1, LANES), lambda *ids: (0, 0)),
        pl.BlockSpec((_N_GATES, 1), lambda *ids: (0, 0)),
        pl.BlockSpec((1, bv), lambda *ids: (0, hblk(*ids))),
    ]
    args = [proj, proj, proj, proj, gates_col, gates_row, brow, bcol, norm_g]
    state_specs = [
        pl.BlockSpec((nseq, heads, ML_DK, ML_DV), lambda *ids: (sblk(*ids), hblk(*ids), 0, 0)),
        pl.BlockSpec((nseq, heads, 1, ML_DK), lambda *ids: (sblk(*ids), hblk(*ids), 0, 0)),
    ]
    scratch = []
    if zero_init:
        scratch = [pltpu.VMEM((SUBLANES, LANES), F32)]
    else:
        in_specs += state_specs
        args += [state[0], state[1]]
    out_specs = [
        pl.BlockSpec((rows, bv), lambda *ids: (row(*ids), hblk(*ids))),
        *state_specs,
        pl.BlockSpec((heads, rows, LANES), lambda *ids: (hblk(*ids), row(*ids), 0)),
    ]
    out_shape = [
        jax.ShapeDtypeStruct((tokens, ML_V), BF16),
        jax.ShapeDtypeStruct((batch, ML_HEADS, ML_DK, ML_DV), F32),
        jax.ShapeDtypeStruct((batch, ML_HEADS, 1, ML_DK), F32),
        jax.ShapeDtypeStruct((ML_HEADS, tokens, LANES), F32),
    ]
    return dict(
        steps=functools.partial(_mlstm_steps, rows=rows, seq_len=seq_len, heads=heads,
                                together=together, zero_init=zero_init, head0=head0),
        n_steps=heads // together * (3 + nseq),
        grid=(ngroup, ML_HEADS // heads, nchunk),
        in_specs=in_specs, args=args, out_specs=out_specs, out_shape=out_shape, scratch=scratch)


def _run_parts(parts, name):
    n_in = [len(p["in_specs"]) for p in parts]
    n_out = [len(p["out_specs"]) for p in parts]
    n_scr = [len(p["scratch"]) for p in parts]

    def kernel(*refs):
        ins = refs[:sum(n_in)]
        outs = refs[sum(n_in):sum(n_in) + sum(n_out)]
        scr = refs[sum(n_in) + sum(n_out):]
        gens = []
        i = o = s = 0
        for p, ni, no, ns in zip(parts, n_in, n_out, n_scr):
            gens.append(p["steps"](*ins[i:i + ni], *outs[o:o + no], *scr[s:s + ns]))
            i, o, s = i + ni, o + no, s + ns
        order = sorted(((j + 0.5) / p["n_steps"], k) for k, p in enumerate(parts)
                       for j in range(p["n_steps"]))
        for _, k in order:
            next(gens[k], None)
        for gen in gens:
            for _ in gen:
                pass

    outs = pl.pallas_call(
        kernel,
        grid=parts[0]["grid"],
        in_specs=[s for p in parts for s in p["in_specs"]],
        out_specs=[s for p in parts for s in p["out_specs"]],
        out_shape=[s for p in parts for s in p["out_shape"]],
        scratch_shapes=[s for p in parts for s in p["scratch"]],
        compiler_params=pltpu.CompilerParams(
            dimension_semantics=("parallel", "parallel", "arbitrary")),
        name=name,
    )(*[a for p in parts for a in p["args"]])
    split, o = [], 0
    for no in n_out:
        split.append(outs[o:o + no])
        o += no
    return split


def _hgrn_level_ids(rows, seq_len, direct, levels):
    t = np.arange(rows)[:, None]
    s = np.arange(rows)[None, :]
    ids = np.full((rows, rows), -1, np.int32)
    count = ((t // direct == s // direct) & (s <= t)).astype(np.int32)
    for idx, (block, sub_size) in enumerate(levels):
        owned = (t // block == s // block) & ((s % block) // sub_size < (t % block) // sub_size)
        ids[owned] = idx
        count += owned
    wanted = (t // seq_len == s // seq_len) & (s <= t)
    assert np.array_equal(count, wanted.astype(np.int32)), (rows, seq_len, direct, levels)
    return ids


_INTRA_YIELD_BLOCKS = 32


def _hgrn_intra_yields(rows, direct, levels):
    return 2 + len(levels) + 2 * (max(rows // direct // _INTRA_YIELD_BLOCKS, 1) - 1)


def _hgrn_intra(q, kin, g2, gk2, level_masks, rows, direct, levels):
    ngroups = max(rows // LANES, 1)
    rowi = lax.broadcasted_iota(jnp.int32, (direct, LANES), 0)
    lanei = lax.broadcasted_iota(jnp.int32, (direct, LANES), 1)
    keep = [jnp.where(lanei == s, rowi, -1) >= s for s in range(direct)]
    zero_group = jnp.zeros((direct, LANES), F32)
    prods = []
    for blk in range(rows // direct):
        r0 = blk * direct
        qb = q[r0:r0 + direct]
        gb = g2[r0:r0 + direct]
        gkb = gk2[r0:r0 + direct]
        for s in range(direct):
            prods.append(qb * jnp.exp2(gb - gkb[s:s + 1, :]))
        if (blk + 1) % _INTRA_YIELD_BLOCKS == 0 and blk + 1 < rows // direct:
            yield
    sums = jnp.dot(jnp.concatenate(prods, axis=0).astype(BF16), jnp.ones((HG_DK, LANES), BF16),
                   preferred_element_type=F32)
    yield
    panels = []
    for blk in range(rows // direct):
        r0 = blk * direct
        ag = zero_group
        for s in range(direct):
            p0 = (blk * direct + s) * direct
            ag = jnp.where(keep[s], sums[p0:p0 + direct], ag)
        lane0 = r0 % LANES
        if lane0:
            ag = pltpu.roll(ag, lane0, axis=1)
        grp = r0 // LANES
        pieces = [zero_group] * grp + [ag] + [zero_group] * (ngroups - grp - 1)
        panels.append(pieces[0] if ngroups == 1 else jnp.concatenate(pieces, axis=1))
        if (blk + 1) % _INTRA_YIELD_BLOCKS == 0 and blk + 1 < rows // direct:
            yield
    a = panels[0] if len(panels) == 1 else jnp.concatenate(panels, axis=0)
    if rows < LANES:
        a = a[:, :rows]
    yield

    if levels:
        rowid = lax.broadcasted_iota(jnp.int32, (rows, 1), 0)
    for idx, (block, sub_size) in enumerate(levels):
        sshift = sub_size.bit_length() - 1
        row_sub = (rowid & (block - 1)) >> sshift
        q_parts, k_parts = [], []
        for j in range(1, block // sub_size):
            refs = []
            for b0 in range(0, rows, block):
                r = b0 + j * sub_size - 1
                refs.append(jnp.broadcast_to(g2[r:r + 1, :], (block, HG_DK)))
            g_ref = refs[0] if len(refs) == 1 else jnp.concatenate(refs, axis=0)
            e = jnp.exp2(-jnp.abs(g2 - g_ref))
            q_parts.append(jnp.where(row_sub == j, q * e, 0.0).astype(BF16))
            k_parts.append((kin * e).astype(BF16))
        qcat = q_parts[0] if len(q_parts) == 1 else jnp.concatenate(q_parts, axis=1)
        kcat = k_parts[0] if len(k_parts) == 1 else jnp.concatenate(k_parts, axis=1)
        a = jnp.where(level_masks[idx], _dot_nt(qcat, kcat), a)
        yield
    return a


def _hgrn_steps(*refs, layer, rows, seq_len, heads, together, direct, levels, zero_init):
    if zero_init:
        q_ref, f_ref, i_ref, g_ref, lbl_ref, ng_ref, lvl_ref, bb_ref, s_ref = refs
        s0_ref = s_ref
    else:
        q_ref, f_ref, i_ref, g_ref, lbl_ref, ng_ref, lvl_ref, s0_ref, bb_ref, s_ref = refs
    nseq = rows // seq_len

    if zero_init:
        @pl.when(pl.program_id(2) == 0)
        def _init():
            s_ref[...] = jnp.zeros(s_ref.shape, F32)

    same, lower, _ = _seq_masks(rows, seq_len)
    lower_b = lower.astype(BF16)
    same_b = same.astype(BF16)
    level_ids = lvl_ref[...]
    level_masks = [level_ids == idx for idx in range(len(levels))]
    if nseq > 1:
        shift = seq_len.bit_length() - 1
        row_seq = lax.broadcasted_iota(jnp.int32, (rows, 1), 0) >> shift

    def decay_cols(row):
        col = jnp.broadcast_to(jnp.exp2(row), (HG_DK, HG_DK)).T
        return jnp.concatenate([col] * (HG_DV // HG_DK), axis=1)

    def head_steps(hh):
        ks = slice(hh * HG_DK, (hh + 1) * HG_DK)
        vs = slice(hh * HG_DV, (hh + 1) * HG_DV)
        lg = lbl_ref[:, ks]
        ex = jnp.exp(lg - jnp.max(lg, axis=0, keepdims=True))
        lb = (jnp.sum(ex[0:layer + 1, :], axis=0, keepdims=True)
              / jnp.sum(ex, axis=0, keepdims=True))

        f = lb + (1.0 - lb) * _sigmoid(f_ref[:, ks])
        kin = 1.0 - f
        lf2 = jnp.log2(f)
        g2 = _dot01_right(lower_b, lf2)
        gk2 = g2 - jnp.log2(kin)
        yield
        q = q_ref[:, ks].astype(F32)
        if nseq == 1:
            g_end = jnp.broadcast_to(g2[rows - 1:rows, :], (rows, HG_DK))
        else:
            g_end = _dot01_right(same_b, lf2)
        qg = (q * jnp.exp2(g2)).astype(BF16)
        kg = kin * jnp.exp2(g_end - g2)
        ib = i_ref[:, vs].astype(BF16)
        yield

        a = yield from _hgrn_intra(q, kin, g2, gk2, level_masks, rows, direct, levels)
        o = jnp.dot(a.astype(BF16), ib, preferred_element_type=F32)

        if nseq == 1:
            s_prev = s0_ref[0, hh]
            o = o + jnp.dot(qg, s_prev.astype(BF16), preferred_element_type=F32)
            s_ref[0, hh] = decay_cols(g_end[0:1, :]) * s_prev + _dot_tn(kg.astype(BF16), ib)
            yield
        else:
            for j in range(nseq):
                in_seq = row_seq == j
                s_prev = s0_ref[j, hh]
                o_j = jnp.dot(qg, s_prev.astype(BF16), preferred_element_type=F32)
                o = o + jnp.where(in_seq, o_j, 0.0)
                kg_j = jnp.where(in_seq, kg, 0.0).astype(BF16)
                s_ref[j, hh] = (decay_cols(g_end[j * seq_len:j * seq_len + 1, :]) * s_prev
                                + _dot_tn(kg_j, ib))
                yield

        o = o * lax.rsqrt(jnp.mean(o * o, axis=1, keepdims=True) + LN_EPS)
        gate = g_ref[:, vs].astype(F32)
        bb_ref[:, vs] = (o * ng_ref[:, vs] * (gate * _sigmoid(gate))).astype(bb_ref.dtype)
        yield

    yield from _interleaved([head_steps(hh) for hh in range(heads)], together)


def _hgrn_parts(proj, hg_f, lb_logits, norm_g, s0, *, layer, row0, batch, seq, rows, seq_len, heads,
                together, direct, levels, index=None):
    tokens = batch * seq
    zero_init = s0 is None
    nseq = rows // seq_len
    nchunk = seq // seq_len if nseq == 1 else 1
    ngroup = tokens // (rows * nchunk)
    rb0 = row0 // rows
    if index is None:
        index = lambda g, h, c: (g * nchunk + c, h, g)
    row = lambda *ids: index(*ids)[0]
    hblk = lambda *ids: index(*ids)[1]
    sblk = lambda *ids: index(*ids)[2]

    wk, wv = heads * HG_DK, heads * HG_DV
    in_specs = [
        pl.BlockSpec((rows, wk), lambda *ids: (rb0 + row(*ids), _P_HG_Q // wk + hblk(*ids))),
        pl.BlockSpec((rows, wk), lambda *ids: (rb0 + row(*ids), hblk(*ids))),
        pl.BlockSpec((rows, wv), lambda *ids: (rb0 + row(*ids), _P_HG_I // wv + hblk(*ids))),
        pl.BlockSpec((rows, wv), lambda *ids: (rb0 + row(*ids), _P_HG_G // wv + hblk(*ids))),
        pl.BlockSpec((DEPTH + 1, wk), lambda *ids: (0, hblk(*ids))),
        pl.BlockSpec((1, wv), lambda *ids: (0, hblk(*ids))),
        pl.BlockSpec((rows, rows), lambda *ids: (0, 0)),
    ]
    level_ids = jnp.asarray(_hgrn_level_ids(rows, seq_len, direct, levels))
    args = [proj, hg_f, proj, proj, lb_logits, norm_g, level_ids]
    state_spec = pl.BlockSpec((nseq, heads, HG_DK, HG_DV),
                              lambda *ids: (sblk(*ids), hblk(*ids), 0, 0))
    if not zero_init:
        in_specs.append(state_spec)
        args.append(s0)
    return dict(
        steps=functools.partial(_hgrn_steps, layer=layer, rows=rows, seq_len=seq_len, heads=heads,
                                together=together, direct=direct, levels=levels,
                                zero_init=zero_init),
        n_steps=heads // together * (3 + _hgrn_intra_yields(rows, direct, levels) + nseq),
        grid=(ngroup, HG_HEADS // heads, nchunk),
        in_specs=in_specs, args=args,
        out_specs=[pl.BlockSpec((rows, wv), lambda *ids: (row(*ids), hblk(*ids))), state_spec],
        out_shape=[jax.ShapeDtypeStruct((tokens, HG_V), BF16),
                   jax.ShapeDtypeStruct((batch, HG_HEADS, HG_DK, HG_DV), F32)],
        scratch=[])


def _merge_kernel(ba_ref, wa_ref, bb_ref, wb_ref, ga_ref, gb_ref, o_ref):
    ya = jnp.dot(ba_ref[...], wa_ref[...], preferred_element_type=F32)
    yb = jnp.dot(bb_ref[...], wb_ref[...], preferred_element_type=F32)
    ga = _sigmoid(ga_ref[...].astype(F32))
    gb = _sigmoid(gb_ref[...].astype(F32))
    o_ref[...] = (ga * ya + gb * yb).astype(o_ref.dtype)


def _merge(branch_a, w_a, branch_b, w_b, proj, row0):
    m = branch_a.shape[0]
    tm, tn = _MERGE_TM, _MERGE_TN
    rb0 = row0 // tm
    ga_blk = _P_GATE_A // tn
    gb_blk = _P_GATE_B // tn
    return pl.pallas_call(
        _merge_kernel,
        grid=(D_MODEL // tn, m // tm),
        in_specs=[pl.BlockSpec((tm, ML_V), lambda j, i: (i, 0)),
                  pl.BlockSpec((ML_V, tn), lambda j, i: (0, j)),
                  pl.BlockSpec((tm, HG_V), lambda j, i: (i, 0)),
                  pl.BlockSpec((HG_V, tn), lambda j, i: (0, j)),
                  pl.BlockSpec((tm, tn), lambda j, i: (rb0 + i, ga_blk + j)),
                  pl.BlockSpec((tm, tn), lambda j, i: (rb0 + i, gb_blk + j))],
        out_specs=pl.BlockSpec((tm, tn), lambda j, i: (i, j)),
        out_shape=jax.ShapeDtypeStruct((m, D_MODEL), BF16),
        compiler_params=pltpu.CompilerParams(dimension_semantics=("parallel", "arbitrary")),
        name="merge",
    )(branch_a, w_a, branch_b, w_b, proj, proj)


def _layernorm_rows(z, g, b):
    mu = jnp.mean(z, axis=1, keepdims=True)
    zc = z - mu
    var = jnp.mean(zc * zc, axis=1, keepdims=True)
    return zc * lax.rsqrt(var + LN_EPS) * g + b


def _outproj_kernel(mg_ref, w_ref, x_ref, g_ref, b_ref, x1_ref):
    mix = jnp.dot(mg_ref[...], w_ref[...], preferred_element_type=F32)
    x1_ref[...] = _layernorm_rows(DEEPNORM_ALPHA * x_ref[...] + mix, g_ref[...], b_ref[...])


def _outproj(merged, w_out, x, ln_g, ln_b):
    tm = _OUT_TM
    m = x.shape[0]
    row = lambda i: (i, 0)
    const = lambda i: (0, 0)
    return pl.pallas_call(
        _outproj_kernel,
        grid=(m // tm,),
        in_specs=[pl.BlockSpec((tm, D_MODEL), row),
                  pl.BlockSpec((D_MODEL, D_MODEL), const),
                  pl.BlockSpec((tm, D_MODEL), row),
                  pl.BlockSpec((1, D_MODEL), const),
                  pl.BlockSpec((1, D_MODEL), const)],
        out_specs=pl.BlockSpec((tm, D_MODEL), row),
        out_shape=jax.ShapeDtypeStruct((m, D_MODEL), F32),
        compiler_params=pltpu.CompilerParams(dimension_semantics=("parallel",)),
        name="out_proj_ln",
    )(merged, w_out, x, ln_g, ln_b)


def _mlp_kernel(wu_ref, wd_ref, x1_ref, g_ref, b_ref, y_ref, x1b_sc):
    f = pl.program_id(1)

    @pl.when(f == 0)
    def _init():
        y_ref[...] = jnp.zeros(y_ref.shape, F32)
        x1b_sc[...] = x1_ref[...].astype(BF16)

    hid = jnp.maximum(jnp.dot(x1b_sc[...], wu_ref[...], preferred_element_type=F32), 0.0)
    hid = (hid * hid).astype(BF16)
    y_ref[...] += jnp.dot(hid, wd_ref[...], preferred_element_type=F32)

    @pl.when(f == pl.num_programs(1) - 1)
    def _finish():
        z = DEEPNORM_ALPHA * x1_ref[...] + y_ref[...]
        y_ref[...] = _layernorm_rows(z, g_ref[...], b_ref[...])


def _mlp(w_up, w_down, x1, ln_g, ln_b):
    tm, tf = _MLP_TM, _MLP_TF
    tokens = x1.shape[0]
    return pl.pallas_call(
        _mlp_kernel,
        grid=(tokens // tm, D_FF // tf),
        in_specs=[pl.BlockSpec((D_MODEL, tf), lambda i, f: (0, f)),
                  pl.BlockSpec((tf, D_MODEL), lambda i, f: (f, 0)),
                  pl.BlockSpec((tm, D_MODEL), lambda i, f: (i, 0)),
                  pl.BlockSpec((1, D_MODEL), lambda i, f: (0, 0)),
                  pl.BlockSpec((1, D_MODEL), lambda i, f: (0, 0))],
        out_specs=pl.BlockSpec((tm, D_MODEL), lambda i, f: (i, 0)),
        out_shape=jax.ShapeDtypeStruct((tokens, D_MODEL), F32),
        scratch_shapes=[pltpu.VMEM((tm, D_MODEL), BF16)],
        compiler_params=pltpu.CompilerParams(dimension_semantics=("parallel", "arbitrary")),
        name="mlp_ln",
    )(w_up, w_down, x1, ln_g, ln_b)


_PROMPT_ML = dict(rows=256, seq_len=256, heads=4, together=1)
_PROMPT_HG = dict(rows=256, seq_len=256, heads=4, together=2, direct=8,
                  levels=((16, 8), (32, 16), (64, 32), (128, 64), (256, 128)))
_SAMPLE_ML = dict(rows=16, seq_len=8, heads=4, together=4)
_SAMPLE_HG = dict(rows=32, seq_len=8, heads=8, together=1, direct=8, levels=())


def kernel(x_prompt, x_sample, state_mlstm_C, state_mlstm_n, state_mlstm_m, state_hgrn_S,
           hg_lb_logits, w_in, b_ig, b_fg, ml_norm_g, hg_norm_g, w_branch_a, w_branch_b, w_out,
           ln1_g, ln1_b, w_up, w_down, ln2_g, ln2_b):
    batch_p, seq_p, _ = x_prompt.shape
    batch_s, seq_s, _ = x_sample.shape
    tok_p, tok_s = batch_p * seq_p, batch_s * seq_s
    total = tok_p + tok_s
    group_p = dict(row0=0, batch=batch_p, seq=seq_p)
    group_s = dict(row0=tok_p, batch=batch_s, seq=seq_s)
    xp = x_prompt.reshape(tok_p, D_MODEL)
    xs = x_sample.reshape(tok_s, D_MODEL)
    lb_logits = hg_lb_logits.astype(F32)
    states_p, states_s = [], []
    for l in range(DEPTH):
        (proj, hg_f, gates, gates_gm), (w_a_b, w_b_b, w_out_b, w_up_b, w_down_b) = _inproj(
            xp.astype(BF16), xs.astype(BF16), jnp.swapaxes(w_in[l], 0, 1),
            (w_branch_a[l], w_branch_b[l], w_out[l], w_up[l], w_down[l]))

        gate_bias = jnp.concatenate([b_ig[l], b_fg[l]]).astype(F32)
        brow = jnp.zeros((1, LANES), F32).at[0, :_N_GATES].set(gate_bias)
        bcol = gate_bias.reshape(_N_GATES, 1)
        ml_g = ml_norm_g[l].reshape(1, ML_V).astype(F32)
        hg_g = hg_norm_g[l].reshape(1, HG_V).astype(F32)

        m_rows = jnp.repeat(state_mlstm_m[l].astype(F32), seq_s, axis=0)
        gates_s = gates[tok_p:].at[:, _N_GATES:_N_GATES + ML_HEADS].set(m_rows)
        rows_s = _SAMPLE_ML["rows"]
        gm_s = gates_gm[tok_p // _PROJ_TM:]
        gm_s = gm_s.reshape(-1, _N_GATES, _PROJ_TM // rows_s, rows_s).transpose(0, 2, 1, 3)
        gm_s = gm_s.reshape(tok_s // rows_s, _N_GATES, rows_s)
        state_ml = (state_mlstm_C[l].astype(F32),
                    state_mlstm_n[l].astype(F32).reshape(batch_s, ML_HEADS, 1, ML_DK))
        ln1 = (ln1_g[l].reshape(1, D_MODEL), ln1_b[l].reshape(1, D_MODEL))
        ln2 = (ln2_g[l].reshape(1, D_MODEL), ln2_b[l].reshape(1, D_MODEL))

        ml_p = _mlstm_parts(proj, gates, gates_gm, brow, bcol, ml_g, None, **group_p, **_PROMPT_ML)
        hg_p = _hgrn_parts(proj, hg_f, lb_logits, hg_g, None, layer=l, **group_p, **_PROMPT_HG)

        def guest(parts_fn, host, **kwargs):
            grid = host["grid"]
            assert math.prod(grid) * kwargs["rows"] == tok_s, (grid, kwargs["rows"])
            flat = lambda g, h, c: (g * grid[1] + h) * grid[2] + c
            return parts_fn(index=lambda g, h, c: (flat(g, h, c), 0, flat(g, h, c)), **kwargs)

        ml_s = guest(functools.partial(_mlstm_parts, proj, gates_s, gm_s, brow, bcol, ml_g, state_ml,
                                       **group_s), hg_p, **_SAMPLE_ML)
        hg_s = guest(functools.partial(_hgrn_parts, proj, hg_f, lb_logits, hg_g,
                                       state_hgrn_S[l].astype(F32), layer=l, **group_s),
                     ml_p, **_SAMPLE_HG)
        (bb_p, s_p), (ba_s, c_s, n_s, m_all_s) = _run_parts([hg_p, ml_s], "hgrn2_prompt_mlstm_sample")
        (ba_p, c_p, n_p, m_all_p), (bb_s, s_s) = _run_parts([ml_p, hg_s], "mlstm_prompt_hgrn2_sample")

        merged_p = _merge(ba_p, w_a_b, bb_p, w_b_b, proj, 0)
        merged_s = _merge(ba_s, w_a_b, bb_s, w_b_b, proj, tok_p)
        x1_p = _outproj(merged_p, w_out_b, xp, *ln1)
        x1_s = _outproj(merged_s, w_out_b, xs, *ln1)
        xp = _mlp(w_up_b, w_down_b, x1_p, *ln2)
        xs = _mlp(w_up_b, w_down_b, x1_s, *ln2)

        states_p.append((c_p, n_p.reshape(batch_p, ML_HEADS, ML_DK),
                         m_all_p[:, seq_p - 1::seq_p, 0].T, s_p))
        states_s.append((c_s, n_s.reshape(batch_s, ML_HEADS, ML_DK),
                         m_all_s[:, seq_s - 1::seq_s, 0].T, s_s))
    stack = lambda states, k: jnp.stack([s[k] for s in states])
    return (xp.reshape(batch_p, seq_p, D_MODEL), xs.reshape(batch_s, seq_s, D_MODEL),
            stack(states_p, 0), stack(states_p, 1), stack(states_p, 2), stack(states_p, 3),
            stack(states_s, 0), stack(states_s, 1), stack(states_s, 2), stack(states_s, 3))
```

```python
import functools
import math

import numpy as np
import jax
import jax.numpy as jnp
from jax import lax
from jax.experimental import pallas as pl
from jax.experimental.pallas import tpu as pltpu

F32 = jnp.float32
BF16 = jnp.bfloat16

D_MODEL = 2048
DEPTH = 1
ML_HEADS, ML_DK, ML_DV = 4, 256, 512
HG_HEADS, HG_DK, HG_DV = 8, 128, 256
ML_QK = ML_HEADS * ML_DK
ML_V = ML_HEADS * ML_DV
HG_K = HG_HEADS * HG_DK
HG_V = HG_HEADS * HG_DV
D_FF = 4 * D_MODEL
LN_EPS = 1e-5
DEEPNORM_ALPHA = (2.0 * DEPTH) ** 0.25
ML_K_SCALE = ML_DK ** -0.5
LANES = 128
SUBLANES = 8

_OFF_ML_Q = 0
_OFF_ML_K = _OFF_ML_Q + ML_QK
_OFF_ML_V = _OFF_ML_K + ML_QK
_OFF_ML_I = _OFF_ML_V + ML_V
_OFF_ML_F = _OFF_ML_I + ML_HEADS
_OFF_ML_O = _OFF_ML_F + ML_HEADS
_OFF_HG_Q = _OFF_ML_O + ML_V
_OFF_HG_F = _OFF_HG_Q + HG_K
_OFF_HG_I = _OFF_HG_F + HG_K
_OFF_HG_G = _OFF_HG_I + HG_V
_OFF_GATE_A = _OFF_HG_G + HG_V
_OFF_GATE_B = _OFF_GATE_A + D_MODEL
D_IN = _OFF_GATE_B + D_MODEL

_N_GATES = 2 * ML_HEADS
_P_ML_Q = _OFF_ML_Q
_P_ML_K = _OFF_ML_K
_P_ML_V = _OFF_ML_V
_P_ML_O = _OFF_ML_O - _N_GATES
_P_HG_Q = _OFF_HG_Q - _N_GATES
_P_HG_F = _OFF_HG_F - _N_GATES
_P_HG_I = _OFF_HG_I - _N_GATES
_P_HG_G = _OFF_HG_G - _N_GATES
_P_GATE_A = _OFF_GATE_A - _N_GATES
_P_GATE_B = _OFF_GATE_B - _N_GATES
_P_WIDTH = D_IN - _N_GATES

_PROJ_TM, _PROJ_TN = 1024, 1024
_CAST_COL_BLOCKS = 8
_MERGE_TM, _MERGE_TN = 1024, 1024
_OUT_TM = 512
_MLP_TM, _MLP_TF = 1024, 512


def _sigmoid(x):
    return 1.0 / (1.0 + jnp.exp(-x))


def _log_sigmoid(x):
    return jnp.minimum(x, 0.0) - jnp.log1p(jnp.exp(-jnp.abs(x)))


def _split2(x):
    hi = x.astype(BF16)
    lo = (x - hi.astype(F32)).astype(BF16)
    return hi, lo


def _dot01_right(t01, x):
    hi, lo = _split2(x)
    return (jnp.dot(t01, hi, preferred_element_type=F32)
            + jnp.dot(t01, lo, preferred_element_type=F32))


def _dot01_left(x, t01):
    hi, lo = _split2(x)
    return (jnp.dot(hi, t01, preferred_element_type=F32)
            + jnp.dot(lo, t01, preferred_element_type=F32))


def _dot_nt(a, b):
    return lax.dot_general(a, b, (((1,), (1,)), ((), ())), preferred_element_type=F32)


def _dot_tn(a, b):
    return lax.dot_general(a, b, (((0,), (0,)), ((), ())), preferred_element_type=F32)


def _interleaved(gens, together):
    done = object()
    for g0 in range(0, len(gens), together):
        alive = gens[g0:g0 + together]
        while alive:
            alive = [g for g in alive if next(g, done) is not done]
            if alive:
                yield


def _seq_masks(rows, seq_len):
    ri = lax.broadcasted_iota(jnp.int32, (rows, rows), 0)
    ci = lax.broadcasted_iota(jnp.int32, (rows, rows), 1)
    lower = jnp.where(ci <= ri, 1.0, 0.0)
    upper = jnp.where(ri <= ci, 1.0, 0.0)
    if seq_len == rows:
        return jnp.ones((rows, rows), F32), lower, upper
    shift = seq_len.bit_length() - 1
    same = jnp.where((ri >> shift) == (ci >> shift), 1.0, 0.0)
    return same, same * lower, same * upper


def _inproj_kernel(xp_ref, xs_ref, wt_hbm, *refs, n_f32, n_gate_tile, prompt_tiles, cast_steps):
    n_cast = (len(refs) - 9) // 2
    cast_in = refs[:n_cast]
    o_ref, f_ref, g_ref, gr_ref = refs[n_cast:n_cast + 4]
    cast_out = refs[n_cast + 4:2 * n_cast + 4]
    wbuf, wbf_sc, wg_sc, sem, gsem = refs[2 * n_cast + 4:]
    n = pl.program_id(0)
    m = pl.program_id(1)
    tn = wbf_sc.shape[1]

    @pl.when(m < cast_steps)
    def _cast_other_weights():
        for src, dst in zip(cast_in, cast_out):
            dst[...] = src[...].astype(BF16)

    def tile_copy(tile):
        start = pl.multiple_of(tile * tn + jnp.where(tile >= n_gate_tile, _N_GATES, 0), _N_GATES)
        return pltpu.make_async_copy(wt_hbm.at[pl.ds(start, tn), :], wbuf, sem.at[0])

    def gate_copy():
        return pltpu.make_async_copy(wt_hbm.at[pl.ds(_OFF_ML_I, _N_GATES), :],
                                     wg_sc.at[pl.ds(0, _N_GATES), :], gsem.at[0])

    @pl.when(m == 0)
    def _next_weight_tile():
        @pl.when(n == 0)
        def _first():
            tile_copy(0).start()
            wg_sc[_N_GATES:, :] = jnp.zeros((LANES - _N_GATES, wg_sc.shape[1]), F32)
            gate_copy().start()
            gate_copy().wait()

        tile_copy(n).wait()
        wbf_sc[...] = wbuf[...].T.astype(BF16)

        @pl.when(n + 1 < pl.num_programs(0))
        def _prefetch():
            tile_copy(n + 1).start()

    x = jnp.where(m < prompt_tiles, xp_ref[...].astype(BF16), xs_ref[...])
    acc = jnp.dot(x, wbf_sc[...], preferred_element_type=F32)
    o_ref[...] = acc.astype(BF16)

    @pl.when(n == n_f32)
    def _f32_outputs():
        f_ref[...] = acc
        wg = wg_sc[...].astype(BF16)
        gates = _dot_nt(x, wg)
        g_ref[...] = gates
        gr_ref[0] = gates.T[0:_N_GATES, :]


def _inproj(xp_b, xs_b, w_t, cast_weights):
    k = xp_b.shape[1]
    tm, tn = _PROJ_TM, _PROJ_TN
    p_tiles = xp_b.shape[0] // tm
    n_m = p_tiles + xs_b.shape[0] // tm
    tokens = n_m * tm
    n_n = _P_WIDTH // tn
    n_f32 = _P_HG_F // tn
    cast_steps = min(n_m, _CAST_COL_BLOCKS)

    def parked(n, m):
        return jnp.where(n < n_f32, 0, jnp.where(n == n_f32, m, n_m - 1))

    def cast_spec(w):
        return pl.BlockSpec((w.shape[0] // n_n, w.shape[1] // cast_steps),
                            lambda n, m: (n, jnp.minimum(m, cast_steps - 1)))

    cast_specs = [cast_spec(w) for w in cast_weights]
    outs = pl.pallas_call(
        functools.partial(_inproj_kernel, n_f32=n_f32, n_gate_tile=_OFF_ML_I // tn,
                          prompt_tiles=p_tiles, cast_steps=cast_steps),
        grid=(n_n, n_m),
        in_specs=[pl.BlockSpec((tm, k), lambda n, m: (jnp.minimum(m, p_tiles - 1), 0)),
                  pl.BlockSpec((tm, k), lambda n, m: (jnp.maximum(m - p_tiles, 0), 0),
                               pipeline_mode=pl.Buffered(1)),
                  pl.BlockSpec(memory_space=pl.ANY),
                  *cast_specs],
        out_specs=[pl.BlockSpec((tm, tn), lambda n, m: (m, n)),
                   pl.BlockSpec((tm, HG_K), lambda n, m: (parked(n, m), 0)),
                   pl.BlockSpec((tm, LANES), lambda n, m: (parked(n, m), 0)),
                   pl.BlockSpec((1, _N_GATES, tm), lambda n, m: (parked(n, m), 0, 0)),
                   *cast_specs],
        out_shape=[jax.ShapeDtypeStruct((tokens, _P_WIDTH), BF16),
                   jax.ShapeDtypeStruct((tokens, HG_K), F32),
                   jax.ShapeDtypeStruct((tokens, LANES), F32),
                   jax.ShapeDtypeStruct((n_m, 8, tm), F32),
                   *[jax.ShapeDtypeStruct(w.shape, BF16) for w in cast_weights]],
        scratch_shapes=[pltpu.VMEM((tn, k), F32),
                        pltpu.VMEM((k, tn), BF16),
                        pltpu.VMEM((LANES, k), F32),
                        pltpu.SemaphoreType.DMA((1,)),
                        pltpu.SemaphoreType.DMA((1,))],
        compiler_params=pltpu.CompilerParams(dimension_semantics=("arbitrary", "arbitrary")),
        name="in_proj",
    )(xp_b, xs_b, w_t, *cast_weights)
    return outs[:4], outs[4:]


def _mlstm_steps(*refs, rows, seq_len, heads, together, zero_init, head0=None):
    if zero_init:
        (q_ref, k_ref, v_ref, og_ref, gc_ref, gr_ref, brow_ref, bcol_ref, ng_ref,
         ba_ref, c_ref, n_ref, mrow_ref, m_sc) = refs
        c0_ref, n0_ref = c_ref, n_ref
    else:
        (q_ref, k_ref, v_ref, og_ref, gc_ref, gr_ref, brow_ref, bcol_ref, ng_ref, c0_ref, n0_ref,
         ba_ref, c_ref, n_ref, mrow_ref) = refs
    nseq = rows // seq_len
    if head0 is None:
        head0 = pl.program_id(1) * heads

    if zero_init:
        @pl.when(pl.program_id(2) == 0)
        def _init():
            c_ref[...] = jnp.zeros(c_ref.shape, F32)
            n_ref[...] = jnp.zeros(n_ref.shape, F32)
            m_sc[...] = jnp.zeros(m_sc.shape, F32)

    same, lower, upper = _seq_masks(rows, seq_len)
    causal = lower > 0.5
    lane = lax.broadcasted_iota(jnp.int32, (rows, LANES), 1)
    sub = lax.broadcasted_iota(jnp.int32, (_N_GATES, rows), 0)

    def sel_lane(x, idx):
        return jnp.sum(jnp.where(lane == idx, x, 0.0), axis=1, keepdims=True)

    def sel_sub(x, idx):
        return jnp.sum(jnp.where(sub == idx, x, 0.0), axis=0, keepdims=True)

    gc_raw = gc_ref[...]
    gc = gc_raw + brow_ref[...]
    lf_cols = _log_sigmoid(gc)
    b_cols = _dot01_right(lower.astype(BF16), lf_cols)
    gr = gr_ref[0] + bcol_ref[...]
    b_rows = _dot01_left(_log_sigmoid(gr), upper.astype(BF16))
    if nseq > 1:
        b_ends = _dot01_right(same.astype(BF16), lf_cols)
        ci = lax.broadcasted_iota(jnp.int32, (rows, rows), 1)
        last = (same * jnp.where((ci & (seq_len - 1)) == seq_len - 1, 1.0, 0.0)).astype(BF16)
        shift = seq_len.bit_length() - 1
        row_seq = lax.broadcasted_iota(jnp.int32, (rows, 1), 0) >> shift

    def head_steps(hh):
        head = head0 + hh
        ks = slice(hh * ML_DK, (hh + 1) * ML_DK)
        vs = slice(hh * ML_DV, (hh + 1) * ML_DV)
        ig_col = sel_lane(gc, head)
        b_col = sel_lane(b_cols, head + ML_HEADS)
        if zero_init:
            m_prev = jnp.broadcast_to(m_sc[hh:hh + 1, 0:1], (rows, 1))
        else:
            m_prev = sel_lane(gc_raw, head + 2 * ML_HEADS)
        ig_row = sel_sub(gr, head)
        b_row = sel_sub(b_rows, head + ML_HEADS)

        logd = jnp.where(causal, (b_col - b_row) + ig_row, -jnp.inf)
        m_t = jnp.maximum(b_col + m_prev, jnp.max(logd, axis=1, keepdims=True))
        d = jnp.exp(logd - m_t)
        w_inter = jnp.exp(b_col + m_prev - m_t)
        yield

        qb = q_ref[:, ks].astype(BF16)
        kb = k_ref[:, ks].astype(BF16)
        vb = v_ref[:, vs].astype(BF16)
        s = _dot_nt(qb, kb) * (d * ML_K_SCALE)
        num = jnp.dot(s.astype(BF16), vb, preferred_element_type=F32)
        den = jnp.sum(s, axis=1, keepdims=True)

        if nseq == 1:
            b_end = b_col[rows - 1:rows, :]
            m_new = m_t[rows - 1:rows, :]
        else:
            b_end = sel_lane(b_ends, head + ML_HEADS)
            m_new = _dot01_right(last, jnp.broadcast_to(m_t, (rows, LANES)))[:, 0:1]
        w_end = jnp.exp(b_end - b_col + ig_col - m_new)
        decay = jnp.exp(b_end + m_prev - m_new)

        qf = qb.astype(F32)
        kw = (w_end * ML_K_SCALE) * kb.astype(F32)
        yield

        if nseq == 1:
            c_prev = c0_ref[0, hh]
            n_prev = n0_ref[0, hh]
            q_c = jnp.dot(qb, c_prev.astype(BF16), preferred_element_type=F32)
            q_n = jnp.sum(qf * n_prev, axis=1, keepdims=True)
            dec = decay[0:1, :]
            c_ref[0, hh] = dec * c_prev + _dot_tn(kw.astype(BF16), vb)
            n_ref[0, hh] = dec * n_prev + jnp.sum(kw, axis=0, keepdims=True)
        else:
            q_c = jnp.zeros((rows, ML_DV), F32)
            q_n = jnp.zeros((rows, 1), F32)
            for j in range(nseq):
                in_seq = row_seq == j
                c_prev = c0_ref[j, hh]
                n_prev = n0_ref[j, hh]
                q_c = jnp.where(
                    in_seq, jnp.dot(qb, c_prev.astype(BF16), preferred_element_type=F32), q_c)
                q_n = jnp.where(in_seq, jnp.sum(qf * n_prev, axis=1, keepdims=True), q_n)
                kw_j = jnp.where(in_seq, kw, 0.0)
                dec = decay[j * seq_len:j * seq_len + 1, :]
                c_ref[j, hh] = dec * c_prev + _dot_tn(kw_j.astype(BF16), vb)
                n_ref[j, hh] = dec * n_prev + jnp.sum(kw_j, axis=0, keepdims=True)
                yield
        if nseq == 1:
            yield

        num = num + w_inter * q_c
        den = den + w_inter * q_n
        h_out = num / jnp.maximum(jnp.abs(den), jnp.exp(-m_t))
        mu =jnp.mean(h_out, axis=1, keepdims=True)
        xc = h_out - mu
        var = jnp.mean(xc * xc, axis=1, keepdims=True)
        hn = xc * lax.rsqrt(var + LN_EPS) * ng_ref[:, vs]
        ba_ref[:, vs] = (hn * _sigmoid(og_ref[:, vs].astype(F32))).astype(ba_ref.dtype)
        mrow_ref[hh] = jnp.broadcast_to(m_t, (rows, LANES))
        if zero_init:
            m_sc[hh:hh + 1, :] = jnp.broadcast_to(m_new, (1, LANES))
        yield

    yield from _interleaved([head_steps(hh) for hh in range(heads)], together)


def _mlstm_parts(proj, gates_col, gates_row, brow, bcol, norm_g, state, *, row0, batch, seq, rows,
                 seq_len, heads, together, index=None):
    tokens = batch * seq
    zero_init = state is None
    nseq = rows // seq_len
    nchunk = seq // seq_len if nseq == 1 else 1
    ngroup = tokens // (rows * nchunk)
    bq, bv = heads * ML_DK, heads * ML_DV
    rb0 = row0 // rows
    per_row_tile = gates_row.shape[2] // rows
    head0 = None
    if index is None:
        index = lambda g, h, c: (g * nchunk + c, h, g)
    else:
        assert heads == ML_HEADS
        head0 = 0
    row = lambda *ids: index(*ids)[0]
    hblk = lambda *ids: index(*ids)[1]
    sblk = lambda *ids: index(*ids)[2]

    in_specs = [
        pl.BlockSpec((rows, bq), lambda *ids: (rb0 + row(*ids), _P_ML_Q // bq + hblk(*ids))),
        pl.BlockSpec((rows, bq), lambda *ids: (rb0 + row(*ids), _P_ML_K // bq + hblk(*ids))),
        pl.BlockSpec((rows, bv), lambda *ids: (rb0 + row(*ids), _P_ML_V // bv + hblk(*ids))),
        pl.BlockSpec((rows, bv), lambda *ids: (rb0 + row(*ids), _P_ML_O // bv + hblk(*ids))),
        pl.BlockSpec((rows, LANES), lambda *ids: (row(*ids), 0)),
        pl.BlockSpec((1, _N_GATES, rows),
                     lambda *ids: (row(*ids) // per_row_tile, 0, row(*ids) % per_row_tile)),
        pl.BlockSpec((1, LANES), lambda *ids: (0, 0)),
        pl.BlockSpec((_N_GATES, 1), lambda *ids: (0, 0)),
        pl.BlockSpec((1, bv), lambda *ids: (0, hblk(*ids))),
    ]
    args = [proj, proj, proj, proj, gates_col, gates_row, brow, bcol, norm_g]
    state_specs = [
        pl.BlockSpec((nseq, heads, ML_DK, ML_DV), lambda *ids: (sblk(*ids), hblk(*ids), 0, 0)),
        pl.BlockSpec((nseq, heads, 1, ML_DK), lambda *ids: (sblk(*ids), hblk(*ids), 0, 0)),
    ]
    scratch = []
    if zero_init:
        scratch = [pltpu.VMEM((SUBLANES, LANES), F32)]
    else:
        in_specs += state_specs
        args += [state[0], state[1]]
    out_specs = [
        pl.BlockSpec((rows, bv), lambda *ids: (row(*ids), hblk(*ids))),
        *state_specs,
        pl.BlockSpec((heads, rows, LANES), lambda *ids: (hblk(*ids), row(*ids), 0)),
    ]
    out_shape = [
        jax.ShapeDtypeStruct((tokens, ML_V), BF16),
        jax.ShapeDtypeStruct((batch, ML_HEADS, ML_DK, ML_DV), F32),
        jax.ShapeDtypeStruct((batch, ML_HEADS, 1, ML_DK), F32),
        jax.ShapeDtypeStruct((ML_HEADS, tokens, LANES), F32),
    ]
    return dict(
        steps=functools.partial(_mlstm_steps, rows=rows, seq_len=seq_len, heads=heads,
                                together=together, zero_init=zero_init, head0=head0),
        n_steps=heads // together * (3 + nseq),
        grid=(ngroup, ML_HEADS // heads, nchunk),
        in_specs=in_specs, args=args, out_specs=out_specs, out_shape=out_shape, scratch=scratch)


def _run_parts(parts, name):
    n_in = [len(p["in_specs"]) for p in parts]
    n_out = [len(p["out_specs"]) for p in parts]
    n_scr = [len(p["scratch"]) for p in parts]

    def kernel(*refs):
        ins = refs[:sum(n_in)]
        outs = refs[sum(n_in):sum(n_in) + sum(n_out)]
        scr = refs[sum(n_in) + sum(n_out):]
        gens = []
        i = o = s = 0
        for p, ni, no, ns in zip(parts, n_in, n_out, n_scr):
            gens.append(p["steps"](*ins[i:i + ni], *outs[o:o + no], *scr[s:s + ns]))
            i, o, s = i + ni, o + no, s + ns
        order = sorted(((j + 0.5) / p["n_steps"], k) for k, p in enumerate(parts)
                       for j in range(p["n_steps"]))
        for _, k in order:
            next(gens[k], None)
        for gen in gens:
            for _ in gen:
                pass

    outs = pl.pallas_call(
        kernel,
        grid=parts[0]["grid"],
        in_specs=[s for p in parts for s in p["in_specs"]],
        out_specs=[s for p in parts for s in p["out_specs"]],
        out_shape=[s for p in parts for s in p["out_shape"]],
        scratch_shapes=[s for p in parts for s in p["scratch"]],
        compiler_params=pltpu.CompilerParams(
            dimension_semantics=("parallel", "parallel", "arbitrary")),
        name=name,
    )(*[a for p in parts for a in p["args"]])
    split, o = [], 0
    for no in n_out:
        split.append(outs[o:o + no])
        o += no
    return split


def _hgrn_level_ids(rows, seq_len, direct, levels):
    t = np.arange(rows)[:, None]
    s = np.arange(rows)[None, :]
    ids = np.full((rows, rows), -1, np.int32)
    count = ((t // direct == s // direct) & (s <= t)).astype(np.int32)
    for idx, (block, sub_size) in enumerate(levels):
        owned = (t // block == s // block) & ((s % block) // sub_size < (t % block) // sub_size)
        ids[owned] = idx
        count += owned
    wanted = (t // seq_len == s // seq_len) & (s <= t)
    assert np.array_equal(count, wanted.astype(np.int32)), (rows, seq_len, direct, levels)
    return ids


_INTRA_YIELD_BLOCKS = 32


def _hgrn_intra_yields(rows, direct, levels):
    return 2 + len(levels) + 2 * (max(rows // direct // _INTRA_YIELD_BLOCKS, 1) - 1)


def _hgrn_intra(q, kin, g2, gk2, level_masks, rows, direct, levels):
    ngroups = max(rows // LANES, 1)
    rowi = lax.broadcasted_iota(jnp.int32, (direct, LANES), 0)
    lanei = lax.broadcasted_iota(jnp.int32, (direct, LANES), 1)
    keep = [jnp.where(lanei == s, rowi, -1) >= s for s in range(direct)]
    zero_group = jnp.zeros((direct, LANES), F32)
    prods = []
    for blk in range(rows // direct):
        r0 = blk * direct
        qb = q[r0:r0 + direct]
        gb = g2[r0:r0 + direct]
        gkb = gk2[r0:r0 + direct]
        for s in range(direct):
            prods.append(qb * jnp.exp2(gb - gkb[s:s + 1, :]))
        if (blk + 1) % _INTRA_YIELD_BLOCKS == 0 and blk + 1 < rows // direct:
            yield
    sums = jnp.dot(jnp.concatenate(prods, axis=0).astype(BF16), jnp.ones((HG_DK, LANES), BF16),
                   preferred_element_type=F32)
    yield
    panels = []
    for blk in range(rows // direct):
        r0 = blk * direct
        ag = zero_group
        for s in range(direct):
            p0 = (blk * direct + s) * direct
            ag = jnp.where(keep[s], sums[p0:p0 + direct], ag)
        lane0 = r0 % LANES
        if lane0:
            ag = pltpu.roll(ag, lane0, axis=1)
        grp = r0 // LANES
        pieces = [zero_group] * grp + [ag] + [zero_group] * (ngroups - grp - 1)
        panels.append(pieces[0] if ngroups == 1 else jnp.concatenate(pieces, axis=1))
        if (blk + 1) % _INTRA_YIELD_BLOCKS == 0 and blk + 1 < rows // direct:
            yield
    a = panels[0] if len(panels) == 1 else jnp.concatenate(panels, axis=0)
    if rows < LANES:
        a = a[:, :rows]
    yield

    if levels:
        rowid = lax.broadcasted_iota(jnp.int32, (rows, 1), 0)
    for idx, (block, sub_size) in enumerate(levels):
        sshift = sub_size.bit_length() - 1
        row_sub = (rowid & (block - 1)) >> sshift
        q_parts, k_parts = [], []
        for j in range(1, block // sub_size):
            refs = []
            for b0 in range(0, rows, block):
                r = b0 + j * sub_size - 1
                refs.append(jnp.broadcast_to(g2[r:r + 1, :], (block, HG_DK)))
            g_ref = refs[0] if len(refs) == 1 else jnp.concatenate(refs, axis=0)
            e = jnp.exp2(-jnp.abs(g2 - g_ref))
            q_parts.append(jnp.where(row_sub == j, q * e, 0.0).astype(BF16))
            k_parts.append((kin * e).astype(BF16))
        qcat = q_parts[0] if len(q_parts) == 1 else jnp.concatenate(q_parts, axis=1)
        kcat = k_parts[0] if len(k_parts) == 1 else jnp.concatenate(k_parts, axis=1)
        a = jnp.where(level_masks[idx], _dot_nt(qcat, kcat), a)
        yield
    return a


def _hgrn_steps(*refs, layer, rows, seq_len, heads, together, direct, levels, zero_init):
    if zero_init:
        q_ref, f_ref, i_ref, g_ref, lbl_ref, ng_ref, lvl_ref, bb_ref, s_ref = refs
        s0_ref = s_ref
    else:
        q_ref, f_ref, i_ref, g_ref, lbl_ref, ng_ref, lvl_ref, s0_ref, bb_ref, s_ref = refs
    nseq = rows // seq_len

    if zero_init:
        @pl.when(pl.program_id(2) == 0)
        def _init():
            s_ref[...] = jnp.zeros(s_ref.shape, F32)

    same, lower, _ = _seq_masks(rows, seq_len)
    lower_b = lower.astype(BF16)
    same_b = same.astype(BF16)
    level_ids = lvl_ref[...]
    level_masks = [level_ids == idx for idx in range(len(levels))]
    if nseq > 1:
        shift = seq_len.bit_length() - 1
        row_seq = lax.broadcasted_iota(jnp.int32, (rows, 1), 0) >> shift

    def decay_cols(row):
        col = jnp.broadcast_to(jnp.exp2(row), (HG_DK, HG_DK)).T
        return jnp.concatenate([col] * (HG_DV // HG_DK), axis=1)

    def head_steps(hh):
        ks = slice(hh * HG_DK, (hh + 1) * HG_DK)
        vs = slice(hh * HG_DV, (hh + 1) * HG_DV)
        lg = lbl_ref[:, ks]
        ex = jnp.exp(lg - jnp.max(lg, axis=0, keepdims=True))
        lb = (jnp.sum(ex[0:layer + 1, :], axis=0, keepdims=True)
              / jnp.sum(ex, axis=0, keepdims=True))

        f = lb + (1.0 - lb) * _sigmoid(f_ref[:, ks])
        kin = 1.0 - f
        lf2 = jnp.log2(f)
        g2 = _dot01_right(lower_b, lf2)
        gk2 = g2 - jnp.log2(kin)
        yield
        q = q_ref[:, ks].astype(F32)
        if nseq == 1:
            g_end = jnp.broadcast_to(g2[rows - 1:rows, :], (rows, HG_DK))
        else:
            g_end = _dot01_right(same_b, lf2)
        qg = (q * jnp.exp2(g2)).astype(BF16)
        kg = kin * jnp.exp2(g_end - g2)
        ib = i_ref[:, vs].astype(BF16)
        yield

        a = yield from _hgrn_intra(q, kin, g2, gk2, level_masks, rows, direct, levels)
        o = jnp.dot(a.astype(BF16), ib, preferred_element_type=F32)

        if nseq == 1:
            s_prev = s0_ref[0, hh]
            o = o + jnp.dot(qg, s_prev.astype(BF16), preferred_element_type=F32)
            s_ref[0, hh] = decay_cols(g_end[0:1, :]) * s_prev + _dot_tn(kg.astype(BF16), ib)
            yield
        else:
            for j in range(nseq):
                in_seq = row_seq == j
                s_prev = s0_ref[j, hh]
                o_j = jnp.dot(qg, s_prev.astype(BF16), preferred_element_type=F32)
                o = o + jnp.where(in_seq, o_j, 0.0)
                kg_j = jnp.where(in_seq, kg, 0.0).astype(BF16)
                s_ref[j, hh] = (decay_cols(g_end[j * seq_len:j * seq_len + 1, :]) * s_prev
                                + _dot_tn(kg_j, ib))
                yield

        o = o * lax.rsqrt(jnp.mean(o * o, axis=1, keepdims=True) + LN_EPS)
        gate = g_ref[:, vs].astype(F32)
        bb_ref[:, vs] = (o * ng_ref[:, vs] * (gate * _sigmoid(gate))).astype(bb_ref.dtype)
        yield

    yield from _interleaved([head_steps(hh) for hh in range(heads)], together)


def _hgrn_parts(proj, hg_f, lb_logits, norm_g, s0, *, layer, row0, batch, seq, rows, seq_len, heads,
                together, direct, levels, index=None):
    tokens = batch * seq
    zero_init = s0 is None
    nseq = rows // seq_len
    nchunk = seq // seq_len if nseq == 1 else 1
    ngroup = tokens // (rows * nchunk)
    rb0 = row0 // rows
    if index is None:
        index = lambda g, h, c: (g * nchunk + c, h, g)
    row = lambda *ids: index(*ids)[0]
    hblk = lambda *ids: index(*ids)[1]
    sblk = lambda *ids: index(*ids)[2]

    wk, wv = heads * HG_DK, heads * HG_DV
    in_specs = [
        pl.BlockSpec((rows, wk), lambda *ids: (rb0 + row(*ids), _P_HG_Q // wk + hblk(*ids))),
        pl.BlockSpec((rows, wk), lambda *ids: (rb0 + row(*ids), hblk(*ids))),
        pl.BlockSpec((rows, wv), lambda *ids: (rb0 + row(*ids), _P_HG_I // wv + hblk(*ids))),
        pl.BlockSpec((rows, wv), lambda *ids: (rb0 + row(*ids), _P_HG_G // wv + hblk(*ids))),
        pl.BlockSpec((DEPTH + 1, wk), lambda *ids: (0, hblk(*ids))),
        pl.BlockSpec((1, wv), lambda *ids: (0, hblk(*ids))),
        pl.BlockSpec((rows, rows), lambda *ids: (0, 0)),
    ]
    level_ids = jnp.asarray(_hgrn_level_ids(rows, seq_len, direct, levels))
    args = [proj, hg_f, proj, proj, lb_logits, norm_g, level_ids]
    state_spec = pl.BlockSpec((nseq, heads, HG_DK, HG_DV),
                              lambda *ids: (sblk(*ids), hblk(*ids), 0, 0))
    if not zero_init:
        in_specs.append(state_spec)
        args.append(s0)
    return dict(
        steps=functools.partial(_hgrn_steps, layer=layer, rows=rows, seq_len=seq_len, heads=heads,
                                together=together, direct=direct, levels=levels,
                                zero_init=zero_init),
        n_steps=heads // together * (3 + _hgrn_intra_yields(rows, direct, levels) + nseq),
        grid=(ngroup, HG_HEADS // heads, nchunk),
        in_specs=in_specs, args=args,
        out_specs=[pl.BlockSpec((rows, wv), lambda *ids: (row(*ids), hblk(*ids))), state_spec],
        out_shape=[jax.ShapeDtypeStruct((tokens, HG_V), BF16),
                   jax.ShapeDtypeStruct((batch, HG_HEADS, HG_DK, HG_DV), F32)],
        scratch=[])


def _merge_kernel(ba_ref, wa_ref, bb_ref, wb_ref, ga_ref, gb_ref, o_ref):
    ya = jnp.dot(ba_ref[...], wa_ref[...], preferred_element_type=F32)
    yb = jnp.dot(bb_ref[...], wb_ref[...], preferred_element_type=F32)
    ga = _sigmoid(ga_ref[...].astype(F32))
    gb = _sigmoid(gb_ref[...].astype(F32))
    o_ref[...] = (ga * ya + gb * yb).astype(o_ref.dtype)


def _merge(branch_a, w_a, branch_b, w_b, proj, row0):
    m = branch_a.shape[0]
    tm, tn = _MERGE_TM, _MERGE_TN
    rb0 = row0 // tm
    ga_blk = _P_GATE_A // tn
    gb_blk = _P_GATE_B // tn
    return pl.pallas_call(
        _merge_kernel,
        grid=(D_MODEL // tn, m // tm),
        in_specs=[pl.BlockSpec((tm, ML_V), lambda j, i: (i, 0)),
                  pl.BlockSpec((ML_V, tn), lambda j, i: (0, j)),
                  pl.BlockSpec((tm, HG_V), lambda j, i: (i, 0)),
                  pl.BlockSpec((HG_V, tn), lambda j, i: (0, j)),
                  pl.BlockSpec((tm, tn), lambda j, i: (rb0 + i, ga_blk + j)),
                  pl.BlockSpec((tm, tn), lambda j, i: (rb0 + i, gb_blk + j))],
        out_specs=pl.BlockSpec((tm, tn), lambda j, i: (i, j)),
        out_shape=jax.ShapeDtypeStruct((m, D_MODEL), BF16),
        compiler_params=pltpu.CompilerParams(dimension_semantics=("parallel", "arbitrary")),
        name="merge",
    )(branch_a, w_a, branch_b, w_b, proj, proj)


def _layernorm_rows(z, g, b):
    mu = jnp.mean(z, axis=1, keepdims=True)
    zc = z - mu
    var = jnp.mean(zc * zc, axis=1, keepdims=True)
    return zc * lax.rsqrt(var + LN_EPS) * g + b


def _outproj_kernel(mg_ref, w_ref, x_ref, g_ref, b_ref, x1_ref):
    mix = jnp.dot(mg_ref[...], w_ref[...], preferred_element_type=F32)
    x1_ref[...] = _layernorm_rows(DEEPNORM_ALPHA * x_ref[...] + mix, g_ref[...], b_ref[...])


def _outproj(merged, w_out, x, ln_g, ln_b):
    tm = _OUT_TM
    m = x.shape[0]
    row = lambda i: (i, 0)
    const = lambda i: (0, 0)
    return pl.pallas_call(
        _outproj_kernel,
        grid=(m // tm,),
        in_specs=[pl.BlockSpec((tm, D_MODEL), row),
                  pl.BlockSpec((D_MODEL, D_MODEL), const),
                  pl.BlockSpec((tm, D_MODEL), row),
                  pl.BlockSpec((1, D_MODEL), const),
                  pl.BlockSpec((1, D_MODEL), const)],
        out_specs=pl.BlockSpec((tm, D_MODEL), row),
        out_shape=jax.ShapeDtypeStruct((m, D_MODEL), F32),
        compiler_params=pltpu.CompilerParams(dimension_semantics=("parallel",)),
        name="out_proj_ln",
    )(merged, w_out, x, ln_g, ln_b)


def _mlp_kernel(wu_ref, wd_ref, x1_ref, g_ref, b_ref, y_ref, x1b_sc):
    f = pl.program_id(1)

    @pl.when(f == 0)
    def _init():
        y_ref[...] = jnp.zeros(y_ref.shape, F32)
        x1b_sc[...] = x1_ref[...].astype(BF16)

    hid = jnp.maximum(jnp.dot(x1b_sc[...], wu_ref[...], preferred_element_type=F32), 0.0)
    hid = (hid * hid).astype(BF16)
    y_ref[...] += jnp.dot(hid, wd_ref[...], preferred_element_type=F32)

    @pl.when(f == pl.num_programs(1) - 1)
    def _finish():
        z = DEEPNORM_ALPHA * x1_ref[...] + y_ref[...]
        y_ref[...] = _layernorm_rows(z, g_ref[...], b_ref[...])


def _mlp(w_up, w_down, x1, ln_g, ln_b):
    tm, tf = _MLP_TM, _MLP_TF
    tokens = x1.shape[0]
    return pl.pallas_call(
        _mlp_kernel,
        grid=(tokens // tm, D_FF // tf),
        in_specs=[pl.BlockSpec((D_MODEL, tf), lambda i, f: (0, f)),
                  pl.BlockSpec((tf, D_MODEL), lambda i, f: (f, 0)),
                  pl.BlockSpec((tm, D_MODEL), lambda i, f: (i, 0)),
                  pl.BlockSpec((1, D_MODEL), lambda i, f: (0, 0)),
                  pl.BlockSpec((1, D_MODEL), lambda i, f: (0, 0))],
        out_specs=pl.BlockSpec((tm, D_MODEL), lambda i, f: (i, 0)),
        out_shape=jax.ShapeDtypeStruct((tokens, D_MODEL), F32),
        scratch_shapes=[pltpu.VMEM((tm, D_MODEL), BF16)],
        compiler_params=pltpu.CompilerParams(dimension_semantics=("parallel", "arbitrary")),
        name="mlp_ln",
    )(w_up, w_down, x1, ln_g, ln_b)


_PROMPT_ML = dict(rows=256, seq_len=256, heads=4, together=1)
_PROMPT_HG = dict(rows=256, seq_len=256, heads=4, together=2, direct=8,
                  levels=((16, 8), (32, 16), (64, 32), (128, 64), (256, 128)))
_SAMPLE_ML = dict(rows=16, seq_len=8, heads=4, together=4)
_SAMPLE_HG = dict(rows=32, seq_len=8, heads=8, together=1, direct=8, levels=())


def kernel(x_prompt, x_sample, state_mlstm_C, state_mlstm_n, state_mlstm_m, state_hgrn_S,
           hg_lb_logits, w_in, b_ig, b_fg, ml_norm_g, hg_norm_g, w_branch_a, w_branch_b, w_out,
           ln1_g, ln1_b, w_up, w_down, ln2_g, ln2_b):
    batch_p, seq_p, _ = x_prompt.shape
    batch_s, seq_s, _ = x_sample.shape
    tok_p, tok_s = batch_p * seq_p, batch_s * seq_s
    total = tok_p + tok_s
    group_p = dict(row0=0, batch=batch_p, seq=seq_p)
    group_s = dict(row0=tok_p, batch=batch_s, seq=seq_s)
    xp = x_prompt.reshape(tok_p, D_MODEL)
    xs = x_sample.reshape(tok_s, D_MODEL)
    lb_logits = hg_lb_logits.astype(F32)
    states_p, states_s = [], []
    for l in range(DEPTH):
        (proj, hg_f, gates, gates_gm), (w_a_b, w_b_b, w_out_b, w_up_b, w_down_b) = _inproj(
            xp, xs.astype(BF16), jnp.swapaxes(w_in[l], 0, 1),
            (w_branch_a[l], w_branch_b[l], w_out[l], w_up[l], w_down[l]))

        gate_bias = jnp.concatenate([b_ig[l], b_fg[l]]).astype(F32)
        brow = jnp.zeros((1, LANES), F32).at[0, :_N_GATES].set(gate_bias)
        bcol = gate_bias.reshape(_N_GATES, 1)
        ml_g = ml_norm_g[l].reshape(1, ML_V).astype(F32)
        hg_g = hg_norm_g[l].reshape(1, HG_V).astype(F32)

        m_rows = jnp.repeat(state_mlstm_m[l].astype(F32), seq_s, axis=0)
        gates_s = gates[tok_p:].at[:, _N_GATES:_N_GATES + ML_HEADS].set(m_rows)
        rows_s = _SAMPLE_ML["rows"]
        gm_s = gates_gm[tok_p // _PROJ_TM:]
        gm_s = gm_s.reshape(-1, _N_GATES, _PROJ_TM // rows_s, rows_s).transpose(0, 2, 1, 3)
        gm_s = gm_s.reshape(tok_s // rows_s, _N_GATES, rows_s)
        state_ml = (state_mlstm_C[l].astype(F32),
                    state_mlstm_n[l].astype(F32).reshape(batch_s, ML_HEADS, 1, ML_DK))
        ln1 = (ln1_g[l].reshape(1, D_MODEL), ln1_b[l].reshape(1, D_MODEL))
        ln2 = (ln2_g[l].reshape(1, D_MODEL), ln2_b[l].reshape(1, D_MODEL))

        ml_p = _mlstm_parts(proj, gates, gates_gm, brow, bcol, ml_g, None, **group_p, **_PROMPT_ML)
        hg_p = _hgrn_parts(proj, hg_f, lb_logits, hg_g, None, layer=l, **group_p, **_PROMPT_HG)

        def guest(parts_fn, host, **kwargs):
            grid = host["grid"]
            assert math.prod(grid) * kwargs["rows"] == tok_s, (grid, kwargs["rows"])
            flat = lambda g, h, c: (g * grid[1] + h) * grid[2] + c
            return parts_fn(index=lambda g, h, c: (flat(g, h, c), 0, flat(g, h, c)), **kwargs)

        ml_s = guest(functools.partial(_mlstm_parts, proj, gates_s, gm_s, brow, bcol, ml_g, state_ml,
                                       **group_s), hg_p, **_SAMPLE_ML)
        hg_s = guest(functools.partial(_hgrn_parts, proj, hg_f, lb_logits, hg_g,
                                       state_hgrn_S[l].astype(F32), layer=l, **group_s),
                     ml_p, **_SAMPLE_HG)
        (bb_p, s_p), (ba_s, c_s, n_s, m_all_s) = _run_parts([hg_p, ml_s], "hgrn2_prompt_mlstm_sample")
        (ba_p, c_p, n_p, m_all_p), (bb_s, s_s) = _run_parts([ml_p, hg_s], "mlstm_prompt_hgrn2_sample")

        merged_p = _merge(ba_p, w_a_b, bb_p, w_b_b, proj, 0)
        merged_s = _merge(ba_s, w_a_b, bb_s, w_b_b, proj, tok_p)
        x1_p = _outproj(merged_p, w_out_b, xp, *ln1)
        x1_s = _outproj(merged_s, w_out_b, xs, *ln1)
        xp = _mlp(w_up_b, w_down_b, x1_p, *ln2)
        xs = _mlp(w_up_b, w_down_b, x1_s, *ln2)

        states_p.append((c_p, n_p.reshape(batch_p, ML_HEADS, ML_DK),
                         m_all_p[:, seq_p - 1::seq_p, 0].T, s_p))
        states_s.append((c_s, n_s.reshape(batch_s, ML_HEADS, ML_DK),
                         m_all_s[:, seq_s - 1::seq_s, 0].T, s_s))
    stack = lambda states, k: jnp.stack([s[k] for s in states])
    return (xp.reshape(batch_p, seq_p, D_MODEL), xs.reshape(batch_s, seq_s, D_MODEL),
            stack(states_p, 0), stack(states_p, 1), stack(states_p, 2), stack(states_p, 3),
            stack(states_s, 0), stack(states_s, 1), stack(states_s, 2), stack(states_s, 3))
```

```python
import functools
import math

import numpy as np
import jax
import jax.numpy as jnp
from jax import lax
from jax.experimental import pallas as pl
from jax.experimental.pallas import tpu as pltpu

F32 = jnp.float32
BF16 = jnp.bfloat16

D_MODEL = 2048
DEPTH = 1
ML_HEADS, ML_DK, ML_DV = 4, 256, 512
HG_HEADS, HG_DK, HG_DV = 8, 128, 256
ML_QK = ML_HEADS * ML_DK
ML_V = ML_HEADS * ML_DV
HG_K = HG_HEADS * HG_DK
HG_V = HG_HEADS * HG_DV
D_FF = 4 * D_MODEL
LN_EPS = 1e-5
DEEPNORM_ALPHA = (2.0 * DEPTH) ** 0.25
ML_K_SCALE = ML_DK ** -0.5
LANES = 128
SUBLANES = 8

_OFF_ML_Q = 0
_OFF_ML_K = _OFF_ML_Q + ML_QK
_OFF_ML_V = _OFF_ML_K + ML_QK
_OFF_ML_I = _OFF_ML_V + ML_V
_OFF_ML_F = _OFF_ML_I + ML_HEADS
_OFF_ML_O = _OFF_ML_F + ML_HEADS
_OFF_HG_Q = _OFF_ML_O + ML_V
_OFF_HG_F = _OFF_HG_Q + HG_K
_OFF_HG_I = _OFF_HG_F + HG_K
_OFF_HG_G = _OFF_HG_I + HG_V
_OFF_GATE_A = _OFF_HG_G + HG_V
_OFF_GATE_B = _OFF_GATE_A + D_MODEL
D_IN = _OFF_GATE_B + D_MODEL

_N_GATES = 2 * ML_HEADS
_P_ML_Q = _OFF_ML_Q
_P_ML_K = _OFF_ML_K
_P_ML_V = _OFF_ML_V
_P_ML_O = _OFF_ML_O - _N_GATES
_P_HG_Q = _OFF_HG_Q - _N_GATES
_P_HG_F = _OFF_HG_F - _N_GATES
_P_HG_I = _OFF_HG_I - _N_GATES
_P_HG_G = _OFF_HG_G - _N_GATES
_P_GATE_A = _OFF_GATE_A - _N_GATES
_P_GATE_B = _OFF_GATE_B - _N_GATES
_P_WIDTH = D_IN - _N_GATES

_PROJ_TM, _PROJ_TN = 1024, 1024
_CAST_COL_BLOCKS = 8
_MERGE_TM, _MERGE_TN = 1024, 1024
_OUT_TM = 1024
_MLP_TM, _MLP_TF = 1024, 1024


def _sigmoid(x):
    return 1.0 / (1.0 + jnp.exp(-x))


def _log_sigmoid(x):
    return jnp.minimum(x, 0.0) - jnp.log1p(jnp.exp(-jnp.abs(x)))


def _split2(x):
    hi = x.astype(BF16)
    lo = (x - hi.astype(F32)).astype(BF16)
    return hi, lo


def _dot01_right(t01, x):
    hi, lo = _split2(x)
    return (jnp.dot(t01, hi, preferred_element_type=F32)
            + jnp.dot(t01, lo, preferred_element_type=F32))


def _dot01_left(x, t01):
    hi, lo = _split2(x)
    return (jnp.dot(hi, t01, preferred_element_type=F32)
            + jnp.dot(lo, t01, preferred_element_type=F32))


def _dot_nt(a, b):
    return lax.dot_general(a, b, (((1,), (1,)), ((), ())), preferred_element_type=F32)


def _dot_tn(a, b):
    return lax.dot_general(a, b, (((0,), (0,)), ((), ())), preferred_element_type=F32)


def _interleaved(gens, together):
    done = object()
    for g0 in range(0, len(gens), together):
        alive = gens[g0:g0 + together]
        while alive:
            alive = [g for g in alive if next(g, done) is not done]
            if alive:
                yield


def _seq_masks(rows, seq_len):
    ri = lax.broadcasted_iota(jnp.int32, (rows, rows), 0)
    ci = lax.broadcasted_iota(jnp.int32, (rows, rows), 1)
    lower = jnp.where(ci <= ri, 1.0, 0.0)
    upper = jnp.where(ri <= ci, 1.0, 0.0)
    if seq_len == rows:
        return jnp.ones((rows, rows), F32), lower, upper
    shift = seq_len.bit_length() - 1
    same = jnp.where((ri >> shift) == (ci >> shift), 1.0, 0.0)
    return same, same * lower, same * upper


def _inproj_kernel(xp_ref, xs_ref, wt_hbm, *refs, n_f32, n_gate_tile, prompt_tiles, cast_steps):
    n_cast = (len(refs) - 9) // 2
    cast_in = refs[:n_cast]
    o_ref, f_ref, g_ref, gr_ref = refs[n_cast:n_cast + 4]
    cast_out = refs[n_cast + 4:2 * n_cast + 4]
    wbuf, wbf_sc, wg_sc, sem, gsem = refs[2 * n_cast + 4:]
    n = pl.program_id(0)
    m = pl.program_id(1)
    tn = wbf_sc.shape[1]

    @pl.when(m < cast_steps)
    def _cast_other_weights():
        for src, dst in zip(cast_in, cast_out):
            dst[...] = src[...].astype(BF16)

    def tile_copy(tile):
        start = pl.multiple_of(tile * tn + jnp.where(tile >= n_gate_tile, _N_GATES, 0), _N_GATES)
        return pltpu.make_async_copy(wt_hbm.at[pl.ds(start, tn), :], wbuf, sem.at[0])

    def gate_copy():
        return pltpu.make_async_copy(wt_hbm.at[pl.ds(_OFF_ML_I, _N_GATES), :],
                                     wg_sc.at[pl.ds(0, _N_GATES), :], gsem.at[0])

    @pl.when(m == 0)
    def _next_weight_tile():
        @pl.when(n == 0)
        def _first():
            tile_copy(0).start()
            wg_sc[_N_GATES:, :] = jnp.zeros((LANES - _N_GATES, wg_sc.shape[1]), F32)
            gate_copy().start()
            gate_copy().wait()

        tile_copy(n).wait()
        wbf_sc[...] = wbuf[...].T.astype(BF16)

        @pl.when(n + 1 < pl.num_programs(0))
        def _prefetch():
            tile_copy(n + 1).start()

    x = jnp.where(m < prompt_tiles, xp_ref[...].astype(BF16), xs_ref[...])
    acc = jnp.dot(x, wbf_sc[...], preferred_element_type=F32)
    o_ref[...] = acc.astype(BF16)

    @pl.when(n == n_f32)
    def _f32_outputs():
        f_ref[...] = acc
        wg = wg_sc[...].astype(BF16)
        gates = _dot_nt(x, wg)
        g_ref[...] = gates
        gr_ref[0] = gates.T[0:_N_GATES, :]


def _inproj(xp_b, xs_b, w_t, cast_weights):
    k = xp_b.shape[1]
    tm, tn = _PROJ_TM, _PROJ_TN
    p_tiles = xp_b.shape[0] // tm
    n_m = p_tiles + xs_b.shape[0] // tm
    tokens = n_m * tm
    n_n = _P_WIDTH // tn
    n_f32 = _P_HG_F // tn
    cast_steps = min(n_m, _CAST_COL_BLOCKS)

    def parked(n, m):
        return jnp.where(n < n_f32, 0, jnp.where(n == n_f32, m, n_m - 1))

    def cast_spec(w):
        return pl.BlockSpec((w.shape[0] // n_n, w.shape[1] // cast_steps),
                            lambda n, m: (n, jnp.minimum(m, cast_steps - 1)))

    cast_specs = [cast_spec(w) for w in cast_weights]
    outs = pl.pallas_call(
        functools.partial(_inproj_kernel, n_f32=n_f32, n_gate_tile=_OFF_ML_I // tn,
                          prompt_tiles=p_tiles, cast_steps=cast_steps),
        grid=(n_n, n_m),
        in_specs=[pl.BlockSpec((tm, k), lambda n, m: (jnp.minimum(m, p_tiles - 1), 0)),
                  pl.BlockSpec((tm, k), lambda n, m: (jnp.maximum(m - p_tiles, 0), 0),
                               pipeline_mode=pl.Buffered(1)),
                  pl.BlockSpec(memory_space=pl.ANY),
                  *cast_specs],
        out_specs=[pl.BlockSpec((tm, tn), lambda n, m: (m, n)),
                   pl.BlockSpec((tm, HG_K), lambda n, m: (parked(n, m), 0)),
                   pl.BlockSpec((tm, LANES), lambda n, m: (parked(n, m), 0)),
                   pl.BlockSpec((1, _N_GATES, tm), lambda n, m: (parked(n, m), 0, 0)),
                   *cast_specs],
        out_shape=[jax.ShapeDtypeStruct((tokens, _P_WIDTH), BF16),
                   jax.ShapeDtypeStruct((tokens, HG_K), F32),
                   jax.ShapeDtypeStruct((tokens, LANES), F32),
                   jax.ShapeDtypeStruct((n_m, 8, tm), F32),
                   *[jax.ShapeDtypeStruct(w.shape, BF16) for w in cast_weights]],
        scratch_shapes=[pltpu.VMEM((tn, k), F32),
                        pltpu.VMEM((k, tn), BF16),
                        pltpu.VMEM((LANES, k), F32),
                        pltpu.SemaphoreType.DMA((1,)),
                        pltpu.SemaphoreType.DMA((1,))],
        compiler_params=pltpu.CompilerParams(dimension_semantics=("arbitrary", "arbitrary")),
        name="in_proj",
    )(xp_b, xs_b, w_t, *cast_weights)
    return outs[:4], outs[4:]


def _mlstm_steps(*refs, rows, seq_len, heads, together, zero_init, head0=None):
    if zero_init:
        (q_ref, k_ref, v_ref, og_ref, gc_ref, gr_ref, brow_ref, bcol_ref, ng_ref,
         ba_ref, c_ref, n_ref, mrow_ref, m_sc) = refs
        c0_ref, n0_ref = c_ref, n_ref
    else:
        (q_ref, k_ref, v_ref, og_ref, gc_ref, gr_ref, brow_ref, bcol_ref, ng_ref, c0_ref, n0_ref,
         ba_ref, c_ref, n_ref, mrow_ref) = refs
    nseq = rows // seq_len
    if head0 is None:
        head0 = pl.program_id(1) * heads

    if zero_init:
        @pl.when(pl.program_id(2) == 0)
        def _init():
            c_ref[...] = jnp.zeros(c_ref.shape, F32)
            n_ref[...] = jnp.zeros(n_ref.shape, F32)
            m_sc[...] = jnp.zeros(m_sc.shape, F32)

    same, lower, upper = _seq_masks(rows, seq_len)
    causal = lower > 0.5
    lane = lax.broadcasted_iota(jnp.int32, (rows, LANES), 1)
    sub = lax.broadcasted_iota(jnp.int32, (_N_GATES, rows), 0)

    def sel_lane(x, idx):
        return jnp.sum(jnp.where(lane == idx, x, 0.0), axis=1, keepdims=True)

    def sel_sub(x, idx):
        return jnp.sum(jnp.where(sub == idx, x, 0.0), axis=0, keepdims=True)

    gc_raw = gc_ref[...]
    gc = gc_raw + brow_ref[...]
    lf_cols = _log_sigmoid(gc)
    b_cols = _dot01_right(lower.astype(BF16), lf_cols)
    gr = gr_ref[0] + bcol_ref[...]
    b_rows = _dot01_left(_log_sigmoid(gr), upper.astype(BF16))
    if nseq > 1:
        b_ends = _dot01_right(same.astype(BF16), lf_cols)
        ci = lax.broadcasted_iota(jnp.int32, (rows, rows), 1)
        last = (same * jnp.where((ci & (seq_len - 1)) == seq_len - 1, 1.0, 0.0)).astype(BF16)
        shift = seq_len.bit_length() - 1
        row_seq = lax.broadcasted_iota(jnp.int32, (rows, 1), 0) >> shift

    def head_steps(hh):
        head = head0 + hh
        ks = slice(hh * ML_DK, (hh + 1) * ML_DK)
        vs = slice(hh * ML_DV, (hh + 1) * ML_DV)
        ig_col = sel_lane(gc, head)
        b_col = sel_lane(b_cols, head + ML_HEADS)
        if zero_init:
            m_prev = jnp.broadcast_to(m_sc[hh:hh + 1, 0:1], (rows, 1))
        else:
            m_prev = sel_lane(gc_raw, head + 2 * ML_HEADS)
        ig_row = sel_sub(gr, head)
        b_row = sel_sub(b_rows, head + ML_HEADS)

        logd = jnp.where(causal, (b_col - b_row) + ig_row, -jnp.inf)
        m_t = jnp.maximum(b_col + m_prev, jnp.max(logd, axis=1, keepdims=True))
        d = jnp.exp(logd - m_t)
        w_inter = jnp.exp(b_col + m_prev - m_t)
        yield

        qb = q_ref[:, ks].astype(BF16)
        kb = k_ref[:, ks].astype(BF16)
        vb = v_ref[:, vs].astype(BF16)
        s = _dot_nt(qb, kb) * (d * ML_K_SCALE)
        num = jnp.dot(s.astype(BF16), vb, preferred_element_type=F32)
        den = jnp.sum(s, axis=1, keepdims=True)

        if nseq == 1:
            b_end = b_col[rows - 1:rows, :]
            m_new = m_t[rows - 1:rows, :]
        else:
            b_end = sel_lane(b_ends, head + ML_HEADS)
            m_new = _dot01_right(last, jnp.broadcast_to(m_t, (rows, LANES)))[:, 0:1]
        w_end = jnp.exp(b_end - b_col + ig_col - m_new)
        decay = jnp.exp(b_end + m_prev - m_new)

        qf = qb.astype(F32)
        kw = (w_end * ML_K_SCALE) * kb.astype(F32)
        yield

        if nseq == 1:
            c_prev = c0_ref[0, hh]
            n_prev = n0_ref[0, hh]
            q_c = jnp.dot(qb, c_prev.astype(BF16), preferred_element_type=F32)
            q_n = jnp.sum(qf * n_prev, axis=1, keepdims=True)
            dec = decay[0:1, :]
            c_ref[0, hh] = dec * c_prev + _dot_tn(kw.astype(BF16), vb)
            n_ref[0, hh] = dec * n_prev + jnp.sum(kw, axis=0, keepdims=True)
        else:
            q_c = jnp.zeros((rows, ML_DV), F32)
            q_n = jnp.zeros((rows, 1), F32)
            for j in range(nseq):
                in_seq = row_seq == j
                c_prev = c0_ref[j, hh]
                n_prev = n0_ref[j, hh]
                q_c = jnp.where(
                    in_seq, jnp.dot(qb, c_prev.astype(BF16), preferred_element_type=F32), q_c)
                q_n = jnp.where(in_seq, jnp.sum(qf * n_prev, axis=1, keepdims=True), q_n)
                kw_j = jnp.where(in_seq, kw, 0.0)
                dec = decay[j * seq_len:j * seq_len + 1, :]
                c_ref[j, hh] = dec * c_prev + _dot_tn(kw_j.astype(BF16), vb)
                n_ref[j, hh] = dec * n_prev + jnp.sum(kw_j, axis=0, keepdims=True)
                yield
        if nseq == 1:
            yield

        num = num + w_inter * q_c
        den = den + w_inter * q_n
        h_out = num / jnp.maximum(jnp.abs(den), jnp.exp(-m_t))
        mu =jnp.mean(h_out, axis=1, keepdims=True)
        xc = h_out - mu
        var = jnp.mean(xc * xc, axis=1, keepdims=True)
        hn = xc * lax.rsqrt(var + LN_EPS) * ng_ref[:, vs]
        ba_ref[:, vs] = (hn * _sigmoid(og_ref[:, vs].astype(F32))).astype(ba_ref.dtype)
        mrow_ref[hh] = jnp.broadcast_to(m_t, (rows, LANES))
        if zero_init:
            m_sc[hh:hh + 1, :] = jnp.broadcast_to(m_new, (1, LANES))
        yield

    yield from _interleaved([head_steps(hh) for hh in range(heads)], together)


def _mlstm_parts(proj, gates_col, gates_row, brow, bcol, norm_g, state, *, row0, batch, seq, rows,
                 seq_len, heads, together, index=None):
    tokens = batch * seq
    zero_init = state is None
    nseq = rows // seq_len
    nchunk = seq // seq_len if nseq == 1 else 1
    ngroup = tokens // (rows * nchunk)
    bq, bv = heads * ML_DK, heads * ML_DV
    rb0 = row0 // rows
    per_row_tile = gates_row.shape[2] // rows
    head0 = None
    if index is None:
        index = lambda g, h, c: (g * nchunk + c, h, g)
    else:
        assert heads == ML_HEADS
        head0 = 0
    row = lambda *ids: index(*ids)[0]
    hblk = lambda *ids: index(*ids)[1]
    sblk = lambda *ids: index(*ids)[2]

    in_specs = [
        pl.BlockSpec((rows, bq), lambda *ids: (rb0 + row(*ids), _P_ML_Q // bq + hblk(*ids))),
        pl.BlockSpec((rows, bq), lambda *ids: (rb0 + row(*ids), _P_ML_K // bq + hblk(*ids))),
        pl.BlockSpec((rows, bv), lambda *ids: (rb0 + row(*ids), _P_ML_V // bv + hblk(*ids))),
        pl.BlockSpec((rows, bv), lambda *ids: (rb0 + row(*ids), _P_ML_O // bv + hblk(*ids))),
        pl.BlockSpec((rows, LANES), lambda *ids: (row(*ids), 0)),
        pl.BlockSpec((1, _N_GATES, rows),
                     lambda *ids: (row(*ids) // per_row_tile, 0, row(*ids) % per_row_tile)),
        pl.BlockSpec((1, LANES), lambda *ids: (0, 0)),
        pl.BlockSpec((_N_GATES, 1), lambda *ids: (0, 0)),
        pl.BlockSpec((1, bv), lambda *ids: (0, hblk(*ids))),
    ]
    args = [proj, proj, proj, proj, gates_col, gates_row, brow, bcol, norm_g]
    state_specs = [
        pl.BlockSpec((nseq, heads, ML_DK, ML_DV), lambda *ids: (sblk(*ids), hblk(*ids), 0, 0)),
        pl.BlockSpec((nseq, heads, 1, ML_DK), lambda *ids: (sblk(*ids), hblk(*ids), 0, 0)),
    ]
    scratch = []
    if zero_init:
        scratch = [pltpu.VMEM((SUBLANES, LANES), F32)]
    else:
        in_specs += state_specs
        args += [state[0], state[1]]
    out_specs = [
        pl.BlockSpec((rows, bv), lambda *ids: (row(*ids), hblk(*ids))),
        *state_specs,
        pl.BlockSpec((heads, rows, LANES), lambda *ids: (hblk(*ids), row(*ids), 0)),
    ]
    out_shape = [
        jax.ShapeDtypeStruct((tokens, ML_V), BF16),
        jax.ShapeDtypeStruct((batch, ML_HEADS, ML_DK, ML_DV), F32),
        jax.ShapeDtypeStruct((batch, ML_HEADS, 1, ML_DK), F32),
        jax.ShapeDtypeStruct((ML_HEADS, tokens, LANES), F32),
    ]
    return dict(
        steps=functools.partial(_mlstm_steps, rows=rows, seq_len=seq_len, heads=heads,
                                together=together, zero_init=zero_init, head0=head0),
        n_steps=heads // together * (3 + nseq),
        grid=(ngroup, ML_HEADS // heads, nchunk),
        in_specs=in_specs, args=args, out_specs=out_specs, out_shape=out_shape, scratch=scratch)


def _run_parts(parts, name):
    n_in = [len(p["in_specs"]) for p in parts]
    n_out = [len(p["out_specs"]) for p in parts]
    n_scr = [len(p["scratch"]) for p in parts]

    def kernel(*refs):
        ins = refs[:sum(n_in)]
        outs = refs[sum(n_in):sum(n_in) + sum(n_out)]
        scr = refs[sum(n_in) + sum(n_out):]
        gens = []
        i = o = s = 0
        for p, ni, no, ns in zip(parts, n_in, n_out, n_scr):
            gens.append(p["steps"](*ins[i:i + ni], *outs[o:o + no], *scr[s:s + ns]))
            i, o, s = i + ni, o + no, s + ns
        order = sorted(((j + 0.5) / p["n_steps"], k) for k, p in enumerate(parts)
                       for j in range(p["n_steps"]))
        for _, k in order:
            next(gens[k], None)
        for gen in gens:
            for _ in gen:
                pass

    outs = pl.pallas_call(
        kernel,
        grid=parts[0]["grid"],
        in_specs=[s for p in parts for s in p["in_specs"]],
        out_specs=[s for p in parts for s in p["out_specs"]],
        out_shape=[s for p in parts for s in p["out_shape"]],
        scratch_shapes=[s for p in parts for s in p["scratch"]],
        compiler_params=pltpu.CompilerParams(
            dimension_semantics=("parallel", "parallel", "arbitrary")),
        name=name,
    )(*[a for p in parts for a in p["args"]])
    split, o = [], 0
    for no in n_out:
        split.append(outs[o:o + no])
        o += no
    return split


def _hgrn_level_ids(rows, seq_len, direct, levels):
    t = np.arange(rows)[:, None]
    s = np.arange(rows)[None, :]
    ids = np.full((rows, rows), -1, np.int32)
    count = ((t // direct == s // direct) & (s <= t)).astype(np.int32)
    for idx, (block, sub_size) in enumerate(levels):
        owned = (t // block == s // block) & ((s % block) // sub_size < (t % block) // sub_size)
        ids[owned] = idx
        count += owned
    wanted = (t // seq_len == s // seq_len) & (s <= t)
    assert np.array_equal(count, wanted.astype(np.int32)), (rows, seq_len, direct, levels)
    return ids


_INTRA_YIELD_BLOCKS = 32


def _hgrn_intra_yields(rows, direct, levels):
    return 2 + len(levels) + 2 * (max(rows // direct // _INTRA_YIELD_BLOCKS, 1) - 1)


def _hgrn_intra(q, kin, g2, gk2, level_masks, rows, direct, levels):
    ngroups = max(rows // LANES, 1)
    rowi = lax.broadcasted_iota(jnp.int32, (direct, LANES), 0)
    lanei = lax.broadcasted_iota(jnp.int32, (direct, LANES), 1)
    keep = [jnp.where(lanei == s, rowi, -1) >= s for s in range(direct)]
    zero_group = jnp.zeros((direct, LANES), F32)
    prods = []
    for blk in range(rows // direct):
        r0 = blk * direct
        qb = q[r0:r0 + direct]
        gb = g2[r0:r0 + direct]
        gkb = gk2[r0:r0 + direct]
        for s in range(direct):
            prods.append(qb * jnp.exp2(gb - gkb[s:s + 1, :]))
        if (blk + 1) % _INTRA_YIELD_BLOCKS == 0 and blk + 1 < rows // direct:
            yield
    sums = jnp.dot(jnp.concatenate(prods, axis=0).astype(BF16), jnp.ones((HG_DK, LANES), BF16),
                   preferred_element_type=F32)
    yield
    panels = []
    for blk in range(rows // direct):
        r0 = blk * direct
        ag = zero_group
        for s in range(direct):
            p0 = (blk * direct + s) * direct
            ag = jnp.where(keep[s], sums[p0:p0 + direct], ag)
        lane0 = r0 % LANES
        if lane0:
            ag = pltpu.roll(ag, lane0, axis=1)
        grp = r0 // LANES
        pieces = [zero_group] * grp + [ag] + [zero_group] * (ngroups - grp - 1)
        panels.append(pieces[0] if ngroups == 1 else jnp.concatenate(pieces, axis=1))
        if (blk + 1) % _INTRA_YIELD_BLOCKS == 0 and blk + 1 < rows // direct:
            yield
    a = panels[0] if len(panels) == 1 else jnp.concatenate(panels, axis=0)
    if rows < LANES:
        a = a[:, :rows]
    yield

    if levels:
        rowid = lax.broadcasted_iota(jnp.int32, (rows, 1), 0)
    for idx, (block, sub_size) in enumerate(levels):
        sshift = sub_size.bit_length() - 1
        row_sub = (rowid & (block - 1)) >> sshift
        q_parts, k_parts = [], []
        for j in range(1, block // sub_size):
            refs = []
            for b0 in range(0, rows, block):
                r = b0 + j * sub_size - 1
                refs.append(jnp.broadcast_to(g2[r:r + 1, :], (block, HG_DK)))
            g_ref = refs[0] if len(refs) == 1 else jnp.concatenate(refs, axis=0)
            e = jnp.exp2(-jnp.abs(g2 - g_ref))
            q_parts.append(jnp.where(row_sub == j, q * e, 0.0).astype(BF16))
            k_parts.append((kin * e).astype(BF16))
        qcat = q_parts[0] if len(q_parts) == 1 else jnp.concatenate(q_parts, axis=1)
        kcat = k_parts[0] if len(k_parts) == 1 else jnp.concatenate(k_parts, axis=1)
        a = jnp.where(level_masks[idx], _dot_nt(qcat, kcat), a)
        yield
    return a


def _hgrn_steps(*refs, layer, rows, seq_len, heads, together, direct, levels, zero_init):
    if zero_init:
        q_ref, f_ref, i_ref, g_ref, lbl_ref, ng_ref, lvl_ref, bb_ref, s_ref = refs
        s0_ref = s_ref
    else:
        q_ref, f_ref, i_ref, g_ref, lbl_ref, ng_ref, lvl_ref, s0_ref, bb_ref, s_ref = refs
    nseq = rows // seq_len

    if zero_init:
        @pl.when(pl.program_id(2) == 0)
        def _init():
            s_ref[...] = jnp.zeros(s_ref.shape, F32)

    same, lower, _ = _seq_masks(rows, seq_len)
    lower_b = lower.astype(BF16)
    same_b = same.astype(BF16)
    level_ids = lvl_ref[...]
    level_masks = [level_ids == idx for idx in range(len(levels))]
    if nseq > 1:
        shift = seq_len.bit_length() - 1
        row_seq = lax.broadcasted_iota(jnp.int32, (rows, 1), 0) >> shift

    def decay_cols(row):
        col = jnp.broadcast_to(jnp.exp2(row), (HG_DK, HG_DK)).T
        return jnp.concatenate([col] * (HG_DV // HG_DK), axis=1)

    def head_steps(hh):
        ks = slice(hh * HG_DK, (hh + 1) * HG_DK)
        vs = slice(hh * HG_DV, (hh + 1) * HG_DV)
        lg = lbl_ref[:, ks]
        ex = jnp.exp(lg - jnp.max(lg, axis=0, keepdims=True))
        lb = (jnp.sum(ex[0:layer + 1, :], axis=0, keepdims=True)
              / jnp.sum(ex, axis=0, keepdims=True))

        f = lb + (1.0 - lb) * _sigmoid(f_ref[:, ks])
        kin = 1.0 - f
        lf2 = jnp.log2(f)
        g2 = _dot01_right(lower_b, lf2)
        gk2 = g2 - jnp.log2(kin)
        yield
        q = q_ref[:, ks].astype(F32)
        if nseq == 1:
            g_end = jnp.broadcast_to(g2[rows - 1:rows, :], (rows, HG_DK))
        else:
            g_end = _dot01_right(same_b, lf2)
        qg = (q * jnp.exp2(g2)).astype(BF16)
        kg = kin * jnp.exp2(g_end - g2)
        ib = i_ref[:, vs].astype(BF16)
        yield

        a = yield from _hgrn_intra(q, kin, g2, gk2, level_masks, rows, direct, levels)
        o = jnp.dot(a.astype(BF16), ib, preferred_element_type=F32)

        if nseq == 1:
            s_prev = s0_ref[0, hh]
            o = o + jnp.dot(qg, s_prev.astype(BF16), preferred_element_type=F32)
            s_ref[0, hh] = decay_cols(g_end[0:1, :]) * s_prev + _dot_tn(kg.astype(BF16), ib)
            yield
        else:
            for j in range(nseq):
                in_seq = row_seq == j
                s_prev = s0_ref[j, hh]
                o_j = jnp.dot(qg, s_prev.astype(BF16), preferred_element_type=F32)
                o = o + jnp.where(in_seq, o_j, 0.0)
                kg_j = jnp.where(in_seq, kg, 0.0).astype(BF16)
                s_ref[j, hh] = (decay_cols(g_end[j * seq_len:j * seq_len + 1, :]) * s_prev
                                + _dot_tn(kg_j, ib))
                yield

        o = o * lax.rsqrt(jnp.mean(o * o, axis=1, keepdims=True) + LN_EPS)
        gate = g_ref[:, vs].astype(F32)
        bb_ref[:, vs] = (o * ng_ref[:, vs] * (gate * _sigmoid(gate))).astype(bb_ref.dtype)
        yield

    yield from _interleaved([head_steps(hh) for hh in range(heads)], together)


def _hgrn_parts(proj, hg_f, lb_logits, norm_g, s0, *, layer, row0, batch, seq, rows, seq_len, heads,
                together, direct, levels, index=None):
    tokens = batch * seq
    zero_init = s0 is None
    nseq = rows // seq_len
    nchunk = seq // seq_len if nseq == 1 else 1
    ngroup = tokens // (rows * nchunk)
    rb0 = row0 // rows
    if index is None:
        index = lambda g, h, c: (g * nchunk + c, h, g)
    row = lambda *ids: index(*ids)[0]
    hblk = lambda *ids: index(*ids)[1]
    sblk = lambda *ids: index(*ids)[2]

    wk, wv = heads * HG_DK, heads * HG_DV
    in_specs = [
        pl.BlockSpec((rows, wk), lambda *ids: (rb0 + row(*ids), _P_HG_Q // wk + hblk(*ids))),
        pl.BlockSpec((rows, wk), lambda *ids: (rb0 + row(*ids), hblk(*ids))),
        pl.BlockSpec((rows, wv), lambda *ids: (rb0 + row(*ids), _P_HG_I // wv + hblk(*ids))),
        pl.BlockSpec((rows, wv), lambda *ids: (rb0 + row(*ids), _P_HG_G // wv + hblk(*ids))),
        pl.BlockSpec((DEPTH + 1, wk), lambda *ids: (0, hblk(*ids))),
        pl.BlockSpec((1, wv), lambda *ids: (0, hblk(*ids))),
        pl.BlockSpec((rows, rows), lambda *ids: (0, 0)),
    ]
    level_ids = jnp.asarray(_hgrn_level_ids(rows, seq_len, direct, levels))
    args = [proj, hg_f, proj, proj, lb_logits, norm_g, level_ids]
    state_spec = pl.BlockSpec((nseq, heads, HG_DK, HG_DV),
                              lambda *ids: (sblk(*ids), hblk(*ids), 0, 0))
    if not zero_init:
        in_specs.append(state_spec)
        args.append(s0)
    return dict(
        steps=functools.partial(_hgrn_steps, layer=layer, rows=rows, seq_len=seq_len, heads=heads,
                                together=together, direct=direct, levels=levels,
                                zero_init=zero_init),
        n_steps=heads // together * (3 + _hgrn_intra_yields(rows, direct, levels) + nseq),
        grid=(ngroup, HG_HEADS // heads, nchunk),
        in_specs=in_specs, args=args,
        out_specs=[pl.BlockSpec((rows, wv), lambda *ids: (row(*ids), hblk(*ids))), state_spec],
        out_shape=[jax.ShapeDtypeStruct((tokens, HG_V), BF16),
                   jax.ShapeDtypeStruct((batch, HG_HEADS, HG_DK, HG_DV), F32)],
        scratch=[])


def _merge_kernel(ba_ref, wa_ref, bb_ref, wb_ref, ga_ref, gb_ref, o_ref):
    ya = jnp.dot(ba_ref[...], wa_ref[...], preferred_element_type=F32)
    yb = jnp.dot(bb_ref[...], wb_ref[...], preferred_element_type=F32)
    ga = _sigmoid(ga_ref[...].astype(F32))
    gb = _sigmoid(gb_ref[...].astype(F32))
    o_ref[...] = (ga * ya + gb * yb).astype(o_ref.dtype)


def _merge(branch_a, w_a, branch_b, w_b, proj, row0):
    m = branch_a.shape[0]
    tm, tn = _MERGE_TM, _MERGE_TN
    rb0 = row0 // tm
    ga_blk = _P_GATE_A // tn
    gb_blk = _P_GATE_B // tn
    return pl.pallas_call(
        _merge_kernel,
        grid=(D_MODEL // tn, m // tm),
        in_specs=[pl.BlockSpec((tm, ML_V), lambda j, i: (i, 0)),
                  pl.BlockSpec((ML_V, tn), lambda j, i: (0, j)),
                  pl.BlockSpec((tm, HG_V), lambda j, i: (i, 0)),
                  pl.BlockSpec((HG_V, tn), lambda j, i: (0, j)),
                  pl.BlockSpec((tm, tn), lambda j, i: (rb0 + i, ga_blk + j)),
                  pl.BlockSpec((tm, tn), lambda j, i: (rb0 + i, gb_blk + j))],
        out_specs=pl.BlockSpec((tm, tn), lambda j, i: (i, j)),
        out_shape=jax.ShapeDtypeStruct((m, D_MODEL), BF16),
        compiler_params=pltpu.CompilerParams(dimension_semantics=("parallel", "arbitrary")),
        name="merge",
    )(branch_a, w_a, branch_b, w_b, proj, proj)


def _layernorm_rows(z, g, b):
    mu = jnp.mean(z, axis=1, keepdims=True)
    zc = z - mu
    var = jnp.mean(zc * zc, axis=1, keepdims=True)
    return zc * lax.rsqrt(var + LN_EPS) * g + b


def _outproj_kernel(mg_ref, w_ref, x_ref, g_ref, b_ref, x1_ref):
    mix = jnp.dot(mg_ref[...], w_ref[...], preferred_element_type=F32)
    x1_ref[...] = _layernorm_rows(DEEPNORM_ALPHA * x_ref[...] + mix, g_ref[...], b_ref[...])


def _outproj(merged, w_out, x, ln_g, ln_b):
    tm = _OUT_TM
    m = x.shape[0]
    row = lambda i: (i, 0)
    const = lambda i: (0, 0)
    return pl.pallas_call(
        _outproj_kernel,
        grid=(m // tm,),
        in_specs=[pl.BlockSpec((tm, D_MODEL), row),
                  pl.BlockSpec((D_MODEL, D_MODEL), const, pipeline_mode=pl.Buffered(1)),
                  pl.BlockSpec((tm, D_MODEL), row),
                  pl.BlockSpec((1, D_MODEL), const),
                  pl.BlockSpec((1, D_MODEL), const)],
        out_specs=pl.BlockSpec((tm, D_MODEL), row),
        out_shape=jax.ShapeDtypeStruct((m, D_MODEL), F32),
        compiler_params=pltpu.CompilerParams(dimension_semantics=("parallel",)),
        name="out_proj_ln",
    )(merged, w_out, x, ln_g, ln_b)


def _mlp_kernel(wu_ref, wd_ref, x1_ref, g_ref, b_ref, y_ref, x1b_sc):
    f = pl.program_id(1)

    @pl.when(f == 0)
    def _init():
        y_ref[...] = jnp.zeros(y_ref.shape, F32)
        x1b_sc[...] = x1_ref[...].astype(BF16)

    hid = jnp.maximum(jnp.dot(x1b_sc[...], wu_ref[...], preferred_element_type=F32), 0.0)
    hid = (hid * hid).astype(BF16)
    y_ref[...] += jnp.dot(hid, wd_ref[...], preferred_element_type=F32)

    @pl.when(f == pl.num_programs(1) - 1)
    def _finish():
        z = DEEPNORM_ALPHA * x1_ref[...] + y_ref[...]
        y_ref[...] = _layernorm_rows(z, g_ref[...], b_ref[...])


def _mlp(w_up, w_down, x1, ln_g, ln_b):
    tm, tf = _MLP_TM, _MLP_TF
    tokens = x1.shape[0]
    return pl.pallas_call(
        _mlp_kernel,
        grid=(tokens // tm, D_FF // tf),
        in_specs=[pl.BlockSpec((D_MODEL, tf), lambda i, f: (0, f)),
                  pl.BlockSpec((tf, D_MODEL), lambda i, f: (f, 0)),
                  pl.BlockSpec((tm, D_MODEL), lambda i, f: (i, 0), pipeline_mode=pl.Buffered(1)),
                  pl.BlockSpec((1, D_MODEL), lambda i, f: (0, 0)),
                  pl.BlockSpec((1, D_MODEL), lambda i, f: (0, 0))],
        out_specs=pl.BlockSpec((tm, D_MODEL), lambda i, f: (i, 0)),
        out_shape=jax.ShapeDtypeStruct((tokens, D_MODEL), F32),
        scratch_shapes=[pltpu.VMEM((tm, D_MODEL), BF16)],
        compiler_params=pltpu.CompilerParams(dimension_semantics=("parallel", "arbitrary")),
        name="mlp_ln",
    )(w_up, w_down, x1, ln_g, ln_b)


_PROMPT_ML = dict(rows=256, seq_len=256, heads=4, together=1)
_PROMPT_HG = dict(rows=256, seq_len=256, heads=4, together=2, direct=8,
                  levels=((16, 8), (32, 16), (64, 32), (128, 64), (256, 128)))
_SAMPLE_ML = dict(rows=16, seq_len=8, heads=4, together=4)
_SAMPLE_HG = dict(rows=32, seq_len=8, heads=8, together=1, direct=8, levels=())


def kernel(x_prompt, x_sample, state_mlstm_C, state_mlstm_n, state_mlstm_m, state_hgrn_S,
           hg_lb_logits, w_in, b_ig, b_fg, ml_norm_g, hg_norm_g, w_branch_a, w_branch_b, w_out,
           ln1_g, ln1_b, w_up, w_down, ln2_g, ln2_b):
    batch_p, seq_p, _ = x_prompt.shape
    batch_s, seq_s, _ = x_sample.shape
    tok_p, tok_s = batch_p * seq_p, batch_s * seq_s
    total = tok_p + tok_s
    group_p = dict(row0=0, batch=batch_p, seq=seq_p)
    group_s = dict(row0=tok_p, batch=batch_s, seq=seq_s)
    xp = x_prompt.reshape(tok_p, D_MODEL)
    xs = x_sample.reshape(tok_s, D_MODEL)
    lb_logits = hg_lb_logits.astype(F32)
    states_p, states_s = [], []
    for l in range(DEPTH):
        (proj, hg_f, gates, gates_gm), (w_a_b, w_b_b, w_out_b, w_up_b, w_down_b) = _inproj(
            xp, xs.astype(BF16), jnp.swapaxes(w_in[l], 0, 1),
            (w_branch_a[l], w_branch_b[l], w_out[l], w_up[l], w_down[l]))

        gate_bias = jnp.concatenate([b_ig[l], b_fg[l]]).astype(F32)
        brow = jnp.zeros((1, LANES), F32).at[0, :_N_GATES].set(gate_bias)
        bcol = gate_bias.reshape(_N_GATES, 1)
        ml_g = ml_norm_g[l].reshape(1, ML_V).astype(F32)
        hg_g = hg_norm_g[l].reshape(1, HG_V).astype(F32)

        m_rows = jnp.repeat(state_mlstm_m[l].astype(F32), seq_s, axis=0)
        gates_s = gates[tok_p:].at[:, _N_GATES:_N_GATES + ML_HEADS].set(m_rows)
        rows_s = _SAMPLE_ML["rows"]
        gm_s = gates_gm[tok_p // _PROJ_TM:]
        gm_s = gm_s.reshape(-1, _N_GATES, _PROJ_TM // rows_s, rows_s).transpose(0, 2, 1, 3)
        gm_s = gm_s.reshape(tok_s // rows_s, _N_GATES, rows_s)
        state_ml = (state_mlstm_C[l].astype(F32),
                    state_mlstm_n[l].astype(F32).reshape(batch_s, ML_HEADS, 1, ML_DK))
        ln1 = (ln1_g[l].reshape(1, D_MODEL), ln1_b[l].reshape(1, D_MODEL))
        ln2 = (ln2_g[l].reshape(1, D_MODEL), ln2_b[l].reshape(1, D_MODEL))

        ml_p = _mlstm_parts(proj, gates, gates_gm, brow, bcol, ml_g, None, **group_p, **_PROMPT_ML)
        hg_p = _hgrn_parts(proj, hg_f, lb_logits, hg_g, None, layer=l, **group_p, **_PROMPT_HG)

        def guest(parts_fn, host, **kwargs):
            grid = host["grid"]
            assert math.prod(grid) * kwargs["rows"] == tok_s, (grid, kwargs["rows"])
            flat = lambda g, h, c: (g * grid[1] + h) * grid[2] + c
            return parts_fn(index=lambda g, h, c: (flat(g, h, c), 0, flat(g, h, c)), **kwargs)

        ml_s = guest(functools.partial(_mlstm_parts, proj, gates_s, gm_s, brow, bcol, ml_g, state_ml,
                                       **group_s), hg_p, **_SAMPLE_ML)
        hg_s = guest(functools.partial(_hgrn_parts, proj, hg_f, lb_logits, hg_g,
                                       state_hgrn_S[l].astype(F32), layer=l, **group_s),
                     ml_p, **_SAMPLE_HG)
        (bb_p, s_p), (ba_s, c_s, n_s, m_all_s) = _run_parts([hg_p, ml_s], "hgrn2_prompt_mlstm_sample")
        (ba_p, c_p, n_p, m_all_p), (bb_s, s_s) = _run_parts([ml_p, hg_s], "mlstm_prompt_hgrn2_sample")

        merged_p = _merge(ba_p, w_a_b, bb_p, w_b_b, proj, 0)
        merged_s = _merge(ba_s, w_a_b, bb_s, w_b_b, proj, tok_p)
        x1_p = _outproj(merged_p, w_out_b, xp, *ln1)
        x1_s = _outproj(merged_s, w_out_b, xs, *ln1)
        xp = _mlp(w_up_b, w_down_b, x1_p, *ln2)
        xs = _mlp(w_up_b, w_down_b, x1_s, *ln2)

        states_p.append((c_p, n_p.reshape(batch_p, ML_HEADS, ML_DK),
                         m_all_p[:, seq_p - 1::seq_p, 0].T, s_p))
        states_s.append((c_s, n_s.reshape(batch_s, ML_HEADS, ML_DK),
                         m_all_s[:, seq_s - 1::seq_s, 0].T, s_s))
    stack = lambda states, k: jnp.stack([s[k] for s in states])
    return (xp.reshape(batch_p, seq_p, D_MODEL), xs.reshape(batch_s, seq_s, D_MODEL),
            stack(states_p, 0), stack(states_p, 1), stack(states_p, 2), stack(states_p, 3),
            stack(states_s, 0), stack(states_s, 1), stack(states_s, 2), stack(states_s, 3))
```

```python
import functools
import math

import numpy as np
import jax
import jax.numpy as jnp
from jax import lax
from jax.experimental import pallas as pl
from jax.experimental.pallas import tpu as pltpu

F32 = jnp.float32
BF16 = jnp.bfloat16

D_MODEL = 2048
DEPTH = 1
ML_HEADS, ML_DK, ML_DV = 4, 256, 512
HG_HEADS, HG_DK, HG_DV = 8, 128, 256
ML_QK = ML_HEADS * ML_DK
ML_V = ML_HEADS * ML_DV
HG_K = HG_HEADS * HG_DK
HG_V = HG_HEADS * HG_DV
D_FF = 4 * D_MODEL
LN_EPS = 1e-5
DEEPNORM_ALPHA = (2.0 * DEPTH) ** 0.25
ML_K_SCALE = ML_DK ** -0.5
LANES = 128
SUBLANES = 8

_OFF_ML_Q = 0
_OFF_ML_K = _OFF_ML_Q + ML_QK
_OFF_ML_V = _OFF_ML_K + ML_QK
_OFF_ML_I = _OFF_ML_V + ML_V
_OFF_ML_F = _OFF_ML_I + ML_HEADS
_OFF_ML_O = _OFF_ML_F + ML_HEADS
_OFF_HG_Q = _OFF_ML_O + ML_V
_OFF_HG_F = _OFF_HG_Q + HG_K
_OFF_HG_I = _OFF_HG_F + HG_K
_OFF_HG_G = _OFF_HG_I + HG_V
_OFF_GATE_A = _OFF_HG_G + HG_V
_OFF_GATE_B = _OFF_GATE_A + D_MODEL
D_IN = _OFF_GATE_B + D_MODEL

_N_GATES = 2 * ML_HEADS
_P_ML_Q = _OFF_ML_Q
_P_ML_K = _OFF_ML_K
_P_ML_V = _OFF_ML_V
_P_ML_O = _OFF_ML_O - _N_GATES
_P_HG_Q = _OFF_HG_Q - _N_GATES
_P_HG_F = _OFF_HG_F - _N_GATES
_P_HG_I = _OFF_HG_I - _N_GATES
_P_HG_G = _OFF_HG_G - _N_GATES
_P_GATE_A = _OFF_GATE_A - _N_GATES
_P_GATE_B = _OFF_GATE_B - _N_GATES
_P_WIDTH = D_IN - _N_GATES

_PROJ_TM, _PROJ_TN = 1024, 1024
_CAST_COL_BLOCKS = 8
_MERGE_TM, _MERGE_TN = 1024, 1024
_OUT_TM = 512
_MLP_TM, _MLP_TF = 512, 1024


def _sigmoid(x):
    return 1.0 / (1.0 + jnp.exp(-x))


def _log_sigmoid(x):
    return jnp.minimum(x, 0.0) - jnp.log1p(jnp.exp(-jnp.abs(x)))


def _split2(x):
    hi = x.astype(BF16)
    lo = (x - hi.astype(F32)).astype(BF16)
    return hi, lo


def _dot01_right(t01, x):
    hi, lo = _split2(x)
    return (jnp.dot(t01, hi, preferred_element_type=F32)
            + jnp.dot(t01, lo, preferred_element_type=F32))


def _dot01_left(x, t01):
    hi, lo = _split2(x)
    return (jnp.dot(hi, t01, preferred_element_type=F32)
            + jnp.dot(lo, t01, preferred_element_type=F32))


def _dot_nt(a, b):
    return lax.dot_general(a, b, (((1,), (1,)), ((), ())), preferred_element_type=F32)


def _dot_tn(a, b):
    return lax.dot_general(a, b, (((0,), (0,)), ((), ())), preferred_element_type=F32)


def _interleaved(gens, together):
    done = object()
    for g0 in range(0, len(gens), together):
        alive = gens[g0:g0 + together]
        while alive:
            alive = [g for g in alive if next(g, done) is not done]
            if alive:
                yield


def _seq_masks(rows, seq_len):
    ri = lax.broadcasted_iota(jnp.int32, (rows, rows), 0)
    ci = lax.broadcasted_iota(jnp.int32, (rows, rows), 1)
    lower = jnp.where(ci <= ri, 1.0, 0.0)
    upper = jnp.where(ri <= ci, 1.0, 0.0)
    if seq_len == rows:
        return jnp.ones((rows, rows), F32), lower, upper
    shift = seq_len.bit_length() - 1
    same = jnp.where((ri >> shift) == (ci >> shift), 1.0, 0.0)
    return same, same * lower, same * upper


def _inproj_kernel(xp_ref, xs_ref, wt_hbm, *refs, n_f32, n_gate_tile, prompt_tiles, cast_steps):
    n_cast = (len(refs) - 9) // 2
    cast_in = refs[:n_cast]
    o_ref, f_ref, g_ref, gr_ref = refs[n_cast:n_cast + 4]
    cast_out = refs[n_cast + 4:2 * n_cast + 4]
    wbuf, wbf_sc, wg_sc, sem, gsem = refs[2 * n_cast + 4:]
    n = pl.program_id(0)
    m = pl.program_id(1)
    tn = wbf_sc.shape[1]

    @pl.when(m < cast_steps)
    def _cast_other_weights():
        for src, dst in zip(cast_in, cast_out):
            dst[...] = src[...].astype(BF16)

    def tile_copy(tile):
        start = pl.multiple_of(tile * tn + jnp.where(tile >= n_gate_tile, _N_GATES, 0), _N_GATES)
        return pltpu.make_async_copy(wt_hbm.at[pl.ds(start, tn), :], wbuf, sem.at[0])

    def gate_copy():
        return pltpu.make_async_copy(wt_hbm.at[pl.ds(_OFF_ML_I, _N_GATES), :],
                                     wg_sc.at[pl.ds(0, _N_GATES), :], gsem.at[0])

    @pl.when(m == 0)
    def _next_weight_tile():
        @pl.when(n == 0)
        def _first():
            tile_copy(0).start()
            wg_sc[_N_GATES:, :] = jnp.zeros((LANES - _N_GATES, wg_sc.shape[1]), F32)
            gate_copy().start()
            gate_copy().wait()

        tile_copy(n).wait()
        wbf_sc[...] = wbuf[...].T.astype(BF16)

        @pl.when(n + 1 < pl.num_programs(0))
        def _prefetch():
            tile_copy(n + 1).start()

    x = jnp.where(m < prompt_tiles, xp_ref[...].astype(BF16), xs_ref[...])
    acc = jnp.dot(x, wbf_sc[...], preferred_element_type=F32)
    o_ref[...] = acc.astype(BF16)

    @pl.when(n == n_f32)
    def _f32_outputs():
        f_ref[...] = acc
        wg = wg_sc[...].astype(BF16)
        gates = _dot_nt(x, wg)
        g_ref[...] = gates
        gr_ref[0] = gates.T[0:_N_GATES, :]


def _inproj(xp_b, xs_b, w_t, cast_weights):
    k = xp_b.shape[1]
    tm, tn = _PROJ_TM, _PROJ_TN
    p_tiles = xp_b.shape[0] // tm
    n_m = p_tiles + xs_b.shape[0] // tm
    tokens = n_m * tm
    n_n = _P_WIDTH // tn
    n_f32 = _P_HG_F // tn
    cast_steps = min(n_m, _CAST_COL_BLOCKS)

    def parked(n, m):
        return jnp.where(n < n_f32, 0, jnp.where(n == n_f32, m, n_m - 1))

    def cast_spec(w):
        return pl.BlockSpec((w.shape[0] // n_n, w.shape[1] // cast_steps),
                            lambda n, m: (n, jnp.minimum(m, cast_steps - 1)))

    cast_specs = [cast_spec(w) for w in cast_weights]
    outs = pl.pallas_call(
        functools.partial(_inproj_kernel, n_f32=n_f32, n_gate_tile=_OFF_ML_I // tn,
                          prompt_tiles=p_tiles, cast_steps=cast_steps),
        grid=(n_n, n_m),
        in_specs=[pl.BlockSpec((tm, k), lambda n, m: (jnp.minimum(m, p_tiles - 1), 0)),
                  pl.BlockSpec((tm, k), lambda n, m: (jnp.maximum(m - p_tiles, 0), 0),
                               pipeline_mode=pl.Buffered(1)),
                  pl.BlockSpec(memory_space=pl.ANY),
                  *cast_specs],
        out_specs=[pl.BlockSpec((tm, tn), lambda n, m: (m, n)),
                   pl.BlockSpec((tm, HG_K), lambda n, m: (parked(n, m), 0)),
                   pl.BlockSpec((tm, LANES), lambda n, m: (parked(n, m), 0)),
                   pl.BlockSpec((1, _N_GATES, tm), lambda n, m: (parked(n, m), 0, 0)),
                   *cast_specs],
        out_shape=[jax.ShapeDtypeStruct((tokens, _P_WIDTH), BF16),
                   jax.ShapeDtypeStruct((tokens, HG_K), F32),
                   jax.ShapeDtypeStruct((tokens, LANES), F32),
                   jax.ShapeDtypeStruct((n_m, 8, tm), F32),
                   *[jax.ShapeDtypeStruct(w.shape, BF16) for w in cast_weights]],
        scratch_shapes=[pltpu.VMEM((tn, k), F32),
                        pltpu.VMEM((k, tn), BF16),
                        pltpu.VMEM((LANES, k), F32),
                        pltpu.SemaphoreType.DMA((1,)),
                        pltpu.SemaphoreType.DMA((1,))],
        compiler_params=pltpu.CompilerParams(dimension_semantics=("arbitrary", "arbitrary")),
        name="in_proj",
    )(xp_b, xs_b, w_t, *cast_weights)
    return outs[:4], outs[4:]


def _mlstm_steps(*refs, rows, seq_len, heads, together, zero_init, head0=None):
    if zero_init:
        (q_ref, k_ref, v_ref, og_ref, gc_ref, gr_ref, brow_ref, bcol_ref, ng_ref,
         ba_ref, c_ref, n_ref, mrow_ref, m_sc) = refs
        c0_ref, n0_ref = c_ref, n_ref
    else:
        (q_ref, k_ref, v_ref, og_ref, gc_ref, gr_ref, brow_ref, bcol_ref, ng_ref, c0_ref, n0_ref,
         ba_ref, c_ref, n_ref, mrow_ref) = refs
    nseq = rows // seq_len
    if head0 is None:
        head0 = pl.program_id(1) * heads

    if zero_init:
        @pl.when(pl.program_id(2) == 0)
        def _init():
            c_ref[...] = jnp.zeros(c_ref.shape, F32)
            n_ref[...] = jnp.zeros(n_ref.shape, F32)
            m_sc[...] = jnp.zeros(m_sc.shape, F32)

    same, lower, upper = _seq_masks(rows, seq_len)
    causal = lower > 0.5
    lane = lax.broadcasted_iota(jnp.int32, (rows, LANES), 1)
    sub = lax.broadcasted_iota(jnp.int32, (_N_GATES, rows), 0)

    def sel_lane(x, idx):
        return jnp.sum(jnp.where(lane == idx, x, 0.0), axis=1, keepdims=True)

    def sel_sub(x, idx):
        return jnp.sum(jnp.where(sub == idx, x, 0.0), axis=0, keepdims=True)

    gc_raw = gc_ref[...]
    gc = gc_raw + brow_ref[...]
    lf_cols = _log_sigmoid(gc)
    b_cols = _dot01_right(lower.astype(BF16), lf_cols)
    gr = gr_ref[0] + bcol_ref[...]
    b_rows = _dot01_left(_log_sigmoid(gr), upper.astype(BF16))
    if nseq > 1:
        b_ends = _dot01_right(same.astype(BF16), lf_cols)
        ci = lax.broadcasted_iota(jnp.int32, (rows, rows), 1)
        last = (same * jnp.where((ci & (seq_len - 1)) == seq_len - 1, 1.0, 0.0)).astype(BF16)
        shift = seq_len.bit_length() - 1
        row_seq = lax.broadcasted_iota(jnp.int32, (rows, 1), 0) >> shift

    def head_steps(hh):
        head = head0 + hh
        ks = slice(hh * ML_DK, (hh + 1) * ML_DK)
        vs = slice(hh * ML_DV, (hh + 1) * ML_DV)
        ig_col = sel_lane(gc, head)
        b_col = sel_lane(b_cols, head + ML_HEADS)
        if zero_init:
            m_prev = jnp.broadcast_to(m_sc[hh:hh + 1, 0:1], (rows, 1))
        else:
            m_prev = sel_lane(gc_raw, head + 2 * ML_HEADS)
        ig_row = sel_sub(gr, head)
        b_row = sel_sub(b_rows, head + ML_HEADS)

        logd = jnp.where(causal, (b_col - b_row) + ig_row, -jnp.inf)
        m_t = jnp.maximum(b_col + m_prev, jnp.max(logd, axis=1, keepdims=True))
        d = jnp.exp(logd - m_t)
        w_inter = jnp.exp(b_col + m_prev - m_t)
        yield

        qb = q_ref[:, ks].astype(BF16)
        kb = k_ref[:, ks].astype(BF16)
        vb = v_ref[:, vs].astype(BF16)
        s = _dot_nt(qb, kb) * (d * ML_K_SCALE)
        num = jnp.dot(s.astype(BF16), vb, preferred_element_type=F32)
        den = jnp.sum(s, axis=1, keepdims=True)

        if nseq == 1:
            b_end = b_col[rows - 1:rows, :]
            m_new = m_t[rows - 1:rows, :]
        else:
            b_end = sel_lane(b_ends, head + ML_HEADS)
            m_new = _dot01_right(last, jnp.broadcast_to(m_t, (rows, LANES)))[:, 0:1]
        w_end = jnp.exp(b_end - b_col + ig_col - m_new)
        decay = jnp.exp(b_end + m_prev - m_new)

        qf = qb.astype(F32)
        kw = (w_end * ML_K_SCALE) * kb.astype(F32)
        yield

        if nseq == 1:
            c_prev = c0_ref[0, hh]
            n_prev = n0_ref[0, hh]
            q_c = jnp.dot(qb, c_prev.astype(BF16), preferred_element_type=F32)
            q_n = jnp.sum(qf * n_prev, axis=1, keepdims=True)
            dec = decay[0:1, :]
            c_ref[0, hh] = dec * c_prev + _dot_tn(kw.astype(BF16), vb)
            n_ref[0, hh] = dec * n_prev + jnp.sum(kw, axis=0, keepdims=True)
        else:
            q_c = jnp.zeros((rows, ML_DV), F32)
            q_n = jnp.zeros((rows, 1), F32)
            for j in range(nseq):
                in_seq = row_seq == j
                c_prev = c0_ref[j, hh]
                n_prev = n0_ref[j, hh]
                q_c = jnp.where(
                    in_seq, jnp.dot(qb, c_prev.astype(BF16), preferred_element_type=F32), q_c)
                q_n = jnp.where(in_seq, jnp.sum(qf * n_prev, axis=1, keepdims=True), q_n)
                kw_j = jnp.where(in_seq, kw, 0.0)
                dec = decay[j * seq_len:j * seq_len + 1, :]
                c_ref[j, hh] = dec * c_prev + _dot_tn(kw_j.astype(BF16), vb)
                n_ref[j, hh] = dec * n_prev + jnp.sum(kw_j, axis=0, keepdims=True)
                yield
        if nseq == 1:
            yield

        num = num + w_inter * q_c
        den = den + w_inter * q_n
        h_out = num / jnp.maximum(jnp.abs(den), jnp.exp(-m_t))
        mu =jnp.mean(h_out, axis=1, keepdims=True)
        xc = h_out - mu
        var = jnp.mean(xc * xc, axis=1, keepdims=True)
        hn = xc * lax.rsqrt(var + LN_EPS) * ng_ref[:, vs]
        ba_ref[:, vs] = (hn * _sigmoid(og_ref[:, vs].astype(F32))).astype(ba_ref.dtype)
        mrow_ref[hh] = jnp.broadcast_to(m_t, (rows, LANES))
        if zero_init:
            m_sc[hh:hh + 1, :] = jnp.broadcast_to(m_new, (1, LANES))
        yield

    yield from _interleaved([head_steps(hh) for hh in range(heads)], together)


def _mlstm_parts(proj, gates_col, gates_row, brow, bcol, norm_g, state, *, row0, batch, seq, rows,
                 seq_len, heads, together, index=None):
    tokens = batch * seq
    zero_init = state is None
    nseq = rows // seq_len
    nchunk = seq // seq_len if nseq == 1 else 1
    ngroup = tokens // (rows * nchunk)
    bq, bv = heads * ML_DK, heads * ML_DV
    rb0 = row0 // rows
    per_row_tile = gates_row.shape[2] // rows
    head0 = None
    if index is None:
        index = lambda g, h, c: (g * nchunk + c, h, g)
    else:
        assert heads == ML_HEADS
        head0 = 0
    row = lambda *ids: index(*ids)[0]
    hblk = lambda *ids: index(*ids)[1]
    sblk = lambda *ids: index(*ids)[2]

    in_specs = [
        pl.BlockSpec((rows, bq), lambda *ids: (rb0 + row(*ids), _P_ML_Q // bq + hblk(*ids))),
        pl.BlockSpec((rows, bq), lambda *ids: (rb0 + row(*ids), _P_ML_K // bq + hblk(*ids))),
        pl.BlockSpec((rows, bv), lambda *ids: (rb0 + row(*ids), _P_ML_V // bv + hblk(*ids))),
        pl.BlockSpec((rows, bv), lambda *ids: (rb0 + row(*ids), _P_ML_O // bv + hblk(*ids))),
        pl.BlockSpec((rows, LANES), lambda *ids: (row(*ids), 0)),
        pl.BlockSpec((1, _N_GATES, rows),
                     lambda *ids: (row(*ids) // per_row_tile, 0, row(*ids) % per_row_tile)),
        pl.BlockSpec((1, LANES), lambda *ids: (0, 0)),
        pl.BlockSpec((_N_GATES, 1), lambda *ids: (0, 0)),
        pl.BlockSpec((1, bv), lambda *ids: (0, hblk(*ids))),
    ]
    args = [proj, proj, proj, proj, gates_col, gates_row, brow, bcol, norm_g]
    state_specs = [
        pl.BlockSpec((nseq, heads, ML_DK, ML_DV), lambda *ids: (sblk(*ids), hblk(*ids), 0, 0)),
        pl.BlockSpec((nseq, heads, 1, ML_DK), lambda *ids: (sblk(*ids), hblk(*ids), 0, 0)),
    ]
    scratch = []
    if zero_init:
        scratch = [pltpu.VMEM((SUBLANES, LANES), F32)]
    else:
        in_specs += state_specs
        args += [state[0], state[1]]
    out_specs = [
        pl.BlockSpec((rows, bv), lambda *ids: (row(*ids), hblk(*ids))),
        *state_specs,
        pl.BlockSpec((heads, rows, LANES), lambda *ids: (hblk(*ids), row(*ids), 0)),
    ]
    out_shape = [
        jax.ShapeDtypeStruct((tokens, ML_V), BF16),
        jax.ShapeDtypeStruct((batch, ML_HEADS, ML_DK, ML_DV), F32),
        jax.ShapeDtypeStruct((batch, ML_HEADS, 1, ML_DK), F32),
        jax.ShapeDtypeStruct((ML_HEADS, tokens, LANES), F32),
    ]
    return dict(
        steps=functools.partial(_mlstm_steps, rows=rows, seq_len=seq_len, heads=heads,
                                together=together, zero_init=zero_init, head0=head0),
        n_steps=heads // together * (3 + nseq),
        grid=(ngroup, ML_HEADS // heads, nchunk),
        in_specs=in_specs, args=args, out_specs=out_specs, out_shape=out_shape, scratch=scratch)


def _run_parts(parts, name):
    n_in = [len(p["in_specs"]) for p in parts]
    n_out = [len(p["out_specs"]) for p in parts]
    n_scr = [len(p["scratch"]) for p in parts]

    def kernel(*refs):
        ins = refs[:sum(n_in)]
        outs = refs[sum(n_in):sum(n_in) + sum(n_out)]
        scr = refs[sum(n_in) + sum(n_out):]
        gens = []
        i = o = s = 0
        for p, ni, no, ns in zip(parts, n_in, n_out, n_scr):
            gens.append(p["steps"](*ins[i:i + ni], *outs[o:o + no], *scr[s:s + ns]))
            i, o, s = i + ni, o + no, s + ns
        order = sorted(((j + 0.5) / p["n_steps"], k) for k, p in enumerate(parts)
                       for j in range(p["n_steps"]))
        for _, k in order:
            next(gens[k], None)
        for gen in gens:
            for _ in gen:
                pass

    outs = pl.pallas_call(
        kernel,
        grid=parts[0]["grid"],
        in_specs=[s for p in parts for s in p["in_specs"]],
        out_specs=[s for p in parts for s in p["out_specs"]],
        out_shape=[s for p in parts for s in p["out_shape"]],
        scratch_shapes=[s for p in parts for s in p["scratch"]],
        compiler_params=pltpu.CompilerParams(
            dimension_semantics=("parallel", "parallel", "arbitrary")),
        name=name,
    )(*[a for p in parts for a in p["args"]])
    split, o = [], 0
    for no in n_out:
        split.append(outs[o:o + no])
        o += no
    return split


def _hgrn_level_ids(rows, seq_len, direct, levels):
    t = np.arange(rows)[:, None]
    s = np.arange(rows)[None, :]
    ids = np.full((rows, rows), -1, np.int32)
    count = ((t // direct == s // direct) & (s <= t)).astype(np.int32)
    for idx, (block, sub_size) in enumerate(levels):
        owned = (t // block == s // block) & ((s % block) // sub_size < (t % block) // sub_size)
        ids[owned] = idx
        count += owned
    wanted = (t // seq_len == s // seq_len) & (s <= t)
    assert np.array_equal(count, wanted.astype(np.int32)), (rows, seq_len, direct, levels)
    return ids


_INTRA_YIELD_BLOCKS = 32


def _hgrn_intra_yields(rows, direct, levels):
    return 2 + len(levels) + 2 * (max(rows // direct // _INTRA_YIELD_BLOCKS, 1) - 1)


def _hgrn_intra(q, kin, g2, gk2, level_masks, rows, direct, levels):
    ngroups = max(rows // LANES, 1)
    rowi = lax.broadcasted_iota(jnp.int32, (direct, LANES), 0)
    lanei = lax.broadcasted_iota(jnp.int32, (direct, LANES), 1)
    keep = [jnp.where(lanei == s, rowi, -1) >= s for s in range(direct)]
    zero_group = jnp.zeros((direct, LANES), F32)
    prods = []
    for blk in range(rows // direct):
        r0 = blk * direct
        qb = q[r0:r0 + direct]
        gb = g2[r0:r0 + direct]
        gkb = gk2[r0:r0 + direct]
        for s in range(direct):
            prods.append(qb * jnp.exp2(gb - gkb[s:s + 1, :]))
        if (blk + 1) % _INTRA_YIELD_BLOCKS == 0 and blk + 1 < rows // direct:
            yield
    sums = jnp.dot(jnp.concatenate(prods, axis=0).astype(BF16), jnp.ones((HG_DK, LANES), BF16),
                   preferred_element_type=F32)
    yield
    panels = []
    for blk in range(rows // direct):
        r0 = blk * direct
        ag = zero_group
        for s in range(direct):
            p0 = (blk * direct + s) * direct
            ag = jnp.where(keep[s], sums[p0:p0 + direct], ag)
        lane0 = r0 % LANES
        if lane0:
            ag = pltpu.roll(ag, lane0, axis=1)
        grp = r0 // LANES
        pieces = [zero_group] * grp + [ag] + [zero_group] * (ngroups - grp - 1)
        panels.append(pieces[0] if ngroups == 1 else jnp.concatenate(pieces, axis=1))
        if (blk + 1) % _INTRA_YIELD_BLOCKS == 0 and blk + 1 < rows // direct:
            yield
    a = panels[0] if len(panels) == 1 else jnp.concatenate(panels, axis=0)
    if rows < LANES:
        a = a[:, :rows]
    yield

    if levels:
        rowid = lax.broadcasted_iota(jnp.int32, (rows, 1), 0)
    for idx, (block, sub_size) in enumerate(levels):
        sshift = sub_size.bit_length() - 1
        row_sub = (rowid & (block - 1)) >> sshift
        q_parts, k_parts = [], []
        for j in range(1, block // sub_size):
            refs = []
            for b0 in range(0, rows, block):
                r = b0 + j * sub_size - 1
                refs.append(jnp.broadcast_to(g2[r:r + 1, :], (block, HG_DK)))
            g_ref = refs[0] if len(refs) == 1 else jnp.concatenate(refs, axis=0)
            e = jnp.exp2(-jnp.abs(g2 - g_ref))
            q_parts.append(jnp.where(row_sub == j, q * e, 0.0).astype(BF16))
            k_parts.append((kin * e).astype(BF16))
        qcat = q_parts[0] if len(q_parts) == 1 else jnp.concatenate(q_parts, axis=1)
        kcat = k_parts[0] if len(k_parts) == 1 else jnp.concatenate(k_parts, axis=1)
        a = jnp.where(level_masks[idx], _dot_nt(qcat, kcat), a)
        yield
    return a


def _hgrn_steps(*refs, layer, rows, seq_len, heads, together, direct, levels, zero_init):
    if zero_init:
        q_ref, f_ref, i_ref, g_ref, lbl_ref, ng_ref, lvl_ref, bb_ref, s_ref = refs
        s0_ref = s_ref
    else:
        q_ref, f_ref, i_ref, g_ref, lbl_ref, ng_ref, lvl_ref, s0_ref, bb_ref, s_ref = refs
    nseq = rows // seq_len

    if zero_init:
        @pl.when(pl.program_id(2) == 0)
        def _init():
            s_ref[...] = jnp.zeros(s_ref.shape, F32)

    same, lower, _ = _seq_masks(rows, seq_len)
    lower_b = lower.astype(BF16)
    same_b = same.astype(BF16)
    level_ids = lvl_ref[...]
    level_masks = [level_ids == idx for idx in range(len(levels))]
    if nseq > 1:
        shift = seq_len.bit_length() - 1
        row_seq = lax.broadcasted_iota(jnp.int32, (rows, 1), 0) >> shift

    def decay_cols(row):
        col = jnp.broadcast_to(jnp.exp2(row), (HG_DK, HG_DK)).T
        return jnp.concatenate([col] * (HG_DV // HG_DK), axis=1)

    def head_steps(hh):
        ks = slice(hh * HG_DK, (hh + 1) * HG_DK)
        vs = slice(hh * HG_DV, (hh + 1) * HG_DV)
        lg = lbl_ref[:, ks]
        ex = jnp.exp(lg - jnp.max(lg, axis=0, keepdims=True))
        lb = (jnp.sum(ex[0:layer + 1, :], axis=0, keepdims=True)
              / jnp.sum(ex, axis=0, keepdims=True))

        f = lb + (1.0 - lb) * _sigmoid(f_ref[:, ks])
        kin = 1.0 - f
        lf2 = jnp.log2(f)
        g2 = _dot01_right(lower_b, lf2)
        gk2 = g2 - jnp.log2(kin)
        yield
        q = q_ref[:, ks].astype(F32)
        if nseq == 1:
            g_end = jnp.broadcast_to(g2[rows - 1:rows, :], (rows, HG_DK))
        else:
            g_end = _dot01_right(same_b, lf2)
        qg = (q * jnp.exp2(g2)).astype(BF16)
        kg = kin * jnp.exp2(g_end - g2)
        ib = i_ref[:, vs].astype(BF16)
        yield

        a = yield from _hgrn_intra(q, kin, g2, gk2, level_masks, rows, direct, levels)
        o = jnp.dot(a.astype(BF16), ib, preferred_element_type=F32)

        if nseq == 1:
            s_prev = s0_ref[0, hh]
            o = o + jnp.dot(qg, s_prev.astype(BF16), preferred_element_type=F32)
            s_ref[0, hh] = decay_cols(g_end[0:1, :]) * s_prev + _dot_tn(kg.astype(BF16), ib)
            yield
        else:
            for j in range(nseq):
                in_seq = row_seq == j
                s_prev = s0_ref[j, hh]
                o_j = jnp.dot(qg, s_prev.astype(BF16), preferred_element_type=F32)
                o = o + jnp.where(in_seq, o_j, 0.0)
                kg_j = jnp.where(in_seq, kg, 0.0).astype(BF16)
                s_ref[j, hh] = (decay_cols(g_end[j * seq_len:j * seq_len + 1, :]) * s_prev
                                + _dot_tn(kg_j, ib))
                yield

        o = o * lax.rsqrt(jnp.mean(o * o, axis=1, keepdims=True) + LN_EPS)
        gate = g_ref[:, vs].astype(F32)
        bb_ref[:, vs] = (o * ng_ref[:, vs] * (gate * _sigmoid(gate))).astype(bb_ref.dtype)
        yield

    yield from _interleaved([head_steps(hh) for hh in range(heads)], together)


def _hgrn_parts(proj, hg_f, lb_logits, norm_g, s0, *, layer, row0, batch, seq, rows, seq_len, heads,
                together, direct, levels, index=None):
    tokens = batch * seq
    zero_init = s0 is None
    nseq = rows // seq_len
    nchunk = seq // seq_len if nseq == 1 else 1
    ngroup = tokens // (rows * nchunk)
    rb0 = row0 // rows
    if index is None:
        index = lambda g, h, c: (g * nchunk + c, h, g)
    row = lambda *ids: index(*ids)[0]
    hblk = lambda *ids: index(*ids)[1]
    sblk = lambda *ids: index(*ids)[2]

    wk, wv = heads * HG_DK, heads * HG_DV
    in_specs = [
        pl.BlockSpec((rows, wk), lambda *ids: (rb0 + row(*ids), _P_HG_Q // wk + hblk(*ids))),
        pl.BlockSpec((rows, wk), lambda *ids: (rb0 + row(*ids), hblk(*ids))),
        pl.BlockSpec((rows, wv), lambda *ids: (rb0 + row(*ids), _P_HG_I // wv + hblk(*ids))),
        pl.BlockSpec((rows, wv), lambda *ids: (rb0 + row(*ids), _P_HG_G // wv + hblk(*ids))),
        pl.BlockSpec((DEPTH + 1, wk), lambda *ids: (0, hblk(*ids))),
        pl.BlockSpec((1, wv), lambda *ids: (0, hblk(*ids))),
        pl.BlockSpec((rows, rows), lambda *ids: (0, 0)),
    ]
    level_ids = jnp.asarray(_hgrn_level_ids(rows, seq_len, direct, levels))
    args = [proj, hg_f, proj, proj, lb_logits, norm_g, level_ids]
    state_spec = pl.BlockSpec((nseq, heads, HG_DK, HG_DV),
                              lambda *ids: (sblk(*ids), hblk(*ids), 0, 0))
    if not zero_init:
        in_specs.append(state_spec)
        args.append(s0)
    return dict(
        steps=functools.partial(_hgrn_steps, layer=layer, rows=rows, seq_len=seq_len, heads=heads,
                                together=together, direct=direct, levels=levels,
                                zero_init=zero_init),
        n_steps=heads // together * (3 + _hgrn_intra_yields(rows, direct, levels) + nseq),
        grid=(ngroup, HG_HEADS // heads, nchunk),
        in_specs=in_specs, args=args,
        out_specs=[pl.BlockSpec((rows, wv), lambda *ids: (row(*ids), hblk(*ids))), state_spec],
        out_shape=[jax.ShapeDtypeStruct((tokens, HG_V), BF16),
                   jax.ShapeDtypeStruct((batch, HG_HEADS, HG_DK, HG_DV), F32)],
        scratch=[])


def _merge_kernel(ba_ref, wa_ref, bb_ref, wb_ref, ga_ref, gb_ref, o_ref):
    ya = jnp.dot(ba_ref[...], wa_ref[...], preferred_element_type=F32)
    yb = jnp.dot(bb_ref[...], wb_ref[...], preferred_element_type=F32)
    ga = _sigmoid(ga_ref[...].astype(F32))
    gb = _sigmoid(gb_ref[...].astype(F32))
    o_ref[...] = (ga * ya + gb * yb).astype(o_ref.dtype)


def _merge(branch_a, w_a, branch_b, w_b, proj, row0):
    m = branch_a.shape[0]
    tm, tn = _MERGE_TM, _MERGE_TN
    rb0 = row0 // tm
    ga_blk = _P_GATE_A // tn
    gb_blk = _P_GATE_B // tn
    return pl.pallas_call(
        _merge_kernel,
        grid=(D_MODEL // tn, m // tm),
        in_specs=[pl.BlockSpec((tm, ML_V), lambda j, i: (i, 0)),
                  pl.BlockSpec((ML_V, tn), lambda j, i: (0, j)),
                  pl.BlockSpec((tm, HG_V), lambda j, i: (i, 0)),
                  pl.BlockSpec((HG_V, tn), lambda j, i: (0, j)),
                  pl.BlockSpec((tm, tn), lambda j, i: (rb0 + i, ga_blk + j)),
                  pl.BlockSpec((tm, tn), lambda j, i: (rb0 + i, gb_blk + j))],
        out_specs=pl.BlockSpec((tm, tn), lambda j, i: (i, j)),
        out_shape=jax.ShapeDtypeStruct((m, D_MODEL), BF16),
        compiler_params=pltpu.CompilerParams(dimension_semantics=("parallel", "arbitrary")),
        name="merge",
    )(branch_a, w_a, branch_b, w_b, proj, proj)


def _layernorm_rows(z, g, b):
    mu = jnp.mean(z, axis=1, keepdims=True)
    zc = z - mu
    var = jnp.mean(zc * zc, axis=1, keepdims=True)
    return zc * lax.rsqrt(var + LN_EPS) * g + b


def _outproj_kernel(mg_ref, w_ref, x_ref, g_ref, b_ref, x1_ref):
    mix = jnp.dot(mg_ref[...], w_ref[...], preferred_element_type=F32)
    x1_ref[...] = _layernorm_rows(DEEPNORM_ALPHA * x_ref[...] + mix, g_ref[...], b_ref[...])


def _outproj(merged, w_out, x, ln_g, ln_b):
    tm = _OUT_TM
    m = x.shape[0]
    row = lambda i: (i, 0)
    const = lambda i: (0, 0)
    return pl.pallas_call(
        _outproj_kernel,
        grid=(m // tm,),
        in_specs=[pl.BlockSpec((tm, D_MODEL), row),
                  pl.BlockSpec((D_MODEL, D_MODEL), const),
                  pl.BlockSpec((tm, D_MODEL), row),
                  pl.BlockSpec((1, D_MODEL), const),
                  pl.BlockSpec((1, D_MODEL), const)],
        out_specs=pl.BlockSpec((tm, D_MODEL), row),
        out_shape=jax.ShapeDtypeStruct((m, D_MODEL), F32),
        compiler_params=pltpu.CompilerParams(dimension_semantics=("parallel",)),
        name="out_proj_ln",
    )(merged, w_out, x, ln_g, ln_b)


def _mlp_kernel(wu_ref, wd_ref, x1_ref, g_ref, b_ref, y_ref, x1b_sc):
    f = pl.program_id(1)

    @pl.when(f == 0)
    def _init():
        y_ref[...] = jnp.zeros(y_ref.shape, F32)
        x1b_sc[...] = x1_ref[...].astype(BF16)

    hid = jnp.maximum(jnp.dot(x1b_sc[...], wu_ref[...], preferred_element_type=F32), 0.0)
    hid = (hid * hid).astype(BF16)
    y_ref[...] += jnp.dot(hid, wd_ref[...], preferred_element_type=F32)

    @pl.when(f == pl.num_programs(1) - 1)
    def _finish():
        z = DEEPNORM_ALPHA * x1_ref[...] + y_ref[...]
        y_ref[...] = _layernorm_rows(z, g_ref[...], b_ref[...])


def _mlp(w_up, w_down, x1, ln_g, ln_b):
    tm, tf = _MLP_TM, _MLP_TF
    tokens = x1.shape[0]
    return pl.pallas_call(
        _mlp_kernel,
        grid=(tokens // tm, D_FF // tf),
        in_specs=[pl.BlockSpec((D_MODEL, tf), lambda i, f: (0, f)),
                  pl.BlockSpec((tf, D_MODEL), lambda i, f: (f, 0)),
                  pl.BlockSpec((tm, D_MODEL), lambda i, f: (i, 0)),
                  pl.BlockSpec((1, D_MODEL), lambda i, f: (0, 0)),
                  pl.BlockSpec((1, D_MODEL), lambda i, f: (0, 0))],
        out_specs=pl.BlockSpec((tm, D_MODEL), lambda i, f: (i, 0)),
        out_shape=jax.ShapeDtypeStruct((tokens, D_MODEL), F32),
        scratch_shapes=[pltpu.VMEM((tm, D_MODEL), BF16)],
        compiler_params=pltpu.CompilerParams(dimension_semantics=("parallel", "arbitrary")),
        name="mlp_ln",
    )(w_up, w_down, x1, ln_g, ln_b)


_PROMPT_ML = dict(rows=256, seq_len=256, heads=4, together=1)
_PROMPT_HG = dict(rows=256, seq_len=256, heads=4, together=2, direct=8,
                  levels=((16, 8), (32, 16), (64, 32), (128, 64), (256, 128)))
_SAMPLE_ML = dict(rows=16, seq_len=8, heads=4, together=4)
_SAMPLE_HG = dict(rows=32, seq_len=8, heads=8, together=1, direct=8, levels=())


def kernel(x_prompt, x_sample, state_mlstm_C, state_mlstm_n, state_mlstm_m, state_hgrn_S,
           hg_lb_logits, w_in, b_ig, b_fg, ml_norm_g, hg_norm_g, w_branch_a, w_branch_b, w_out,
           ln1_g, ln1_b, w_up, w_down, ln2_g, ln2_b):
    batch_p, seq_p, _ = x_prompt.shape
    batch_s, seq_s, _ = x_sample.shape
    tok_p, tok_s = batch_p * seq_p, batch_s * seq_s
    total = tok_p + tok_s
    group_p = dict(row0=0, batch=batch_p, seq=seq_p)
    group_s = dict(row0=tok_p, batch=batch_s, seq=seq_s)
    xp = x_prompt.reshape(tok_p, D_MODEL)
    xs = x_sample.reshape(tok_s, D_MODEL)
    lb_logits = hg_lb_logits.astype(F32)
    states_p, states_s = [], []
    for l in range(DEPTH):
        (proj, hg_f, gates, gates_gm), (w_a_b, w_b_b, w_out_b, w_up_b, w_down_b) = _inproj(
            xp, xs.astype(BF16), jnp.swapaxes(w_in[l], 0, 1),
            (w_branch_a[l], w_branch_b[l], w_out[l], w_up[l], w_down[l]))

        gate_bias = jnp.concatenate([b_ig[l], b_fg[l]]).astype(F32)
        brow = jnp.zeros((1, LANES), F32).at[0, :_N_GATES].set(gate_bias)
        bcol = gate_bias.reshape(_N_GATES, 1)
        ml_g = ml_norm_g[l].reshape(1, ML_V).astype(F32)
        hg_g = hg_norm_g[l].reshape(1, HG_V).astype(F32)

        m_rows = jnp.repeat(state_mlstm_m[l].astype(F32), seq_s, axis=0)
        gates_s = gates[tok_p:].at[:, _N_GATES:_N_GATES + ML_HEADS].set(m_rows)
        rows_s = _SAMPLE_ML["rows"]
        gm_s = gates_gm[tok_p // _PROJ_TM:]
        gm_s = gm_s.reshape(-1, _N_GATES, _PROJ_TM // rows_s, rows_s).transpose(0, 2, 1, 3)
        gm_s = gm_s.reshape(tok_s // rows_s, _N_GATES, rows_s)
        state_ml = (state_mlstm_C[l].astype(F32),
                    state_mlstm_n[l].astype(F32).reshape(batch_s, ML_HEADS, 1, ML_DK))
        ln1 = (ln1_g[l].reshape(1, D_MODEL), ln1_b[l].reshape(1, D_MODEL))
        ln2 = (ln2_g[l].reshape(1, D_MODEL), ln2_b[l].reshape(1, D_MODEL))

        ml_p = _mlstm_parts(proj, gates, gates_gm, brow, bcol, ml_g, None, **group_p, **_PROMPT_ML)
        hg_p = _hgrn_parts(proj, hg_f, lb_logits, hg_g, None, layer=l, **group_p, **_PROMPT_HG)

        def guest(parts_fn, host, **kwargs):
            grid = host["grid"]
            assert math.prod(grid) * kwargs["rows"] == tok_s, (grid, kwargs["rows"])
            flat = lambda g, h, c: (g * grid[1] + h) * grid[2] + c
            return parts_fn(index=lambda g, h, c: (flat(g, h, c), 0, flat(g, h, c)), **kwargs)

        ml_s = guest(functools.partial(_mlstm_parts, proj, gates_s, gm_s, brow, bcol, ml_g, state_ml,
                                       **group_s), hg_p, **_SAMPLE_ML)
        hg_s = guest(functools.partial(_hgrn_parts, proj, hg_f, lb_logits, hg_g,
                                       state_hgrn_S[l].astype(F32), layer=l, **group_s),
                     ml_p, **_SAMPLE_HG)
        (bb_p, s_p), (ba_s, c_s, n_s, m_all_s) = _run_parts([hg_p, ml_s], "hgrn2_prompt_mlstm_sample")
        (ba_p, c_p, n_p, m_all_p), (bb_s, s_s) = _run_parts([ml_p, hg_s], "mlstm_prompt_hgrn2_sample")

        merged_p = _merge(ba_p, w_a_b, bb_p, w_b_b, proj, 0)
        merged_s = _merge(ba_s, w_a_b, bb_s, w_b_b, proj, tok_p)
        x1_p = _outproj(merged_p, w_out_b, xp, *ln1)
        x1_s = _outproj(merged_s, w_out_b, xs, *ln1)
        xp = _mlp(w_up_b, w_down_b, x1_p, *ln2)
        xs = _mlp(w_up_b, w_down_b, x1_s, *ln2)

        states_p.append((c_p, n_p.reshape(batch_p, ML_HEADS, ML_DK),
                         m_all_p[:, seq_p - 1::seq_p, 0].T, s_p))
        states_s.append((c_s, n_s.reshape(batch_s, ML_HEADS, ML_DK),
                         m_all_s[:, seq_s - 1::seq_s, 0].T, s_s))
    stack = lambda states, k: jnp.stack([s[k] for s in states])
    return (xp.reshape(batch_p, seq_p, D_MODEL), xs.reshape(batch_s, seq_s, D_MODEL),
            stack(states_p, 0), stack(states_p, 1), stack(states_p, 2), stack(states_p, 3),
            stack(states_s, 0), stack(states_s, 1), stack(states_s, 2), stack(states_s, 3))
```

```python
import functools
import math

import numpy as np
import jax
import jax.numpy as jnp
from jax import lax
from jax.experimental import pallas as pl
from jax.experimental.pallas import tpu as pltpu

F32 = jnp.float32
BF16 = jnp.bfloat16

D_MODEL = 2048
DEPTH = 1
ML_HEADS, ML_DK, ML_DV = 4, 256, 512
HG_HEADS, HG_DK, HG_DV = 8, 128, 256
ML_QK = ML_HEADS * ML_DK
ML_V = ML_HEADS * ML_DV
HG_K = HG_HEADS * HG_DK
HG_V = HG_HEADS * HG_DV
D_FF = 4 * D_MODEL
LN_EPS = 1e-5
DEEPNORM_ALPHA = (2.0 * DEPTH) ** 0.25
ML_K_SCALE = ML_DK ** -0.5
LANES = 128
SUBLANES = 8

_OFF_ML_Q = 0
_OFF_ML_K = _OFF_ML_Q + ML_QK
_OFF_ML_V = _OFF_ML_K + ML_QK
_OFF_ML_I = _OFF_ML_V + ML_V
_OFF_ML_F = _OFF_ML_I + ML_HEADS
_OFF_ML_O = _OFF_ML_F + ML_HEADS
_OFF_HG_Q = _OFF_ML_O + ML_V
_OFF_HG_F = _OFF_HG_Q + HG_K
_OFF_HG_I = _OFF_HG_F + HG_K
_OFF_HG_G = _OFF_HG_I + HG_V
_OFF_GATE_A = _OFF_HG_G + HG_V
_OFF_GATE_B = _OFF_GATE_A + D_MODEL
D_IN = _OFF_GATE_B + D_MODEL

_N_GATES = 2 * ML_HEADS
_P_ML_Q = _OFF_ML_Q
_P_ML_K = _OFF_ML_K
_P_ML_V = _OFF_ML_V
_P_ML_O = _OFF_ML_O - _N_GATES
_P_HG_Q = _OFF_HG_Q - _N_GATES
_P_HG_F = _OFF_HG_F - _N_GATES
_P_HG_I = _OFF_HG_I - _N_GATES
_P_HG_G = _OFF_HG_G - _N_GATES
_P_GATE_A = _OFF_GATE_A - _N_GATES
_P_GATE_B = _OFF_GATE_B - _N_GATES
_P_WIDTH = D_IN - _N_GATES

_PROJ_TM, _PROJ_TN = 1024, 1024
_CAST_COL_BLOCKS = 8
_MERGE_TM, _MERGE_TN = 1024, 1024
_OUT_TM = 512
_MLP_TM, _MLP_TF = 512, 1024


def _sigmoid(x):
    return 1.0 / (1.0 + jnp.exp(-x))


def _log_sigmoid(x):
    return jnp.minimum(x, 0.0) - jnp.log1p(jnp.exp(-jnp.abs(x)))


def _split2(x):
    hi = x.astype(BF16)
    lo = (x - hi.astype(F32)).astype(BF16)
    return hi, lo


def _dot01_right(t01, x):
    hi, lo = _split2(x)
    return (jnp.dot(t01, hi, preferred_element_type=F32)
            + jnp.dot(t01, lo, preferred_element_type=F32))


def _dot01_left(x, t01):
    hi, lo = _split2(x)
    return (jnp.dot(hi, t01, preferred_element_type=F32)
            + jnp.dot(lo, t01, preferred_element_type=F32))


def _dot_nt(a, b):
    return lax.dot_general(a, b, (((1,), (1,)), ((), ())), preferred_element_type=F32)


def _dot_tn(a, b):
    return lax.dot_general(a, b, (((0,), (0,)), ((), ())), preferred_element_type=F32)


def _interleaved(gens, together):
    done = object()
    for g0 in range(0, len(gens), together):
        alive = gens[g0:g0 + together]
        while alive:
            alive = [g for g in alive if next(g, done) is not done]
            if alive:
                yield


def _seq_masks(rows, seq_len):
    ri = lax.broadcasted_iota(jnp.int32, (rows, rows), 0)
    ci = lax.broadcasted_iota(jnp.int32, (rows, rows), 1)
    lower = jnp.where(ci <= ri, 1.0, 0.0)
    upper = jnp.where(ri <= ci, 1.0, 0.0)
    if seq_len == rows:
        return jnp.ones((rows, rows), F32), lower, upper
    shift = seq_len.bit_length() - 1
    same = jnp.where((ri >> shift) == (ci >> shift), 1.0, 0.0)
    return same, same * lower, same * upper


def _inproj_kernel(xp_ref, xs_ref, wt_hbm, *refs, n_f32, n_gate_tile, prompt_tiles, cast_steps):
    n_cast = (len(refs) - 11) // 2
    cast_in = refs[:n_cast]
    o_ref, f_ref, g_ref, gr_ref = refs[n_cast:n_cast + 4]
    cast_out = refs[n_cast + 4:2 * n_cast + 4]
    wbuf, wbf_sc, wg_sc, xs_sc, sem, gsem, xsem = refs[2 * n_cast + 4:]
    n = pl.program_id(0)
    m = pl.program_id(1)
    tn = wbf_sc.shape[1]

    @pl.when(m < cast_steps)
    def _cast_other_weights():
        for src, dst in zip(cast_in, cast_out):
            dst[...] = src[...].astype(BF16)

    def tile_copy(tile):
        start = pl.multiple_of(tile * tn + jnp.where(tile >= n_gate_tile, _N_GATES, 0), _N_GATES)
        return pltpu.make_async_copy(wt_hbm.at[pl.ds(start, tn), :], wbuf, sem.at[0])

    def gate_copy():
        return pltpu.make_async_copy(wt_hbm.at[pl.ds(_OFF_ML_I, _N_GATES), :],
                                     wg_sc.at[pl.ds(0, _N_GATES), :], gsem.at[0])

    @pl.when(m == 0)
    def _next_weight_tile():
        @pl.when(n == 0)
        def _first():
            tile_copy(0).start()
            wg_sc[_N_GATES:, :] = jnp.zeros((LANES - _N_GATES, wg_sc.shape[1]), F32)
            gate_copy().start()
            gate_copy().wait()
            sample_copy = pltpu.make_async_copy(xs_ref, xs_sc, xsem.at[0])
            sample_copy.start()
            sample_copy.wait()

        tile_copy(n).wait()
        wbf_sc[...] = wbuf[...].T.astype(BF16)

        @pl.when(n + 1 < pl.num_programs(0))
        def _prefetch():
            tile_copy(n + 1).start()

    x = jnp.where(m < prompt_tiles, xp_ref[...].astype(BF16), xs_sc[...])
    acc = jnp.dot(x, wbf_sc[...], preferred_element_type=F32)
    o_ref[...] = acc.astype(BF16)

    @pl.when(n == n_f32)
    def _f32_outputs():
        f_ref[...] = acc
        wg = wg_sc[...].astype(BF16)
        gates = _dot_nt(x, wg)
        g_ref[...] = gates
        gr_ref[0] = gates.T[0:_N_GATES, :]


def _inproj(xp_b, xs_b, w_t, cast_weights):
    k = xp_b.shape[1]
    tm, tn = _PROJ_TM, _PROJ_TN
    p_tiles = xp_b.shape[0] // tm
    assert xs_b.shape[0] == tm
    n_m = p_tiles + 1
    tokens = n_m * tm
    n_n = _P_WIDTH // tn
    n_f32 = _P_HG_F // tn
    cast_steps = min(n_m, _CAST_COL_BLOCKS)

    def parked(n, m):
        return jnp.where(n < n_f32, 0, jnp.where(n == n_f32, m, n_m - 1))

    def cast_spec(w):
        return pl.BlockSpec((w.shape[0] // n_n, w.shape[1] // cast_steps),
                            lambda n, m: (n, jnp.minimum(m, cast_steps - 1)))

    cast_specs = [cast_spec(w) for w in cast_weights]
    outs = pl.pallas_call(
        functools.partial(_inproj_kernel, n_f32=n_f32, n_gate_tile=_OFF_ML_I // tn,
                          prompt_tiles=p_tiles, cast_steps=cast_steps),
        grid=(n_n, n_m),
        in_specs=[pl.BlockSpec((tm, k), lambda n, m: (jnp.minimum(m, p_tiles - 1), 0)),
                  pl.BlockSpec(memory_space=pl.ANY),
                  pl.BlockSpec(memory_space=pl.ANY),
                  *cast_specs],
        out_specs=[pl.BlockSpec((tm, tn), lambda n, m: (m, n)),
                   pl.BlockSpec((tm, HG_K), lambda n, m: (parked(n, m), 0)),
                   pl.BlockSpec((tm, LANES), lambda n, m: (parked(n, m), 0)),
                   pl.BlockSpec((1, _N_GATES, tm), lambda n, m: (parked(n, m), 0, 0)),
                   *cast_specs],
        out_shape=[jax.ShapeDtypeStruct((tokens, _P_WIDTH), BF16),
                   jax.ShapeDtypeStruct((tokens, HG_K), F32),
                   jax.ShapeDtypeStruct((tokens, LANES), F32),
                   jax.ShapeDtypeStruct((n_m, 8, tm), F32),
                   *[jax.ShapeDtypeStruct(w.shape, BF16) for w in cast_weights]],
        scratch_shapes=[pltpu.VMEM((tn, k), F32),
                        pltpu.VMEM((k, tn), BF16),
                        pltpu.VMEM((LANES, k), F32),
                        pltpu.VMEM((tm, k), BF16),
                        pltpu.SemaphoreType.DMA((1,)),
                        pltpu.SemaphoreType.DMA((1,)),
                        pltpu.SemaphoreType.DMA((1,))],
        compiler_params=pltpu.CompilerParams(dimension_semantics=("arbitrary", "arbitrary")),
        name="in_proj",
    )(xp_b, xs_b, w_t, *cast_weights)
    return outs[:4], outs[4:]


def _mlstm_steps(*refs, rows, seq_len, heads, together, zero_init, head0=None):
    if zero_init:
        (q_ref, k_ref, v_ref, og_ref, gc_ref, gr_ref, brow_ref, bcol_ref, ng_ref,
         ba_ref, c_ref, n_ref, mrow_ref, m_sc) = refs
        c0_ref, n0_ref = c_ref, n_ref
    else:
        (q_ref, k_ref, v_ref, og_ref, gc_ref, gr_ref, brow_ref, bcol_ref, ng_ref, c0_ref, n0_ref,
         ba_ref, c_ref, n_ref, mrow_ref) = refs
    nseq = rows // seq_len
    if head0 is None:
        head0 = pl.program_id(1) * heads

    if zero_init:
        @pl.when(pl.program_id(2) == 0)
        def _init():
            c_ref[...] = jnp.zeros(c_ref.shape, F32)
            n_ref[...] = jnp.zeros(n_ref.shape, F32)
            m_sc[...] = jnp.zeros(m_sc.shape, F32)

    same, lower, upper = _seq_masks(rows, seq_len)
    causal = lower > 0.5
    lane = lax.broadcasted_iota(jnp.int32, (rows, LANES), 1)
    sub = lax.broadcasted_iota(jnp.int32, (_N_GATES, rows), 0)

    def sel_lane(x, idx):
        return jnp.sum(jnp.where(lane == idx, x, 0.0), axis=1, keepdims=True)

    def sel_sub(x, idx):
        return jnp.sum(jnp.where(sub == idx, x, 0.0), axis=0, keepdims=True)

    gc_raw = gc_ref[...]
    gc = gc_raw + brow_ref[...]
    lf_cols = _log_sigmoid(gc)
    b_cols = _dot01_right(lower.astype(BF16), lf_cols)
    gr = gr_ref[0] + bcol_ref[...]
    b_rows = _dot01_left(_log_sigmoid(gr), upper.astype(BF16))
    if nseq > 1:
        b_ends = _dot01_right(same.astype(BF16), lf_cols)
        ci = lax.broadcasted_iota(jnp.int32, (rows, rows), 1)
        last = (same * jnp.where((ci & (seq_len - 1)) == seq_len - 1, 1.0, 0.0)).astype(BF16)
        shift = seq_len.bit_length() - 1
        row_seq = lax.broadcasted_iota(jnp.int32, (rows, 1), 0) >> shift

    def head_steps(hh):
        head = head0 + hh
        ks = slice(hh * ML_DK, (hh + 1) * ML_DK)
        vs = slice(hh * ML_DV, (hh + 1) * ML_DV)
        ig_col = sel_lane(gc, head)
        b_col = sel_lane(b_cols, head + ML_HEADS)
        if zero_init:
            m_prev = jnp.broadcast_to(m_sc[hh:hh + 1, 0:1], (rows, 1))
        else:
            m_prev = sel_lane(gc_raw, head + 2 * ML_HEADS)
        ig_row = sel_sub(gr, head)
        b_row = sel_sub(b_rows, head + ML_HEADS)

        logd = jnp.where(causal, (b_col - b_row) + ig_row, -jnp.inf)
        m_t = jnp.maximum(b_col + m_prev, jnp.max(logd, axis=1, keepdims=True))
        d = jnp.exp(logd - m_t)
        w_inter = jnp.exp(b_col + m_prev - m_t)
        yield

        qb = q_ref[:, ks].astype(BF16)
        kb = k_ref[:, ks].astype(BF16)
        vb = v_ref[:, vs].astype(BF16)
        s = _dot_nt(qb, kb) * (d * ML_K_SCALE)
        num = jnp.dot(s.astype(BF16), vb, preferred_element_type=F32)
        den = jnp.sum(s, axis=1, keepdims=True)

        if nseq == 1:
            b_end = b_col[rows - 1:rows, :]
            m_new = m_t[rows - 1:rows, :]
        else:
            b_end = sel_lane(b_ends, head + ML_HEADS)
            m_new = _dot01_right(last, jnp.broadcast_to(m_t, (rows, LANES)))[:, 0:1]
        w_end = jnp.exp(b_end - b_col + ig_col - m_new)
        decay = jnp.exp(b_end + m_prev - m_new)

        qf = qb.astype(F32)
        kw = (w_end * ML_K_SCALE) * kb.astype(F32)
        yield

        if nseq == 1:
            c_prev = c0_ref[0, hh]
            n_prev = n0_ref[0, hh]
            q_c = jnp.dot(qb, c_prev.astype(BF16), preferred_element_type=F32)
            q_n = jnp.sum(qf * n_prev, axis=1, keepdims=True)
            dec = decay[0:1, :]
            c_ref[0, hh] = dec * c_prev + _dot_tn(kw.astype(BF16), vb)
            n_ref[0, hh] = dec * n_prev + jnp.sum(kw, axis=0, keepdims=True)
        else:
            q_c = jnp.zeros((rows, ML_DV), F32)
            q_n = jnp.zeros((rows, 1), F32)
            for j in range(nseq):
                in_seq = row_seq == j
                c_prev = c0_ref[j, hh]
                n_prev = n0_ref[j, hh]
                q_c = jnp.where(
                    in_seq, jnp.dot(qb, c_prev.astype(BF16), preferred_element_type=F32), q_c)
                q_n = jnp.where(in_seq, jnp.sum(qf * n_prev, axis=1, keepdims=True), q_n)
                kw_j = jnp.where(in_seq, kw, 0.0)
                dec = decay[j * seq_len:j * seq_len + 1, :]
                c_ref[j, hh] = dec * c_prev + _dot_tn(kw_j.astype(BF16), vb)
                n_ref[j, hh] = dec * n_prev + jnp.sum(kw_j, axis=0, keepdims=True)
                yield
        if nseq == 1:
            yield

        num = num + w_inter * q_c
        den = den + w_inter * q_n
        h_out = num / jnp.maximum(jnp.abs(den), jnp.exp(-m_t))
        mu =jnp.mean(h_out, axis=1, keepdims=True)
        xc = h_out - mu
        var = jnp.mean(xc * xc, axis=1, keepdims=True)
        hn = xc * lax.rsqrt(var + LN_EPS) * ng_ref[:, vs]
        ba_ref[:, vs] = (hn * _sigmoid(og_ref[:, vs].astype(F32))).astype(ba_ref.dtype)
        mrow_ref[hh] = jnp.broadcast_to(m_t, (rows, LANES))
        if zero_init:
            m_sc[hh:hh + 1, :] = jnp.broadcast_to(m_new, (1, LANES))
        yield

    yield from _interleaved([head_steps(hh) for hh in range(heads)], together)


def _mlstm_parts(proj, gates_col, gates_row, brow, bcol, norm_g, state, *, row0, batch, seq, rows,
                 seq_len, heads, together, index=None):
    tokens = batch * seq
    zero_init = state is None
    nseq = rows // seq_len
    nchunk = seq // seq_len if nseq == 1 else 1
    ngroup = tokens // (rows * nchunk)
    bq, bv = heads * ML_DK, heads * ML_DV
    rb0 = row0 // rows
    per_row_tile = gates_row.shape[2] // rows
    head0 = None
    if index is None:
        index = lambda g, h, c: (g * nchunk + c, h, g)
    else:
        assert heads == ML_HEADS
        head0 = 0
    row = lambda *ids: index(*ids)[0]
    hblk = lambda *ids: index(*ids)[1]
    sblk = lambda *ids: index(*ids)[2]

    in_specs = [
        pl.BlockSpec((rows, bq), lambda *ids: (rb0 + row(*ids), _P_ML_Q // bq + hblk(*ids))),
        pl.BlockSpec((rows, bq), lambda *ids: (rb0 + row(*ids), _P_ML_K // bq + hblk(*ids))),
        pl.BlockSpec((rows, bv), lambda *ids: (rb0 + row(*ids), _P_ML_V // bv + hblk(*ids))),
        pl.BlockSpec((rows, bv), lambda *ids: (rb0 + row(*ids), _P_ML_O // bv + hblk(*ids))),
        pl.BlockSpec((rows, LANES), lambda *ids: (row(*ids), 0)),
        pl.BlockSpec((1, _N_GATES, rows),
                     lambda *ids: (row(*ids) // per_row_tile, 0, row(*ids) % per_row_tile)),
        pl.BlockSpec((1, LANES), lambda *ids: (0, 0)),
        pl.BlockSpec((_N_GATES, 1), lambda *ids: (0, 0)),
        pl.BlockSpec((1, bv), lambda *ids: (0, hblk(*ids))),
    ]
    args = [proj, proj, proj, proj, gates_col, gates_row, brow, bcol, norm_g]
    state_specs = [
        pl.BlockSpec((nseq, heads, ML_DK, ML_DV), lambda *ids: (sblk(*ids), hblk(*ids), 0, 0)),
        pl.BlockSpec((nseq, heads, 1, ML_DK), lambda *ids: (sblk(*ids), hblk(*ids), 0, 0)),
    ]
    scratch = []
    if zero_init:
        scratch = [pltpu.VMEM((SUBLANES, LANES), F32)]
    else:
        in_specs += state_specs
        args += [state[0], state[1]]
    out_specs = [
        pl.BlockSpec((rows, bv), lambda *ids: (row(*ids), hblk(*ids))),
        *state_specs,
        pl.BlockSpec((heads, rows, LANES), lambda *ids: (hblk(*ids), row(*ids), 0)),
    ]
    out_shape = [
        jax.ShapeDtypeStruct((tokens, ML_V), BF16),
        jax.ShapeDtypeStruct((batch, ML_HEADS, ML_DK, ML_DV), F32),
        jax.ShapeDtypeStruct((batch, ML_HEADS, 1, ML_DK), F32),
        jax.ShapeDtypeStruct((ML_HEADS, tokens, LANES), F32),
    ]
    return dict(
        steps=functools.partial(_mlstm_steps, rows=rows, seq_len=seq_len, heads=heads,
                                together=together, zero_init=zero_init, head0=head0),
        n_steps=heads // together * (3 + nseq),
        grid=(ngroup, ML_HEADS // heads, nchunk),
        in_specs=in_specs, args=args, out_specs=out_specs, out_shape=out_shape, scratch=scratch)


def _run_parts(parts, name):
    n_in = [len(p["in_specs"]) for p in parts]
    n_out = [len(p["out_specs"]) for p in parts]
    n_scr = [len(p["scratch"]) for p in parts]

    def kernel(*refs):
        ins = refs[:sum(n_in)]
        outs = refs[sum(n_in):sum(n_in) + sum(n_out)]
        scr = refs[sum(n_in) + sum(n_out):]
        gens = []
        i = o = s = 0
        for p, ni, no, ns in zip(parts, n_in, n_out, n_scr):
            gens.append(p["steps"](*ins[i:i + ni], *outs[o:o + no], *scr[s:s + ns]))
            i, o, s = i + ni, o + no, s + ns
        order = sorted(((j + 0.5) / p["n_steps"], k) for k, p in enumerate(parts)
                       for j in range(p["n_steps"]))
        for _, k in order:
            next(gens[k], None)
        for gen in gens:
            for _ in gen:
                pass

    outs = pl.pallas_call(
        kernel,
        grid=parts[0]["grid"],
        in_specs=[s for p in parts for s in p["in_specs"]],
        out_specs=[s for p in parts for s in p["out_specs"]],
        out_shape=[s for p in parts for s in p["out_shape"]],
        scratch_shapes=[s for p in parts for s in p["scratch"]],
        compiler_params=pltpu.CompilerParams(
            dimension_semantics=("parallel", "parallel", "arbitrary")),
        name=name,
    )(*[a for p in parts for a in p["args"]])
    split, o = [], 0
    for no in n_out:
        split.append(outs[o:o + no])
        o += no
    return split


def _hgrn_level_ids(rows, seq_len, direct, levels):
    t = np.arange(rows)[:, None]
    s = np.arange(rows)[None, :]
    ids = np.full((rows, rows), -1, np.int32)
    count = ((t // direct == s // direct) & (s <= t)).astype(np.int32)
    for idx, (block, sub_size) in enumerate(levels):
        owned = (t // block == s // block) & ((s % block) // sub_size < (t % block) // sub_size)
        ids[owned] = idx
        count += owned
    wanted = (t // seq_len == s // seq_len) & (s <= t)
    assert np.array_equal(count, wanted.astype(np.int32)), (rows, seq_len, direct, levels)
    return ids


_INTRA_YIELD_BLOCKS = 32


def _hgrn_intra_yields(rows, direct, levels):
    return 2 + len(levels) + 2 * (max(rows // direct // _INTRA_YIELD_BLOCKS, 1) - 1)


def _hgrn_intra(q, kin, g2, gk2, level_masks, rows, direct, levels):
    ngroups = max(rows // LANES, 1)
    rowi = lax.broadcasted_iota(jnp.int32, (direct, LANES), 0)
    lanei = lax.broadcasted_iota(jnp.int32, (direct, LANES), 1)
    keep = [jnp.where(lanei == s, rowi, -1) >= s for s in range(direct)]
    zero_group = jnp.zeros((direct, LANES), F32)
    prods = []
    for blk in range(rows // direct):
        r0 = blk * direct
        qb = q[r0:r0 + direct]
        gb = g2[r0:r0 + direct]
        gkb = gk2[r0:r0 + direct]
        for s in range(direct):
            prods.append(qb * jnp.exp2(gb - gkb[s:s + 1, :]))
        if (blk + 1) % _INTRA_YIELD_BLOCKS == 0 and blk + 1 < rows // direct:
            yield
    sums = jnp.dot(jnp.concatenate(prods, axis=0).astype(BF16), jnp.ones((HG_DK, LANES), BF16),
                   preferred_element_type=F32)
    yield
    panels = []
    for blk in range(rows // direct):
        r0 = blk * direct
        ag = zero_group
        for s in range(direct):
            p0 = (blk * direct + s) * direct
            ag = jnp.where(keep[s], sums[p0:p0 + direct], ag)
        lane0 = r0 % LANES
        if lane0:
            ag = pltpu.roll(ag, lane0, axis=1)
        grp = r0 // LANES
        pieces = [zero_group] * grp + [ag] + [zero_group] * (ngroups - grp - 1)
        panels.append(pieces[0] if ngroups == 1 else jnp.concatenate(pieces, axis=1))
        if (blk + 1) % _INTRA_YIELD_BLOCKS == 0 and blk + 1 < rows // direct:
            yield
    a = panels[0] if len(panels) == 1 else jnp.concatenate(panels, axis=0)
    if rows < LANES:
        a = a[:, :rows]
    yield

    if levels:
        rowid = lax.broadcasted_iota(jnp.int32, (rows, 1), 0)
    for idx, (block, sub_size) in enumerate(levels):
        sshift = sub_size.bit_length() - 1
        row_sub = (rowid & (block - 1)) >> sshift
        q_parts, k_parts = [], []
        for j in range(1, block // sub_size):
            refs = []
            for b0 in range(0, rows, block):
                r = b0 + j * sub_size - 1
                refs.append(jnp.broadcast_to(g2[r:r + 1, :], (block, HG_DK)))
            g_ref = refs[0] if len(refs) == 1 else jnp.concatenate(refs, axis=0)
            e = jnp.exp2(-jnp.abs(g2 - g_ref))
            q_parts.append(jnp.where(row_sub == j, q * e, 0.0).astype(BF16))
            k_parts.append((kin * e).astype(BF16))
        qcat = q_parts[0] if len(q_parts) == 1 else jnp.concatenate(q_parts, axis=1)
        kcat = k_parts[0] if len(k_parts) == 1 else jnp.concatenate(k_parts, axis=1)
        a = jnp.where(level_masks[idx], _dot_nt(qcat, kcat), a)
        yield
    return a


def _hgrn_steps(*refs, layer, rows, seq_len, heads, together, direct, levels, zero_init):
    if zero_init:
        q_ref, f_ref, i_ref, g_ref, lbl_ref, ng_ref, lvl_ref, bb_ref, s_ref = refs
        s0_ref = s_ref
    else:
        q_ref, f_ref, i_ref, g_ref, lbl_ref, ng_ref, lvl_ref, s0_ref, bb_ref, s_ref = refs
    nseq = rows // seq_len

    if zero_init:
        @pl.when(pl.program_id(2) == 0)
        def _init():
            s_ref[...] = jnp.zeros(s_ref.shape, F32)

    same, lower, _ = _seq_masks(rows, seq_len)
    lower_b = lower.astype(BF16)
    same_b = same.astype(BF16)
    level_ids = lvl_ref[...]
    level_masks = [level_ids == idx for idx in range(len(levels))]
    if nseq > 1:
        shift = seq_len.bit_length() - 1
        row_seq = lax.broadcasted_iota(jnp.int32, (rows, 1), 0) >> shift

    def decay_cols(row):
        col = jnp.broadcast_to(jnp.exp2(row), (HG_DK, HG_DK)).T
        return jnp.concatenate([col] * (HG_DV // HG_DK), axis=1)

    def head_steps(hh):
        ks = slice(hh * HG_DK, (hh + 1) * HG_DK)
        vs = slice(hh * HG_DV, (hh + 1) * HG_DV)
        lg = lbl_ref[:, ks]
        ex = jnp.exp(lg - jnp.max(lg, axis=0, keepdims=True))
        lb = (jnp.sum(ex[0:layer + 1, :], axis=0, keepdims=True)
              / jnp.sum(ex, axis=0, keepdims=True))

        f = lb + (1.0 - lb) * _sigmoid(f_ref[:, ks])
        kin = 1.0 - f
        lf2 = jnp.log2(f)
        g2 = _dot01_right(lower_b, lf2)
        gk2 = g2 - jnp.log2(kin)
        yield
        q = q_ref[:, ks].astype(F32)
        if nseq == 1:
            g_end = jnp.broadcast_to(g2[rows - 1:rows, :], (rows, HG_DK))
        else:
            g_end = _dot01_right(same_b, lf2)
        qg = (q * jnp.exp2(g2)).astype(BF16)
        kg = kin * jnp.exp2(g_end - g2)
        ib = i_ref[:, vs].astype(BF16)
        yield

        a = yield from _hgrn_intra(q, kin, g2, gk2, level_masks, rows, direct, levels)
        o = jnp.dot(a.astype(BF16), ib, preferred_element_type=F32)

        if nseq == 1:
            s_prev = s0_ref[0, hh]
            o = o + jnp.dot(qg, s_prev.astype(BF16), preferred_element_type=F32)
            s_ref[0, hh] = decay_cols(g_end[0:1, :]) * s_prev + _dot_tn(kg.astype(BF16), ib)
            yield
        else:
            for j in range(nseq):
                in_seq = row_seq == j
                s_prev = s0_ref[j, hh]
                o_j = jnp.dot(qg, s_prev.astype(BF16), preferred_element_type=F32)
                o = o + jnp.where(in_seq, o_j, 0.0)
                kg_j = jnp.where(in_seq, kg, 0.0).astype(BF16)
                s_ref[j, hh] = (decay_cols(g_end[j * seq_len:j * seq_len + 1, :]) * s_prev
                                + _dot_tn(kg_j, ib))
                yield

        o = o * lax.rsqrt(jnp.mean(o * o, axis=1, keepdims=True) + LN_EPS)
        gate = g_ref[:, vs].astype(F32)
        bb_ref[:, vs] = (o * ng_ref[:, vs] * (gate * _sigmoid(gate))).astype(bb_ref.dtype)
        yield

    yield from _interleaved([head_steps(hh) for hh in range(heads)], together)


def _hgrn_parts(proj, hg_f, lb_logits, norm_g, s0, *, layer, row0, batch, seq, rows, seq_len, heads,
                together, direct, levels, index=None):
    tokens = batch * seq
    zero_init = s0 is None
    nseq = rows // seq_len
    nchunk = seq // seq_len if nseq == 1 else 1
    ngroup = tokens // (rows * nchunk)
    rb0 = row0 // rows
    if index is None:
        index = lambda g, h, c: (g * nchunk + c, h, g)
    row = lambda *ids: index(*ids)[0]
    hblk = lambda *ids: index(*ids)[1]
    sblk = lambda *ids: index(*ids)[2]

    wk, wv = heads * HG_DK, heads * HG_DV
    in_specs = [
        pl.BlockSpec((rows, wk), lambda *ids: (rb0 + row(*ids), _P_HG_Q // wk + hblk(*ids))),
        pl.BlockSpec((rows, wk), lambda *ids: (rb0 + row(*ids), hblk(*ids))),
        pl.BlockSpec((rows, wv), lambda *ids: (rb0 + row(*ids), _P_HG_I // wv + hblk(*ids))),
        pl.BlockSpec((rows, wv), lambda *ids: (rb0 + row(*ids), _P_HG_G // wv + hblk(*ids))),
        pl.BlockSpec((DEPTH + 1, wk), lambda *ids: (0, hblk(*ids))),
        pl.BlockSpec((1, wv), lambda *ids: (0, hblk(*ids))),
        pl.BlockSpec((rows, rows), lambda *ids: (0, 0)),
    ]
    level_ids = jnp.asarray(_hgrn_level_ids(rows, seq_len, direct, levels))
    args = [proj, hg_f, proj, proj, lb_logits, norm_g, level_ids]
    state_spec = pl.BlockSpec((nseq, heads, HG_DK, HG_DV),
                              lambda *ids: (sblk(*ids), hblk(*ids), 0, 0))
    if not zero_init:
        in_specs.append(state_spec)
        args.append(s0)
    return dict(
        steps=functools.partial(_hgrn_steps, layer=layer, rows=rows, seq_len=seq_len, heads=heads,
                                together=together, direct=direct, levels=levels,
                                zero_init=zero_init),
        n_steps=heads // together * (3 + _hgrn_intra_yields(rows, direct, levels) + nseq),
        grid=(ngroup, HG_HEADS // heads, nchunk),
        in_specs=in_specs, args=args,
        out_specs=[pl.BlockSpec((rows, wv), lambda *ids: (row(*ids), hblk(*ids))), state_spec],
        out_shape=[jax.ShapeDtypeStruct((tokens, HG_V), BF16),
                   jax.ShapeDtypeStruct((batch, HG_HEADS, HG_DK, HG_DV), F32)],
        scratch=[])


def _merge_kernel(ba_ref, wa_ref, bb_ref, wb_ref, ga_ref, gb_ref, o_ref):
    ya = jnp.dot(ba_ref[...], wa_ref[...], preferred_element_type=F32)
    yb = jnp.dot(bb_ref[...], wb_ref[...], preferred_element_type=F32)
    ga = _sigmoid(ga_ref[...].astype(F32))
    gb = _sigmoid(gb_ref[...].astype(F32))
    o_ref[...] = (ga * ya + gb * yb).astype(o_ref.dtype)


def _merge(branch_a, w_a, branch_b, w_b, proj, row0):
    m = branch_a.shape[0]
    tm, tn = _MERGE_TM, _MERGE_TN
    rb0 = row0 // tm
    ga_blk = _P_GATE_A // tn
    gb_blk = _P_GATE_B // tn
    return pl.pallas_call(
        _merge_kernel,
        grid=(D_MODEL // tn, m // tm),
        in_specs=[pl.BlockSpec((tm, ML_V), lambda j, i: (i, 0)),
                  pl.BlockSpec((ML_V, tn), lambda j, i: (0, j)),
                  pl.BlockSpec((tm, HG_V), lambda j, i: (i, 0)),
                  pl.BlockSpec((HG_V, tn), lambda j, i: (0, j)),
                  pl.BlockSpec((tm, tn), lambda j, i: (rb0 + i, ga_blk + j)),
                  pl.BlockSpec((tm, tn), lambda j, i: (rb0 + i, gb_blk + j))],
        out_specs=pl.BlockSpec((tm, tn), lambda j, i: (i, j)),
        out_shape=jax.ShapeDtypeStruct((m, D_MODEL), BF16),
        compiler_params=pltpu.CompilerParams(dimension_semantics=("parallel", "arbitrary")),
        name="merge",
    )(branch_a, w_a, branch_b, w_b, proj, proj)


def _layernorm_rows(z, g, b):
    mu = jnp.mean(z, axis=1, keepdims=True)
    zc = z - mu
    var = jnp.mean(zc * zc, axis=1, keepdims=True)
    return zc * lax.rsqrt(var + LN_EPS) * g + b


def _outproj_kernel(mg_ref, w_ref, x_ref, g_ref, b_ref, x1_ref):
    mix = jnp.dot(mg_ref[...], w_ref[...], preferred_element_type=F32)
    x1_ref[...] = _layernorm_rows(DEEPNORM_ALPHA * x_ref[...] + mix, g_ref[...], b_ref[...])


def _outproj(merged, w_out, x, ln_g, ln_b):
    tm = _OUT_TM
    m = x.shape[0]
    row = lambda i: (i, 0)
    const = lambda i: (0, 0)
    return pl.pallas_call(
        _outproj_kernel,
        grid=(m // tm,),
        in_specs=[pl.BlockSpec((tm, D_MODEL), row),
                  pl.BlockSpec((D_MODEL, D_MODEL), const),
                  pl.BlockSpec((tm, D_MODEL), row),
                  pl.BlockSpec((1, D_MODEL), const),
                  pl.BlockSpec((1, D_MODEL), const)],
        out_specs=pl.BlockSpec((tm, D_MODEL), row),
        out_shape=jax.ShapeDtypeStruct((m, D_MODEL), F32),
        compiler_params=pltpu.CompilerParams(dimension_semantics=("parallel",)),
        name="out_proj_ln",
    )(merged, w_out, x, ln_g, ln_b)


def _mlp_kernel(wu_ref, wd_ref, x1_ref, g_ref, b_ref, y_ref, x1b_sc):
    f = pl.program_id(1)

    @pl.when(f == 0)
    def _init():
        y_ref[...] = jnp.zeros(y_ref.shape, F32)
        x1b_sc[...] = x1_ref[...].astype(BF16)

    hid = jnp.maximum(jnp.dot(x1b_sc[...], wu_ref[...], preferred_element_type=F32), 0.0)
    hid = (hid * hid).astype(BF16)
    y_ref[...] += jnp.dot(hid, wd_ref[...], preferred_element_type=F32)

    @pl.when(f == pl.num_programs(1) - 1)
    def _finish():
        z = DEEPNORM_ALPHA * x1_ref[...] + y_ref[...]
        y_ref[...] = _layernorm_rows(z, g_ref[...], b_ref[...])


def _mlp(w_up, w_down, x1, ln_g, ln_b):
    tm, tf = _MLP_TM, _MLP_TF
    tokens = x1.shape[0]
    return pl.pallas_call(
        _mlp_kernel,
        grid=(tokens // tm, D_FF // tf),
        in_specs=[pl.BlockSpec((D_MODEL, tf), lambda i, f: (0, f)),
                  pl.BlockSpec((tf, D_MODEL), lambda i, f: (f, 0)),
                  pl.BlockSpec((tm, D_MODEL), lambda i, f: (i, 0)),
                  pl.BlockSpec((1, D_MODEL), lambda i, f: (0, 0)),
                  pl.BlockSpec((1, D_MODEL), lambda i, f: (0, 0))],
        out_specs=pl.BlockSpec((tm, D_MODEL), lambda i, f: (i, 0)),
        out_shape=jax.ShapeDtypeStruct((tokens, D_MODEL), F32),
        scratch_shapes=[pltpu.VMEM((tm, D_MODEL), BF16)],
        compiler_params=pltpu.CompilerParams(dimension_semantics=("parallel", "arbitrary")),
        name="mlp_ln",
    )(w_up, w_down, x1, ln_g, ln_b)


_PROMPT_ML = dict(rows=256, seq_len=256, heads=4, together=1)
_PROMPT_HG = dict(rows=256, seq_len=256, heads=4, together=2, direct=8,
                  levels=((16, 8), (32, 16), (64, 32), (128, 64), (256, 128)))
_SAMPLE_ML = dict(rows=16, seq_len=8, heads=4, together=4)
_SAMPLE_HG = dict(rows=32, seq_len=8, heads=8, together=1, direct=8, levels=())


def kernel(x_prompt, x_sample, state_mlstm_C, state_mlstm_n, state_mlstm_m, state_hgrn_S,
           hg_lb_logits, w_in, b_ig, b_fg, ml_norm_g, hg_norm_g, w_branch_a, w_branch_b, w_out,
           ln1_g, ln1_b, w_up, w_down, ln2_g, ln2_b):
    batch_p, seq_p, _ = x_prompt.shape
    batch_s, seq_s, _ = x_sample.shape
    tok_p, tok_s = batch_p * seq_p, batch_s * seq_s
    total = tok_p + tok_s
    group_p = dict(row0=0, batch=batch_p, seq=seq_p)
    group_s = dict(row0=tok_p, batch=batch_s, seq=seq_s)
    xp = x_prompt.reshape(tok_p, D_MODEL)
    xs = x_sample.reshape(tok_s, D_MODEL)
    lb_logits = hg_lb_logits.astype(F32)
    states_p, states_s = [], []
    for l in range(DEPTH):
        (proj, hg_f, gates, gates_gm), (w_a_b, w_b_b, w_out_b, w_up_b, w_down_b) = _inproj(
            xp, xs.astype(BF16), jnp.swapaxes(w_in[l], 0, 1),
            (w_branch_a[l], w_branch_b[l], w_out[l], w_up[l], w_down[l]))

        gate_bias = jnp.concatenate([b_ig[l], b_fg[l]]).astype(F32)
        brow = jnp.zeros((1, LANES), F32).at[0, :_N_GATES].set(gate_bias)
        bcol = gate_bias.reshape(_N_GATES, 1)
        ml_g = ml_norm_g[l].reshape(1, ML_V).astype(F32)
        hg_g = hg_norm_g[l].reshape(1, HG_V).astype(F32)

        m_rows = jnp.repeat(state_mlstm_m[l].astype(F32), seq_s, axis=0)
        gates_s = gates[tok_p:].at[:, _N_GATES:_N_GATES + ML_HEADS].set(m_rows)
        rows_s = _SAMPLE_ML["rows"]
        gm_s = gates_gm[tok_p // _PROJ_TM:]
        gm_s = gm_s.reshape(-1, _N_GATES, _PROJ_TM // rows_s, rows_s).transpose(0, 2, 1, 3)
        gm_s = gm_s.reshape(tok_s // rows_s, _N_GATES, rows_s)
        state_ml = (state_mlstm_C[l].astype(F32),
                    state_mlstm_n[l].astype(F32).reshape(batch_s, ML_HEADS, 1, ML_DK))
        ln1 = (ln1_g[l].reshape(1, D_MODEL), ln1_b[l].reshape(1, D_MODEL))
        ln2 = (ln2_g[l].reshape(1, D_MODEL), ln2_b[l].reshape(1, D_MODEL))

        ml_p = _mlstm_parts(proj, gates, gates_gm, brow, bcol, ml_g, None, **group_p, **_PROMPT_ML)
        hg_p = _hgrn_parts(proj, hg_f, lb_logits, hg_g, None, layer=l, **group_p, **_PROMPT_HG)

        def guest(parts_fn, host, **kwargs):
            grid = host["grid"]
            assert math.prod(grid) * kwargs["rows"] == tok_s, (grid, kwargs["rows"])
            flat = lambda g, h, c: (g * grid[1] + h) * grid[2] + c
            return parts_fn(index=lambda g, h, c: (flat(g, h, c), 0, flat(g, h, c)), **kwargs)

        ml_s = guest(functools.partial(_mlstm_parts, proj, gates_s, gm_s, brow, bcol, ml_g, state_ml,
                                       **group_s), hg_p, **_SAMPLE_ML)
        hg_s = guest(functools.partial(_hgrn_parts, proj, hg_f, lb_logits, hg_g,
                                       state_hgrn_S[l].astype(F32), layer=l, **group_s),
                     ml_p, **_SAMPLE_HG)
        (bb_p, s_p), (ba_s, c_s, n_s, m_all_s) = _run_parts([hg_p, ml_s], "hgrn2_prompt_mlstm_sample")
        (ba_p, c_p, n_p, m_all_p), (bb_s, s_s) = _run_parts([ml_p, hg_s], "mlstm_prompt_hgrn2_sample")

        merged_p = _merge(ba_p, w_a_b, bb_p, w_b_b, proj, 0)
        merged_s = _merge(ba_s, w_a_b, bb_s, w_b_b, proj, tok_p)
        x1_p = _outproj(merged_p, w_out_b, xp, *ln1)
        x1_s = _outproj(merged_s, w_out_b, xs, *ln1)
        xp = _mlp(w_up_b, w_down_b, x1_p, *ln2)
        xs = _mlp(w_up_b, w_down_b, x1_s, *ln2)

        states_p.append((c_p, n_p.reshape(batch_p, ML_HEADS, ML_DK),
                         m_all_p[:, seq_p - 1::seq_p, 0].T, s_p))
        states_s.append((c_s, n_s.reshape(batch_s, ML_HEADS, ML_DK),
                         m_all_s[:, seq_s - 1::seq_s, 0].T, s_s))
    stack = lambda states, k: jnp.stack([s[k] for s in states])
    return (xp.reshape(batch_p, seq_p, D_MODEL), xs.reshape(batch_s, seq_s, D_MODEL),
            stack(states_p, 0), stack(states_p, 1), stack(states_p, 2), stack(states_p, 3),
            stack(states_s, 0), stack(states_s, 1), stack(states_s, 2), stack(states_s, 3))
```
